```python
import math
import jax
import jax.numpy as jnp
from jax import lax
import numpy as np


D_MODEL = 2048
BATCH = 8
SEQ = 4096
DEPTH = 1

RWKV_WIDTH = D_MODEL // 2
RWKV_HEAD_DIM = 64
RWKV_HEADS = RWKV_WIDTH // RWKV_HEAD_DIM
DECAY_LORA = max(32, int(round(1.8 * RWKV_WIDTH ** 0.5 / 32)) * 32)
AAA_LORA = max(32, int(round(1.8 * RWKV_WIDTH ** 0.5 / 32)) * 32)
GATE_LORA = max(32, int(round(0.6 * RWKV_WIDTH ** 0.8 / 32)) * 32)
RWKV_COLS = 3 * RWKV_WIDTH + DECAY_LORA + AAA_LORA + GATE_LORA
RWKV_SPLITS = (RWKV_WIDTH, 2 * RWKV_WIDTH, 3 * RWKV_WIDTH,
               3 * RWKV_WIDTH + DECAY_LORA, 3 * RWKV_WIDTH + DECAY_LORA + AAA_LORA)
GN_EPS = 64e-5
L2_EPS = 1e-12

POOL_WIDTH = D_MODEL // 2
POOL_WINDOWS = (2, 4, 8, 16)
POOL_GROUPS = len(POOL_WINDOWS)
POOL_GROUP_DIM = POOL_WIDTH // POOL_GROUPS

IN_COLS = RWKV_COLS + POOL_WIDTH + 2 * D_MODEL
IN_SPLITS = (RWKV_COLS, RWKV_COLS + POOL_WIDTH, RWKV_COLS + POOL_WIDTH + D_MODEL)

D_FF = int(math.ceil(8 * D_MODEL / 3 / 256)) * 256
MACARON_WEIGHT = 0.5
NORM_EPS = 1e-6

kernel_name = 'rwkv7_pool_macaron_hybrid'


def _rmsnorm(x, g):
    xf = x.astype(jnp.float32)
    y = xf * lax.rsqrt(jnp.mean(xf * xf, axis=-1, keepdims=True) + NORM_EPS)
    return (y * g.astype(jnp.float32)).astype(x.dtype)


def _swiglu(x, w_gate, w_up, w_down):
    return (jax.nn.silu(x @ w_gate) * (x @ w_up)) @ w_down


def _token_shift(z):
    return jnp.pad(z, ((0, 0), (1, 0), (0, 0)))[:, :-1]


def _wkv7_scan(r, w, k, v, a, b):
    bsz, _, nh, nd = r.shape
    xs = tuple(jnp.moveaxis(t, 1, 0) for t in (r, w, k, v, a, b))

    def step(state, inp):
        r_t, w_t, k_t, v_t, a_t, b_t = inp
        sa = jnp.einsum('bhvk,bhk->bhv', state, a_t)
        state = (state * w_t[:, :, None, :]
                 + sa[..., None] * b_t[:, :, None, :]
                 + v_t[..., None] * k_t[:, :, None, :])
        y_t = jnp.einsum('bhvk,bhk->bhv', state, r_t)
        return state, y_t

    s0 = jnp.zeros((bsz, nh, nd, nd), jnp.float32)
    _, ys = lax.scan(step, s0, xs)
    return jnp.moveaxis(ys, 0, 1)


def _rwkv7_branch(z, mu, w0, w2, a0, a2, g2, k_k, k_a, r_k, gn_w, gn_b):
    bsz, seq, _ = z.shape
    f32 = jnp.float32
    zf = z.astype(f32)
    zs = zf + (_token_shift(zf) - zf) * mu.astype(f32)
    r, k, v, lw, la, lg = jnp.split(zs, RWKV_SPLITS, axis=-1)
    w = -jax.nn.softplus(-(w0.astype(f32) + jnp.tanh(lw) @ w2.astype(f32))) - 0.5
    decay = jnp.exp(-jnp.exp(w))
    a = jax.nn.sigmoid(a0.astype(f32) + la @ a2.astype(f32))
    g = jax.nn.sigmoid(lg) @ g2.astype(f32)
    hv = lambda t: t.reshape(bsz, seq, RWKV_HEADS, RWKV_HEAD_DIM)
    kk = hv(k * k_k.astype(f32))
    kk = kk / jnp.maximum(jnp.sqrt(jnp.sum(kk * kk, axis=-1, keepdims=True)), L2_EPS)
    a_h = hv(a)
    k = k * (1.0 + (a - 1.0) * k_a.astype(f32))
    r_h, k_h, v_h = hv(r), hv(k), hv(v)
    y = _wkv7_scan(r_h, hv(decay), k_h, v_h, -kk, kk * a_h)
    mean = jnp.mean(y, axis=-1, keepdims=True)
    var = jnp.mean(jnp.square(y - mean), axis=-1, keepdims=True)
    y = (y - mean) * lax.rsqrt(var + GN_EPS)
    y = y.reshape(bsz, seq, RWKV_WIDTH) * gn_w.astype(f32) + gn_b.astype(f32)
    bonus = jnp.sum(r_h * k_h * r_k.astype(f32), axis=-1, keepdims=True) * v_h
    y = y + bonus.reshape(bsz, seq, RWKV_WIDTH)
    return (y * g).astype(z.dtype)


def _pool_branch(z, pool_w, pool_scale):
    bsz, seq, _ = z.shape
    zf = z.astype(jnp.float32).reshape(bsz, seq, POOL_GROUPS, POOL_GROUP_DIM)
    c0 = jnp.pad(jnp.cumsum(zf, axis=1), ((0, 0), (1, 0), (0, 0), (0, 0)))
    t = jnp.arange(1, seq + 1, dtype=jnp.float32)
    outs = []
    for gi, win in enumerate(POOL_WINDOWS):
        cg = c0[:, :, gi]
        upper = cg[:, 1:]
        lower = jnp.pad(cg[:, :seq - win + 1], ((0, 0), (win - 1, 0), (0, 0)))
        cnt = jnp.minimum(t, float(win))[None, :, None]
        outs.append((upper - lower) / cnt)
    pooled = jnp.stack(outs, axis=2)
    mixed = pooled - zf
    y = jnp.einsum('bsgc,gcd->bsgd', mixed, pool_w.astype(jnp.float32))
    y = y.reshape(bsz, seq, POOL_WIDTH) * pool_scale.astype(jnp.float32)
    return y.astype(z.dtype)


def _hybrid_mixer(u, w_in, rwkv_mu, rwkv_w0, rwkv_w2, rwkv_a0, rwkv_a2, rwkv_g2, rwkv_k_k,
                  rwkv_k_a, rwkv_r_k, rwkv_gn_w, rwkv_gn_b, w_proj_a, pool_w, pool_scale,
                  w_proj_b, w_out):
    p = u @ w_in
    z_a, z_b, g_a, g_b = jnp.split(p, IN_SPLITS, axis=-1)
    y_a = _rwkv7_branch(z_a, rwkv_mu, rwkv_w0, rwkv_w2, rwkv_a0, rwkv_a2, rwkv_g2,
                        rwkv_k_k, rwkv_k_a, rwkv_r_k, rwkv_gn_w, rwkv_gn_b) @ w_proj_a
    y_b = _pool_branch(z_b, pool_w, pool_scale) @ w_proj_b
    m = jax.nn.sigmoid(g_a) * y_a + jax.nn.sigmoid(g_b) * y_b
    return m @ w_out


def _fwd_setup_inputs(seed: int = 0) -> dict:
    key = jax.random.key(seed)
    ks = iter(jax.random.split(key, 40))
    L, D, W = DEPTH, D_MODEL, RWKV_WIDTH

    def nrm(shape, scale):
        return jax.random.normal(next(ks), shape, jnp.float32) * scale

    def gain(shape):
        return 1.0 + nrm(shape, 0.1)

    return {
        'x': nrm((BATCH, SEQ, D), 1.0),
        'ln_ffn1_pre': gain((L, D)),
        'ln_ffn1_post': gain((L, D)),
        'ffn1_gate': nrm((L, D, D_FF), D ** -0.5),
        'ffn1_up': nrm((L, D, D_FF), D ** -0.5),
        'ffn1_down': nrm((L, D_FF, D), D_FF ** -0.5),
        'ln_mix_pre': gain((L, D)),
        'ln_mix_post': gain((L, D)),
        'w_in': nrm((L, D, IN_COLS), D ** -0.5),
        'rwkv_mu': jax.random.uniform(next(ks), (L, RWKV_COLS), jnp.float32),
        'rwkv_w0': jax.random.uniform(next(ks), (L, W), jnp.float32, -6.0, -0.5),
        'rwkv_w2': nrm((L, DECAY_LORA, W), 0.1),
        'rwkv_a0': nrm((L, W), 0.3),
        'rwkv_a2': nrm((L, AAA_LORA, W), 0.1),
        'rwkv_g2': nrm((L, GATE_LORA, W), GATE_LORA ** -0.5),
        'rwkv_k_k': 0.85 + nrm((L, W), 0.1),
        'rwkv_k_a': gain((L, W)),
        'rwkv_r_k': nrm((L, RWKV_HEADS, RWKV_HEAD_DIM), 0.1),
        'rwkv_gn_w': gain((L, W)),
        'rwkv_gn_b': nrm((L, W), 0.01),
        'w_proj_a': nrm((L, W, D), W ** -0.5),
        'pool_w': nrm((L, POOL_GROUPS, POOL_GROUP_DIM, POOL_GROUP_DIM), POOL_GROUP_DIM ** -0.5),
        'pool_scale': gain((L, POOL_WIDTH)),
        'w_proj_b': nrm((L, POOL_WIDTH, D), POOL_WIDTH ** -0.5),
        'w_out': nrm((L, D, D), D ** -0.5),
        'ln_ffn2_pre': gain((L, D)),
        'ln_ffn2_post': gain((L, D)),
        'ffn2_gate': nrm((L, D, D_FF), D ** -0.5),
        'ffn2_up': nrm((L, D, D_FF), D ** -0.5),
        'ffn2_down': nrm((L, D_FF, D), D_FF ** -0.5),
    }


def _fwd_reference(x, ln_ffn1_pre, ln_ffn1_post, ffn1_gate, ffn1_up, ffn1_down, ln_mix_pre,
              ln_mix_post, w_in, rwkv_mu, rwkv_w0, rwkv_w2, rwkv_a0, rwkv_a2, rwkv_g2,
              rwkv_k_k, rwkv_k_a, rwkv_r_k, rwkv_gn_w, rwkv_gn_b, w_proj_a, pool_w, pool_scale,
              w_proj_b, w_out, ln_ffn2_pre, ln_ffn2_post, ffn2_gate, ffn2_up, ffn2_down):
    h = x
    for l in range(DEPTH):
        f = _swiglu(_rmsnorm(h, ln_ffn1_pre[l]), ffn1_gate[l], ffn1_up[l], ffn1_down[l])
        h = h + MACARON_WEIGHT * _rmsnorm(f, ln_ffn1_post[l])
        mx = _hybrid_mixer(_rmsnorm(h, ln_mix_pre[l]), w_in[l], rwkv_mu[l], rwkv_w0[l],
                           rwkv_w2[l], rwkv_a0[l], rwkv_a2[l], rwkv_g2[l], rwkv_k_k[l],
                           rwkv_k_a[l], rwkv_r_k[l], rwkv_gn_w[l], rwkv_gn_b[l], w_proj_a[l],
                           pool_w[l], pool_scale[l], w_proj_b[l], w_out[l])
        h = h + _rmsnorm(mx, ln_mix_post[l])
        f = _swiglu(_rmsnorm(h, ln_ffn2_pre[l]), ffn2_gate[l], ffn2_up[l], ffn2_down[l])
        h = h + MACARON_WEIGHT * _rmsnorm(f, ln_ffn2_post[l])
    return h


import jax as _jax
import jax.numpy as _jnp

TWIN_FORMAT = 'train_step'
FWD_PARAMS = ['x', 'ln_ffn1_pre', 'ln_ffn1_post', 'ffn1_gate', 'ffn1_up', 'ffn1_down', 'ln_mix_pre', 'ln_mix_post', 'w_in', 'rwkv_mu', 'rwkv_w0', 'rwkv_w2', 'rwkv_a0', 'rwkv_a2', 'rwkv_g2', 'rwkv_k_k', 'rwkv_k_a', 'rwkv_r_k', 'rwkv_gn_w', 'rwkv_gn_b', 'w_proj_a', 'pool_w', 'pool_scale', 'w_proj_b', 'w_out', 'ln_ffn2_pre', 'ln_ffn2_post', 'ffn2_gate', 'ffn2_up', 'ffn2_down']
TWIN_WEIGHTS = ['ln_ffn1_pre', 'ln_ffn1_post', 'ffn1_gate', 'ffn1_up', 'ffn1_down', 'ln_mix_pre', 'ln_mix_post', 'w_in', 'rwkv_mu', 'rwkv_w0', 'rwkv_w2', 'rwkv_a0', 'rwkv_a2', 'rwkv_g2', 'rwkv_k_k', 'rwkv_k_a', 'rwkv_r_k', 'rwkv_gn_w', 'rwkv_gn_b', 'w_proj_a', 'pool_w', 'pool_scale', 'w_proj_b', 'w_out', 'ln_ffn2_pre', 'ln_ffn2_post', 'ffn2_gate', 'ffn2_up', 'ffn2_down']
TWIN_DIFF_INPUT = 'x'
TWIN_INPUTS = ['x', 'ln_ffn1_pre', 'ln_ffn1_post', 'ffn1_gate', 'ffn1_up', 'ffn1_down', 'ln_mix_pre', 'ln_mix_post', 'w_in', 'rwkv_mu', 'rwkv_w0', 'rwkv_w2', 'rwkv_a0', 'rwkv_a2', 'rwkv_g2', 'rwkv_k_k', 'rwkv_k_a', 'rwkv_r_k', 'rwkv_gn_w', 'rwkv_gn_b', 'w_proj_a', 'pool_w', 'pool_scale', 'w_proj_b', 'w_out', 'ln_ffn2_pre', 'ln_ffn2_post', 'ffn2_gate', 'ffn2_up', 'ffn2_down', 'loss_target', 'm_ln_ffn1_pre', 'm_ln_ffn1_post', 'm_ffn1_gate', 'm_ffn1_up', 'm_ffn1_down', 'm_ln_mix_pre', 'm_ln_mix_post', 'm_w_in', 'm_rwkv_mu', 'm_rwkv_w0', 'm_rwkv_w2', 'm_rwkv_a0', 'm_rwkv_a2', 'm_rwkv_g2', 'm_rwkv_k_k', 'm_rwkv_k_a', 'm_rwkv_r_k', 'm_rwkv_gn_w', 'm_rwkv_gn_b', 'm_w_proj_a', 'm_pool_w', 'm_pool_scale', 'm_w_proj_b', 'm_w_out', 'm_ln_ffn2_pre', 'm_ln_ffn2_post', 'm_ffn2_gate', 'm_ffn2_up', 'm_ffn2_down', 'v_ln_ffn1_pre', 'v_ln_ffn1_post', 'v_ffn1_gate', 'v_ffn1_up', 'v_ffn1_down', 'v_ln_mix_pre', 'v_ln_mix_post', 'v_w_in', 'v_rwkv_mu', 'v_rwkv_w0', 'v_rwkv_w2', 'v_rwkv_a0', 'v_rwkv_a2', 'v_rwkv_g2', 'v_rwkv_k_k', 'v_rwkv_k_a', 'v_rwkv_r_k', 'v_rwkv_gn_w', 'v_rwkv_gn_b', 'v_w_proj_a', 'v_pool_w', 'v_pool_scale', 'v_w_proj_b', 'v_w_out', 'v_ln_ffn2_pre', 'v_ln_ffn2_post', 'v_ffn2_gate', 'v_ffn2_up', 'v_ffn2_down']
TWIN_OUTPUTS = ['loss', 'grad_x', 'grad_ln_ffn1_pre', 'grad_ln_ffn1_post', 'grad_ffn1_gate', 'grad_ffn1_up', 'grad_ffn1_down', 'grad_ln_mix_pre', 'grad_ln_mix_post', 'grad_w_in', 'grad_rwkv_mu', 'grad_rwkv_w0', 'grad_rwkv_w2', 'grad_rwkv_a0', 'grad_rwkv_a2', 'grad_rwkv_g2', 'grad_rwkv_k_k', 'grad_rwkv_k_a', 'grad_rwkv_r_k', 'grad_rwkv_gn_w', 'grad_rwkv_gn_b', 'grad_w_proj_a', 'grad_pool_w', 'grad_pool_scale', 'grad_w_proj_b', 'grad_w_out', 'grad_ln_ffn2_pre', 'grad_ln_ffn2_post', 'grad_ffn2_gate', 'grad_ffn2_up', 'grad_ffn2_down', 'delta_ln_ffn1_pre', 'delta_ln_ffn1_post', 'delta_ffn1_gate', 'delta_ffn1_up', 'delta_ffn1_down', 'delta_ln_mix_pre', 'delta_ln_mix_post', 'delta_w_in', 'delta_rwkv_mu', 'delta_rwkv_w0', 'delta_rwkv_w2', 'delta_rwkv_a0', 'delta_rwkv_a2', 'delta_rwkv_g2', 'delta_rwkv_k_k', 'delta_rwkv_k_a', 'delta_rwkv_r_k', 'delta_rwkv_gn_w', 'delta_rwkv_gn_b', 'delta_w_proj_a', 'delta_pool_w', 'delta_pool_scale', 'delta_w_proj_b', 'delta_w_out', 'delta_ln_ffn2_pre', 'delta_ln_ffn2_post', 'delta_ffn2_gate', 'delta_ffn2_up', 'delta_ffn2_down', 'new_m_ln_ffn1_pre', 'new_m_ln_ffn1_post', 'new_m_ffn1_gate', 'new_m_ffn1_up', 'new_m_ffn1_down', 'new_m_ln_mix_pre', 'new_m_ln_mix_post', 'new_m_w_in', 'new_m_rwkv_mu', 'new_m_rwkv_w0', 'new_m_rwkv_w2', 'new_m_rwkv_a0', 'new_m_rwkv_a2', 'new_m_rwkv_g2', 'new_m_rwkv_k_k', 'new_m_rwkv_k_a', 'new_m_rwkv_r_k', 'new_m_rwkv_gn_w', 'new_m_rwkv_gn_b', 'new_m_w_proj_a', 'new_m_pool_w', 'new_m_pool_scale', 'new_m_w_proj_b', 'new_m_w_out', 'new_m_ln_ffn2_pre', 'new_m_ln_ffn2_post', 'new_m_ffn2_gate', 'new_m_ffn2_up', 'new_m_ffn2_down', 'new_v_ln_ffn1_pre', 'new_v_ln_ffn1_post', 'new_v_ffn1_gate', 'new_v_ffn1_up', 'new_v_ffn1_down', 'new_v_ln_mix_pre', 'new_v_ln_mix_post', 'new_v_w_in', 'new_v_rwkv_mu', 'new_v_rwkv_w0', 'new_v_rwkv_w2', 'new_v_rwkv_a0', 'new_v_rwkv_a2', 'new_v_rwkv_g2', 'new_v_rwkv_k_k', 'new_v_rwkv_k_a', 'new_v_rwkv_r_k', 'new_v_rwkv_gn_w', 'new_v_rwkv_gn_b', 'new_v_w_proj_a', 'new_v_pool_w', 'new_v_pool_scale', 'new_v_w_proj_b', 'new_v_w_out', 'new_v_ln_ffn2_pre', 'new_v_ln_ffn2_post', 'new_v_ffn2_gate', 'new_v_ffn2_up', 'new_v_ffn2_down']
TWIN_LEAF_KINDS = {'loss': 'loss', 'grad_x': 'grad_x', 'grad_ln_ffn1_pre': 'grad_w', 'grad_ln_ffn1_post': 'grad_w', 'grad_ffn1_gate': 'grad_w', 'grad_ffn1_up': 'grad_w', 'grad_ffn1_down': 'grad_w', 'grad_ln_mix_pre': 'grad_w', 'grad_ln_mix_post': 'grad_w', 'grad_w_in': 'grad_w', 'grad_rwkv_mu': 'grad_w', 'grad_rwkv_w0': 'grad_w', 'grad_rwkv_w2': 'grad_w', 'grad_rwkv_a0': 'grad_w', 'grad_rwkv_a2': 'grad_w', 'grad_rwkv_g2': 'grad_w', 'grad_rwkv_k_k': 'grad_w', 'grad_rwkv_k_a': 'grad_w', 'grad_rwkv_r_k': 'grad_w', 'grad_rwkv_gn_w': 'grad_w', 'grad_rwkv_gn_b': 'grad_w', 'grad_w_proj_a': 'grad_w', 'grad_pool_w': 'grad_w', 'grad_pool_scale': 'grad_w', 'grad_w_proj_b': 'grad_w', 'grad_w_out': 'grad_w', 'grad_ln_ffn2_pre': 'grad_w', 'grad_ln_ffn2_post': 'grad_w', 'grad_ffn2_gate': 'grad_w', 'grad_ffn2_up': 'grad_w', 'grad_ffn2_down': 'grad_w', 'delta_ln_ffn1_pre': 'delta_w', 'delta_ln_ffn1_post': 'delta_w', 'delta_ffn1_gate': 'delta_w', 'delta_ffn1_up': 'delta_w', 'delta_ffn1_down': 'delta_w', 'delta_ln_mix_pre': 'delta_w', 'delta_ln_mix_post': 'delta_w', 'delta_w_in': 'delta_w', 'delta_rwkv_mu': 'delta_w', 'delta_rwkv_w0': 'delta_w', 'delta_rwkv_w2': 'delta_w', 'delta_rwkv_a0': 'delta_w', 'delta_rwkv_a2': 'delta_w', 'delta_rwkv_g2': 'delta_w', 'delta_rwkv_k_k': 'delta_w', 'delta_rwkv_k_a': 'delta_w', 'delta_rwkv_r_k': 'delta_w', 'delta_rwkv_gn_w': 'delta_w', 'delta_rwkv_gn_b': 'delta_w', 'delta_w_proj_a': 'delta_w', 'delta_pool_w': 'delta_w', 'delta_pool_scale': 'delta_w', 'delta_w_proj_b': 'delta_w', 'delta_w_out': 'delta_w', 'delta_ln_ffn2_pre': 'delta_w', 'delta_ln_ffn2_post': 'delta_w', 'delta_ffn2_gate': 'delta_w', 'delta_ffn2_up': 'delta_w', 'delta_ffn2_down': 'delta_w', 'new_m_ln_ffn1_pre': 'new_m', 'new_m_ln_ffn1_post': 'new_m', 'new_m_ffn1_gate': 'new_m', 'new_m_ffn1_up': 'new_m', 'new_m_ffn1_down': 'new_m', 'new_m_ln_mix_pre': 'new_m', 'new_m_ln_mix_post': 'new_m', 'new_m_w_in': 'new_m', 'new_m_rwkv_mu': 'new_m', 'new_m_rwkv_w0': 'new_m', 'new_m_rwkv_w2': 'new_m', 'new_m_rwkv_a0': 'new_m', 'new_m_rwkv_a2': 'new_m', 'new_m_rwkv_g2': 'new_m', 'new_m_rwkv_k_k': 'new_m', 'new_m_rwkv_k_a': 'new_m', 'new_m_rwkv_r_k': 'new_m', 'new_m_rwkv_gn_w': 'new_m', 'new_m_rwkv_gn_b': 'new_m', 'new_m_w_proj_a': 'new_m', 'new_m_pool_w': 'new_m', 'new_m_pool_scale': 'new_m', 'new_m_w_proj_b': 'new_m', 'new_m_w_out': 'new_m', 'new_m_ln_ffn2_pre': 'new_m', 'new_m_ln_ffn2_post': 'new_m', 'new_m_ffn2_gate': 'new_m', 'new_m_ffn2_up': 'new_m', 'new_m_ffn2_down': 'new_m', 'new_v_ln_ffn1_pre': 'new_v', 'new_v_ln_ffn1_post': 'new_v', 'new_v_ffn1_gate': 'new_v', 'new_v_ffn1_up': 'new_v', 'new_v_ffn1_down': 'new_v', 'new_v_ln_mix_pre': 'new_v', 'new_v_ln_mix_post': 'new_v', 'new_v_w_in': 'new_v', 'new_v_rwkv_mu': 'new_v', 'new_v_rwkv_w0': 'new_v', 'new_v_rwkv_w2': 'new_v', 'new_v_rwkv_a0': 'new_v', 'new_v_rwkv_a2': 'new_v', 'new_v_rwkv_g2': 'new_v', 'new_v_rwkv_k_k': 'new_v', 'new_v_rwkv_k_a': 'new_v', 'new_v_rwkv_r_k': 'new_v', 'new_v_rwkv_gn_w': 'new_v', 'new_v_rwkv_gn_b': 'new_v', 'new_v_w_proj_a': 'new_v', 'new_v_pool_w': 'new_v', 'new_v_pool_scale': 'new_v', 'new_v_w_proj_b': 'new_v', 'new_v_w_out': 'new_v', 'new_v_ln_ffn2_pre': 'new_v', 'new_v_ln_ffn2_post': 'new_v', 'new_v_ffn2_gate': 'new_v', 'new_v_ffn2_up': 'new_v', 'new_v_ffn2_down': 'new_v'}


def _forward(args):
    return _fwd_reference(*[args[k] for k in FWD_PARAMS])


def _output_shape():
    out = _jax.eval_shape(lambda: _forward(_fwd_setup_inputs(0)))
    return out.shape, out.dtype

N_MICROBATCH = 1
ADAM_LR = 0.001
ADAM_B1 = 0.9
ADAM_B2 = 0.999
ADAM_EPS = 1e-08
ADAM_WD = 0.01
ADAM_STEP = 10
PER_EXAMPLE_BATCH_AXIS = {'x': 0, 'loss_target': 0}
SHARED_INPUTS = []
_WEIGHT_DTYPES = {'ln_ffn1_pre': _jnp.float32, 'ln_ffn1_post': _jnp.float32, 'ffn1_gate': _jnp.float32, 'ffn1_up': _jnp.float32, 'ffn1_down': _jnp.float32, 'ln_mix_pre': _jnp.float32, 'ln_mix_post': _jnp.float32, 'w_in': _jnp.float32, 'rwkv_mu': _jnp.float32, 'rwkv_w0': _jnp.float32, 'rwkv_w2': _jnp.float32, 'rwkv_a0': _jnp.float32, 'rwkv_a2': _jnp.float32, 'rwkv_g2': _jnp.float32, 'rwkv_k_k': _jnp.float32, 'rwkv_k_a': _jnp.float32, 'rwkv_r_k': _jnp.float32, 'rwkv_gn_w': _jnp.float32, 'rwkv_gn_b': _jnp.float32, 'w_proj_a': _jnp.float32, 'pool_w': _jnp.float32, 'pool_scale': _jnp.float32, 'w_proj_b': _jnp.float32, 'w_out': _jnp.float32, 'ln_ffn2_pre': _jnp.float32, 'ln_ffn2_post': _jnp.float32, 'ffn2_gate': _jnp.float32, 'ffn2_up': _jnp.float32, 'ffn2_down': _jnp.float32}
MOMENT_SCALE = {'ln_ffn1_pre': 1.951571e-01, 'ln_ffn1_post': 3.975560e+00, 'ffn1_gate': 8.333831e-02, 'ffn1_up': 8.456571e-02, 'ffn1_down': 1.428800e-01, 'ln_mix_pre': 2.657073e-01, 'ln_mix_post': 1.605952e+01, 'w_in': 1.340757e-01, 'rwkv_mu': 2.147589e-01, 'rwkv_w0': 4.904635e-02, 'rwkv_w2': 6.183912e-03, 'rwkv_a0': 4.900087e-02, 'rwkv_a2': 4.512730e-02, 'rwkv_g2': 1.211233e-01, 'rwkv_k_k': 1.692162e-01, 'rwkv_k_a': 1.334096e-01, 'rwkv_r_k': 2.716469e-01, 'rwkv_gn_w': 1.251691e-01, 'rwkv_gn_b': 1.705978e-01, 'w_proj_a': 9.057416e-02, 'pool_w': 3.227247e-01, 'pool_scale': 3.332712e-01, 'w_proj_b': 2.317988e-01, 'w_out': 2.511301e-01, 'ln_ffn2_pre': 1.334330e-01, 'ln_ffn2_post': 3.986048e+00, 'ffn2_gate': 5.039242e-02, 'ffn2_up': 6.526882e-02, 'ffn2_down': 1.079301e-01}


def _to_microbatches(a, axis):
    t = _jnp.moveaxis(a, axis, 0)
    t = t.reshape((N_MICROBATCH, t.shape[0] // N_MICROBATCH) + t.shape[1:])
    return _jnp.moveaxis(t, 1, axis + 1)


def setup_inputs(seed: int = 0) -> dict:
    inp = _fwd_setup_inputs(seed)
    key = _jax.random.fold_in(_jax.random.key(seed), 7919)
    shape, _ = _output_shape()
    out = dict(inp)
    out["loss_target"] = _jax.random.normal(_jax.random.fold_in(key, 0), shape, _jnp.float32)
    for i, name in enumerate(TWIN_WEIGHTS):
        w = inp[name].astype(_jnp.float32)
        if MOMENT_SCALE is None:
            s = _jnp.sqrt(_jnp.mean(_jnp.square(w)) + 1e-30)
        else:
            s = MOMENT_SCALE[name]
        km, kv = _jax.random.split(_jax.random.fold_in(key, i + 1))
        out[name] = w
        out["m_" + name] = s * _jax.random.normal(km, w.shape, _jnp.float32)
        out["v_" + name] = (s * s) * _jax.random.uniform(kv, w.shape, _jnp.float32, 0.5, 1.5)
    if N_MICROBATCH > 1:
        for name, axis in PER_EXAMPLE_BATCH_AXIS.items():
            out[name] = _to_microbatches(out[name], axis)
    return {'x': out['x'], 'ln_ffn1_pre': out['ln_ffn1_pre'], 'ln_ffn1_post': out['ln_ffn1_post'], 'ffn1_gate': out['ffn1_gate'], 'ffn1_up': out['ffn1_up'], 'ffn1_down': out['ffn1_down'], 'ln_mix_pre': out['ln_mix_pre'], 'ln_mix_post': out['ln_mix_post'], 'w_in': out['w_in'], 'rwkv_mu': out['rwkv_mu'], 'rwkv_w0': out['rwkv_w0'], 'rwkv_w2': out['rwkv_w2'], 'rwkv_a0': out['rwkv_a0'], 'rwkv_a2': out['rwkv_a2'], 'rwkv_g2': out['rwkv_g2'], 'rwkv_k_k': out['rwkv_k_k'], 'rwkv_k_a': out['rwkv_k_a'], 'rwkv_r_k': out['rwkv_r_k'], 'rwkv_gn_w': out['rwkv_gn_w'], 'rwkv_gn_b': out['rwkv_gn_b'], 'w_proj_a': out['w_proj_a'], 'pool_w': out['pool_w'], 'pool_scale': out['pool_scale'], 'w_proj_b': out['w_proj_b'], 'w_out': out['w_out'], 'ln_ffn2_pre': out['ln_ffn2_pre'], 'ln_ffn2_post': out['ln_ffn2_post'], 'ffn2_gate': out['ffn2_gate'], 'ffn2_up': out['ffn2_up'], 'ffn2_down': out['ffn2_down'], 'loss_target': out['loss_target'], 'm_ln_ffn1_pre': out['m_ln_ffn1_pre'], 'm_ln_ffn1_post': out['m_ln_ffn1_post'], 'm_ffn1_gate': out['m_ffn1_gate'], 'm_ffn1_up': out['m_ffn1_up'], 'm_ffn1_down': out['m_ffn1_down'], 'm_ln_mix_pre': out['m_ln_mix_pre'], 'm_ln_mix_post': out['m_ln_mix_post'], 'm_w_in': out['m_w_in'], 'm_rwkv_mu': out['m_rwkv_mu'], 'm_rwkv_w0': out['m_rwkv_w0'], 'm_rwkv_w2': out['m_rwkv_w2'], 'm_rwkv_a0': out['m_rwkv_a0'], 'm_rwkv_a2': out['m_rwkv_a2'], 'm_rwkv_g2': out['m_rwkv_g2'], 'm_rwkv_k_k': out['m_rwkv_k_k'], 'm_rwkv_k_a': out['m_rwkv_k_a'], 'm_rwkv_r_k': out['m_rwkv_r_k'], 'm_rwkv_gn_w': out['m_rwkv_gn_w'], 'm_rwkv_gn_b': out['m_rwkv_gn_b'], 'm_w_proj_a': out['m_w_proj_a'], 'm_pool_w': out['m_pool_w'], 'm_pool_scale': out['m_pool_scale'], 'm_w_proj_b': out['m_w_proj_b'], 'm_w_out': out['m_w_out'], 'm_ln_ffn2_pre': out['m_ln_ffn2_pre'], 'm_ln_ffn2_post': out['m_ln_ffn2_post'], 'm_ffn2_gate': out['m_ffn2_gate'], 'm_ffn2_up': out['m_ffn2_up'], 'm_ffn2_down': out['m_ffn2_down'], 'v_ln_ffn1_pre': out['v_ln_ffn1_pre'], 'v_ln_ffn1_post': out['v_ln_ffn1_post'], 'v_ffn1_gate': out['v_ffn1_gate'], 'v_ffn1_up': out['v_ffn1_up'], 'v_ffn1_down': out['v_ffn1_down'], 'v_ln_mix_pre': out['v_ln_mix_pre'], 'v_ln_mix_post': out['v_ln_mix_post'], 'v_w_in': out['v_w_in'], 'v_rwkv_mu': out['v_rwkv_mu'], 'v_rwkv_w0': out['v_rwkv_w0'], 'v_rwkv_w2': out['v_rwkv_w2'], 'v_rwkv_a0': out['v_rwkv_a0'], 'v_rwkv_a2': out['v_rwkv_a2'], 'v_rwkv_g2': out['v_rwkv_g2'], 'v_rwkv_k_k': out['v_rwkv_k_k'], 'v_rwkv_k_a': out['v_rwkv_k_a'], 'v_rwkv_r_k': out['v_rwkv_r_k'], 'v_rwkv_gn_w': out['v_rwkv_gn_w'], 'v_rwkv_gn_b': out['v_rwkv_gn_b'], 'v_w_proj_a': out['v_w_proj_a'], 'v_pool_w': out['v_pool_w'], 'v_pool_scale': out['v_pool_scale'], 'v_w_proj_b': out['v_w_proj_b'], 'v_w_out': out['v_w_out'], 'v_ln_ffn2_pre': out['v_ln_ffn2_pre'], 'v_ln_ffn2_post': out['v_ln_ffn2_post'], 'v_ffn2_gate': out['v_ffn2_gate'], 'v_ffn2_up': out['v_ffn2_up'], 'v_ffn2_down': out['v_ffn2_down']}


def _loss(weights, diff, rest, loss_target):
    with _jax.named_scope("forward"):
        args = {**rest, TWIN_DIFF_INPUT: diff, **{k: w.astype(_WEIGHT_DTYPES[k]) for k, w in weights.items()}}
        y = _forward(args)
    with _jax.named_scope("loss_head"):
        err = _jnp.square(y.astype(_jnp.float32) - loss_target)
        return 0.5 * _jnp.sum(_jnp.mean(err, axis=-1)) if err.ndim else 0.5 * err


def _adamw(w, g, m, v):
    m = ADAM_B1 * m + (1.0 - ADAM_B1) * g
    v = ADAM_B2 * v + (1.0 - ADAM_B2) * _jnp.square(g)
    m_hat = m / (1.0 - ADAM_B1 ** ADAM_STEP)
    v_hat = v / (1.0 - ADAM_B2 ** ADAM_STEP)
    delta = -ADAM_LR * (m_hat / (_jnp.sqrt(v_hat) + ADAM_EPS) + ADAM_WD * w)
    return delta, m, v


def reference(x, ln_ffn1_pre, ln_ffn1_post, ffn1_gate, ffn1_up, ffn1_down, ln_mix_pre, ln_mix_post, w_in, rwkv_mu, rwkv_w0, rwkv_w2, rwkv_a0, rwkv_a2, rwkv_g2, rwkv_k_k, rwkv_k_a, rwkv_r_k, rwkv_gn_w, rwkv_gn_b, w_proj_a, pool_w, pool_scale, w_proj_b, w_out, ln_ffn2_pre, ln_ffn2_post, ffn2_gate, ffn2_up, ffn2_down, loss_target, m_ln_ffn1_pre, m_ln_ffn1_post, m_ffn1_gate, m_ffn1_up, m_ffn1_down, m_ln_mix_pre, m_ln_mix_post, m_w_in, m_rwkv_mu, m_rwkv_w0, m_rwkv_w2, m_rwkv_a0, m_rwkv_a2, m_rwkv_g2, m_rwkv_k_k, m_rwkv_k_a, m_rwkv_r_k, m_rwkv_gn_w, m_rwkv_gn_b, m_w_proj_a, m_pool_w, m_pool_scale, m_w_proj_b, m_w_out, m_ln_ffn2_pre, m_ln_ffn2_post, m_ffn2_gate, m_ffn2_up, m_ffn2_down, v_ln_ffn1_pre, v_ln_ffn1_post, v_ffn1_gate, v_ffn1_up, v_ffn1_down, v_ln_mix_pre, v_ln_mix_post, v_w_in, v_rwkv_mu, v_rwkv_w0, v_rwkv_w2, v_rwkv_a0, v_rwkv_a2, v_rwkv_g2, v_rwkv_k_k, v_rwkv_k_a, v_rwkv_r_k, v_rwkv_gn_w, v_rwkv_gn_b, v_w_proj_a, v_pool_w, v_pool_scale, v_w_proj_b, v_w_out, v_ln_ffn2_pre, v_ln_ffn2_post, v_ffn2_gate, v_ffn2_up, v_ffn2_down):
    given = dict(x=x, ln_ffn1_pre=ln_ffn1_pre, ln_ffn1_post=ln_ffn1_post, ffn1_gate=ffn1_gate, ffn1_up=ffn1_up, ffn1_down=ffn1_down, ln_mix_pre=ln_mix_pre, ln_mix_post=ln_mix_post, w_in=w_in, rwkv_mu=rwkv_mu, rwkv_w0=rwkv_w0, rwkv_w2=rwkv_w2, rwkv_a0=rwkv_a0, rwkv_a2=rwkv_a2, rwkv_g2=rwkv_g2, rwkv_k_k=rwkv_k_k, rwkv_k_a=rwkv_k_a, rwkv_r_k=rwkv_r_k, rwkv_gn_w=rwkv_gn_w, rwkv_gn_b=rwkv_gn_b, w_proj_a=w_proj_a, pool_w=pool_w, pool_scale=pool_scale, w_proj_b=w_proj_b, w_out=w_out, ln_ffn2_pre=ln_ffn2_pre, ln_ffn2_post=ln_ffn2_post, ffn2_gate=ffn2_gate, ffn2_up=ffn2_up, ffn2_down=ffn2_down, loss_target=loss_target, m_ln_ffn1_pre=m_ln_ffn1_pre, m_ln_ffn1_post=m_ln_ffn1_post, m_ffn1_gate=m_ffn1_gate, m_ffn1_up=m_ffn1_up, m_ffn1_down=m_ffn1_down, m_ln_mix_pre=m_ln_mix_pre, m_ln_mix_post=m_ln_mix_post, m_w_in=m_w_in, m_rwkv_mu=m_rwkv_mu, m_rwkv_w0=m_rwkv_w0, m_rwkv_w2=m_rwkv_w2, m_rwkv_a0=m_rwkv_a0, m_rwkv_a2=m_rwkv_a2, m_rwkv_g2=m_rwkv_g2, m_rwkv_k_k=m_rwkv_k_k, m_rwkv_k_a=m_rwkv_k_a, m_rwkv_r_k=m_rwkv_r_k, m_rwkv_gn_w=m_rwkv_gn_w, m_rwkv_gn_b=m_rwkv_gn_b, m_w_proj_a=m_w_proj_a, m_pool_w=m_pool_w, m_pool_scale=m_pool_scale, m_w_proj_b=m_w_proj_b, m_w_out=m_w_out, m_ln_ffn2_pre=m_ln_ffn2_pre, m_ln_ffn2_post=m_ln_ffn2_post, m_ffn2_gate=m_ffn2_gate, m_ffn2_up=m_ffn2_up, m_ffn2_down=m_ffn2_down, v_ln_ffn1_pre=v_ln_ffn1_pre, v_ln_ffn1_post=v_ln_ffn1_post, v_ffn1_gate=v_ffn1_gate, v_ffn1_up=v_ffn1_up, v_ffn1_down=v_ffn1_down, v_ln_mix_pre=v_ln_mix_pre, v_ln_mix_post=v_ln_mix_post, v_w_in=v_w_in, v_rwkv_mu=v_rwkv_mu, v_rwkv_w0=v_rwkv_w0, v_rwkv_w2=v_rwkv_w2, v_rwkv_a0=v_rwkv_a0, v_rwkv_a2=v_rwkv_a2, v_rwkv_g2=v_rwkv_g2, v_rwkv_k_k=v_rwkv_k_k, v_rwkv_k_a=v_rwkv_k_a, v_rwkv_r_k=v_rwkv_r_k, v_rwkv_gn_w=v_rwkv_gn_w, v_rwkv_gn_b=v_rwkv_gn_b, v_w_proj_a=v_w_proj_a, v_pool_w=v_pool_w, v_pool_scale=v_pool_scale, v_w_proj_b=v_w_proj_b, v_w_out=v_w_out, v_ln_ffn2_pre=v_ln_ffn2_pre, v_ln_ffn2_post=v_ln_ffn2_post, v_ffn2_gate=v_ffn2_gate, v_ffn2_up=v_ffn2_up, v_ffn2_down=v_ffn2_down)
    weights = {n: given[n] for n in TWIN_WEIGHTS}
    shared = {n: given[n] for n in SHARED_INPUTS}
    per_example = {n: given[n] for n in ['x']}
    grad_fn = _jax.value_and_grad(_loss, argnums=(0, 1))

    def one_microbatch(ex, loss_target):
        ex = dict(ex)
        diff = ex.pop(TWIN_DIFF_INPUT)
        return grad_fn(weights, diff, {**shared, **ex}, loss_target)

    if N_MICROBATCH == 1:
        loss, (grad_w, grad_x) = one_microbatch(per_example, given["loss_target"])
    else:
        def body(carry, xs):
            loss_sum, grad_sum = carry
            l_k, (gw_k, gx_k) = one_microbatch(xs[0], xs[1])
            with _jax.named_scope("update"):
                return (loss_sum + l_k, _jax.tree.map(_jnp.add, grad_sum, gw_k)), gx_k

        init = (_jnp.zeros((), _jnp.float32), _jax.tree.map(_jnp.zeros_like, weights))
        (loss, grad_w), grad_x = _jax.lax.scan(body, init, (per_example, given["loss_target"]))
    with _jax.named_scope("update"):
        delta_w, new_m, new_v = {}, {}, {}
        for n in TWIN_WEIGHTS:
            delta_w[n], new_m[n], new_v[n] = _adamw(weights[n], grad_w[n], given["m_" + n], given["v_" + n])
    return (loss, grad_x, *[grad_w[n] for n in TWIN_WEIGHTS], *[delta_w[n] for n in TWIN_WEIGHTS],
            *[new_m[n] for n in TWIN_WEIGHTS], *[new_v[n] for n in TWIN_WEIGHTS])
```

```python
import jax
import jax.numpy as jnp
from jax import lax
from jax.experimental import pallas as pl
from jax.experimental.pallas import tpu as pltpu

F32, BF16 = jnp.float32, jnp.bfloat16
N_DEV = 8
N_CHIP = 4
HEAD = 64
LANES = 2 * HEAD
NORM_EPS, GN_EPS, L2_EPS = 1e-6, 64e-5, 1e-12
POOL_WINDOWS = (2, 4, 8, 16)
POOL_HALO = 16
MACARON = 0.5
ADAM_LR, ADAM_B1, ADAM_B2, ADAM_EPS, ADAM_WD, ADAM_STEP = 0.001, 0.9, 0.999, 1e-08, 0.01, 10
VMEM_LIMIT = 48 * 1024 * 1024
ROW_TILE = 256
RWKV_ROW_TILE = 128
LAT_ALIGN = 512
WKV_CHUNK, WKV_PAIRS = 32, 4
MESH = pl.DeviceIdType.MESH


def _pallas(body, **kw):
    return pl.pallas_call(body, **kw)


def _params(*sem):
    return pltpu.CompilerParams(dimension_semantics=sem, vmem_limit_bytes=VMEM_LIMIT)


def _tile(n, target, align=128):
    best = None
    for d in range(align, min(n, target) + 1, align):
        if n % d == 0:
            best = d
    return best if best is not None else n


def _round_up(n, m):
    return (n + m - 1) // m * m


def _mm(name, a_list, b_list, mode, out_dtypes, *, sum_pairs=False, extras=(), epilogue=None,
        tm=1024, tn=512, tk=1024):
    n_pairs = len(a_list)
    a0, b0 = a_list[0], b_list[0]
    if mode == "nn":
        (M, K), N = a0.shape, b0.shape[1]
    elif mode == "nt":
        (M, K), N = a0.shape, b0.shape[0]
    else:
        (K, M), N = a0.shape, b0.shape[1]
    tm, tn, tk = _tile(M, tm), _tile(N, tn), _tile(K, tk)
    nk = K // tk
    n_acc = 1 if sum_pairs else n_pairs
    n_ex = len(extras)
    if mode == "tn":
        a_spec = pl.BlockSpec((tk, tm), lambda i, j, k: (k, i))
    else:
        a_spec = pl.BlockSpec((tm, tk), lambda i, j, k: (i, k))
    if mode == "nt":
        b_spec = pl.BlockSpec((tn, tk), lambda i, j, k: (j, k))
    else:
        b_spec = pl.BlockSpec((tk, tn), lambda i, j, k: (k, j))
    contract = {"nn": ((1,), (0,)), "nt": ((1,), (1,)), "tn": ((0,), (0,))}[mode]
    e_specs = []
    for _, col in extras:
        assert col % tn == 0
        e_specs.append(pl.BlockSpec((tm, tn), lambda i, j, k, off=col // tn: (i, j + off)))
    o_spec = pl.BlockSpec((tm, tn), lambda i, j, k: (i, j))

    def body(*refs):
        a_refs, b_refs = refs[:n_pairs], refs[n_pairs:2 * n_pairs]
        e_refs = refs[2 * n_pairs:2 * n_pairs + n_ex]
        o_refs = refs[2 * n_pairs + n_ex:len(refs) - n_acc]
        acc_refs = refs[len(refs) - n_acc:]
        k = pl.program_id(2)

        @pl.when(k == 0)
        def _():
            for acc in acc_refs:
                acc[...] = jnp.zeros_like(acc)

        for p in range(n_pairs):
            acc = acc_refs[0 if sum_pairs else p]
            acc[...] += lax.dot_general(a_refs[p][...], b_refs[p][...], (contract, ((), ())),
                                        preferred_element_type=F32)

        @pl.when(k == nk - 1)
        def _():
            accs = [acc[...] for acc in acc_refs]
            outs = epilogue(accs, [e[...] for e in e_refs]) if epilogue else accs
            for o_ref, o in zip(o_refs, outs):
                o_ref[...] = o.astype(o_ref.dtype)

    return _pallas(
        body, name=name, grid=(M // tm, N // tn, nk),
        in_specs=[a_spec] * n_pairs + [b_spec] * n_pairs + e_specs,
        out_specs=[o_spec] * len(out_dtypes),
        out_shape=[jax.ShapeDtypeStruct((M, N), dt) for dt in out_dtypes],
        scratch_shapes=[pltpu.VMEM((tm, tn), F32)] * n_acc,
        compiler_params=_params("parallel", "parallel", "arbitrary"),
    )(*a_list, *b_list, *[e for e, _ in extras])


def _swiglu_fwd_epi(accs, _):
    g, u = accs
    return [g, u, g * jax.nn.sigmoid(g) * u]


def _swiglu_bwd_epi(accs, ex):
    dact = accs[0]
    g, u = ex[0].astype(F32), ex[1].astype(F32)
    sg = jax.nn.sigmoid(g)
    return [dact * u * (sg * (1.0 + g * (1.0 - sg))), dact * (g * sg)]


def _mix_fwd_epi(accs, ex):
    ya, yb = accs
    return [jax.nn.sigmoid(ex[0]) * ya + jax.nn.sigmoid(ex[1]) * yb, ya, yb]


def _mix_bwd_epi(accs, ex):
    dm = accs[0]
    sa, sb = jax.nn.sigmoid(ex[0]), jax.nn.sigmoid(ex[1])
    ya, yb = ex[2].astype(F32), ex[3].astype(F32)
    return [dm * sa, dm * sb, dm * ya * sa * (1.0 - sa), dm * yb * sb * (1.0 - sb)]


def _row_call(name, body, T, tiled, params, outs, accs=(), prev=(), nxt=(), halo=8, tile=ROW_TILE):
    tm = min(tile, T)
    n_tiles = T // tm

    def norm(e):
        return e if isinstance(e, tuple) else (e, e.shape[1], 0)

    tiled, prev, nxt = [norm(e) for e in tiled], [norm(e) for e in prev], [norm(e) for e in nxt]
    per_halo, n_halo = tm // halo, T // halo
    in_specs = [pl.BlockSpec((tm, w), lambda i, cb=cb: (i, cb)) for _, w, cb in tiled]
    in_specs += [pl.BlockSpec((halo, w), lambda i, cb=cb: (jnp.maximum(i * per_halo - 1, 0), cb)) for _, w, cb in prev]
    in_specs += [pl.BlockSpec((halo, w), lambda i, cb=cb: (jnp.minimum((i + 1) * per_halo, n_halo - 1), cb))
                 for _, w, cb in nxt]
    in_specs += [pl.BlockSpec(p.shape, lambda i, nd=p.ndim: (0,) * nd) for p in params]
    out_specs = [pl.BlockSpec((tm, o.shape[1]), lambda i: (i, 0)) for o in outs]
    out_specs += [pl.BlockSpec(a.shape, lambda i, nd=len(a.shape): (0,) * nd) for a in accs]
    n1, n2, n3, n4, n5 = len(tiled), len(prev), len(nxt), len(params), len(outs)

    def kernel_body(*refs):
        i = pl.program_id(0)
        acc_refs = refs[n1 + n2 + n3 + n4 + n5:]

        @pl.when(i == 0)
        def _():
            for a in acc_refs:
                a[...] = jnp.zeros_like(a)

        body(i, n_tiles, refs[:n1], refs[n1:n1 + n2], refs[n1 + n2:n1 + n2 + n3],
             refs[n1 + n2 + n3:n1 + n2 + n3 + n4], refs[n1 + n2 + n3 + n4:n1 + n2 + n3 + n4 + n5], acc_refs)

    return _pallas(
        kernel_body, name=name, grid=(n_tiles,), in_specs=in_specs, out_specs=out_specs,
        out_shape=list(outs) + list(accs),
        compiler_params=_params("arbitrary"),
    )(*[e[0] for e in tiled + prev + nxt], *params)


def _sds(shape, dtype=F32):
    return jax.ShapeDtypeStruct(tuple(shape), dtype)


def _rstd(x):
    return lax.rsqrt(jnp.mean(x * x, axis=-1, keepdims=True) + NORM_EPS)


def _colsum(x):
    return jnp.sum(x, axis=0, keepdims=True)


def rms_pre(x, g):
    T, D = x.shape

    def body(i, n, tiled, prev, nxt, params, outs, accs):
        xv = tiled[0][...]
        outs[0][...] = (xv * _rstd(xv) * params[0][...]).astype(BF16)

    return _row_call("rms_pre", body, T, [x], [g], [_sds((T, D), BF16)])[0]


def post_pre(h, f, g_post, g_pre, scale):
    T, D = h.shape

    def body(i, n, tiled, prev, nxt, params, outs, accs):
        hv, fv = tiled[0][...], tiled[1][...]
        h2 = hv + scale * (fv * _rstd(fv) * params[0][...])
        outs[0][...] = h2
        outs[1][...] = (h2 * _rstd(h2) * params[1][...]).astype(BF16)

    return _row_call("post_pre", body, T, [h, f], [g_post, g_pre], [_sds((T, D)), _sds((T, D), BF16)])


def _post_bwd_math(dh, fv, g, scale):
    r = _rstd(fv)
    fhat = fv * r
    dy = scale * dh
    z = dy * g
    df = r * (z - fhat * jnp.mean(z * fhat, axis=-1, keepdims=True))
    return df, _colsum(dy * fhat)


def loss_post_bwd(h, f, g_post, target, scale):
    T, D = h.shape

    def body(i, n, tiled, prev, nxt, params, outs, accs):
        hv, fv, tv = tiled[0][...], tiled[1][...], tiled[2][...]
        g = params[0][...]
        e = hv + scale * (fv * _rstd(fv) * g) - tv
        accs[0][...] += jnp.full(accs[0].shape, 0.5 / D, F32) * jnp.sum(e * e)
        dh = e * (1.0 / D)
        outs[0][...] = dh
        df, dg = _post_bwd_math(dh, fv, g, scale)
        outs[1][...] = df.astype(BF16)
        accs[1][...] += dg

    return _row_call("loss_post_bwd", body, T, [h, f, target], [g_post],
                     [_sds((T, D)), _sds((T, D), BF16)], [_sds((1, LANES)), _sds((1, D))])


def post_bwd(dh, f, g_post, scale):
    T, D = dh.shape

    def body(i, n, tiled, prev, nxt, params, outs, accs):
        df, dg = _post_bwd_math(tiled[0][...], tiled[1][...], params[0][...], scale)
        outs[0][...] = df.astype(BF16)
        accs[0][...] += dg

    return _row_call("post_bwd", body, T, [dh, f], [g_post], [_sds((T, D), BF16)], [_sds((1, D))])


def pre_bwd(dn, h, g_pre, dres):
    T, D = h.shape

    def body(i, n, tiled, prev, nxt, params, outs, accs):
        dnv, hv = tiled[0][...], tiled[1][...]
        r = _rstd(hv)
        hhat = hv * r
        z = dnv * params[0][...]
        outs[0][...] = tiled[2][...] + r * (z - hhat * jnp.mean(z * hhat, axis=-1, keepdims=True))
        accs[0][...] += _colsum(dnv * hhat)

    return _row_call("pre_bwd", body, T, [dn, h, dres], [g_pre], [_sds((T, D))], [_sds((1, D))])


def _head_ones():
    i = lax.broadcasted_iota(jnp.int32, (LANES, LANES), 0)
    j = lax.broadcasted_iota(jnp.int32, (LANES, LANES), 1)
    return jnp.where((i < HEAD) == (j < HEAD), 1.0, 0.0).astype(F32)


def _headsum(x):
    e = _head_ones()
    parts = [jnp.dot(x[:, s:s + LANES], e, precision=lax.Precision.HIGHEST, preferred_element_type=F32)
             for s in range(0, x.shape[1], LANES)]
    return parts[0] if len(parts) == 1 else jnp.concatenate(parts, axis=1)


def _shift_down(x, before):
    row = lax.broadcasted_iota(jnp.int32, x.shape, 0)
    return jnp.where(row == 0, before, pltpu.roll(x, 1, 0))


def _shift_up(x, after):
    row = lax.broadcasted_iota(jnp.int32, x.shape, 0)
    return jnp.where(row == x.shape[0] - 1, after, pltpu.roll(x, x.shape[0] - 1, 0))


def _last_row(ref, keep):
    r = ref[ref.shape[0] - 1:ref.shape[0], :]
    return jnp.where(keep, r, jnp.zeros_like(r))


def _first_row(ref, keep):
    r = ref[0:1, :]
    return jnp.where(keep, r, jnp.zeros_like(r))


def _softplus(u):
    return jnp.maximum(u, 0.0) + jnp.log(1.0 + jnp.exp(-jnp.abs(u)))


def _dotb(a, b, contract):
    return lax.dot_general(a.astype(BF16), b.astype(BF16), (contract, ((), ())), preferred_element_type=F32)


_NN, _NT, _TN = ((1,), (0,)), ((1,), (1,)), ((0,), (0,))


def _prep_forward(z, zprev_row, zl, zlprev_row, mu, mul, w0, a0, kk_w, ka_w, w2p, a2p, g2p):
    W = w0.shape[1]
    zs = z + (_shift_down(z, zprev_row) - z) * mu
    zls = zl + (_shift_down(zl, zlprev_row) - zl) * mul
    r, k, v = zs[:, :W], zs[:, W:2 * W], zs[:, 2 * W:]
    th, sg = jnp.tanh(zls), jax.nn.sigmoid(zls)
    xw = w0 + _dotb(th, w2p, _NN)
    wlog = -_softplus(-xw) - 0.5
    ew = jnp.exp(wlog)
    decay = jnp.exp(-ew)
    a = jax.nn.sigmoid(a0 + _dotb(zls, a2p, _NN))
    gate = _dotb(sg, g2p, _NN)
    q = k * kk_w
    nrm = jnp.sqrt(_headsum(q * q))
    den = jnp.maximum(nrm, L2_EPS)
    kk = q / den
    kmod = k * (1.0 + (a - 1.0) * ka_w)
    return dict(zs=zs, zls=zls, r=r, k=k, v=v, th=th, sg=sg, xw=xw, ew=ew, decay=decay, a=a, gate=gate,
                nrm=nrm, den=den, kk=kk, kmod=kmod)


def rwkv_prep(p, cols, mu, mul, w0, a0, kk_w, ka_w, w2p, a2p, g2p):
    T = p.shape[0]
    W = w0.shape[1]

    def body(i, n, tiled, prev, nxt, params, outs, accs):
        c = _prep_forward(tiled[0][...], _last_row(prev[0], i > 0), tiled[1][...], _last_row(prev[1], i > 0),
                          *[q[...] for q in params])
        for o, val in zip(outs, (c["r"], c["decay"], c["kmod"], c["v"], -c["kk"], c["kk"] * c["a"], c["gate"])):
            o[...] = val

    return _row_call("rwkv_prep", body, T, [cols["rkv"], cols["lat"]],
                     [mu, mul, w0, a0, kk_w, ka_w, w2p, a2p, g2p], [_sds((T, W))] * 7,
                     prev=[cols["rkv"], cols["lat"]], tile=RWKV_ROW_TILE)


def _post_forward(y, r, kmod, v, gn_w, gn_b, rk):
    mean = _headsum(y) * (1.0 / HEAD)
    yc = y - mean
    rstd = lax.rsqrt(_headsum(yc * yc) * (1.0 / HEAD) + GN_EPS)
    yn = yc * rstd
    s = _headsum(r * kmod * rk)
    return yn, rstd, s, yn * gn_w + gn_b + s * v


def rwkv_post(y, r, kmod, v, gate, gn_w, gn_b, rk):
    T, W = y.shape

    def body(i, n, tiled, prev, nxt, params, outs, accs):
        yv, rv, kv, vv, gv = [t[...] for t in tiled]
        _, _, _, o = _post_forward(yv, rv, kv, vv, *[q[...] for q in params])
        outs[0][...] = (o * gv).astype(BF16)

    return _row_call("rwkv_post", body, T, [y, r, kmod, v, gate], [gn_w, gn_b, rk], [_sds((T, W), BF16)],
                     tile=RWKV_ROW_TILE)[0]


def rwkv_post_bwd(dout, y, r, kmod, v, gate, gn_w, gn_b, rk):
    T, W = y.shape

    def body(i, n, tiled, prev, nxt, params, outs, accs):
        dv_, yv, rv, kv, vv, gv = [t[...] for t in tiled]
        gn_w_, gn_b_, rk_ = [q[...] for q in params]
        yn, rstd, s, o = _post_forward(yv, rv, kv, vv, gn_w_, gn_b_, rk_)
        do = dv_ * gv
        outs[4][...] = dv_ * o
        accs[0][...] += _colsum(do * yn)
        accs[1][...] += _colsum(do)
        dyn = do * gn_w_
        outs[0][...] = rstd * (dyn - _headsum(dyn) * (1.0 / HEAD) - yn * (_headsum(dyn * yn) * (1.0 / HEAD)))
        ds = _headsum(do * vv)
        outs[1][...] = ds * kv * rk_
        outs[2][...] = ds * rv * rk_
        outs[3][...] = do * s
        accs[2][...] += _colsum(ds * rv * kv)

    return _row_call("rwkv_post_bwd", body, T, [dout, y, r, kmod, v, gate], [gn_w, gn_b, rk],
                     [_sds((T, W))] * 5, [_sds((1, W))] * 3, tile=RWKV_ROW_TILE)


def rwkv_prep_bwd(p, cols, grads, mu, mul, w0, a0, kk_w, ka_w, w2p, a2p, g2p):
    T = p.shape[0]
    W = w0.shape[1]
    latp = w2p.shape[0]

    def body(i, n, tiled, prev, nxt, params, outs, accs):
        pv = [q[...] for q in params]
        mu_, mul_, w0_, a0_, kk_w_, ka_w_, w2p_, a2p_, g2p_ = pv
        c = _prep_forward(tiled[0][...], _last_row(prev[0], i > 0), tiled[1][...], _last_row(prev[1], i > 0), *pv)
        dr_s, dr_x, ddecay, dk_s, dk_x, dv_s, dv_x, dneg, db, dgate = [t[...] for t in tiled[2:]]
        k, a, kk = c["k"], c["a"], c["kk"]
        dkmod = dk_s + dk_x
        dk = dkmod * (1.0 + (a - 1.0) * ka_w_)
        da = dkmod * k * ka_w_ + db * kk
        accs[0][...] += _colsum(dkmod * k * (a - 1.0))
        dkk = db * a - dneg
        dq = jnp.where(c["nrm"] > L2_EPS, dkk - kk * _headsum(dkk * kk), dkk) / c["den"]
        dk = dk + dq * kk_w_
        accs[1][...] += _colsum(dq * k)
        dxa = da * a * (1.0 - a)
        accs[2][...] += _colsum(dxa)
        accs[4][...] += _dotb(c["zls"], dxa, _TN)
        dzls = _dotb(dxa, a2p_, _NT)
        dxw = (-ddecay * c["decay"] * c["ew"]) * jax.nn.sigmoid(-c["xw"])
        accs[3][...] += _colsum(dxw)
        accs[5][...] += _dotb(c["th"], dxw, _TN)
        dzls = dzls + _dotb(dxw, w2p_, _NT) * (1.0 - c["th"] * c["th"])
        accs[6][...] += _dotb(c["sg"], dgate, _TN)
        dzls = dzls + _dotb(dgate, g2p_, _NT) * c["sg"] * (1.0 - c["sg"])
        outs[0][...] = jnp.concatenate([dr_s + dr_x, dk, dv_s + dv_x], axis=1)
        outs[1][...] = dzls

    return _row_call("rwkv_prep_bwd", body, T, [cols["rkv"], cols["lat"]] + list(grads),
                     [mu, mul, w0, a0, kk_w, ka_w, w2p, a2p, g2p], [_sds((T, 3 * W)), _sds((T, latp))],
                     [_sds((1, W))] * 4 + [_sds((latp, W))] * 3, prev=[cols["rkv"], cols["lat"]], tile=RWKV_ROW_TILE)


def shift_bwd(cols, dzs, dzls, mu, mul):
    T = dzs.shape[0]

    def body(i, n, tiled, prev, nxt, params, outs, accs):
        for j in range(2):
            z, d, m = tiled[j][...], tiled[2 + j][...], params[j][...]
            zprev = _shift_down(z, _last_row(prev[j], i > 0))
            dnext = _shift_up(d, _first_row(nxt[j], i < n - 1))
            outs[j][...] = (d * (1.0 - m) + dnext * m).astype(BF16)
            accs[j][...] += _colsum(d * (zprev - z))

    return _row_call("shift_bwd", body, T, [cols["rkv"], cols["lat"], dzs, dzls], [mu, mul],
                     [_sds(dzs.shape, BF16), _sds(dzls.shape, BF16)], [_sds(mu.shape), _sds(mul.shape)],
                     prev=[cols["rkv"], cols["lat"]], nxt=[dzs, dzls])


def _window_pick(x, windows):
    gid = lax.broadcasted_iota(jnp.int32, x.shape, 1) // (x.shape[1] // len(windows))
    out = windows[-1]
    for g in range(len(windows) - 2, -1, -1):
        out = jnp.where(gid == g, windows[g], out)
    return out


def _pool_counts(t0, rows, width):
    t = (t0 + lax.broadcasted_iota(jnp.int32, (rows, width), 0) + 1).astype(F32)
    return _window_pick(t, [jnp.minimum(t, float(w)) for w in POOL_WINDOWS])


def _pool_mixed(x, before, t0):
    tm, width = x.shape
    xe = jnp.concatenate([before, x], axis=0)
    sums, s, span = [], xe, 1
    for w in POOL_WINDOWS:
        while span < w:
            s = s + pltpu.roll(s, span, 0)
            span *= 2
        sums.append(s[POOL_HALO:, :])
    return _window_pick(x, sums) / _pool_counts(t0, tm, width) - x


def _group_dot(x, w_ref, contract):
    gd = w_ref.shape[-1]
    parts = [_dotb(x[:, g * gd:(g + 1) * gd], w_ref[g], contract) for g in range(w_ref.shape[0])]
    return jnp.concatenate(parts, axis=1)


def pool_fwd(cols, pool_w, pool_scale):
    T, width = cols["pool"][0].shape[0], cols["pool"][1]
    tm = min(ROW_TILE, T)

    def body(i, n, tiled, prev, nxt, params, outs, accs):
        before = jnp.where(i > 0, prev[0][...], 0.0)
        mixed = _pool_mixed(tiled[0][...], before, i * tm)
        outs[0][...] = (_group_dot(mixed, params[0], _NN) * params[1][...]).astype(BF16)

    return _row_call("pool_fwd", body, T, [cols["pool"]], [pool_w, pool_scale], [_sds((T, width), BF16)],
                     prev=[cols["pool"]], halo=POOL_HALO)[0]


def pool_bwd(cols, dout, pool_w, pool_scale):
    T, width = dout.shape
    tm = min(ROW_TILE, T)

    def body(i, n, tiled, prev, nxt, params, outs, accs):
        w_ref, scale = params[0], params[1][...]
        before = jnp.where(i > 0, prev[0][...], 0.0)
        mixed = _pool_mixed(tiled[0][...], before, i * tm)
        dv = tiled[1][...]
        accs[1][...] += _colsum(dv * _group_dot(mixed, w_ref, _NN))
        after = jnp.where(i < n - 1, nxt[0][...], 0.0)
        dys = jnp.concatenate([dv, after], axis=0) * scale
        gd = w_ref.shape[-1]
        for g in range(w_ref.shape[0]):
            accs[0][g] += _dotb(mixed[:, g * gd:(g + 1) * gd], dys[:tm, g * gd:(g + 1) * gd], _TN)
        dmixed = _group_dot(dys, w_ref, _NT)
        u = dmixed / _pool_counts(i * tm, tm + POOL_HALO, width)
        rows = tm + POOL_HALO
        sums, s, span = [], u, 1
        for w in POOL_WINDOWS:
            while span < w:
                s = s + pltpu.roll(s, rows - span, 0)
                span *= 2
            sums.append(s[:tm, :])
        outs[0][...] = (_window_pick(dv, sums) - dmixed[:tm, :]).astype(BF16)

    return _row_call("pool_bwd", body, T, [cols["pool"], dout], [pool_w, pool_scale], [_sds((T, width), BF16)],
                     [_sds(pool_w.shape), _sds((1, width))], prev=[cols["pool"]], nxt=[dout], halo=POOL_HALO)


def _wkv_masks():
    lane = lax.broadcasted_iota(jnp.int32, (HEAD, LANES), 1)
    sub = lax.broadcasted_iota(jnp.int32, (HEAD, LANES), 0)
    return lane < HEAD, jnp.where((lane & (HEAD - 1)) == sub, 1.0, 0.0).astype(F32)


def _segsum(p, in_a):
    sa = jnp.sum(jnp.where(in_a, p, 0.0), axis=1, keepdims=True)
    sb = jnp.sum(jnp.where(in_a, 0.0, p), axis=1, keepdims=True)
    return jnp.where(in_a, sa, sb)


def _lane_cat(parts):
    return parts[0] if len(parts) == 1 else jnp.concatenate(parts, axis=1)


def wkv_fwd(r, w, k, v, a, b):
    T, W = r.shape
    P = W // LANES
    PB = min(WKV_PAIRS, P)
    chunk = min(WKV_CHUNK, T)
    NC = T // chunk

    def body(r_ref, w_ref, k_ref, v_ref, a_ref, b_ref, y_ref, ck_ref, s_ref):
        c = pl.program_id(1)

        @pl.when(c == 0)
        def _():
            s_ref[...] = jnp.zeros_like(s_ref)

        ck_ref[0] = s_ref[...]
        in_a, diag = _wkv_masks()

        def step(t, states):
            new, ys = [], []
            rows = [ref[pl.ds(t, 1), :] for ref in (r_ref, w_ref, k_ref, v_ref, a_ref, b_ref)]
            for p in range(PB):
                rt, wt, kt, vt, at, bt = [x[:, p * LANES:(p + 1) * LANES] for x in rows]
                S = states[p]
                vcol = _segsum(vt * diag, in_a)
                sa = _segsum(S * at, in_a)
                S = S * wt + sa * bt + vcol * kt
                ys.append(_colsum(_segsum(S * rt, in_a) * diag))
                new.append(S)
            y_ref[pl.ds(t, 1), :] = _lane_cat(ys)
            return tuple(new)

        fin = lax.fori_loop(0, chunk, step, tuple(s_ref[p] for p in range(PB)))
        for p in range(PB):
            s_ref[p] = fin[p]

    spec = pl.BlockSpec((chunk, PB * LANES), lambda g, c: (c, g))
    return _pallas(
        body, name="wkv_fwd", grid=(P // PB, NC), in_specs=[spec] * 6,
        out_specs=[spec, pl.BlockSpec((1, PB, HEAD, LANES), lambda g, c: (c, g, 0, 0))],
        out_shape=[_sds((T, W)), _sds((NC, P, HEAD, LANES))],
        scratch_shapes=[pltpu.VMEM((PB, HEAD, LANES), F32)],
        compiler_params=_params("parallel", "arbitrary"),
    )(r, w, k, v, a, b)


def wkv_bwd(r, w, k, v, a, b, dy, ckpt):
    T, W = r.shape
    P = W // LANES
    PB = min(WKV_PAIRS, P)
    chunk = min(WKV_CHUNK, T)
    NC = T // chunk

    def body(r_ref, w_ref, k_ref, v_ref, a_ref, b_ref, dy_ref, ck_ref,
             dr_ref, dw_ref, dk_ref, dv_ref, da_ref, db_ref, ds_ref, st_ref, sa_ref):
        c = pl.program_id(1)

        @pl.when(c == 0)
        def _():
            ds_ref[...] = jnp.zeros_like(ds_ref)

        in_a, diag = _wkv_masks()

        def fstep(t, states):
            new = []
            rows = [ref[pl.ds(t, 1), :] for ref in (w_ref, k_ref, v_ref, a_ref, b_ref)]
            for p in range(PB):
                wt, kt, vt, at, bt = [x[:, p * LANES:(p + 1) * LANES] for x in rows]
                S = states[p]
                st_ref[t, p] = S
                sa = _segsum(S * at, in_a)
                sa_ref[t, p] = sa
                new.append(S * wt + sa * bt + _segsum(vt * diag, in_a) * kt)
            return tuple(new)

        fin = lax.fori_loop(0, chunk, fstep, tuple(ck_ref[0, p] for p in range(PB)))
        for p in range(PB):
            st_ref[chunk, p] = fin[p]

        def bstep(n, dstates):
            t = chunk - 1 - n
            new = []
            outs = [[] for _ in range(6)]
            rows = [ref[pl.ds(t, 1), :] for ref in (r_ref, w_ref, k_ref, v_ref, a_ref, b_ref, dy_ref)]
            for p in range(PB):
                rt, wt, kt, vt, at, bt, dyt = [x[:, p * LANES:(p + 1) * LANES] for x in rows]
                vcol = _segsum(vt * diag, in_a)
                dycol = _segsum(dyt * diag, in_a)
                sp, sn, sa = st_ref[t, p], st_ref[t + 1, p], sa_ref[t, p]
                dS = dstates[p] + dycol * rt
                dsa = _segsum(dS * bt, in_a)
                outs[0].append(_colsum(sn * dycol))
                outs[1].append(_colsum(dS * sp))
                outs[2].append(_colsum(dS * vcol))
                outs[3].append(_colsum(_segsum(dS * kt, in_a) * diag))
                outs[4].append(_colsum(sp * dsa))
                outs[5].append(_colsum(dS * sa))
                new.append(dS * wt + dsa * at)
            for ref, parts in zip((dr_ref, dw_ref, dk_ref, dv_ref, da_ref, db_ref), outs):
                ref[pl.ds(t, 1), :] = _lane_cat(parts)
            return tuple(new)

        dfin = lax.fori_loop(0, chunk, bstep, tuple(ds_ref[p] for p in range(PB)))
        for p in range(PB):
            ds_ref[p] = dfin[p]

    spec = pl.BlockSpec((chunk, PB * LANES), lambda g, c: (NC - 1 - c, g))
    return _pallas(
        body, name="wkv_bwd", grid=(P // PB, NC),
        in_specs=[spec] * 7 + [pl.BlockSpec((1, PB, HEAD, LANES), lambda g, c: (NC - 1 - c, g, 0, 0))],
        out_specs=[spec] * 6, out_shape=[_sds((T, W))] * 6,
        scratch_shapes=[pltpu.VMEM((PB, HEAD, LANES), F32), pltpu.VMEM((chunk + 1, PB, HEAD, LANES), F32),
                        pltpu.VMEM((chunk, PB, HEAD, LANES), F32)],
        compiler_params=_params("parallel", "arbitrary"),
    )(r, w, k, v, a, b, dy, ckpt)


def _position():
    return lax.axis_index("x"), lax.axis_index("y"), lax.axis_index("c")


def _other_chips(x, y):
    return [(1 - x, y), (x, 1 - y), (1 - x, 1 - y)]


ANY = pl.BlockSpec(memory_space=pl.ANY)


def all_gather(shards):
    n = len(shards)

    def body(*refs):
        x_refs, out_refs = refs[:n], refs[n:2 * n]
        send_sems, recv_sems, local_sems = refs[2 * n:]
        x, y, c = _position()
        me, sibling = (x, y, c), (x, y, 1 - c)
        chips = _other_chips(x, y)

        def slot(ref, pos):
            return ref.at[4 * pos[0] + 2 * pos[1] + pos[2]]

        def copy(t, j, block, to, src=None):
            dst = slot(out_refs[t], block)
            return pltpu.make_async_remote_copy(
                src_ref=dst if src is None else src, dst_ref=dst, send_sem=send_sems.at[t, j],
                recv_sem=recv_sems.at[t, j], device_id=to, device_id_type=MESH)

        mine = [pltpu.make_async_copy(x_refs[t], slot(out_refs[t], me), local_sems.at[t]) for t in range(n)]
        first, passed = [], []
        for t in range(n):
            mine[t].start()
            first.append(copy(t, 0, me, sibling, src=x_refs[t]))
            first += [copy(t, 1 + j, me, (*chip, c), src=x_refs[t]) for j, chip in enumerate(chips)]
        for cp in first:
            cp.start()
        for t in range(n):
            for j, chip in enumerate(chips):
                copy(t, 1 + j, (*chip, c), me).wait_recv()
                fwd = copy(t, 4 + j, (*chip, c), sibling)
                fwd.start()
                passed.append(fwd)
        for t in range(n):
            copy(t, 0, sibling, me).wait_recv()
            for j, chip in enumerate(chips):
                copy(t, 4 + j, (*chip, 1 - c), me).wait_recv()
        for cp in first + passed:
            cp.wait_send()
        for cp in mine:
            cp.wait()

    return _pallas(
        body, name="all_gather", in_specs=[ANY] * n, out_specs=[ANY] * n,
        out_shape=[_sds((N_DEV,) + s.shape, s.dtype) for s in shards],
        scratch_shapes=[pltpu.SemaphoreType.DMA((n, 7)), pltpu.SemaphoreType.DMA((n, 7)), pltpu.SemaphoreType.DMA((n,))],
    )(*shards)


def exchange_sibling(parts):
    n = len(parts)

    def body(*refs):
        p_refs, out_refs = refs[:n], refs[n:2 * n]
        send_sems, recv_sems = refs[2 * n:]
        x, y, c = _position()
        copies = []
        for t in range(n):
            for q in range(N_CHIP):
                cp = pltpu.make_async_remote_copy(
                    src_ref=p_refs[t].at[q, 1 - c], dst_ref=out_refs[t].at[q], send_sem=send_sems.at[t, q],
                    recv_sem=recv_sems.at[t, q], device_id=(x, y, 1 - c), device_id_type=MESH)
                cp.start()
                copies.append(cp)
        for cp in copies:
            cp.wait()

    return _pallas(
        body, name="exchange_sibling", in_specs=[ANY] * n, out_specs=[ANY] * n,
        out_shape=[_sds((N_CHIP,) + p.shape[2:], p.dtype) for p in parts],
        scratch_shapes=[pltpu.SemaphoreType.DMA((n, N_CHIP)), pltpu.SemaphoreType.DMA((n, N_CHIP))],
    )(*parts)


def exchange_chips(parts):
    n = len(parts)

    def body(*refs):
        p_refs, out_refs = refs[:n], refs[n:2 * n]
        send_sems, recv_sems, local_sems = refs[2 * n:]
        x, y, c = _position()
        chips = _other_chips(x, y)
        copies, local = [], []
        for t in range(n):
            own = pltpu.make_async_copy(p_refs[t].at[2 * x + y], out_refs[t].at[3], local_sems.at[t])
            own.start()
            local.append(own)
            for j, (cx, cy) in enumerate(chips):
                cp = pltpu.make_async_remote_copy(
                    src_ref=p_refs[t].at[2 * cx + cy], dst_ref=out_refs[t].at[j], send_sem=send_sems.at[t, j],
                    recv_sem=recv_sems.at[t, j], device_id=(cx, cy, c), device_id_type=MESH)
                cp.start()
                copies.append(cp)
        for cp in copies:
            cp.wait()
        for cp in local:
            cp.wait()

    return _pallas(
        body, name="exchange_chips", in_specs=[ANY] * n, out_specs=[ANY] * n,
        out_shape=[_sds(p.shape, p.dtype) for p in parts],
        scratch_shapes=[pltpu.SemaphoreType.DMA((n, 3)), pltpu.SemaphoreType.DMA((n, 3)), pltpu.SemaphoreType.DMA((n,))],
    )(*parts)


def _flat_tile(rows, cols):
    tr = rows
    for d in range(8, min(rows, 512) + 1, 8):
        if rows % d == 0 and d * cols * 4 <= 2 * 1024 * 1024:
            tr = d
    return tr


def pair_add(part, recv):
    _, _, R, C = part.shape
    tr = _flat_tile(R, C)
    core = jnp.reshape(lax.axis_index("c"), (1,)).astype(jnp.int32)

    def body(core_ref, p_ref, r_ref, o_ref):
        o_ref[...] = p_ref[...] + r_ref[...]

    grid_spec = pltpu.PrefetchScalarGridSpec(
        num_scalar_prefetch=1, grid=(N_CHIP, R // tr),
        in_specs=[pl.BlockSpec((None, None, tr, C), lambda q, i, core_ref: (q, core_ref[0], i, 0)),
                  pl.BlockSpec((None, tr, C), lambda q, i, core_ref: (q, i, 0))],
        out_specs=pl.BlockSpec((None, tr, C), lambda q, i, core_ref: (q, i, 0)))
    return _pallas(body, name="pair_add", grid_spec=grid_spec, out_shape=_sds((N_CHIP, R, C)),
                   compiler_params=_params("parallel", "parallel"))(core, part, recv)


def adamw(w, m, v, slabs):
    R, C = w.shape
    tr = _flat_tile(R, C)
    n = slabs.shape[0]

    def body(w_ref, m_ref, v_ref, s_ref, g_ref, d_ref, nm_ref, nv_ref):
        g = s_ref[0]
        for j in range(1, n):
            g = g + s_ref[j]
        m2 = ADAM_B1 * m_ref[...] + (1.0 - ADAM_B1) * g
        v2 = ADAM_B2 * v_ref[...] + (1.0 - ADAM_B2) * (g * g)
        m_hat = m2 / (1.0 - ADAM_B1 ** ADAM_STEP)
        v_hat = v2 / (1.0 - ADAM_B2 ** ADAM_STEP)
        g_ref[...] = g
        d_ref[...] = -ADAM_LR * (m_hat / (jnp.sqrt(v_hat) + ADAM_EPS) + ADAM_WD * w_ref[...])
        nm_ref[...] = m2
        nv_ref[...] = v2

    spec = pl.BlockSpec((tr, C), lambda i: (i, 0))
    return _pallas(body, name="adamw", grid=(R // tr,),
                   in_specs=[spec] * 3 + [pl.BlockSpec((n, tr, C), lambda i: (0, i, 0))], out_specs=[spec] * 4,
                   out_shape=[_sds((R, C))] * 4, compiler_params=_params("parallel"))(w, m, v, slabs)


def _unshard_cols(g):
    return jnp.transpose(g, (1, 0, 2)).reshape(g.shape[1], -1)


def _unshard_rows(g):
    return g.reshape(-1, g.shape[2])


def _shard_cols(full):
    R, C = full.shape
    return jnp.transpose(full.reshape(R, N_DEV, C // N_DEV), (1, 0, 2)).reshape(N_CHIP, 2, R, C // N_DEV)


def _shard_rows(full):
    R, C = full.shape
    return full.reshape(N_CHIP, 2, R // N_DEV, C)


WEIGHTS = ['ln_ffn1_pre', 'ln_ffn1_post', 'ffn1_gate', 'ffn1_up', 'ffn1_down', 'ln_mix_pre', 'ln_mix_post', 'w_in',
           'rwkv_mu', 'rwkv_w0', 'rwkv_w2', 'rwkv_a0', 'rwkv_a2', 'rwkv_g2', 'rwkv_k_k', 'rwkv_k_a', 'rwkv_r_k',
           'rwkv_gn_w', 'rwkv_gn_b', 'w_proj_a', 'pool_w', 'pool_scale', 'w_proj_b', 'w_out', 'ln_ffn2_pre',
           'ln_ffn2_post', 'ffn2_gate', 'ffn2_up', 'ffn2_down']
COL_SHARDED = ['ffn1_gate', 'ffn1_up', 'ffn2_gate', 'ffn2_up', 'w_in', 'rwkv_w2', 'rwkv_a2', 'rwkv_g2', 'w_proj_a',
               'w_proj_b']
ROW_SHARDED = ['ffn1_down', 'ffn2_down', 'w_out', 'pool_w']
SHARDED = COL_SHARDED + ROW_SHARDED
REPLICATED = [n for n in WEIGHTS if n not in SHARDED]


def _step(args):
    wts = {n: args[n] if args[n].ndim == 2 else args[n][0] for n in WEIGHTS}
    x, target = args["x"][0], args["loss_target"][0]
    T, D = x.shape
    W = wts["rwkv_w0"].shape[1]
    PW = wts["pool_scale"].shape[1]
    LW, LA, LG = wts["rwkv_w2"].shape[0], wts["rwkv_a2"].shape[0], wts["rwkv_g2"].shape[0]
    lat = LW + LA + LG
    latp = _round_up(lat, LAT_ALIGN)
    rc = 3 * W + lat
    base = 3 * W + PW + 2 * D
    n_groups, gshard, gd = wts["pool_w"].shape

    pool_w_shard = wts["pool_w"].reshape(n_groups * gshard, gd)
    shards = {n: (pool_w_shard if n == "pool_w" else wts[n]).astype(BF16) for n in SHARDED}
    gathered = dict(zip(SHARDED, all_gather([shards[n] for n in SHARDED])))
    full = {n: _unshard_cols(gathered[n]) for n in COL_SHARDED}
    full.update({n: _unshard_rows(gathered[n]) for n in ('ffn1_down', 'ffn2_down', 'w_out')})
    pool_w = jnp.transpose(gathered["pool_w"].reshape(N_DEV, n_groups, gshard, gd), (1, 0, 2, 3)).reshape(n_groups, gd, gd)
    w_in = full["w_in"]
    w_in_p = jnp.concatenate([w_in[:, :3 * W], w_in[:, rc:], w_in[:, 3 * W:rc], jnp.zeros((D, latp - lat), BF16)], axis=1)

    def pad_rows(m, at):
        return jnp.zeros((latp, W), BF16).at[at:at + m.shape[0]].set(m)

    w2p, a2p, g2p = pad_rows(full["rwkv_w2"], 0), pad_rows(full["rwkv_a2"], LW), pad_rows(full["rwkv_g2"], LW + LA)
    mu = wts["rwkv_mu"]
    mu_rkv = mu[:, :3 * W]
    mu_lat = jnp.concatenate([mu[:, 3 * W:], jnp.zeros((1, latp - lat), F32)], axis=1)
    rk = wts["rwkv_r_k"].reshape(1, W)
    small = [mu_rkv, mu_lat, wts["rwkv_w0"], wts["rwkv_a0"], wts["rwkv_k_k"], wts["rwkv_k_a"], w2p, a2p, g2p]

    def ffn_fwd(tag, n_in, gate, up, down):
        g, u, act = _mm(tag + "_up", [n_in, n_in], [gate, up], "nn", [BF16] * 3, epilogue=_swiglu_fwd_epi)
        return g, u, act, _mm(tag + "_down", [act], [down], "nn", [F32])[0]

    n1 = rms_pre(x, wts["ln_ffn1_pre"])
    g1, u1, act1, f1 = ffn_fwd("ffn1", n1, full["ffn1_gate"], full["ffn1_up"], full["ffn1_down"])
    h1, nm = post_pre(x, f1, wts["ln_ffn1_post"], wts["ln_mix_pre"], MACARON)
    p = _mm("in_proj", [nm], [w_in_p], "nn", [F32])[0]
    cols = {"rkv": (p, 3 * W, 0), "pool": (p, PW, 3 * W // PW), "lat": (p, latp, base // latp)}
    r, decay, kmod, v, aneg, bpos, gate = rwkv_prep(p, cols, *small)
    y, ckpt = wkv_fwd(r, decay, kmod, v, aneg, bpos)
    ya_in = rwkv_post(y, r, kmod, v, gate, wts["rwkv_gn_w"], wts["rwkv_gn_b"], rk)
    yb_in = pool_fwd(cols, pool_w, wts["pool_scale"])
    gates = [(p, 3 * W + PW), (p, 3 * W + PW + D)]
    m, ya, yb = _mm("mix", [ya_in, yb_in], [full["w_proj_a"], full["w_proj_b"]], "nn", [BF16] * 3,
                    extras=gates, epilogue=_mix_fwd_epi)
    mx = _mm("out_proj", [m], [full["w_out"]], "nn", [F32])[0]
    h2, n2 = post_pre(h1, mx, wts["ln_mix_post"], wts["ln_ffn2_pre"], 1.0)
    g2_, u2, act2, f2 = ffn_fwd("ffn2", n2, full["ffn2_gate"], full["ffn2_up"], full["ffn2_down"])

    grads = {}
    dh3, df2, loss_part, grads["ln_ffn2_post"] = loss_post_bwd(h2, f2, wts["ln_ffn2_post"], target, MACARON)

    def ffn_bwd(tag, df, n_in, g, u, act, gate, up, down):
        dg, du = _mm(tag + "_dact", [df], [down], "nt", [BF16] * 2, extras=[(g, 0), (u, 0)], epilogue=_swiglu_bwd_epi)
        grads[tag + "_down"] = _mm(tag + "_ddown", [act], [df], "tn", [F32])[0]
        grads[tag + "_gate"], grads[tag + "_up"] = _mm(tag + "_dup", [n_in, n_in], [dg, du], "tn", [F32] * 2)
        return _mm(tag + "_dn", [dg, du], [gate, up], "nt", [F32], sum_pairs=True)[0]

    dn2 = ffn_bwd("ffn2", df2, n2, g2_, u2, act2, full["ffn2_gate"], full["ffn2_up"], full["ffn2_down"])
    dh2, grads["ln_ffn2_pre"] = pre_bwd(dn2, h2, wts["ln_ffn2_pre"], dh3)
    dmx, grads["ln_mix_post"] = post_bwd(dh2, mx, wts["ln_mix_post"], 1.0)
    dya, dyb, dga, dgb = _mm("dmix", [dmx], [full["w_out"]], "nt", [BF16] * 4,
                             extras=gates + [(ya, 0), (yb, 0)], epilogue=_mix_bwd_epi)
    grads["w_out"] = _mm("dw_out", [m], [dmx], "tn", [F32])[0]
    dya_in = _mm("dproj_a", [dya], [full["w_proj_a"]], "nt", [F32])[0]
    dyb_in = _mm("dproj_b", [dyb], [full["w_proj_b"]], "nt", [F32])[0]
    grads["w_proj_a"] = _mm("dw_proj_a", [ya_in], [dya], "tn", [F32])[0]
    grads["w_proj_b"] = _mm("dw_proj_b", [yb_in], [dyb], "tn", [F32])[0]
    dz_pool, dpool_w, grads["pool_scale"] = pool_bwd(cols, dyb_in, pool_w, wts["pool_scale"])
    dy, dr_x, dk_x, dv_x, dgate, grads["rwkv_gn_w"], grads["rwkv_gn_b"], drk = rwkv_post_bwd(
        dya_in, y, r, kmod, v, gate, wts["rwkv_gn_w"], wts["rwkv_gn_b"], rk)
    grads["rwkv_r_k"] = drk.reshape(wts["rwkv_r_k"].shape)
    dr_s, ddecay, dk_s, dv_s, dneg, dbpos = wkv_bwd(r, decay, kmod, v, aneg, bpos, dy, ckpt)
    (dzs, dzls, grads["rwkv_k_a"], grads["rwkv_k_k"], grads["rwkv_a0"], grads["rwkv_w0"], da2p, dw2p, dg2p) = rwkv_prep_bwd(
        p, cols, [dr_s, dr_x, ddecay, dk_s, dk_x, dv_s, dv_x, dneg, dbpos, dgate], *small)
    grads["rwkv_w2"], grads["rwkv_a2"], grads["rwkv_g2"] = dw2p[:LW], da2p[LW:LW + LA], dg2p[LW + LA:lat]
    dz_rkv, dz_lat, dmu_rkv, dmu_lat = shift_bwd(cols, dzs, dzls, mu_rkv, mu_lat)
    grads["rwkv_mu"] = jnp.concatenate([dmu_rkv, dmu_lat[:, :lat]], axis=1)
    dp = jnp.concatenate([dz_rkv, dz_pool, dga, dgb, dz_lat], axis=1)
    dnm = _mm("din_proj", [dp], [w_in_p], "nt", [F32])[0]
    dw_in_p = _mm("dw_in", [nm], [dp], "tn", [F32])[0]
    grads["w_in"] = jnp.concatenate([dw_in_p[:, :3 * W], dw_in_p[:, base:base + lat], dw_in_p[:, 3 * W:base]], axis=1)
    dh1, grads["ln_mix_pre"] = pre_bwd(dnm, h1, wts["ln_mix_pre"], dh2)
    df1, grads["ln_ffn1_post"] = post_bwd(dh1, f1, wts["ln_ffn1_post"], MACARON)
    dn1 = ffn_bwd("ffn1", df1, n1, g1, u1, act1, full["ffn1_gate"], full["ffn1_up"], full["ffn1_down"])
    grad_x, grads["ln_ffn1_pre"] = pre_bwd(dn1, x, wts["ln_ffn1_pre"], dh1)

    parts = {n: _shard_cols(grads[n]) for n in COL_SHARDED}
    parts.update({n: _shard_rows(grads[n]) for n in ('ffn1_down', 'ffn2_down', 'w_out')})
    parts["pool_w"] = jnp.transpose(dpool_w.reshape(n_groups, N_DEV, gshard, gd), (1, 0, 2, 3)).reshape(
        N_CHIP, 2, n_groups * gshard, gd)
    from_sibling = exchange_sibling([parts[n] for n in SHARDED])
    pair_sums = [pair_add(parts[n], rcv) for n, rcv in zip(SHARDED, from_sibling)]
    slabs = dict(zip(SHARDED, exchange_chips(pair_sums)))

    flat = jnp.concatenate([grads[n].reshape(-1) for n in REPLICATED])
    n_small = flat.shape[0]
    rows = _round_up(n_small, 8 * LANES) // LANES
    flat = jnp.concatenate([flat, jnp.zeros((rows * LANES - n_small,), F32)]).reshape(rows, LANES)
    small_slabs = all_gather([flat])[0]

    def packed(prefix):
        vals = jnp.concatenate([args[prefix + n].reshape(-1) for n in REPLICATED])
        return jnp.concatenate([vals, jnp.ones((rows * LANES - n_small,), F32)]).reshape(rows, LANES)

    outs = {}
    small_out = adamw(packed(""), packed("m_"), packed("v_"), small_slabs)
    offset = 0
    for n in REPLICATED:
        size = args[n].size
        outs[n] = [o.reshape(-1)[offset:offset + size].reshape(args[n].shape) for o in small_out]
        offset += size
    for n in SHARDED:
        shard2d = slabs[n].shape[1:]
        res = adamw(*[args[pre + n].reshape(shard2d) for pre in ("", "m_", "v_")], slabs[n])
        outs[n] = [o.reshape(args[n].shape) for o in res]

    loss = lax.psum(loss_part[0, 0], ("x", "y", "c"))
    return (loss, grad_x[None], *[outs[n][0] for n in WEIGHTS], *[outs[n][1] for n in WEIGHTS],
            *[outs[n][2] for n in WEIGHTS], *[outs[n][3] for n in WEIGHTS])


ARG_NAMES = ["x"] + WEIGHTS + ["loss_target"] + ["m_" + n for n in WEIGHTS] + ["v_" + n for n in WEIGHTS]


def kernel(x, ln_ffn1_pre, ln_ffn1_post, ffn1_gate, ffn1_up, ffn1_down, ln_mix_pre, ln_mix_post, w_in, rwkv_mu, rwkv_w0,
           rwkv_w2, rwkv_a0, rwkv_a2, rwkv_g2, rwkv_k_k, rwkv_k_a, rwkv_r_k, rwkv_gn_w, rwkv_gn_b, w_proj_a, pool_w,
           pool_scale, w_proj_b, w_out, ln_ffn2_pre, ln_ffn2_post, ffn2_gate, ffn2_up, ffn2_down, loss_target,
           m_ln_ffn1_pre, m_ln_ffn1_post, m_ffn1_gate, m_ffn1_up, m_ffn1_down, m_ln_mix_pre, m_ln_mix_post, m_w_in,
           m_rwkv_mu, m_rwkv_w0, m_rwkv_w2, m_rwkv_a0, m_rwkv_a2, m_rwkv_g2, m_rwkv_k_k, m_rwkv_k_a, m_rwkv_r_k,
           m_rwkv_gn_w, m_rwkv_gn_b, m_w_proj_a, m_pool_w, m_pool_scale, m_w_proj_b, m_w_out, m_ln_ffn2_pre,
           m_ln_ffn2_post, m_ffn2_gate, m_ffn2_up, m_ffn2_down, v_ln_ffn1_pre, v_ln_ffn1_post, v_ffn1_gate, v_ffn1_up,
           v_ffn1_down, v_ln_mix_pre, v_ln_mix_post, v_w_in, v_rwkv_mu, v_rwkv_w0, v_rwkv_w2, v_rwkv_a0, v_rwkv_a2,
           v_rwkv_g2, v_rwkv_k_k, v_rwkv_k_a, v_rwkv_r_k, v_rwkv_gn_w, v_rwkv_gn_b, v_w_proj_a, v_pool_w, v_pool_scale,
           v_w_proj_b, v_w_out, v_ln_ffn2_pre, v_ln_ffn2_post, v_ffn2_gate, v_ffn2_up, v_ffn2_down):
    given = locals()
    return _step({n: given[n] for n in ARG_NAMES})
```

```python
import jax
import jax.numpy as jnp
from jax import lax
from jax.experimental import pallas as pl
from jax.experimental.pallas import tpu as pltpu

F32, BF16 = jnp.float32, jnp.bfloat16
N_DEV = 8
N_CHIP = 4
HEAD = 64
LANES = 2 * HEAD
NORM_EPS, GN_EPS, L2_EPS = 1e-6, 64e-5, 1e-12
POOL_WINDOWS = (2, 4, 8, 16)
POOL_HALO = 16
MACARON = 0.5
ADAM_LR, ADAM_B1, ADAM_B2, ADAM_EPS, ADAM_WD, ADAM_STEP = 0.001, 0.9, 0.999, 1e-08, 0.01, 10
VMEM_LIMIT = 48 * 1024 * 1024
ROW_TILE = 256
RWKV_ROW_TILE = 128
LAT_ALIGN = 512
WKV_CHUNK, WKV_PAIRS = 16, 8
WKV_UNROLL = 4
MESH = pl.DeviceIdType.MESH


def _pallas(body, **kw):
    return pl.pallas_call(body, **kw)


def _params(*sem):
    return pltpu.CompilerParams(dimension_semantics=sem, vmem_limit_bytes=VMEM_LIMIT)


def _tile(n, target, align=128):
    best = None
    for d in range(align, min(n, target) + 1, align):
        if n % d == 0:
            best = d
    return best if best is not None else n


def _round_up(n, m):
    return (n + m - 1) // m * m


def _mm(name, a_list, b_list, mode, out_dtypes, *, sum_pairs=False, extras=(), epilogue=None,
        tm=1024, tn=512, tk=1024):
    n_pairs = len(a_list)
    a0, b0 = a_list[0], b_list[0]
    if mode == "nn":
        (M, K), N = a0.shape, b0.shape[1]
    elif mode == "nt":
        (M, K), N = a0.shape, b0.shape[0]
    else:
        (K, M), N = a0.shape, b0.shape[1]
    tm, tn, tk = _tile(M, tm), _tile(N, tn), _tile(K, tk)
    nk = K // tk
    n_acc = 1 if sum_pairs else n_pairs
    n_ex = len(extras)
    if mode == "tn":
        a_spec = pl.BlockSpec((tk, tm), lambda i, j, k: (k, i))
    else:
        a_spec = pl.BlockSpec((tm, tk), lambda i, j, k: (i, k))
    if mode == "nt":
        b_spec = pl.BlockSpec((tn, tk), lambda i, j, k: (j, k))
    else:
        b_spec = pl.BlockSpec((tk, tn), lambda i, j, k: (k, j))
    contract = {"nn": ((1,), (0,)), "nt": ((1,), (1,)), "tn": ((0,), (0,))}[mode]
    e_specs = []
    for _, col in extras:
        assert col % tn == 0
        e_specs.append(pl.BlockSpec((tm, tn), lambda i, j, k, off=col // tn: (i, j + off)))
    o_spec = pl.BlockSpec((tm, tn), lambda i, j, k: (i, j))

    def body(*refs):
        a_refs, b_refs = refs[:n_pairs], refs[n_pairs:2 * n_pairs]
        e_refs = refs[2 * n_pairs:2 * n_pairs + n_ex]
        o_refs = refs[2 * n_pairs + n_ex:len(refs) - n_acc]
        acc_refs = refs[len(refs) - n_acc:]
        k = pl.program_id(2)

        @pl.when(k == 0)
        def _():
            for acc in acc_refs:
                acc[...] = jnp.zeros_like(acc)

        for p in range(n_pairs):
            acc = acc_refs[0 if sum_pairs else p]
            acc[...] += lax.dot_general(a_refs[p][...], b_refs[p][...], (contract, ((), ())),
                                        preferred_element_type=F32)

        @pl.when(k == nk - 1)
        def _():
            accs = [acc[...] for acc in acc_refs]
            outs = epilogue(accs, [e[...] for e in e_refs]) if epilogue else accs
            for o_ref, o in zip(o_refs, outs):
                o_ref[...] = o.astype(o_ref.dtype)

    return _pallas(
        body, name=name, grid=(M // tm, N // tn, nk),
        in_specs=[a_spec] * n_pairs + [b_spec] * n_pairs + e_specs,
        out_specs=[o_spec] * len(out_dtypes),
        out_shape=[jax.ShapeDtypeStruct((M, N), dt) for dt in out_dtypes],
        scratch_shapes=[pltpu.VMEM((tm, tn), F32)] * n_acc,
        compiler_params=_params("parallel", "parallel", "arbitrary"),
    )(*a_list, *b_list, *[e for e, _ in extras])


def _swiglu_fwd_epi(accs, _):
    g, u = accs
    return [g, u, g * jax.nn.sigmoid(g) * u]


def _swiglu_bwd_epi(accs, ex):
    dact = accs[0]
    g, u = ex[0].astype(F32), ex[1].astype(F32)
    sg = jax.nn.sigmoid(g)
    return [dact * u * (sg * (1.0 + g * (1.0 - sg))), dact * (g * sg)]


def _mix_fwd_epi(accs, ex):
    ya, yb = accs
    return [jax.nn.sigmoid(ex[0]) * ya + jax.nn.sigmoid(ex[1]) * yb, ya, yb]


def _mix_bwd_epi(accs, ex):
    dm = accs[0]
    sa, sb = jax.nn.sigmoid(ex[0]), jax.nn.sigmoid(ex[1])
    ya, yb = ex[2].astype(F32), ex[3].astype(F32)
    return [dm * sa, dm * sb, dm * ya * sa * (1.0 - sa), dm * yb * sb * (1.0 - sb)]


def _row_call(name, body, T, tiled, params, outs, accs=(), prev=(), nxt=(), halo=8, tile=ROW_TILE):
    tm = min(tile, T)
    n_tiles = T // tm

    def norm(e):
        return e if isinstance(e, tuple) else (e, e.shape[1], 0)

    tiled, prev, nxt = [norm(e) for e in tiled], [norm(e) for e in prev], [norm(e) for e in nxt]
    per_halo, n_halo = tm // halo, T // halo
    in_specs = [pl.BlockSpec((tm, w), lambda i, cb=cb: (i, cb)) for _, w, cb in tiled]
    in_specs += [pl.BlockSpec((halo, w), lambda i, cb=cb: (jnp.maximum(i * per_halo - 1, 0), cb)) for _, w, cb in prev]
    in_specs += [pl.BlockSpec((halo, w), lambda i, cb=cb: (jnp.minimum((i + 1) * per_halo, n_halo - 1), cb))
                 for _, w, cb in nxt]
    in_specs += [pl.BlockSpec(p.shape, lambda i, nd=p.ndim: (0,) * nd) for p in params]
    out_specs = [pl.BlockSpec((tm, o.shape[1]), lambda i: (i, 0)) for o in outs]
    out_specs += [pl.BlockSpec(a.shape, lambda i, nd=len(a.shape): (0,) * nd) for a in accs]
    n1, n2, n3, n4, n5 = len(tiled), len(prev), len(nxt), len(params), len(outs)

    def kernel_body(*refs):
        i = pl.program_id(0)
        acc_refs = refs[n1 + n2 + n3 + n4 + n5:]

        @pl.when(i == 0)
        def _():
            for a in acc_refs:
                a[...] = jnp.zeros_like(a)

        body(i, n_tiles, refs[:n1], refs[n1:n1 + n2], refs[n1 + n2:n1 + n2 + n3],
             refs[n1 + n2 + n3:n1 + n2 + n3 + n4], refs[n1 + n2 + n3 + n4:n1 + n2 + n3 + n4 + n5], acc_refs)

    return _pallas(
        kernel_body, name=name, grid=(n_tiles,), in_specs=in_specs, out_specs=out_specs,
        out_shape=list(outs) + list(accs),
        compiler_params=_params("arbitrary"),
    )(*[e[0] for e in tiled + prev + nxt], *params)


def _sds(shape, dtype=F32):
    return jax.ShapeDtypeStruct(tuple(shape), dtype)


def _rstd(x):
    return lax.rsqrt(jnp.mean(x * x, axis=-1, keepdims=True) + NORM_EPS)


def _colsum(x):
    return jnp.sum(x, axis=0, keepdims=True)


def rms_pre(x, g):
    T, D = x.shape

    def body(i, n, tiled, prev, nxt, params, outs, accs):
        xv = tiled[0][...]
        outs[0][...] = (xv * _rstd(xv) * params[0][...]).astype(BF16)

    return _row_call("rms_pre", body, T, [x], [g], [_sds((T, D), BF16)])[0]


def post_pre(h, f, g_post, g_pre, scale):
    T, D = h.shape

    def body(i, n, tiled, prev, nxt, params, outs, accs):
        hv, fv = tiled[0][...], tiled[1][...]
        h2 = hv + scale * (fv * _rstd(fv) * params[0][...])
        outs[0][...] = h2
        outs[1][...] = (h2 * _rstd(h2) * params[1][...]).astype(BF16)

    return _row_call("post_pre", body, T, [h, f], [g_post, g_pre], [_sds((T, D)), _sds((T, D), BF16)])


def _post_bwd_math(dh, fv, g, scale):
    r = _rstd(fv)
    fhat = fv * r
    dy = scale * dh
    z = dy * g
    df = r * (z - fhat * jnp.mean(z * fhat, axis=-1, keepdims=True))
    return df, _colsum(dy * fhat)


def loss_post_bwd(h, f, g_post, target, scale):
    T, D = h.shape

    def body(i, n, tiled, prev, nxt, params, outs, accs):
        hv, fv, tv = tiled[0][...], tiled[1][...], tiled[2][...]
        g = params[0][...]
        e = hv + scale * (fv * _rstd(fv) * g) - tv
        accs[0][...] += jnp.full(accs[0].shape, 0.5 / D, F32) * jnp.sum(e * e)
        dh = e * (1.0 / D)
        outs[0][...] = dh
        df, dg = _post_bwd_math(dh, fv, g, scale)
        outs[1][...] = df.astype(BF16)
        accs[1][...] += dg

    return _row_call("loss_post_bwd", body, T, [h, f, target], [g_post],
                     [_sds((T, D)), _sds((T, D), BF16)], [_sds((1, LANES)), _sds((1, D))])


def post_bwd(dh, f, g_post, scale):
    T, D = dh.shape

    def body(i, n, tiled, prev, nxt, params, outs, accs):
        df, dg = _post_bwd_math(tiled[0][...], tiled[1][...], params[0][...], scale)
        outs[0][...] = df.astype(BF16)
        accs[0][...] += dg

    return _row_call("post_bwd", body, T, [dh, f], [g_post], [_sds((T, D), BF16)], [_sds((1, D))])


def pre_bwd(dn, h, g_pre, dres):
    T, D = h.shape

    def body(i, n, tiled, prev, nxt, params, outs, accs):
        dnv, hv = tiled[0][...], tiled[1][...]
        r = _rstd(hv)
        hhat = hv * r
        z = dnv * params[0][...]
        outs[0][...] = tiled[2][...] + r * (z - hhat * jnp.mean(z * hhat, axis=-1, keepdims=True))
        accs[0][...] += _colsum(dnv * hhat)

    return _row_call("pre_bwd", body, T, [dn, h, dres], [g_pre], [_sds((T, D))], [_sds((1, D))])


def _head_ones():
    i = lax.broadcasted_iota(jnp.int32, (LANES, LANES), 0)
    j = lax.broadcasted_iota(jnp.int32, (LANES, LANES), 1)
    return jnp.where((i < HEAD) == (j < HEAD), 1.0, 0.0).astype(F32)


def _headsum(x):
    e = _head_ones()
    parts = [jnp.dot(x[:, s:s + LANES], e, precision=lax.Precision.HIGHEST, preferred_element_type=F32)
             for s in range(0, x.shape[1], LANES)]
    return parts[0] if len(parts) == 1 else jnp.concatenate(parts, axis=1)


def _shift_down(x, before):
    row = lax.broadcasted_iota(jnp.int32, x.shape, 0)
    return jnp.where(row == 0, before, pltpu.roll(x, 1, 0))


def _shift_up(x, after):
    row = lax.broadcasted_iota(jnp.int32, x.shape, 0)
    return jnp.where(row == x.shape[0] - 1, after, pltpu.roll(x, x.shape[0] - 1, 0))


def _last_row(ref, keep):
    r = ref[ref.shape[0] - 1:ref.shape[0], :]
    return jnp.where(keep, r, jnp.zeros_like(r))


def _first_row(ref, keep):
    r = ref[0:1, :]
    return jnp.where(keep, r, jnp.zeros_like(r))


def _softplus(u):
    return jnp.maximum(u, 0.0) + jnp.log(1.0 + jnp.exp(-jnp.abs(u)))


def _dotb(a, b, contract):
    return lax.dot_general(a.astype(BF16), b.astype(BF16), (contract, ((), ())), preferred_element_type=F32)


_NN, _NT, _TN = ((1,), (0,)), ((1,), (1,)), ((0,), (0,))


def _prep_forward(z, zprev_row, zl, zlprev_row, mu, mul, w0, a0, kk_w, ka_w, w2p, a2p, g2p):
    W = w0.shape[1]
    zs = z + (_shift_down(z, zprev_row) - z) * mu
    zls = zl + (_shift_down(zl, zlprev_row) - zl) * mul
    r, k, v = zs[:, :W], zs[:, W:2 * W], zs[:, 2 * W:]
    th, sg = jnp.tanh(zls), jax.nn.sigmoid(zls)
    xw = w0 + _dotb(th, w2p, _NN)
    wlog = -_softplus(-xw) - 0.5
    ew = jnp.exp(wlog)
    decay = jnp.exp(-ew)
    a = jax.nn.sigmoid(a0 + _dotb(zls, a2p, _NN))
    gate = _dotb(sg, g2p, _NN)
    q = k * kk_w
    nrm = jnp.sqrt(_headsum(q * q))
    den = jnp.maximum(nrm, L2_EPS)
    kk = q / den
    kmod = k * (1.0 + (a - 1.0) * ka_w)
    return dict(zs=zs, zls=zls, r=r, k=k, v=v, th=th, sg=sg, xw=xw, ew=ew, decay=decay, a=a, gate=gate,
                nrm=nrm, den=den, kk=kk, kmod=kmod)


def rwkv_prep(p, cols, mu, mul, w0, a0, kk_w, ka_w, w2p, a2p, g2p):
    T = p.shape[0]
    W = w0.shape[1]

    def body(i, n, tiled, prev, nxt, params, outs, accs):
        c = _prep_forward(tiled[0][...], _last_row(prev[0], i > 0), tiled[1][...], _last_row(prev[1], i > 0),
                          *[q[...] for q in params])
        for o, val in zip(outs, (c["r"], c["decay"], c["kmod"], c["v"], -c["kk"], c["kk"] * c["a"], c["gate"])):
            o[...] = val

    return _row_call("rwkv_prep", body, T, [cols["rkv"], cols["lat"]],
                     [mu, mul, w0, a0, kk_w, ka_w, w2p, a2p, g2p], [_sds((T, W))] * 7,
                     prev=[cols["rkv"], cols["lat"]], tile=RWKV_ROW_TILE)


def _post_forward(y, r, kmod, v, gn_w, gn_b, rk):
    mean = _headsum(y) * (1.0 / HEAD)
    yc = y - mean
    rstd = lax.rsqrt(_headsum(yc * yc) * (1.0 / HEAD) + GN_EPS)
    yn = yc * rstd
    s = _headsum(r * kmod * rk)
    return yn, rstd, s, yn * gn_w + gn_b + s * v


def rwkv_post(y, r, kmod, v, gate, gn_w, gn_b, rk):
    T, W = y.shape

    def body(i, n, tiled, prev, nxt, params, outs, accs):
        yv, rv, kv, vv, gv = [t[...] for t in tiled]
        _, _, _, o = _post_forward(yv, rv, kv, vv, *[q[...] for q in params])
        outs[0][...] = (o * gv).astype(BF16)

    return _row_call("rwkv_post", body, T, [y, r, kmod, v, gate], [gn_w, gn_b, rk], [_sds((T, W), BF16)],
                     tile=RWKV_ROW_TILE)[0]


def rwkv_post_bwd(dout, y, r, kmod, v, gate, gn_w, gn_b, rk):
    T, W = y.shape

    def body(i, n, tiled, prev, nxt, params, outs, accs):
        dv_, yv, rv, kv, vv, gv = [t[...] for t in tiled]
        gn_w_, gn_b_, rk_ = [q[...] for q in params]
        yn, rstd, s, o = _post_forward(yv, rv, kv, vv, gn_w_, gn_b_, rk_)
        do = dv_ * gv
        outs[4][...] = dv_ * o
        accs[0][...] += _colsum(do * yn)
        accs[1][...] += _colsum(do)
        dyn = do * gn_w_
        outs[0][...] = rstd * (dyn - _headsum(dyn) * (1.0 / HEAD) - yn * (_headsum(dyn * yn) * (1.0 / HEAD)))
        ds = _headsum(do * vv)
        outs[1][...] = ds * kv * rk_
        outs[2][...] = ds * rv * rk_
        outs[3][...] = do * s
        accs[2][...] += _colsum(ds * rv * kv)

    return _row_call("rwkv_post_bwd", body, T, [dout, y, r, kmod, v, gate], [gn_w, gn_b, rk],
                     [_sds((T, W))] * 5, [_sds((1, W))] * 3, tile=RWKV_ROW_TILE)


def rwkv_prep_bwd(p, cols, grads, mu, mul, w0, a0, kk_w, ka_w, w2p, a2p, g2p):
    T = p.shape[0]
    W = w0.shape[1]
    latp = w2p.shape[0]

    def body(i, n, tiled, prev, nxt, params, outs, accs):
        pv = [q[...] for q in params]
        mu_, mul_, w0_, a0_, kk_w_, ka_w_, w2p_, a2p_, g2p_ = pv
        c = _prep_forward(tiled[0][...], _last_row(prev[0], i > 0), tiled[1][...], _last_row(prev[1], i > 0), *pv)
        dr_s, dr_x, ddecay, dk_s, dk_x, dv_s, dv_x, dneg, db, dgate = [t[...] for t in tiled[2:]]
        k, a, kk = c["k"], c["a"], c["kk"]
        dkmod = dk_s + dk_x
        dk = dkmod * (1.0 + (a - 1.0) * ka_w_)
        da = dkmod * k * ka_w_ + db * kk
        accs[0][...] += _colsum(dkmod * k * (a - 1.0))
        dkk = db * a - dneg
        dq = jnp.where(c["nrm"] > L2_EPS, dkk - kk * _headsum(dkk * kk), dkk) / c["den"]
        dk = dk + dq * kk_w_
        accs[1][...] += _colsum(dq * k)
        dxa = da * a * (1.0 - a)
        accs[2][...] += _colsum(dxa)
        accs[4][...] += _dotb(c["zls"], dxa, _TN)
        dzls = _dotb(dxa, a2p_, _NT)
        dxw = (-ddecay * c["decay"] * c["ew"]) * jax.nn.sigmoid(-c["xw"])
        accs[3][...] += _colsum(dxw)
        accs[5][...] += _dotb(c["th"], dxw, _TN)
        dzls = dzls + _dotb(dxw, w2p_, _NT) * (1.0 - c["th"] * c["th"])
        accs[6][...] += _dotb(c["sg"], dgate, _TN)
        dzls = dzls + _dotb(dgate, g2p_, _NT) * c["sg"] * (1.0 - c["sg"])
        outs[0][...] = jnp.concatenate([dr_s + dr_x, dk, dv_s + dv_x], axis=1)
        outs[1][...] = dzls

    return _row_call("rwkv_prep_bwd", body, T, [cols["rkv"], cols["lat"]] + list(grads),
                     [mu, mul, w0, a0, kk_w, ka_w, w2p, a2p, g2p], [_sds((T, 3 * W)), _sds((T, latp))],
                     [_sds((1, W))] * 4 + [_sds((latp, W))] * 3, prev=[cols["rkv"], cols["lat"]], tile=RWKV_ROW_TILE)


def shift_bwd(cols, dzs, dzls, mu, mul):
    T = dzs.shape[0]

    def body(i, n, tiled, prev, nxt, params, outs, accs):
        for j in range(2):
            z, d, m = tiled[j][...], tiled[2 + j][...], params[j][...]
            zprev = _shift_down(z, _last_row(prev[j], i > 0))
            dnext = _shift_up(d, _first_row(nxt[j], i < n - 1))
            outs[j][...] = (d * (1.0 - m) + dnext * m).astype(BF16)
            accs[j][...] += _colsum(d * (zprev - z))

    return _row_call("shift_bwd", body, T, [cols["rkv"], cols["lat"], dzs, dzls], [mu, mul],
                     [_sds(dzs.shape, BF16), _sds(dzls.shape, BF16)], [_sds(mu.shape), _sds(mul.shape)],
                     prev=[cols["rkv"], cols["lat"]], nxt=[dzs, dzls])


def _window_pick(x, windows):
    gid = lax.broadcasted_iota(jnp.int32, x.shape, 1) // (x.shape[1] // len(windows))
    out = windows[-1]
    for g in range(len(windows) - 2, -1, -1):
        out = jnp.where(gid == g, windows[g], out)
    return out


def _pool_counts(t0, rows, width):
    t = (t0 + lax.broadcasted_iota(jnp.int32, (rows, width), 0) + 1).astype(F32)
    return _window_pick(t, [jnp.minimum(t, float(w)) for w in POOL_WINDOWS])


def _pool_mixed(x, before, t0):
    tm, width = x.shape
    xe = jnp.concatenate([before, x], axis=0)
    sums, s, span = [], xe, 1
    for w in POOL_WINDOWS:
        while span < w:
            s = s + pltpu.roll(s, span, 0)
            span *= 2
        sums.append(s[POOL_HALO:, :])
    return _window_pick(x, sums) / _pool_counts(t0, tm, width) - x


def _group_dot(x, w_ref, contract):
    gd = w_ref.shape[-1]
    parts = [_dotb(x[:, g * gd:(g + 1) * gd], w_ref[g], contract) for g in range(w_ref.shape[0])]
    return jnp.concatenate(parts, axis=1)


def pool_fwd(cols, pool_w, pool_scale):
    T, width = cols["pool"][0].shape[0], cols["pool"][1]
    tm = min(ROW_TILE, T)

    def body(i, n, tiled, prev, nxt, params, outs, accs):
        before = jnp.where(i > 0, prev[0][...], 0.0)
        mixed = _pool_mixed(tiled[0][...], before, i * tm)
        outs[0][...] = (_group_dot(mixed, params[0], _NN) * params[1][...]).astype(BF16)

    return _row_call("pool_fwd", body, T, [cols["pool"]], [pool_w, pool_scale], [_sds((T, width), BF16)],
                     prev=[cols["pool"]], halo=POOL_HALO)[0]


def pool_bwd(cols, dout, pool_w, pool_scale):
    T, width = dout.shape
    tm = min(ROW_TILE, T)

    def body(i, n, tiled, prev, nxt, params, outs, accs):
        w_ref, scale = params[0], params[1][...]
        before = jnp.where(i > 0, prev[0][...], 0.0)
        mixed = _pool_mixed(tiled[0][...], before, i * tm)
        dv = tiled[1][...]
        accs[1][...] += _colsum(dv * _group_dot(mixed, w_ref, _NN))
        after = jnp.where(i < n - 1, nxt[0][...], 0.0)
        dys = jnp.concatenate([dv, after], axis=0) * scale
        gd = w_ref.shape[-1]
        for g in range(w_ref.shape[0]):
            accs[0][g] += _dotb(mixed[:, g * gd:(g + 1) * gd], dys[:tm, g * gd:(g + 1) * gd], _TN)
        dmixed = _group_dot(dys, w_ref, _NT)
        u = dmixed / _pool_counts(i * tm, tm + POOL_HALO, width)
        rows = tm + POOL_HALO
        sums, s, span = [], u, 1
        for w in POOL_WINDOWS:
            while span < w:
                s = s + pltpu.roll(s, rows - span, 0)
                span *= 2
            sums.append(s[:tm, :])
        outs[0][...] = (_window_pick(dv, sums) - dmixed[:tm, :]).astype(BF16)

    return _row_call("pool_bwd", body, T, [cols["pool"], dout], [pool_w, pool_scale], [_sds((T, width), BF16)],
                     [_sds(pool_w.shape), _sds((1, width))], prev=[cols["pool"]], nxt=[dout], halo=POOL_HALO)


def _wkv_consts(pairs):
    lane = lax.broadcasted_iota(jnp.int32, (HEAD, LANES), 1)
    sub = lax.broadcasted_iota(jnp.int32, (pairs * HEAD, LANES), 0)
    lane_all = lax.broadcasted_iota(jnp.int32, (pairs * HEAD, LANES), 1)
    i = lax.broadcasted_iota(jnp.int32, (LANES, LANES), 0)
    j = lax.broadcasted_iota(jnp.int32, (LANES, LANES), 1)
    ones = jnp.where((i < HEAD) == (j < HEAD), 1.0, 0.0).astype(BF16)
    diag = jnp.where((lane_all & (HEAD - 1)) == (sub & (HEAD - 1)), 1.0, 0.0).astype(F32)
    return lane < HEAD, diag, ones


def _segsum(p, in_a):
    sa = jnp.sum(jnp.where(in_a, p, 0.0), axis=1, keepdims=True)
    sb = jnp.sum(jnp.where(in_a, 0.0, p), axis=1, keepdims=True)
    return jnp.where(in_a, sa, sb)


def _hi_lo(p):
    hi = lax.bitcast_convert_type(lax.bitcast_convert_type(p, jnp.uint32) & jnp.uint32(0xFFFF0000), F32)
    return hi, p - hi


def _segsum_mxu(p, ones):
    hi, lo = _hi_lo(p)
    return (jnp.dot(hi.astype(BF16), ones, preferred_element_type=F32)
            + jnp.dot(lo.astype(BF16), ones, preferred_element_type=F32))


def _cat(parts, axis):
    return parts[0] if len(parts) == 1 else jnp.concatenate(parts, axis=axis)


def _tile_rows(row, pairs):
    return _cat([jnp.broadcast_to(row[:, p * LANES:(p + 1) * LANES], (HEAD, LANES)) for p in range(pairs)], 0)


def _spread(row, pairs, diag16, ones):
    hi, lo = _hi_lo(row)
    return (jnp.dot(_tile_rows(hi.astype(BF16), pairs) * diag16, ones, preferred_element_type=F32)
            + jnp.dot(_tile_rows(lo.astype(BF16), pairs) * diag16, ones, preferred_element_type=F32))


def _pair_colsums(x, pairs):
    return _cat([_colsum(x[p * HEAD:(p + 1) * HEAD]) for p in range(pairs)], 1)


def wkv_fwd(r, w, k, v, a, b):
    T, W = r.shape
    P = W // LANES
    PB = min(WKV_PAIRS, P)
    chunk = min(WKV_CHUNK, T)
    NC = T // chunk
    R = PB * HEAD

    def body(r_ref, w_ref, k_ref, v_ref, a_ref, b_ref, y_ref, ck_ref, vt_ref, st_ref):
        c = pl.program_id(1)

        @pl.when(c == 0)
        def _():
            st_ref[chunk] = jnp.zeros((R, LANES), F32)

        st_ref[0] = st_ref[chunk]
        ck_ref[0] = st_ref[chunk]
        in_a, diag, ones = _wkv_consts(PB)
        diag16 = diag.astype(BF16)

        def spread(t, _):
            vt_ref[t] = _spread(v_ref[pl.ds(t, 1), :], PB, diag16, ones)
            return 0

        lax.fori_loop(0, chunk, spread, 0, unroll=WKV_UNROLL)

        def step(t, _):
            rows = [ref[pl.ds(t, 1), :] for ref in (w_ref, k_ref, a_ref, b_ref)]
            for p in range(PB):
                wt, kt, at, bt = [x[:, p * LANES:(p + 1) * LANES] for x in rows]
                rs = pl.ds(p * HEAD, HEAD)
                S = st_ref[t, rs]
                st_ref[t + 1, rs] = S * wt + _segsum(S * at, in_a) * bt + vt_ref[t, rs] * kt
            return 0

        lax.fori_loop(0, chunk, step, 0, unroll=WKV_UNROLL)

        def readout(t, _):
            yt = _segsum_mxu(st_ref[t + 1] * _tile_rows(r_ref[pl.ds(t, 1), :], PB), ones) * diag
            y_ref[pl.ds(t, 1), :] = _pair_colsums(yt, PB)
            return 0

        lax.fori_loop(0, chunk, readout, 0, unroll=WKV_UNROLL)

    spec = pl.BlockSpec((chunk, PB * LANES), lambda g, c: (c, g))
    return _pallas(
        body, name="wkv_fwd", grid=(P // PB, NC), in_specs=[spec] * 6,
        out_specs=[spec, pl.BlockSpec((1, R, LANES), lambda g, c: (c, g, 0))],
        out_shape=[_sds((T, W)), _sds((NC, P * HEAD, LANES))],
        scratch_shapes=[pltpu.VMEM((chunk, R, LANES), F32), pltpu.VMEM((chunk + 1, R, LANES), F32)],
        compiler_params=_params("parallel", "arbitrary"),
    )(r, w, k, v, a, b)


def wkv_bwd(r, w, k, v, a, b, dy, ckpt):
    T, W = r.shape
    P = W // LANES
    PB = min(WKV_PAIRS, P)
    chunk = min(WKV_CHUNK, T)
    NC = T // chunk
    R = PB * HEAD

    def body(r_ref, w_ref, k_ref, v_ref, a_ref, b_ref, dy_ref, ck_ref, dr_ref, dw_ref, dk_ref, dv_ref, da_ref, db_ref,
             ds_ref, vt_ref, dyt_ref, st_ref, sa_ref, dst_ref, dsa_ref):
        c = pl.program_id(1)

        @pl.when(c == 0)
        def _():
            ds_ref[...] = jnp.zeros_like(ds_ref)

        in_a, diag, ones = _wkv_consts(PB)
        diag16 = diag.astype(BF16)

        def spread(t, _):
            vt_ref[t] = _spread(v_ref[pl.ds(t, 1), :], PB, diag16, ones)
            dyt_ref[t] = _spread(dy_ref[pl.ds(t, 1), :], PB, diag16, ones)
            return 0

        lax.fori_loop(0, chunk, spread, 0, unroll=WKV_UNROLL)
        st_ref[0] = ck_ref[0]

        def fstep(t, _):
            rows = [ref[pl.ds(t, 1), :] for ref in (w_ref, k_ref, a_ref, b_ref)]
            for p in range(PB):
                wt, kt, at, bt = [x[:, p * LANES:(p + 1) * LANES] for x in rows]
                rs = pl.ds(p * HEAD, HEAD)
                S = st_ref[t, rs]
                sa = _segsum(S * at, in_a)
                sa_ref[t, rs] = sa
                st_ref[t + 1, rs] = S * wt + sa * bt + vt_ref[t, rs] * kt
            return 0

        lax.fori_loop(0, chunk, fstep, 0, unroll=WKV_UNROLL)

        def bstep(n, _):
            t = chunk - 1 - n
            rows = [ref[pl.ds(t, 1), :] for ref in (r_ref, w_ref, a_ref, b_ref)]
            for p in range(PB):
                rt, wt, at, bt = [x[:, p * LANES:(p + 1) * LANES] for x in rows]
                rs = pl.ds(p * HEAD, HEAD)
                dS = ds_ref[rs] + dyt_ref[t, rs] * rt
                dst_ref[t, rs] = dS
                dsa = _segsum(dS * bt, in_a)
                dsa_ref[t, rs] = dsa
                ds_ref[rs] = dS * wt + dsa * at
            return 0

        lax.fori_loop(0, chunk, bstep, 0, unroll=WKV_UNROLL)

        def collect(t, _):
            sp, sn, dS, dsa = st_ref[t], st_ref[t + 1], dst_ref[t], dsa_ref[t]
            dvt = _segsum_mxu(dS * _tile_rows(k_ref[pl.ds(t, 1), :], PB), ones) * diag
            for ref, val in ((dr_ref, sn * dyt_ref[t]), (dw_ref, dS * sp), (dk_ref, dS * vt_ref[t]), (dv_ref, dvt),
                             (da_ref, sp * dsa), (db_ref, dS * sa_ref[t])):
                ref[pl.ds(t, 1), :] = _pair_colsums(val, PB)
            return 0

        lax.fori_loop(0, chunk, collect, 0, unroll=WKV_UNROLL)

    spec = pl.BlockSpec((chunk, PB * LANES), lambda g, c: (NC - 1 - c, g))

    def tiles(n):
        return pltpu.VMEM((n, R, LANES), F32)

    return _pallas(
        body, name="wkv_bwd", grid=(P // PB, NC),
        in_specs=[spec] * 7 + [pl.BlockSpec((1, R, LANES), lambda g, c: (NC - 1 - c, g, 0))],
        out_specs=[spec] * 6, out_shape=[_sds((T, W))] * 6,
        scratch_shapes=[pltpu.VMEM((R, LANES), F32), tiles(chunk), tiles(chunk), tiles(chunk + 1), tiles(chunk),
                        tiles(chunk), tiles(chunk)],
        compiler_params=_params("parallel", "arbitrary"),
    )(r, w, k, v, a, b, dy, ckpt)


def _position():
    return lax.axis_index("x"), lax.axis_index("y"), lax.axis_index("c")


def _other_chips(x, y):
    return [(1 - x, y), (x, 1 - y), (1 - x, 1 - y)]


ANY = pl.BlockSpec(memory_space=pl.ANY)


def all_gather(shards):
    n = len(shards)

    def body(*refs):
        x_refs, out_refs = refs[:n], refs[n:2 * n]
        send_sems, recv_sems, local_sems = refs[2 * n:]
        x, y, c = _position()
        me, sibling = (x, y, c), (x, y, 1 - c)
        chips = _other_chips(x, y)

        def slot(ref, pos):
            return ref.at[4 * pos[0] + 2 * pos[1] + pos[2]]

        def copy(t, j, block, to, src=None):
            dst = slot(out_refs[t], block)
            return pltpu.make_async_remote_copy(
                src_ref=dst if src is None else src, dst_ref=dst, send_sem=send_sems.at[t, j],
                recv_sem=recv_sems.at[t, j], device_id=to, device_id_type=MESH)

        mine = [pltpu.make_async_copy(x_refs[t], slot(out_refs[t], me), local_sems.at[t]) for t in range(n)]
        first, passed = [], []
        for t in range(n):
            mine[t].start()
            first.append(copy(t, 0, me, sibling, src=x_refs[t]))
            first += [copy(t, 1 + j, me, (*chip, c), src=x_refs[t]) for j, chip in enumerate(chips)]
        for cp in first:
            cp.start()
        for t in range(n):
            for j, chip in enumerate(chips):
                copy(t, 1 + j, (*chip, c), me).wait_recv()
                fwd = copy(t, 4 + j, (*chip, c), sibling)
                fwd.start()
                passed.append(fwd)
        for t in range(n):
            copy(t, 0, sibling, me).wait_recv()
            for j, chip in enumerate(chips):
                copy(t, 4 + j, (*chip, 1 - c), me).wait_recv()
        for cp in first + passed:
            cp.wait_send()
        for cp in mine:
            cp.wait()

    return _pallas(
        body, name="all_gather", in_specs=[ANY] * n, out_specs=[ANY] * n,
        out_shape=[_sds((N_DEV,) + s.shape, s.dtype) for s in shards],
        scratch_shapes=[pltpu.SemaphoreType.DMA((n, 7)), pltpu.SemaphoreType.DMA((n, 7)), pltpu.SemaphoreType.DMA((n,))],
    )(*shards)


def exchange_sibling(parts):
    n = len(parts)

    def body(*refs):
        p_refs, out_refs = refs[:n], refs[n:2 * n]
        send_sems, recv_sems = refs[2 * n:]
        x, y, c = _position()
        copies = []
        for t in range(n):
            for q in range(N_CHIP):
                cp = pltpu.make_async_remote_copy(
                    src_ref=p_refs[t].at[q, 1 - c], dst_ref=out_refs[t].at[q], send_sem=send_sems.at[t, q],
                    recv_sem=recv_sems.at[t, q], device_id=(x, y, 1 - c), device_id_type=MESH)
                cp.start()
                copies.append(cp)
        for cp in copies:
            cp.wait()

    return _pallas(
        body, name="exchange_sibling", in_specs=[ANY] * n, out_specs=[ANY] * n,
        out_shape=[_sds((N_CHIP,) + p.shape[2:], p.dtype) for p in parts],
        scratch_shapes=[pltpu.SemaphoreType.DMA((n, N_CHIP)), pltpu.SemaphoreType.DMA((n, N_CHIP))],
    )(*parts)


def exchange_chips(parts):
    n = len(parts)

    def body(*refs):
        p_refs, out_refs = refs[:n], refs[n:2 * n]
        send_sems, recv_sems, local_sems = refs[2 * n:]
        x, y, c = _position()
        chips = _other_chips(x, y)
        copies, local = [], []
        for t in range(n):
            own = pltpu.make_async_copy(p_refs[t].at[2 * x + y], out_refs[t].at[3], local_sems.at[t])
            own.start()
            local.append(own)
            for j, (cx, cy) in enumerate(chips):
                cp = pltpu.make_async_remote_copy(
                    src_ref=p_refs[t].at[2 * cx + cy], dst_ref=out_refs[t].at[j], send_sem=send_sems.at[t, j],
                    recv_sem=recv_sems.at[t, j], device_id=(cx, cy, c), device_id_type=MESH)
                cp.start()
                copies.append(cp)
        for cp in copies:
            cp.wait()
        for cp in local:
            cp.wait()

    return _pallas(
        body, name="exchange_chips", in_specs=[ANY] * n, out_specs=[ANY] * n,
        out_shape=[_sds(p.shape, p.dtype) for p in parts],
        scratch_shapes=[pltpu.SemaphoreType.DMA((n, 3)), pltpu.SemaphoreType.DMA((n, 3)), pltpu.SemaphoreType.DMA((n,))],
    )(*parts)


def _flat_tile(rows, cols):
    tr = rows
    for d in range(16, min(rows, 512) + 1, 16):
        if rows % d == 0 and d * cols * 4 <= 2 * 1024 * 1024:
            tr = d
    return tr


def pair_add(part, recv):
    _, _, R, C = part.shape
    tr = _flat_tile(R, C)
    core = jnp.reshape(lax.axis_index("c"), (1,)).astype(jnp.int32)

    def body(core_ref, p_ref, r_ref, o_ref):
        o_ref[...] = (p_ref[...] + r_ref[...]).astype(BF16)

    grid_spec = pltpu.PrefetchScalarGridSpec(
        num_scalar_prefetch=1, grid=(N_CHIP, R // tr),
        in_specs=[pl.BlockSpec((None, None, tr, C), lambda q, i, core_ref: (q, core_ref[0], i, 0)),
                  pl.BlockSpec((None, tr, C), lambda q, i, core_ref: (q, i, 0))],
        out_specs=pl.BlockSpec((None, tr, C), lambda q, i, core_ref: (q, i, 0)))
    return _pallas(body, name="pair_add", grid_spec=grid_spec, out_shape=_sds((N_CHIP, R, C), BF16),
                   compiler_params=_params("parallel", "parallel"))(core, part, recv)


def adamw(w, m, v, slabs):
    R, C = w.shape
    tr = _flat_tile(R, C)
    n = slabs.shape[0]

    def body(w_ref, m_ref, v_ref, s_ref, g_ref, d_ref, nm_ref, nv_ref):
        g = s_ref[0].astype(F32)
        for j in range(1, n):
            g = g + s_ref[j].astype(F32)
        m2 = ADAM_B1 * m_ref[...] + (1.0 - ADAM_B1) * g
        v2 = ADAM_B2 * v_ref[...] + (1.0 - ADAM_B2) * (g * g)
        m_hat = m2 / (1.0 - ADAM_B1 ** ADAM_STEP)
        v_hat = v2 / (1.0 - ADAM_B2 ** ADAM_STEP)
        g_ref[...] = g
        d_ref[...] = -ADAM_LR * (m_hat / (jnp.sqrt(v_hat) + ADAM_EPS) + ADAM_WD * w_ref[...])
        nm_ref[...] = m2
        nv_ref[...] = v2

    spec = pl.BlockSpec((tr, C), lambda i: (i, 0))
    return _pallas(body, name="adamw", grid=(R // tr,),
                   in_specs=[spec] * 3 + [pl.BlockSpec((n, tr, C), lambda i: (0, i, 0))], out_specs=[spec] * 4,
                   out_shape=[_sds((R, C))] * 4, compiler_params=_params("parallel"))(w, m, v, slabs)


def _unshard_cols(g):
    return jnp.transpose(g, (1, 0, 2)).reshape(g.shape[1], -1)


def _unshard_rows(g):
    return g.reshape(-1, g.shape[2])


def _shard_cols(full):
    R, C = full.shape
    return jnp.transpose(full.reshape(R, N_DEV, C // N_DEV), (1, 0, 2)).reshape(N_CHIP, 2, R, C // N_DEV)


def _shard_rows(full):
    R, C = full.shape
    return full.reshape(N_CHIP, 2, R // N_DEV, C)


WEIGHTS = ['ln_ffn1_pre', 'ln_ffn1_post', 'ffn1_gate', 'ffn1_up', 'ffn1_down', 'ln_mix_pre', 'ln_mix_post', 'w_in',
           'rwkv_mu', 'rwkv_w0', 'rwkv_w2', 'rwkv_a0', 'rwkv_a2', 'rwkv_g2', 'rwkv_k_k', 'rwkv_k_a', 'rwkv_r_k',
           'rwkv_gn_w', 'rwkv_gn_b', 'w_proj_a', 'pool_w', 'pool_scale', 'w_proj_b', 'w_out', 'ln_ffn2_pre',
           'ln_ffn2_post', 'ffn2_gate', 'ffn2_up', 'ffn2_down']
COL_SHARDED = ['ffn1_gate', 'ffn1_up', 'ffn2_gate', 'ffn2_up', 'w_in', 'rwkv_w2', 'rwkv_a2', 'rwkv_g2', 'w_proj_a',
               'w_proj_b']
ROW_SHARDED = ['ffn1_down', 'ffn2_down', 'w_out', 'pool_w']
SHARDED = COL_SHARDED + ROW_SHARDED
REPLICATED = [n for n in WEIGHTS if n not in SHARDED]


def _step(args):
    wts = {n: args[n] if args[n].ndim == 2 else args[n][0] for n in WEIGHTS}
    x, target = args["x"][0], args["loss_target"][0]
    T, D = x.shape
    W = wts["rwkv_w0"].shape[1]
    PW = wts["pool_scale"].shape[1]
    LW, LA, LG = wts["rwkv_w2"].shape[0], wts["rwkv_a2"].shape[0], wts["rwkv_g2"].shape[0]
    lat = LW + LA + LG
    latp = _round_up(lat, LAT_ALIGN)
    rc = 3 * W + lat
    base = 3 * W + PW + 2 * D
    n_groups, gshard, gd = wts["pool_w"].shape

    pool_w_shard = wts["pool_w"].reshape(n_groups * gshard, gd)
    shards = {n: (pool_w_shard if n == "pool_w" else wts[n]).astype(BF16) for n in SHARDED}
    gathered = dict(zip(SHARDED, all_gather([shards[n] for n in SHARDED])))
    full = {n: _unshard_cols(gathered[n]) for n in COL_SHARDED}
    full.update({n: _unshard_rows(gathered[n]) for n in ('ffn1_down', 'ffn2_down', 'w_out')})
    pool_w = jnp.transpose(gathered["pool_w"].reshape(N_DEV, n_groups, gshard, gd), (1, 0, 2, 3)).reshape(n_groups, gd, gd)
    w_in = full["w_in"]
    w_in_p = jnp.concatenate([w_in[:, :3 * W], w_in[:, rc:], w_in[:, 3 * W:rc], jnp.zeros((D, latp - lat), BF16)], axis=1)

    def pad_rows(m, at):
        return jnp.zeros((latp, W), BF16).at[at:at + m.shape[0]].set(m)

    w2p, a2p, g2p = pad_rows(full["rwkv_w2"], 0), pad_rows(full["rwkv_a2"], LW), pad_rows(full["rwkv_g2"], LW + LA)
    mu = wts["rwkv_mu"]
    mu_rkv = mu[:, :3 * W]
    mu_lat = jnp.concatenate([mu[:, 3 * W:], jnp.zeros((1, latp - lat), F32)], axis=1)
    rk = wts["rwkv_r_k"].reshape(1, W)
    small = [mu_rkv, mu_lat, wts["rwkv_w0"], wts["rwkv_a0"], wts["rwkv_k_k"], wts["rwkv_k_a"], w2p, a2p, g2p]

    def ffn_fwd(tag, n_in, gate, up, down):
        g, u, act = _mm(tag + "_up", [n_in, n_in], [gate, up], "nn", [BF16] * 3, epilogue=_swiglu_fwd_epi)
        return g, u, act, _mm(tag + "_down", [act], [down], "nn", [F32])[0]

    n1 = rms_pre(x, wts["ln_ffn1_pre"])
    g1, u1, act1, f1 = ffn_fwd("ffn1", n1, full["ffn1_gate"], full["ffn1_up"], full["ffn1_down"])
    h1, nm = post_pre(x, f1, wts["ln_ffn1_post"], wts["ln_mix_pre"], MACARON)
    p = _mm("in_proj", [nm], [w_in_p], "nn", [F32])[0]
    cols = {"rkv": (p, 3 * W, 0), "pool": (p, PW, 3 * W // PW), "lat": (p, latp, base // latp)}
    r, decay, kmod, v, aneg, bpos, gate = rwkv_prep(p, cols, *small)
    y, ckpt = wkv_fwd(r, decay, kmod, v, aneg, bpos)
    ya_in = rwkv_post(y, r, kmod, v, gate, wts["rwkv_gn_w"], wts["rwkv_gn_b"], rk)
    yb_in = pool_fwd(cols, pool_w, wts["pool_scale"])
    gates = [(p, 3 * W + PW), (p, 3 * W + PW + D)]
    m, ya, yb = _mm("mix", [ya_in, yb_in], [full["w_proj_a"], full["w_proj_b"]], "nn", [BF16] * 3,
                    extras=gates, epilogue=_mix_fwd_epi)
    mx = _mm("out_proj", [m], [full["w_out"]], "nn", [F32])[0]
    h2, n2 = post_pre(h1, mx, wts["ln_mix_post"], wts["ln_ffn2_pre"], 1.0)
    g2_, u2, act2, f2 = ffn_fwd("ffn2", n2, full["ffn2_gate"], full["ffn2_up"], full["ffn2_down"])

    grads = {}
    dh3, df2, loss_part, grads["ln_ffn2_post"] = loss_post_bwd(h2, f2, wts["ln_ffn2_post"], target, MACARON)

    def ffn_bwd(tag, df, n_in, g, u, act, gate, up, down):
        dg, du = _mm(tag + "_dact", [df], [down], "nt", [BF16] * 2, extras=[(g, 0), (u, 0)], epilogue=_swiglu_bwd_epi)
        grads[tag + "_down"] = _mm(tag + "_ddown", [act], [df], "tn", [F32])[0]
        grads[tag + "_gate"], grads[tag + "_up"] = _mm(tag + "_dup", [n_in, n_in], [dg, du], "tn", [F32] * 2)
        return _mm(tag + "_dn", [dg, du], [gate, up], "nt", [F32], sum_pairs=True)[0]

    dn2 = ffn_bwd("ffn2", df2, n2, g2_, u2, act2, full["ffn2_gate"], full["ffn2_up"], full["ffn2_down"])
    dh2, grads["ln_ffn2_pre"] = pre_bwd(dn2, h2, wts["ln_ffn2_pre"], dh3)
    dmx, grads["ln_mix_post"] = post_bwd(dh2, mx, wts["ln_mix_post"], 1.0)
    dya, dyb, dga, dgb = _mm("dmix", [dmx], [full["w_out"]], "nt", [BF16] * 4,
                             extras=gates + [(ya, 0), (yb, 0)], epilogue=_mix_bwd_epi)
    grads["w_out"] = _mm("dw_out", [m], [dmx], "tn", [F32])[0]
    dya_in = _mm("dproj_a", [dya], [full["w_proj_a"]], "nt", [F32])[0]
    dyb_in = _mm("dproj_b", [dyb], [full["w_proj_b"]], "nt", [F32])[0]
    grads["w_proj_a"] = _mm("dw_proj_a", [ya_in], [dya], "tn", [F32])[0]
    grads["w_proj_b"] = _mm("dw_proj_b", [yb_in], [dyb], "tn", [F32])[0]
    dz_pool, dpool_w, grads["pool_scale"] = pool_bwd(cols, dyb_in, pool_w, wts["pool_scale"])
    dy, dr_x, dk_x, dv_x, dgate, grads["rwkv_gn_w"], grads["rwkv_gn_b"], drk = rwkv_post_bwd(
        dya_in, y, r, kmod, v, gate, wts["rwkv_gn_w"], wts["rwkv_gn_b"], rk)
    grads["rwkv_r_k"] = drk.reshape(wts["rwkv_r_k"].shape)
    dr_s, ddecay, dk_s, dv_s, dneg, dbpos = wkv_bwd(r, decay, kmod, v, aneg, bpos, dy, ckpt)
    (dzs, dzls, grads["rwkv_k_a"], grads["rwkv_k_k"], grads["rwkv_a0"], grads["rwkv_w0"], da2p, dw2p, dg2p) = rwkv_prep_bwd(
        p, cols, [dr_s, dr_x, ddecay, dk_s, dk_x, dv_s, dv_x, dneg, dbpos, dgate], *small)
    grads["rwkv_w2"], grads["rwkv_a2"], grads["rwkv_g2"] = dw2p[:LW], da2p[LW:LW + LA], dg2p[LW + LA:lat]
    dz_rkv, dz_lat, dmu_rkv, dmu_lat = shift_bwd(cols, dzs, dzls, mu_rkv, mu_lat)
    grads["rwkv_mu"] = jnp.concatenate([dmu_rkv, dmu_lat[:, :lat]], axis=1)
    dp = jnp.concatenate([dz_rkv, dz_pool, dga, dgb, dz_lat], axis=1)
    dnm = _mm("din_proj", [dp], [w_in_p], "nt", [F32])[0]
    dw_in_p = _mm("dw_in", [nm], [dp], "tn", [F32])[0]
    grads["w_in"] = jnp.concatenate([dw_in_p[:, :3 * W], dw_in_p[:, base:base + lat], dw_in_p[:, 3 * W:base]], axis=1)
    dh1, grads["ln_mix_pre"] = pre_bwd(dnm, h1, wts["ln_mix_pre"], dh2)
    df1, grads["ln_ffn1_post"] = post_bwd(dh1, f1, wts["ln_ffn1_post"], MACARON)
    dn1 = ffn_bwd("ffn1", df1, n1, g1, u1, act1, full["ffn1_gate"], full["ffn1_up"], full["ffn1_down"])
    grad_x, grads["ln_ffn1_pre"] = pre_bwd(dn1, x, wts["ln_ffn1_pre"], dh1)

    parts = {n: _shard_cols(grads[n]) for n in COL_SHARDED}
    parts.update({n: _shard_rows(grads[n]) for n in ('ffn1_down', 'ffn2_down', 'w_out')})
    parts["pool_w"] = jnp.transpose(dpool_w.reshape(n_groups, N_DEV, gshard, gd), (1, 0, 2, 3)).reshape(
        N_CHIP, 2, n_groups * gshard, gd)
    from_sibling = exchange_sibling([parts[n] for n in SHARDED])
    pair_sums = [pair_add(parts[n], rcv) for n, rcv in zip(SHARDED, from_sibling)]
    slabs = dict(zip(SHARDED, exchange_chips(pair_sums)))

    flat = jnp.concatenate([grads[n].reshape(-1) for n in REPLICATED])
    n_small = flat.shape[0]
    rows = _round_up(n_small, 8 * LANES) // LANES
    flat = jnp.concatenate([flat, jnp.zeros((rows * LANES - n_small,), F32)]).reshape(rows, LANES)
    small_slabs = all_gather([flat])[0]

    def packed(prefix):
        vals = jnp.concatenate([args[prefix + n].reshape(-1) for n in REPLICATED])
        return jnp.concatenate([vals, jnp.ones((rows * LANES - n_small,), F32)]).reshape(rows, LANES)

    outs = {}
    small_out = adamw(packed(""), packed("m_"), packed("v_"), small_slabs)
    offset = 0
    for n in REPLICATED:
        size = args[n].size
        outs[n] = [o.reshape(-1)[offset:offset + size].reshape(args[n].shape) for o in small_out]
        offset += size
    for n in SHARDED:
        shard2d = slabs[n].shape[1:]
        res = adamw(*[args[pre + n].reshape(shard2d) for pre in ("", "m_", "v_")], slabs[n])
        outs[n] = [o.reshape(args[n].shape) for o in res]

    loss = lax.psum(loss_part[0, 0], ("x", "y", "c"))
    return (loss, grad_x[None], *[outs[n][0] for n in WEIGHTS], *[outs[n][1] for n in WEIGHTS],
            *[outs[n][2] for n in WEIGHTS], *[outs[n][3] for n in WEIGHTS])


ARG_NAMES = ["x"] + WEIGHTS + ["loss_target"] + ["m_" + n for n in WEIGHTS] + ["v_" + n for n in WEIGHTS]


def kernel(x, ln_ffn1_pre, ln_ffn1_post, ffn1_gate, ffn1_up, ffn1_down, ln_mix_pre, ln_mix_post, w_in, rwkv_mu, rwkv_w0,
           rwkv_w2, rwkv_a0, rwkv_a2, rwkv_g2, rwkv_k_k, rwkv_k_a, rwkv_r_k, rwkv_gn_w, rwkv_gn_b, w_proj_a, pool_w,
           pool_scale, w_proj_b, w_out, ln_ffn2_pre, ln_ffn2_post, ffn2_gate, ffn2_up, ffn2_down, loss_target,
           m_ln_ffn1_pre, m_ln_ffn1_post, m_ffn1_gate, m_ffn1_up, m_ffn1_down, m_ln_mix_pre, m_ln_mix_post, m_w_in,
           m_rwkv_mu, m_rwkv_w0, m_rwkv_w2, m_rwkv_a0, m_rwkv_a2, m_rwkv_g2, m_rwkv_k_k, m_rwkv_k_a, m_rwkv_r_k,
           m_rwkv_gn_w, m_rwkv_gn_b, m_w_proj_a, m_pool_w, m_pool_scale, m_w_proj_b, m_w_out, m_ln_ffn2_pre,
           m_ln_ffn2_post, m_ffn2_gate, m_ffn2_up, m_ffn2_down, v_ln_ffn1_pre, v_ln_ffn1_post, v_ffn1_gate, v_ffn1_up,
           v_ffn1_down, v_ln_mix_pre, v_ln_mix_post, v_w_in, v_rwkv_mu, v_rwkv_w0, v_rwkv_w2, v_rwkv_a0, v_rwkv_a2,
           v_rwkv_g2, v_rwkv_k_k, v_rwkv_k_a, v_rwkv_r_k, v_rwkv_gn_w, v_rwkv_gn_b, v_w_proj_a, v_pool_w, v_pool_scale,
           v_w_proj_b, v_w_out, v_ln_ffn2_pre, v_ln_ffn2_post, v_ffn2_gate, v_ffn2_up, v_ffn2_down):
    given = locals()
    return _step({n: given[n] for n in ARG_NAMES})
```

```python
import jax
import jax.numpy as jnp
from jax import lax
from jax.experimental import pallas as pl
from jax.experimental.pallas import tpu as pltpu

F32, BF16 = jnp.float32, jnp.bfloat16
N_DEV = 8
N_CHIP = 4
HEAD = 64
LANES = 2 * HEAD
NORM_EPS, GN_EPS, L2_EPS = 1e-6, 64e-5, 1e-12
POOL_WINDOWS = (2, 4, 8, 16)
POOL_HALO = 16
MACARON = 0.5
ADAM_LR, ADAM_B1, ADAM_B2, ADAM_EPS, ADAM_WD, ADAM_STEP = 0.001, 0.9, 0.999, 1e-08, 0.01, 10
VMEM_LIMIT = 48 * 1024 * 1024
MM_VMEM_BUDGET = 36 * 1024 * 1024
ROW_TILE = 256
RWKV_ROW_TILE = 128
LAT_ALIGN = 512
WKV_CHUNK, WKV_PAIRS = 16, 8
WKV_UNROLL = 4
MESH = pl.DeviceIdType.MESH


def _pallas(body, **kw):
    return pl.pallas_call(body, **kw)


def _params(*sem):
    return pltpu.CompilerParams(dimension_semantics=sem, vmem_limit_bytes=VMEM_LIMIT)


def _tile(n, target, align=128):
    best = None
    for d in range(align, min(n, target) + 1, align):
        if n % d == 0:
            best = d
    return best if best is not None else n


def _round_up(n, m):
    return (n + m - 1) // m * m


def _mm(name, a_list, b_list, mode, out_dtypes, *, sum_pairs=False, extras=(), epilogue=None, tm=1024, tn=512):
    n_a, n_b = len(a_list), len(b_list)
    assert n_a in (1, n_b)
    a0, b0 = a_list[0], b_list[0]
    if mode == "nn":
        (M, K), N = a0.shape, b0.shape[1]
    elif mode == "nt":
        (M, K), N = a0.shape, b0.shape[0]
    else:
        (K, M), N = a0.shape, b0.shape[1]
    tm, tn = _tile(M, tm), _tile(N, tn)
    n_acc = 1 if sum_pairs else n_b
    n_ex = len(extras)

    def planned(tk):
        operands = 2 * 2 * tk * (n_a * tm + n_b * tn)
        tiles = 2 * tm * tn * (sum(e.dtype.itemsize for e, _ in extras) + sum(jnp.dtype(d).itemsize for d in out_dtypes))
        return operands + tiles + 4 * tm * tn * (n_acc + len(out_dtypes))

    tk = max([d for d in range(128, K + 1, 128) if K % d == 0 and planned(d) <= MM_VMEM_BUDGET] or [_tile(K, 512)])
    nk = K // tk
    if mode == "tn":
        a_spec = pl.BlockSpec((tk, tm), lambda i, j, k: (k, i))
    else:
        a_spec = pl.BlockSpec((tm, tk), lambda i, j, k: (i, k))
    if mode == "nt":
        b_spec = pl.BlockSpec((tn, tk), lambda i, j, k: (j, k))
    else:
        b_spec = pl.BlockSpec((tk, tn), lambda i, j, k: (k, j))
    contract = {"nn": ((1,), (0,)), "nt": ((1,), (1,)), "tn": ((0,), (0,))}[mode]
    e_specs = []
    for _, col in extras:
        assert col % tn == 0
        e_specs.append(pl.BlockSpec((tm, tn), lambda i, j, k, off=col // tn: (i, j + off)))
    o_spec = pl.BlockSpec((tm, tn), lambda i, j, k: (i, j))

    def body(*refs):
        a_refs, b_refs = refs[:n_a], refs[n_a:n_a + n_b]
        e_refs = refs[n_a + n_b:n_a + n_b + n_ex]
        o_refs = refs[n_a + n_b + n_ex:n_a + n_b + n_ex + len(out_dtypes)]
        acc_refs = refs[n_a + n_b + n_ex + len(out_dtypes):]

        def products():
            a_vals = [a[...] for a in a_refs]
            prods = [lax.dot_general(a_vals[p if n_a > 1 else 0], b_refs[p][...], (contract, ((), ())),
                                     preferred_element_type=F32) for p in range(n_b)]
            return [sum(prods[1:], prods[0])] if sum_pairs else prods

        def finish(results):
            outs = epilogue(results, [e[...] for e in e_refs]) if epilogue else results
            for o_ref, o in zip(o_refs, outs):
                o_ref[...] = o.astype(o_ref.dtype)

        if nk == 1:
            finish(products())
            return
        k = pl.program_id(2)

        @pl.when(k == 0)
        def _():
            for acc in acc_refs:
                acc[...] = jnp.zeros_like(acc)

        for acc, prod in zip(acc_refs, products()):
            acc[...] += prod

        @pl.when(k == nk - 1)
        def _():
            finish([acc[...] for acc in acc_refs])

    return _pallas(
        body, name=name, grid=(M // tm, N // tn, nk),
        in_specs=[a_spec] * n_a + [b_spec] * n_b + e_specs,
        out_specs=[o_spec] * len(out_dtypes),
        out_shape=[jax.ShapeDtypeStruct((M, N), dt) for dt in out_dtypes],
        scratch_shapes=[pltpu.VMEM((tm, tn), F32)] * (n_acc if nk > 1 else 0),
        compiler_params=_params("parallel", "parallel", "arbitrary"),
    )(*a_list, *b_list, *[e for e, _ in extras])


def _swiglu_fwd_epi(accs, _):
    g, u = accs
    return [g, u, g * jax.nn.sigmoid(g) * u]


def _swiglu_bwd_epi(accs, ex):
    dact = accs[0]
    g, u = ex[0].astype(F32), ex[1].astype(F32)
    sg = jax.nn.sigmoid(g)
    return [dact * u * (sg * (1.0 + g * (1.0 - sg))), dact * (g * sg)]


def _mix_fwd_epi(accs, ex):
    ya, yb = accs
    return [jax.nn.sigmoid(ex[0]) * ya + jax.nn.sigmoid(ex[1]) * yb, ya, yb]


def _mix_bwd_epi(accs, ex):
    dm = accs[0]
    sa, sb = jax.nn.sigmoid(ex[0]), jax.nn.sigmoid(ex[1])
    ya, yb = ex[2].astype(F32), ex[3].astype(F32)
    return [dm * sa, dm * sb, dm * ya * sa * (1.0 - sa), dm * yb * sb * (1.0 - sb)]


def _row_call(name, body, T, tiled, params, outs, accs=(), prev=(), nxt=(), halo=8, tile=ROW_TILE):
    tm = min(tile, T)
    n_tiles = T // tm

    def norm(e):
        return e if isinstance(e, tuple) else (e, e.shape[1], 0)

    tiled, prev, nxt = [norm(e) for e in tiled], [norm(e) for e in prev], [norm(e) for e in nxt]
    per_halo, n_halo = tm // halo, T // halo
    in_specs = [pl.BlockSpec((tm, w), lambda i, cb=cb: (i, cb)) for _, w, cb in tiled]
    in_specs += [pl.BlockSpec((halo, w), lambda i, cb=cb: (jnp.maximum(i * per_halo - 1, 0), cb)) for _, w, cb in prev]
    in_specs += [pl.BlockSpec((halo, w), lambda i, cb=cb: (jnp.minimum((i + 1) * per_halo, n_halo - 1), cb))
                 for _, w, cb in nxt]
    in_specs += [pl.BlockSpec(p.shape, lambda i, nd=p.ndim: (0,) * nd) for p in params]
    out_specs = [pl.BlockSpec((tm, o.shape[1]), lambda i: (i, 0)) for o in outs]
    out_specs += [pl.BlockSpec(a.shape, lambda i, nd=len(a.shape): (0,) * nd) for a in accs]
    n1, n2, n3, n4, n5 = len(tiled), len(prev), len(nxt), len(params), len(outs)

    def kernel_body(*refs):
        i = pl.program_id(0)
        acc_refs = refs[n1 + n2 + n3 + n4 + n5:]

        @pl.when(i == 0)
        def _():
            for a in acc_refs:
                a[...] = jnp.zeros_like(a)

        body(i, n_tiles, refs[:n1], refs[n1:n1 + n2], refs[n1 + n2:n1 + n2 + n3],
             refs[n1 + n2 + n3:n1 + n2 + n3 + n4], refs[n1 + n2 + n3 + n4:n1 + n2 + n3 + n4 + n5], acc_refs)

    return _pallas(
        kernel_body, name=name, grid=(n_tiles,), in_specs=in_specs, out_specs=out_specs,
        out_shape=list(outs) + list(accs),
        compiler_params=_params("arbitrary"),
    )(*[e[0] for e in tiled + prev + nxt], *params)


def _sds(shape, dtype=F32):
    return jax.ShapeDtypeStruct(tuple(shape), dtype)


def _rstd(x):
    return lax.rsqrt(jnp.mean(x * x, axis=-1, keepdims=True) + NORM_EPS)


def _colsum(x):
    return jnp.sum(x, axis=0, keepdims=True)


def rms_pre(x, g):
    T, D = x.shape

    def body(i, n, tiled, prev, nxt, params, outs, accs):
        xv = tiled[0][...]
        outs[0][...] = (xv * _rstd(xv) * params[0][...]).astype(BF16)

    return _row_call("rms_pre", body, T, [x], [g], [_sds((T, D), BF16)])[0]


def post_pre(h, f, g_post, g_pre, scale):
    T, D = h.shape

    def body(i, n, tiled, prev, nxt, params, outs, accs):
        hv, fv = tiled[0][...], tiled[1][...]
        h2 = hv + scale * (fv * _rstd(fv) * params[0][...])
        outs[0][...] = h2
        outs[1][...] = (h2 * _rstd(h2) * params[1][...]).astype(BF16)

    return _row_call("post_pre", body, T, [h, f], [g_post, g_pre], [_sds((T, D)), _sds((T, D), BF16)])


def _post_bwd_math(dh, fv, g, scale):
    r = _rstd(fv)
    fhat = fv * r
    dy = scale * dh
    z = dy * g
    df = r * (z - fhat * jnp.mean(z * fhat, axis=-1, keepdims=True))
    return df, _colsum(dy * fhat)


def loss_post_bwd(h, f, g_post, target, scale):
    T, D = h.shape

    def body(i, n, tiled, prev, nxt, params, outs, accs):
        hv, fv, tv = tiled[0][...], tiled[1][...], tiled[2][...]
        g = params[0][...]
        e = hv + scale * (fv * _rstd(fv) * g) - tv
        accs[0][...] += jnp.full(accs[0].shape, 0.5 / D, F32) * jnp.sum(e * e)
        dh = e * (1.0 / D)
        outs[0][...] = dh
        df, dg = _post_bwd_math(dh, fv, g, scale)
        outs[1][...] = df.astype(BF16)
        accs[1][...] += dg

    return _row_call("loss_post_bwd", body, T, [h, f, target], [g_post],
                     [_sds((T, D)), _sds((T, D), BF16)], [_sds((1, LANES)), _sds((1, D))])


def post_bwd(dh, f, g_post, scale):
    T, D = dh.shape

    def body(i, n, tiled, prev, nxt, params, outs, accs):
        df, dg = _post_bwd_math(tiled[0][...], tiled[1][...], params[0][...], scale)
        outs[0][...] = df.astype(BF16)
        accs[0][...] += dg

    return _row_call("post_bwd", body, T, [dh, f], [g_post], [_sds((T, D), BF16)], [_sds((1, D))])


def pre_bwd(dn, h, g_pre, dres):
    T, D = h.shape

    def body(i, n, tiled, prev, nxt, params, outs, accs):
        dnv, hv = tiled[0][...], tiled[1][...]
        r = _rstd(hv)
        hhat = hv * r
        z = dnv * params[0][...]
        outs[0][...] = tiled[2][...] + r * (z - hhat * jnp.mean(z * hhat, axis=-1, keepdims=True))
        accs[0][...] += _colsum(dnv * hhat)

    return _row_call("pre_bwd", body, T, [dn, h, dres], [g_pre], [_sds((T, D))], [_sds((1, D))])


def _head_ones():
    i = lax.broadcasted_iota(jnp.int32, (LANES, LANES), 0)
    j = lax.broadcasted_iota(jnp.int32, (LANES, LANES), 1)
    return jnp.where((i < HEAD) == (j < HEAD), 1.0, 0.0).astype(F32)


def _headsum(x):
    e = _head_ones()
    parts = [jnp.dot(x[:, s:s + LANES], e, precision=lax.Precision.HIGHEST, preferred_element_type=F32)
             for s in range(0, x.shape[1], LANES)]
    return parts[0] if len(parts) == 1 else jnp.concatenate(parts, axis=1)


def _shift_down(x, before):
    row = lax.broadcasted_iota(jnp.int32, x.shape, 0)
    return jnp.where(row == 0, before, pltpu.roll(x, 1, 0))


def _shift_up(x, after):
    row = lax.broadcasted_iota(jnp.int32, x.shape, 0)
    return jnp.where(row == x.shape[0] - 1, after, pltpu.roll(x, x.shape[0] - 1, 0))


def _last_row(ref, keep):
    r = ref[ref.shape[0] - 1:ref.shape[0], :]
    return jnp.where(keep, r, jnp.zeros_like(r))


def _first_row(ref, keep):
    r = ref[0:1, :]
    return jnp.where(keep, r, jnp.zeros_like(r))


def _softplus(u):
    return jnp.maximum(u, 0.0) + jnp.log(1.0 + jnp.exp(-jnp.abs(u)))


def _dotb(a, b, contract):
    return lax.dot_general(a.astype(BF16), b.astype(BF16), (contract, ((), ())), preferred_element_type=F32)


_NN, _NT, _TN = ((1,), (0,)), ((1,), (1,)), ((0,), (0,))


def _prep_forward(z, zprev_row, zl, zlprev_row, mu, mul, w0, a0, kk_w, ka_w, w2p, a2p, g2p):
    W = w0.shape[1]
    zs = z + (_shift_down(z, zprev_row) - z) * mu
    zls = zl + (_shift_down(zl, zlprev_row) - zl) * mul
    r, k, v = zs[:, :W], zs[:, W:2 * W], zs[:, 2 * W:]
    th, sg = jnp.tanh(zls), jax.nn.sigmoid(zls)
    xw = w0 + _dotb(th, w2p, _NN)
    wlog = -_softplus(-xw) - 0.5
    ew = jnp.exp(wlog)
    decay = jnp.exp(-ew)
    a = jax.nn.sigmoid(a0 + _dotb(zls, a2p, _NN))
    gate = _dotb(sg, g2p, _NN)
    q = k * kk_w
    nrm = jnp.sqrt(_headsum(q * q))
    den = jnp.maximum(nrm, L2_EPS)
    kk = q / den
    kmod = k * (1.0 + (a - 1.0) * ka_w)
    return dict(zs=zs, zls=zls, r=r, k=k, v=v, th=th, sg=sg, xw=xw, ew=ew, decay=decay, a=a, gate=gate,
                nrm=nrm, den=den, kk=kk, kmod=kmod)


def rwkv_prep(p, cols, mu, mul, w0, a0, kk_w, ka_w, w2p, a2p, g2p):
    T = p.shape[0]
    W = w0.shape[1]

    def body(i, n, tiled, prev, nxt, params, outs, accs):
        c = _prep_forward(tiled[0][...], _last_row(prev[0], i > 0), tiled[1][...], _last_row(prev[1], i > 0),
                          *[q[...] for q in params])
        for o, val in zip(outs, (c["r"], c["decay"], c["kmod"], c["v"], -c["kk"], c["kk"] * c["a"], c["gate"])):
            o[...] = val

    return _row_call("rwkv_prep", body, T, [cols["rkv"], cols["lat"]],
                     [mu, mul, w0, a0, kk_w, ka_w, w2p, a2p, g2p], [_sds((T, W))] * 7,
                     prev=[cols["rkv"], cols["lat"]], tile=RWKV_ROW_TILE)


def _post_forward(y, r, kmod, v, gn_w, gn_b, rk):
    mean = _headsum(y) * (1.0 / HEAD)
    yc = y - mean
    rstd = lax.rsqrt(_headsum(yc * yc) * (1.0 / HEAD) + GN_EPS)
    yn = yc * rstd
    s = _headsum(r * kmod * rk)
    return yn, rstd, s, yn * gn_w + gn_b + s * v


def rwkv_post(y, r, kmod, v, gate, gn_w, gn_b, rk):
    T, W = y.shape

    def body(i, n, tiled, prev, nxt, params, outs, accs):
        yv, rv, kv, vv, gv = [t[...] for t in tiled]
        _, _, _, o = _post_forward(yv, rv, kv, vv, *[q[...] for q in params])
        outs[0][...] = (o * gv).astype(BF16)

    return _row_call("rwkv_post", body, T, [y, r, kmod, v, gate], [gn_w, gn_b, rk], [_sds((T, W), BF16)],
                     tile=RWKV_ROW_TILE)[0]


def rwkv_post_bwd(dout, y, r, kmod, v, gate, gn_w, gn_b, rk):
    T, W = y.shape

    def body(i, n, tiled, prev, nxt, params, outs, accs):
        dv_, yv, rv, kv, vv, gv = [t[...] for t in tiled]
        gn_w_, gn_b_, rk_ = [q[...] for q in params]
        yn, rstd, s, o = _post_forward(yv, rv, kv, vv, gn_w_, gn_b_, rk_)
        do = dv_ * gv
        outs[4][...] = dv_ * o
        accs[0][...] += _colsum(do * yn)
        accs[1][...] += _colsum(do)
        dyn = do * gn_w_
        outs[0][...] = rstd * (dyn - _headsum(dyn) * (1.0 / HEAD) - yn * (_headsum(dyn * yn) * (1.0 / HEAD)))
        ds = _headsum(do * vv)
        outs[1][...] = ds * kv * rk_
        outs[2][...] = ds * rv * rk_
        outs[3][...] = do * s
        accs[2][...] += _colsum(ds * rv * kv)

    return _row_call("rwkv_post_bwd", body, T, [dout, y, r, kmod, v, gate], [gn_w, gn_b, rk],
                     [_sds((T, W))] * 5, [_sds((1, W))] * 3, tile=RWKV_ROW_TILE)


def rwkv_prep_bwd(p, cols, grads, mu, mul, w0, a0, kk_w, ka_w, w2p, a2p, g2p):
    T = p.shape[0]
    W = w0.shape[1]
    latp = w2p.shape[0]

    def body(i, n, tiled, prev, nxt, params, outs, accs):
        pv = [q[...] for q in params]
        mu_, mul_, w0_, a0_, kk_w_, ka_w_, w2p_, a2p_, g2p_ = pv
        c = _prep_forward(tiled[0][...], _last_row(prev[0], i > 0), tiled[1][...], _last_row(prev[1], i > 0), *pv)
        dr_s, dr_x, ddecay, dk_s, dk_x, dv_s, dv_x, dneg, db, dgate = [t[...] for t in tiled[2:]]
        k, a, kk = c["k"], c["a"], c["kk"]
        dkmod = dk_s + dk_x
        dk = dkmod * (1.0 + (a - 1.0) * ka_w_)
        da = dkmod * k * ka_w_ + db * kk
        accs[0][...] += _colsum(dkmod * k * (a - 1.0))
        dkk = db * a - dneg
        dq = jnp.where(c["nrm"] > L2_EPS, dkk - kk * _headsum(dkk * kk), dkk) / c["den"]
        dk = dk + dq * kk_w_
        accs[1][...] += _colsum(dq * k)
        dxa = da * a * (1.0 - a)
        accs[2][...] += _colsum(dxa)
        accs[4][...] += _dotb(c["zls"], dxa, _TN)
        dzls = _dotb(dxa, a2p_, _NT)
        dxw = (-ddecay * c["decay"] * c["ew"]) * jax.nn.sigmoid(-c["xw"])
        accs[3][...] += _colsum(dxw)
        accs[5][...] += _dotb(c["th"], dxw, _TN)
        dzls = dzls + _dotb(dxw, w2p_, _NT) * (1.0 - c["th"] * c["th"])
        accs[6][...] += _dotb(c["sg"], dgate, _TN)
        dzls = dzls + _dotb(dgate, g2p_, _NT) * c["sg"] * (1.0 - c["sg"])
        outs[0][...] = jnp.concatenate([dr_s + dr_x, dk, dv_s + dv_x], axis=1)
        outs[1][...] = dzls

    return _row_call("rwkv_prep_bwd", body, T, [cols["rkv"], cols["lat"]] + list(grads),
                     [mu, mul, w0, a0, kk_w, ka_w, w2p, a2p, g2p], [_sds((T, 3 * W)), _sds((T, latp))],
                     [_sds((1, W))] * 4 + [_sds((latp, W))] * 3, prev=[cols["rkv"], cols["lat"]], tile=RWKV_ROW_TILE)


def shift_bwd(cols, dzs, dzls, mu, mul):
    T = dzs.shape[0]

    def body(i, n, tiled, prev, nxt, params, outs, accs):
        for j in range(2):
            z, d, m = tiled[j][...], tiled[2 + j][...], params[j][...]
            zprev = _shift_down(z, _last_row(prev[j], i > 0))
            dnext = _shift_up(d, _first_row(nxt[j], i < n - 1))
            outs[j][...] = (d * (1.0 - m) + dnext * m).astype(BF16)
            accs[j][...] += _colsum(d * (zprev - z))

    return _row_call("shift_bwd", body, T, [cols["rkv"], cols["lat"], dzs, dzls], [mu, mul],
                     [_sds(dzs.shape, BF16), _sds(dzls.shape, BF16)], [_sds(mu.shape), _sds(mul.shape)],
                     prev=[cols["rkv"], cols["lat"]], nxt=[dzs, dzls])


def _window_pick(x, windows):
    gid = lax.broadcasted_iota(jnp.int32, x.shape, 1) // (x.shape[1] // len(windows))
    out = windows[-1]
    for g in range(len(windows) - 2, -1, -1):
        out = jnp.where(gid == g, windows[g], out)
    return out


def _pool_counts(t0, rows, width):
    t = (t0 + lax.broadcasted_iota(jnp.int32, (rows, width), 0) + 1).astype(F32)
    return _window_pick(t, [jnp.minimum(t, float(w)) for w in POOL_WINDOWS])


def _pool_mixed(x, before, t0):
    tm, width = x.shape
    xe = jnp.concatenate([before, x], axis=0)
    sums, s, span = [], xe, 1
    for w in POOL_WINDOWS:
        while span < w:
            s = s + pltpu.roll(s, span, 0)
            span *= 2
        sums.append(s[POOL_HALO:, :])
    return _window_pick(x, sums) / _pool_counts(t0, tm, width) - x


def _group_dot(x, w_ref, contract):
    gd = w_ref.shape[-1]
    parts = [_dotb(x[:, g * gd:(g + 1) * gd], w_ref[g], contract) for g in range(w_ref.shape[0])]
    return jnp.concatenate(parts, axis=1)


def pool_fwd(cols, pool_w, pool_scale):
    T, width = cols["pool"][0].shape[0], cols["pool"][1]
    tm = min(ROW_TILE, T)

    def body(i, n, tiled, prev, nxt, params, outs, accs):
        before = jnp.where(i > 0, prev[0][...], 0.0)
        mixed = _pool_mixed(tiled[0][...], before, i * tm)
        outs[0][...] = (_group_dot(mixed, params[0], _NN) * params[1][...]).astype(BF16)

    return _row_call("pool_fwd", body, T, [cols["pool"]], [pool_w, pool_scale], [_sds((T, width), BF16)],
                     prev=[cols["pool"]], halo=POOL_HALO)[0]


def pool_bwd(cols, dout, pool_w, pool_scale):
    T, width = dout.shape
    tm = min(ROW_TILE, T)

    def body(i, n, tiled, prev, nxt, params, outs, accs):
        w_ref, scale = params[0], params[1][...]
        before = jnp.where(i > 0, prev[0][...], 0.0)
        mixed = _pool_mixed(tiled[0][...], before, i * tm)
        dv = tiled[1][...]
        accs[1][...] += _colsum(dv * _group_dot(mixed, w_ref, _NN))
        after = jnp.where(i < n - 1, nxt[0][...], 0.0)
        dys = jnp.concatenate([dv, after], axis=0) * scale
        gd = w_ref.shape[-1]
        for g in range(w_ref.shape[0]):
            accs[0][g] += _dotb(mixed[:, g * gd:(g + 1) * gd], dys[:tm, g * gd:(g + 1) * gd], _TN)
        dmixed = _group_dot(dys, w_ref, _NT)
        u = dmixed / _pool_counts(i * tm, tm + POOL_HALO, width)
        rows = tm + POOL_HALO
        sums, s, span = [], u, 1
        for w in POOL_WINDOWS:
            while span < w:
                s = s + pltpu.roll(s, rows - span, 0)
                span *= 2
            sums.append(s[:tm, :])
        outs[0][...] = (_window_pick(dv, sums) - dmixed[:tm, :]).astype(BF16)

    return _row_call("pool_bwd", body, T, [cols["pool"], dout], [pool_w, pool_scale], [_sds((T, width), BF16)],
                     [_sds(pool_w.shape), _sds((1, width))], prev=[cols["pool"]], nxt=[dout], halo=POOL_HALO)


def _wkv_consts(pairs):
    lane = lax.broadcasted_iota(jnp.int32, (HEAD, LANES), 1)
    sub = lax.broadcasted_iota(jnp.int32, (pairs * HEAD, LANES), 0)
    lane_all = lax.broadcasted_iota(jnp.int32, (pairs * HEAD, LANES), 1)
    i = lax.broadcasted_iota(jnp.int32, (LANES, LANES), 0)
    j = lax.broadcasted_iota(jnp.int32, (LANES, LANES), 1)
    ones = jnp.where((i < HEAD) == (j < HEAD), 1.0, 0.0).astype(BF16)
    diag = jnp.where((lane_all & (HEAD - 1)) == (sub & (HEAD - 1)), 1.0, 0.0).astype(F32)
    return lane < HEAD, diag, ones


def _segsum(p, in_a):
    sa = jnp.sum(jnp.where(in_a, p, 0.0), axis=1, keepdims=True)
    sb = jnp.sum(jnp.where(in_a, 0.0, p), axis=1, keepdims=True)
    return jnp.where(in_a, sa, sb)


def _hi_lo(p):
    hi = lax.bitcast_convert_type(lax.bitcast_convert_type(p, jnp.uint32) & jnp.uint32(0xFFFF0000), F32)
    return hi, p - hi


def _segsum_mxu(p, ones):
    hi, lo = _hi_lo(p)
    return (jnp.dot(hi.astype(BF16), ones, preferred_element_type=F32)
            + jnp.dot(lo.astype(BF16), ones, preferred_element_type=F32))


def _cat(parts, axis):
    return parts[0] if len(parts) == 1 else jnp.concatenate(parts, axis=axis)


def _tile_rows(row, pairs):
    return _cat([jnp.broadcast_to(row[:, p * LANES:(p + 1) * LANES], (HEAD, LANES)) for p in range(pairs)], 0)


def _spread(row, pairs, diag16, ones):
    hi, lo = _hi_lo(row)
    return (jnp.dot(_tile_rows(hi.astype(BF16), pairs) * diag16, ones, preferred_element_type=F32)
            + jnp.dot(_tile_rows(lo.astype(BF16), pairs) * diag16, ones, preferred_element_type=F32))


def _pair_colsums(x, pairs):
    return _cat([_colsum(x[p * HEAD:(p + 1) * HEAD]) for p in range(pairs)], 1)


def wkv_fwd(r, w, k, v, a, b):
    T, W = r.shape
    P = W // LANES
    PB = min(WKV_PAIRS, P)
    chunk = min(WKV_CHUNK, T)
    NC = T // chunk
    R = PB * HEAD

    def body(r_ref, w_ref, k_ref, v_ref, a_ref, b_ref, y_ref, ck_ref, vt_ref, st_ref):
        c = pl.program_id(1)

        @pl.when(c == 0)
        def _():
            st_ref[chunk] = jnp.zeros((R, LANES), F32)

        st_ref[0] = st_ref[chunk]
        ck_ref[0] = st_ref[chunk]
        in_a, diag, ones = _wkv_consts(PB)
        diag16 = diag.astype(BF16)

        def spread(t, _):
            vt_ref[t] = _spread(v_ref[pl.ds(t, 1), :], PB, diag16, ones)
            return 0

        lax.fori_loop(0, chunk, spread, 0, unroll=WKV_UNROLL)

        def step(t, _):
            rows = [ref[pl.ds(t, 1), :] for ref in (w_ref, k_ref, a_ref, b_ref)]
            for p in range(PB):
                wt, kt, at, bt = [x[:, p * LANES:(p + 1) * LANES] for x in rows]
                rs = pl.ds(p * HEAD, HEAD)
                S = st_ref[t, rs]
                st_ref[t + 1, rs] = S * wt + _segsum(S * at, in_a) * bt + vt_ref[t, rs] * kt
            return 0

        lax.fori_loop(0, chunk, step, 0, unroll=WKV_UNROLL)

        def readout(t, _):
            yt = _segsum_mxu(st_ref[t + 1] * _tile_rows(r_ref[pl.ds(t, 1), :], PB), ones) * diag
            y_ref[pl.ds(t, 1), :] = _pair_colsums(yt, PB)
            return 0

        lax.fori_loop(0, chunk, readout, 0, unroll=WKV_UNROLL)

    spec = pl.BlockSpec((chunk, PB * LANES), lambda g, c: (c, g))
    return _pallas(
        body, name="wkv_fwd", grid=(P // PB, NC), in_specs=[spec] * 6,
        out_specs=[spec, pl.BlockSpec((1, R, LANES), lambda g, c: (c, g, 0))],
        out_shape=[_sds((T, W)), _sds((NC, P * HEAD, LANES))],
        scratch_shapes=[pltpu.VMEM((chunk, R, LANES), F32), pltpu.VMEM((chunk + 1, R, LANES), F32)],
        compiler_params=_params("parallel", "arbitrary"),
    )(r, w, k, v, a, b)


def wkv_bwd(r, w, k, v, a, b, dy, ckpt):
    T, W = r.shape
    P = W // LANES
    PB = min(WKV_PAIRS, P)
    chunk = min(WKV_CHUNK, T)
    NC = T // chunk
    R = PB * HEAD

    def body(r_ref, w_ref, k_ref, v_ref, a_ref, b_ref, dy_ref, ck_ref, dr_ref, dw_ref, dk_ref, dv_ref, da_ref, db_ref,
             ds_ref, vt_ref, dyt_ref, st_ref, sa_ref, dst_ref, dsa_ref):
        c = pl.program_id(1)

        @pl.when(c == 0)
        def _():
            ds_ref[...] = jnp.zeros_like(ds_ref)

        in_a, diag, ones = _wkv_consts(PB)
        diag16 = diag.astype(BF16)

        def spread(t, _):
            vt_ref[t] = _spread(v_ref[pl.ds(t, 1), :], PB, diag16, ones)
            dyt_ref[t] = _spread(dy_ref[pl.ds(t, 1), :], PB, diag16, ones)
            return 0

        lax.fori_loop(0, chunk, spread, 0, unroll=WKV_UNROLL)
        st_ref[0] = ck_ref[0]

        def fstep(t, _):
            rows = [ref[pl.ds(t, 1), :] for ref in (w_ref, k_ref, a_ref, b_ref)]
            for p in range(PB):
                wt, kt, at, bt = [x[:, p * LANES:(p + 1) * LANES] for x in rows]
                rs = pl.ds(p * HEAD, HEAD)
                S = st_ref[t, rs]
                sa = _segsum(S * at, in_a)
                sa_ref[t, rs] = sa
                st_ref[t + 1, rs] = S * wt + sa * bt + vt_ref[t, rs] * kt
            return 0

        lax.fori_loop(0, chunk, fstep, 0, unroll=WKV_UNROLL)

        def bstep(n, _):
            t = chunk - 1 - n
            rows = [ref[pl.ds(t, 1), :] for ref in (r_ref, w_ref, a_ref, b_ref)]
            for p in range(PB):
                rt, wt, at, bt = [x[:, p * LANES:(p + 1) * LANES] for x in rows]
                rs = pl.ds(p * HEAD, HEAD)
                dS = ds_ref[rs] + dyt_ref[t, rs] * rt
                dst_ref[t, rs] = dS
                dsa = _segsum(dS * bt, in_a)
                dsa_ref[t, rs] = dsa
                ds_ref[rs] = dS * wt + dsa * at
            return 0

        lax.fori_loop(0, chunk, bstep, 0, unroll=WKV_UNROLL)

        def collect(t, _):
            sp, sn, dS, dsa = st_ref[t], st_ref[t + 1], dst_ref[t], dsa_ref[t]
            dvt = _segsum_mxu(dS * _tile_rows(k_ref[pl.ds(t, 1), :], PB), ones) * diag
            for ref, val in ((dr_ref, sn * dyt_ref[t]), (dw_ref, dS * sp), (dk_ref, dS * vt_ref[t]), (dv_ref, dvt),
                             (da_ref, sp * dsa), (db_ref, dS * sa_ref[t])):
                ref[pl.ds(t, 1), :] = _pair_colsums(val, PB)
            return 0

        lax.fori_loop(0, chunk, collect, 0, unroll=WKV_UNROLL)

    spec = pl.BlockSpec((chunk, PB * LANES), lambda g, c: (NC - 1 - c, g))

    def tiles(n):
        return pltpu.VMEM((n, R, LANES), F32)

    return _pallas(
        body, name="wkv_bwd", grid=(P // PB, NC),
        in_specs=[spec] * 7 + [pl.BlockSpec((1, R, LANES), lambda g, c: (NC - 1 - c, g, 0))],
        out_specs=[spec] * 6, out_shape=[_sds((T, W))] * 6,
        scratch_shapes=[pltpu.VMEM((R, LANES), F32), tiles(chunk), tiles(chunk), tiles(chunk + 1), tiles(chunk),
                        tiles(chunk), tiles(chunk)],
        compiler_params=_params("parallel", "arbitrary"),
    )(r, w, k, v, a, b, dy, ckpt)


def _position():
    return lax.axis_index("x"), lax.axis_index("y"), lax.axis_index("c")


def _other_chips(x, y):
    return [(1 - x, y), (x, 1 - y), (1 - x, 1 - y)]


ANY = pl.BlockSpec(memory_space=pl.ANY)


def all_gather(shards):
    n = len(shards)

    def body(*refs):
        x_refs, out_refs = refs[:n], refs[n:2 * n]
        send_sems, recv_sems, local_sems = refs[2 * n:]
        x, y, c = _position()
        me, sibling = (x, y, c), (x, y, 1 - c)
        chips = _other_chips(x, y)

        def slot(ref, pos):
            return ref.at[4 * pos[0] + 2 * pos[1] + pos[2]]

        def copy(t, j, block, to, src=None):
            dst = slot(out_refs[t], block)
            return pltpu.make_async_remote_copy(
                src_ref=dst if src is None else src, dst_ref=dst, send_sem=send_sems.at[t, j],
                recv_sem=recv_sems.at[t, j], device_id=to, device_id_type=MESH)

        mine = [pltpu.make_async_copy(x_refs[t], slot(out_refs[t], me), local_sems.at[t]) for t in range(n)]
        first, passed = [], []
        for t in range(n):
            mine[t].start()
            first.append(copy(t, 0, me, sibling, src=x_refs[t]))
            first += [copy(t, 1 + j, me, (*chip, c), src=x_refs[t]) for j, chip in enumerate(chips)]
        for cp in first:
            cp.start()
        for t in range(n):
            for j, chip in enumerate(chips):
                copy(t, 1 + j, (*chip, c), me).wait_recv()
                fwd = copy(t, 4 + j, (*chip, c), sibling)
                fwd.start()
                passed.append(fwd)
        for t in range(n):
            copy(t, 0, sibling, me).wait_recv()
            for j, chip in enumerate(chips):
                copy(t, 4 + j, (*chip, 1 - c), me).wait_recv()
        for cp in first + passed:
            cp.wait_send()
        for cp in mine:
            cp.wait()

    return _pallas(
        body, name="all_gather", in_specs=[ANY] * n, out_specs=[ANY] * n,
        out_shape=[_sds((N_DEV,) + s.shape, s.dtype) for s in shards],
        scratch_shapes=[pltpu.SemaphoreType.DMA((n, 7)), pltpu.SemaphoreType.DMA((n, 7)), pltpu.SemaphoreType.DMA((n,))],
    )(*shards)


def exchange_sibling(parts):
    n = len(parts)

    def body(*refs):
        p_refs, out_refs = refs[:n], refs[n:2 * n]
        send_sems, recv_sems = refs[2 * n:]
        x, y, c = _position()
        copies = []
        for t in range(n):
            for q in range(N_CHIP):
                cp = pltpu.make_async_remote_copy(
                    src_ref=p_refs[t].at[q, 1 - c], dst_ref=out_refs[t].at[q], send_sem=send_sems.at[t, q],
                    recv_sem=recv_sems.at[t, q], device_id=(x, y, 1 - c), device_id_type=MESH)
                cp.start()
                copies.append(cp)
        for cp in copies:
            cp.wait()

    return _pallas(
        body, name="exchange_sibling", in_specs=[ANY] * n, out_specs=[ANY] * n,
        out_shape=[_sds((N_CHIP,) + p.shape[2:], p.dtype) for p in parts],
        scratch_shapes=[pltpu.SemaphoreType.DMA((n, N_CHIP)), pltpu.SemaphoreType.DMA((n, N_CHIP))],
    )(*parts)


def exchange_chips(parts):
    n = len(parts)

    def body(*refs):
        p_refs, out_refs = refs[:n], refs[n:2 * n]
        send_sems, recv_sems, local_sems = refs[2 * n:]
        x, y, c = _position()
        chips = _other_chips(x, y)
        copies, local = [], []
        for t in range(n):
            own = pltpu.make_async_copy(p_refs[t].at[2 * x + y], out_refs[t].at[3], local_sems.at[t])
            own.start()
            local.append(own)
            for j, (cx, cy) in enumerate(chips):
                cp = pltpu.make_async_remote_copy(
                    src_ref=p_refs[t].at[2 * cx + cy], dst_ref=out_refs[t].at[j], send_sem=send_sems.at[t, j],
                    recv_sem=recv_sems.at[t, j], device_id=(cx, cy, c), device_id_type=MESH)
                cp.start()
                copies.append(cp)
        for cp in copies:
            cp.wait()
        for cp in local:
            cp.wait()

    return _pallas(
        body, name="exchange_chips", in_specs=[ANY] * n, out_specs=[ANY] * n,
        out_shape=[_sds(p.shape, p.dtype) for p in parts],
        scratch_shapes=[pltpu.SemaphoreType.DMA((n, 3)), pltpu.SemaphoreType.DMA((n, 3)), pltpu.SemaphoreType.DMA((n,))],
    )(*parts)


def _flat_tile(rows, cols):
    tr = rows
    for d in range(16, min(rows, 512) + 1, 16):
        if rows % d == 0 and d * cols * 4 <= 2 * 1024 * 1024:
            tr = d
    return tr


def pair_add(part, recv):
    _, _, R, C = part.shape
    tr = _flat_tile(R, C)
    core = jnp.reshape(lax.axis_index("c"), (1,)).astype(jnp.int32)

    def body(core_ref, p_ref, r_ref, o_ref):
        o_ref[...] = (p_ref[...] + r_ref[...]).astype(BF16)

    grid_spec = pltpu.PrefetchScalarGridSpec(
        num_scalar_prefetch=1, grid=(N_CHIP, R // tr),
        in_specs=[pl.BlockSpec((None, None, tr, C), lambda q, i, core_ref: (q, core_ref[0], i, 0)),
                  pl.BlockSpec((None, tr, C), lambda q, i, core_ref: (q, i, 0))],
        out_specs=pl.BlockSpec((None, tr, C), lambda q, i, core_ref: (q, i, 0)))
    return _pallas(body, name="pair_add", grid_spec=grid_spec, out_shape=_sds((N_CHIP, R, C), BF16),
                   compiler_params=_params("parallel", "parallel"))(core, part, recv)


def adamw(w, m, v, slabs):
    R, C = w.shape
    tr = _flat_tile(R, C)
    n = slabs.shape[0]

    def body(w_ref, m_ref, v_ref, s_ref, g_ref, d_ref, nm_ref, nv_ref):
        g = s_ref[0].astype(F32)
        for j in range(1, n):
            g = g + s_ref[j].astype(F32)
        m2 = ADAM_B1 * m_ref[...] + (1.0 - ADAM_B1) * g
        v2 = ADAM_B2 * v_ref[...] + (1.0 - ADAM_B2) * (g * g)
        m_hat = m2 / (1.0 - ADAM_B1 ** ADAM_STEP)
        v_hat = v2 / (1.0 - ADAM_B2 ** ADAM_STEP)
        g_ref[...] = g
        d_ref[...] = -ADAM_LR * (m_hat / (jnp.sqrt(v_hat) + ADAM_EPS) + ADAM_WD * w_ref[...])
        nm_ref[...] = m2
        nv_ref[...] = v2

    spec = pl.BlockSpec((tr, C), lambda i: (i, 0))
    return _pallas(body, name="adamw", grid=(R // tr,),
                   in_specs=[spec] * 3 + [pl.BlockSpec((n, tr, C), lambda i: (0, i, 0))], out_specs=[spec] * 4,
                   out_shape=[_sds((R, C))] * 4, compiler_params=_params("parallel"))(w, m, v, slabs)


def _unshard_cols(g):
    return jnp.transpose(g, (1, 0, 2)).reshape(g.shape[1], -1)


def _unshard_rows(g):
    return g.reshape(-1, g.shape[2])


def _shard_cols(full):
    R, C = full.shape
    return jnp.transpose(full.reshape(R, N_DEV, C // N_DEV), (1, 0, 2)).reshape(N_CHIP, 2, R, C // N_DEV)


def _shard_rows(full):
    R, C = full.shape
    return full.reshape(N_CHIP, 2, R // N_DEV, C)


WEIGHTS = ['ln_ffn1_pre', 'ln_ffn1_post', 'ffn1_gate', 'ffn1_up', 'ffn1_down', 'ln_mix_pre', 'ln_mix_post', 'w_in',
           'rwkv_mu', 'rwkv_w0', 'rwkv_w2', 'rwkv_a0', 'rwkv_a2', 'rwkv_g2', 'rwkv_k_k', 'rwkv_k_a', 'rwkv_r_k',
           'rwkv_gn_w', 'rwkv_gn_b', 'w_proj_a', 'pool_w', 'pool_scale', 'w_proj_b', 'w_out', 'ln_ffn2_pre',
           'ln_ffn2_post', 'ffn2_gate', 'ffn2_up', 'ffn2_down']
COL_SHARDED = ['ffn1_gate', 'ffn1_up', 'ffn2_gate', 'ffn2_up', 'w_in', 'rwkv_w2', 'rwkv_a2', 'rwkv_g2', 'w_proj_a',
               'w_proj_b']
ROW_SHARDED = ['ffn1_down', 'ffn2_down', 'w_out', 'pool_w']
SHARDED = COL_SHARDED + ROW_SHARDED
REPLICATED = [n for n in WEIGHTS if n not in SHARDED]


def _step(args):
    wts = {n: args[n] if args[n].ndim == 2 else args[n][0] for n in WEIGHTS}
    x, target = args["x"][0], args["loss_target"][0]
    T, D = x.shape
    W = wts["rwkv_w0"].shape[1]
    PW = wts["pool_scale"].shape[1]
    LW, LA, LG = wts["rwkv_w2"].shape[0], wts["rwkv_a2"].shape[0], wts["rwkv_g2"].shape[0]
    lat = LW + LA + LG
    latp = _round_up(lat, LAT_ALIGN)
    rc = 3 * W + lat
    base = 3 * W + PW + 2 * D
    n_groups, gshard, gd = wts["pool_w"].shape

    pool_w_shard = wts["pool_w"].reshape(n_groups * gshard, gd)
    shards = {n: (pool_w_shard if n == "pool_w" else wts[n]).astype(BF16) for n in SHARDED}
    gathered = dict(zip(SHARDED, all_gather([shards[n] for n in SHARDED])))
    full = {n: _unshard_cols(gathered[n]) for n in COL_SHARDED}
    full.update({n: _unshard_rows(gathered[n]) for n in ('ffn1_down', 'ffn2_down', 'w_out')})
    pool_w = jnp.transpose(gathered["pool_w"].reshape(N_DEV, n_groups, gshard, gd), (1, 0, 2, 3)).reshape(n_groups, gd, gd)
    w_in = full["w_in"]
    w_in_p = jnp.concatenate([w_in[:, :3 * W], w_in[:, rc:], w_in[:, 3 * W:rc], jnp.zeros((D, latp - lat), BF16)], axis=1)

    def pad_rows(m, at):
        return jnp.zeros((latp, W), BF16).at[at:at + m.shape[0]].set(m)

    w2p, a2p, g2p = pad_rows(full["rwkv_w2"], 0), pad_rows(full["rwkv_a2"], LW), pad_rows(full["rwkv_g2"], LW + LA)
    mu = wts["rwkv_mu"]
    mu_rkv = mu[:, :3 * W]
    mu_lat = jnp.concatenate([mu[:, 3 * W:], jnp.zeros((1, latp - lat), F32)], axis=1)
    rk = wts["rwkv_r_k"].reshape(1, W)
    small = [mu_rkv, mu_lat, wts["rwkv_w0"], wts["rwkv_a0"], wts["rwkv_k_k"], wts["rwkv_k_a"], w2p, a2p, g2p]

    def ffn_fwd(tag, n_in, gate, up, down):
        g, u, act = _mm(tag + "_up", [n_in], [gate, up], "nn", [BF16] * 3, epilogue=_swiglu_fwd_epi)
        return g, u, act, _mm(tag + "_down", [act], [down], "nn", [F32], tm=512)[0]

    n1 = rms_pre(x, wts["ln_ffn1_pre"])
    g1, u1, act1, f1 = ffn_fwd("ffn1", n1, full["ffn1_gate"], full["ffn1_up"], full["ffn1_down"])
    h1, nm = post_pre(x, f1, wts["ln_ffn1_post"], wts["ln_mix_pre"], MACARON)
    p = _mm("in_proj", [nm], [w_in_p], "nn", [F32])[0]
    cols = {"rkv": (p, 3 * W, 0), "pool": (p, PW, 3 * W // PW), "lat": (p, latp, base // latp)}
    r, decay, kmod, v, aneg, bpos, gate = rwkv_prep(p, cols, *small)
    y, ckpt = wkv_fwd(r, decay, kmod, v, aneg, bpos)
    ya_in = rwkv_post(y, r, kmod, v, gate, wts["rwkv_gn_w"], wts["rwkv_gn_b"], rk)
    yb_in = pool_fwd(cols, pool_w, wts["pool_scale"])
    gates = [(p, 3 * W + PW), (p, 3 * W + PW + D)]
    m, ya, yb = _mm("mix", [ya_in, yb_in], [full["w_proj_a"], full["w_proj_b"]], "nn", [BF16] * 3,
                    extras=gates, epilogue=_mix_fwd_epi)
    mx = _mm("out_proj", [m], [full["w_out"]], "nn", [F32])[0]
    h2, n2 = post_pre(h1, mx, wts["ln_mix_post"], wts["ln_ffn2_pre"], 1.0)
    g2_, u2, act2, f2 = ffn_fwd("ffn2", n2, full["ffn2_gate"], full["ffn2_up"], full["ffn2_down"])

    grads = {}
    dh3, df2, loss_part, grads["ln_ffn2_post"] = loss_post_bwd(h2, f2, wts["ln_ffn2_post"], target, MACARON)

    def ffn_bwd(tag, df, n_in, g, u, act, gate, up, down):
        dg, du = _mm(tag + "_dact", [df], [down], "nt", [BF16] * 2, extras=[(g, 0), (u, 0)], epilogue=_swiglu_bwd_epi)
        grads[tag + "_down"] = _mm(tag + "_ddown", [act], [df], "tn", [F32], tm=512, tn=1024)[0]
        grads[tag + "_gate"], grads[tag + "_up"] = _mm(tag + "_dup", [n_in], [dg, du], "tn", [F32] * 2, tm=512)
        return _mm(tag + "_dn", [dg, du], [gate, up], "nt", [F32], sum_pairs=True, tm=512)[0]

    dn2 = ffn_bwd("ffn2", df2, n2, g2_, u2, act2, full["ffn2_gate"], full["ffn2_up"], full["ffn2_down"])
    dh2, grads["ln_ffn2_pre"] = pre_bwd(dn2, h2, wts["ln_ffn2_pre"], dh3)
    dmx, grads["ln_mix_post"] = post_bwd(dh2, mx, wts["ln_mix_post"], 1.0)
    dya, dyb, dga, dgb = _mm("dmix", [dmx], [full["w_out"]], "nt", [BF16] * 4,
                             extras=gates + [(ya, 0), (yb, 0)], epilogue=_mix_bwd_epi)
    grads["w_out"] = _mm("dw_out", [m], [dmx], "tn", [F32])[0]
    dya_in = _mm("dproj_a", [dya], [full["w_proj_a"]], "nt", [F32])[0]
    dyb_in = _mm("dproj_b", [dyb], [full["w_proj_b"]], "nt", [F32])[0]
    grads["w_proj_a"] = _mm("dw_proj_a", [ya_in], [dya], "tn", [F32])[0]
    grads["w_proj_b"] = _mm("dw_proj_b", [yb_in], [dyb], "tn", [F32])[0]
    dz_pool, dpool_w, grads["pool_scale"] = pool_bwd(cols, dyb_in, pool_w, wts["pool_scale"])
    dy, dr_x, dk_x, dv_x, dgate, grads["rwkv_gn_w"], grads["rwkv_gn_b"], drk = rwkv_post_bwd(
        dya_in, y, r, kmod, v, gate, wts["rwkv_gn_w"], wts["rwkv_gn_b"], rk)
    grads["rwkv_r_k"] = drk.reshape(wts["rwkv_r_k"].shape)
    dr_s, ddecay, dk_s, dv_s, dneg, dbpos = wkv_bwd(r, decay, kmod, v, aneg, bpos, dy, ckpt)
    (dzs, dzls, grads["rwkv_k_a"], grads["rwkv_k_k"], grads["rwkv_a0"], grads["rwkv_w0"], da2p, dw2p, dg2p) = rwkv_prep_bwd(
        p, cols, [dr_s, dr_x, ddecay, dk_s, dk_x, dv_s, dv_x, dneg, dbpos, dgate], *small)
    grads["rwkv_w2"], grads["rwkv_a2"], grads["rwkv_g2"] = dw2p[:LW], da2p[LW:LW + LA], dg2p[LW + LA:lat]
    dz_rkv, dz_lat, dmu_rkv, dmu_lat = shift_bwd(cols, dzs, dzls, mu_rkv, mu_lat)
    grads["rwkv_mu"] = jnp.concatenate([dmu_rkv, dmu_lat[:, :lat]], axis=1)
    dp = jnp.concatenate([dz_rkv, dz_pool, dga, dgb, dz_lat], axis=1)
    dnm = _mm("din_proj", [dp], [w_in_p], "nt", [F32], tm=512)[0]
    dw_in_p = _mm("dw_in", [nm], [dp], "tn", [F32])[0]
    grads["w_in"] = jnp.concatenate([dw_in_p[:, :3 * W], dw_in_p[:, base:base + lat], dw_in_p[:, 3 * W:base]], axis=1)
    dh1, grads["ln_mix_pre"] = pre_bwd(dnm, h1, wts["ln_mix_pre"], dh2)
    df1, grads["ln_ffn1_post"] = post_bwd(dh1, f1, wts["ln_ffn1_post"], MACARON)
    dn1 = ffn_bwd("ffn1", df1, n1, g1, u1, act1, full["ffn1_gate"], full["ffn1_up"], full["ffn1_down"])
    grad_x, grads["ln_ffn1_pre"] = pre_bwd(dn1, x, wts["ln_ffn1_pre"], dh1)

    parts = {n: _shard_cols(grads[n]) for n in COL_SHARDED}
    parts.update({n: _shard_rows(grads[n]) for n in ('ffn1_down', 'ffn2_down', 'w_out')})
    parts["pool_w"] = jnp.transpose(dpool_w.reshape(n_groups, N_DEV, gshard, gd), (1, 0, 2, 3)).reshape(
        N_CHIP, 2, n_groups * gshard, gd)
    from_sibling = exchange_sibling([parts[n] for n in SHARDED])
    pair_sums = [pair_add(parts[n], rcv) for n, rcv in zip(SHARDED, from_sibling)]
    slabs = dict(zip(SHARDED, exchange_chips(pair_sums)))

    flat = jnp.concatenate([grads[n].reshape(-1) for n in REPLICATED])
    n_small = flat.shape[0]
    rows = _round_up(n_small, 8 * LANES) // LANES
    flat = jnp.concatenate([flat, jnp.zeros((rows * LANES - n_small,), F32)]).reshape(rows, LANES)
    small_slabs = all_gather([flat])[0]

    def packed(prefix):
        vals = jnp.concatenate([args[prefix + n].reshape(-1) for n in REPLICATED])
        return jnp.concatenate([vals, jnp.ones((rows * LANES - n_small,), F32)]).reshape(rows, LANES)

    outs = {}
    small_out = adamw(packed(""), packed("m_"), packed("v_"), small_slabs)
    offset = 0
    for n in REPLICATED:
        size = args[n].size
        outs[n] = [o.reshape(-1)[offset:offset + size].reshape(args[n].shape) for o in small_out]
        offset += size
    for n in SHARDED:
        shard2d = slabs[n].shape[1:]
        res = adamw(*[args[pre + n].reshape(shard2d) for pre in ("", "m_", "v_")], slabs[n])
        outs[n] = [o.reshape(args[n].shape) for o in res]

    loss = lax.psum(loss_part[0, 0], ("x", "y", "c"))
    return (loss, grad_x[None], *[outs[n][0] for n in WEIGHTS], *[outs[n][1] for n in WEIGHTS],
            *[outs[n][2] for n in WEIGHTS], *[outs[n][3] for n in WEIGHTS])


ARG_NAMES = ["x"] + WEIGHTS + ["loss_target"] + ["m_" + n for n in WEIGHTS] + ["v_" + n for n in WEIGHTS]


def kernel(x, ln_ffn1_pre, ln_ffn1_post, ffn1_gate, ffn1_up, ffn1_down, ln_mix_pre, ln_mix_post, w_in, rwkv_mu, rwkv_w0,
           rwkv_w2, rwkv_a0, rwkv_a2, rwkv_g2, rwkv_k_k, rwkv_k_a, rwkv_r_k, rwkv_gn_w, rwkv_gn_b, w_proj_a, pool_w,
           pool_scale, w_proj_b, w_out, ln_ffn2_pre, ln_ffn2_post, ffn2_gate, ffn2_up, ffn2_down, loss_target,
           m_ln_ffn1_pre, m_ln_ffn1_post, m_ffn1_gate, m_ffn1_up, m_ffn1_down, m_ln_mix_pre, m_ln_mix_post, m_w_in,
           m_rwkv_mu, m_rwkv_w0, m_rwkv_w2, m_rwkv_a0, m_rwkv_a2, m_rwkv_g2, m_rwkv_k_k, m_rwkv_k_a, m_rwkv_r_k,
           m_rwkv_gn_w, m_rwkv_gn_b, m_w_proj_a, m_pool_w, m_pool_scale, m_w_proj_b, m_w_out, m_ln_ffn2_pre,
           m_ln_ffn2_post, m_ffn2_gate, m_ffn2_up, m_ffn2_down, v_ln_ffn1_pre, v_ln_ffn1_post, v_ffn1_gate, v_ffn1_up,
           v_ffn1_down, v_ln_mix_pre, v_ln_mix_post, v_w_in, v_rwkv_mu, v_rwkv_w0, v_rwkv_w2, v_rwkv_a0, v_rwkv_a2,
           v_rwkv_g2, v_rwkv_k_k, v_rwkv_k_a, v_rwkv_r_k, v_rwkv_gn_w, v_rwkv_gn_b, v_w_proj_a, v_pool_w, v_pool_scale,
           v_w_proj_b, v_w_out, v_ln_ffn2_pre, v_ln_ffn2_post, v_ffn2_gate, v_ffn2_up, v_ffn2_down):
    given = locals()
    return _step({n: given[n] for n in ARG_NAMES})
```

```python
import jax
import jax.numpy as jnp
from jax import lax
from jax.experimental import pallas as pl
from jax.experimental.pallas import tpu as pltpu

F32, BF16 = jnp.float32, jnp.bfloat16
N_DEV = 8
N_CHIP = 4
HEAD = 64
LANES = 2 * HEAD
NORM_EPS, GN_EPS, L2_EPS = 1e-6, 64e-5, 1e-12
POOL_WINDOWS = (2, 4, 8, 16)
POOL_HALO = 16
MACARON = 0.5
ADAM_LR, ADAM_B1, ADAM_B2, ADAM_EPS, ADAM_WD, ADAM_STEP = 0.001, 0.9, 0.999, 1e-08, 0.01, 10
VMEM_LIMIT = 48 * 1024 * 1024
MM_VMEM_BUDGET = 36 * 1024 * 1024
ROW_TILE = 256
RWKV_ROW_TILE = 128
LAT_ALIGN = 512
WKV_CHUNK, WKV_PAIRS = 16, 8
WKV_UNROLL = 4
MESH = pl.DeviceIdType.MESH


def _pallas(body, **kw):
    return pl.pallas_call(body, **kw)


def _params(*sem):
    return pltpu.CompilerParams(dimension_semantics=sem, vmem_limit_bytes=VMEM_LIMIT)


def _tile(n, target, align=128):
    best = None
    for d in range(align, min(n, target) + 1, align):
        if n % d == 0:
            best = d
    return best if best is not None else n


def _round_up(n, m):
    return (n + m - 1) // m * m


def _mm(name, a_list, b_list, mode, out_dtypes, *, sum_pairs=False, extras=(), epilogue=None, tm=1024, tn=512):
    n_a, n_b = len(a_list), len(b_list)
    assert n_a in (1, n_b)
    a0, b0 = a_list[0], b_list[0]
    if mode == "nn":
        (M, K), N = a0.shape, b0.shape[1]
    elif mode == "nt":
        (M, K), N = a0.shape, b0.shape[0]
    else:
        (K, M), N = a0.shape, b0.shape[1]
    tm, tn = _tile(M, tm), _tile(N, tn)
    n_acc = 1 if sum_pairs else n_b
    n_ex = len(extras)

    def planned(tk):
        operands = 2 * 2 * tk * (n_a * tm + n_b * tn)
        tiles = 2 * tm * tn * (sum(e.dtype.itemsize for e, _ in extras) + sum(jnp.dtype(d).itemsize for d in out_dtypes))
        return operands + tiles + 4 * tm * tn * (n_acc + len(out_dtypes))

    tk = max([d for d in range(128, K + 1, 128) if K % d == 0 and planned(d) <= MM_VMEM_BUDGET] or [_tile(K, 512)])
    nk = K // tk
    if mode == "tn":
        a_spec = pl.BlockSpec((tk, tm), lambda i, j, k: (k, i))
    else:
        a_spec = pl.BlockSpec((tm, tk), lambda i, j, k: (i, k))
    if mode == "nt":
        b_spec = pl.BlockSpec((tn, tk), lambda i, j, k: (j, k))
    else:
        b_spec = pl.BlockSpec((tk, tn), lambda i, j, k: (k, j))
    contract = {"nn": ((1,), (0,)), "nt": ((1,), (1,)), "tn": ((0,), (0,))}[mode]
    e_specs = []
    for _, col in extras:
        assert col % tn == 0
        e_specs.append(pl.BlockSpec((tm, tn), lambda i, j, k, off=col // tn: (i, j + off)))
    o_spec = pl.BlockSpec((tm, tn), lambda i, j, k: (i, j))

    def body(*refs):
        a_refs, b_refs = refs[:n_a], refs[n_a:n_a + n_b]
        e_refs = refs[n_a + n_b:n_a + n_b + n_ex]
        o_refs = refs[n_a + n_b + n_ex:n_a + n_b + n_ex + len(out_dtypes)]
        acc_refs = refs[n_a + n_b + n_ex + len(out_dtypes):]

        def products():
            a_vals = [a[...] for a in a_refs]
            prods = [lax.dot_general(a_vals[p if n_a > 1 else 0], b_refs[p][...], (contract, ((), ())),
                                     preferred_element_type=F32) for p in range(n_b)]
            return [sum(prods[1:], prods[0])] if sum_pairs else prods

        def finish(results):
            outs = epilogue(results, [e[...] for e in e_refs]) if epilogue else results
            for o_ref, o in zip(o_refs, outs):
                o_ref[...] = o.astype(o_ref.dtype)

        if nk == 1:
            finish(products())
            return
        k = pl.program_id(2)

        @pl.when(k == 0)
        def _():
            for acc in acc_refs:
                acc[...] = jnp.zeros_like(acc)

        for acc, prod in zip(acc_refs, products()):
            acc[...] += prod

        @pl.when(k == nk - 1)
        def _():
            finish([acc[...] for acc in acc_refs])

    return _pallas(
        body, name=name, grid=(M // tm, N // tn, nk),
        in_specs=[a_spec] * n_a + [b_spec] * n_b + e_specs,
        out_specs=[o_spec] * len(out_dtypes),
        out_shape=[jax.ShapeDtypeStruct((M, N), dt) for dt in out_dtypes],
        scratch_shapes=[pltpu.VMEM((tm, tn), F32)] * (n_acc if nk > 1 else 0),
        compiler_params=_params("parallel", "parallel", "arbitrary"),
    )(*a_list, *b_list, *[e for e, _ in extras])


def _swiglu_fwd_epi(accs, _):
    g, u = accs
    return [g, u, g * jax.nn.sigmoid(g) * u]


def _swiglu_bwd_epi(accs, ex):
    dact = accs[0]
    g, u = ex[0].astype(F32), ex[1].astype(F32)
    sg = jax.nn.sigmoid(g)
    return [dact * u * (sg * (1.0 + g * (1.0 - sg))), dact * (g * sg)]


def _mix_fwd_epi(accs, ex):
    ya, yb = accs
    return [jax.nn.sigmoid(ex[0]) * ya + jax.nn.sigmoid(ex[1]) * yb, ya, yb]


def _mix_bwd_epi(accs, ex):
    dm = accs[0]
    sa, sb = jax.nn.sigmoid(ex[0]), jax.nn.sigmoid(ex[1])
    ya, yb = ex[2].astype(F32), ex[3].astype(F32)
    return [dm * sa, dm * sb, dm * ya * sa * (1.0 - sa), dm * yb * sb * (1.0 - sb)]


def _row_call(name, body, T, tiled, params, outs, accs=(), prev=(), nxt=(), halo=8, tile=ROW_TILE):
    tm = min(tile, T)
    n_tiles = T // tm

    def norm(e):
        return e if isinstance(e, tuple) else (e, e.shape[1], 0)

    tiled, prev, nxt = [norm(e) for e in tiled], [norm(e) for e in prev], [norm(e) for e in nxt]
    per_halo, n_halo = tm // halo, T // halo
    in_specs = [pl.BlockSpec((tm, w), lambda i, cb=cb: (i, cb)) for _, w, cb in tiled]
    in_specs += [pl.BlockSpec((halo, w), lambda i, cb=cb: (jnp.maximum(i * per_halo - 1, 0), cb)) for _, w, cb in prev]
    in_specs += [pl.BlockSpec((halo, w), lambda i, cb=cb: (jnp.minimum((i + 1) * per_halo, n_halo - 1), cb))
                 for _, w, cb in nxt]
    in_specs += [pl.BlockSpec(p.shape, lambda i, nd=p.ndim: (0,) * nd) for p in params]
    out_specs = [pl.BlockSpec((tm, o.shape[1]), lambda i: (i, 0)) for o in outs]
    out_specs += [pl.BlockSpec(a.shape, lambda i, nd=len(a.shape): (0,) * nd) for a in accs]
    n1, n2, n3, n4, n5 = len(tiled), len(prev), len(nxt), len(params), len(outs)

    def kernel_body(*refs):
        i = pl.program_id(0)
        acc_refs = refs[n1 + n2 + n3 + n4 + n5:]

        @pl.when(i == 0)
        def _():
            for a in acc_refs:
                a[...] = jnp.zeros_like(a)

        body(i, n_tiles, refs[:n1], refs[n1:n1 + n2], refs[n1 + n2:n1 + n2 + n3],
             refs[n1 + n2 + n3:n1 + n2 + n3 + n4], refs[n1 + n2 + n3 + n4:n1 + n2 + n3 + n4 + n5], acc_refs)

    return _pallas(
        kernel_body, name=name, grid=(n_tiles,), in_specs=in_specs, out_specs=out_specs,
        out_shape=list(outs) + list(accs),
        compiler_params=_params("arbitrary"),
    )(*[e[0] for e in tiled + prev + nxt], *params)


def _sds(shape, dtype=F32):
    return jax.ShapeDtypeStruct(tuple(shape), dtype)


def _rstd(x):
    return lax.rsqrt(jnp.mean(x * x, axis=-1, keepdims=True) + NORM_EPS)


def _colsum(x):
    return jnp.sum(x, axis=0, keepdims=True)


def rms_pre(x, g):
    T, D = x.shape

    def body(i, n, tiled, prev, nxt, params, outs, accs):
        xv = tiled[0][...]
        outs[0][...] = (xv * _rstd(xv) * params[0][...]).astype(BF16)

    return _row_call("rms_pre", body, T, [x], [g], [_sds((T, D), BF16)])[0]


def post_pre(h, f, g_post, g_pre, scale):
    T, D = h.shape

    def body(i, n, tiled, prev, nxt, params, outs, accs):
        hv, fv = tiled[0][...], tiled[1][...]
        h2 = hv + scale * (fv * _rstd(fv) * params[0][...])
        outs[0][...] = h2
        outs[1][...] = (h2 * _rstd(h2) * params[1][...]).astype(BF16)

    return _row_call("post_pre", body, T, [h, f], [g_post, g_pre], [_sds((T, D)), _sds((T, D), BF16)])


def _post_bwd_math(dh, fv, g, scale):
    r = _rstd(fv)
    fhat = fv * r
    dy = scale * dh
    z = dy * g
    df = r * (z - fhat * jnp.mean(z * fhat, axis=-1, keepdims=True))
    return df, _colsum(dy * fhat)


def loss_post_bwd(h, f, g_post, target, scale):
    T, D = h.shape

    def body(i, n, tiled, prev, nxt, params, outs, accs):
        hv, fv, tv = tiled[0][...], tiled[1][...], tiled[2][...]
        g = params[0][...]
        e = hv + scale * (fv * _rstd(fv) * g) - tv
        accs[0][...] += jnp.full(accs[0].shape, 0.5 / D, F32) * jnp.sum(e * e)
        dh = e * (1.0 / D)
        outs[0][...] = dh
        df, dg = _post_bwd_math(dh, fv, g, scale)
        outs[1][...] = df.astype(BF16)
        accs[1][...] += dg

    return _row_call("loss_post_bwd", body, T, [h, f, target], [g_post],
                     [_sds((T, D)), _sds((T, D), BF16)], [_sds((1, LANES)), _sds((1, D))])


def post_bwd(dh, f, g_post, scale):
    T, D = dh.shape

    def body(i, n, tiled, prev, nxt, params, outs, accs):
        df, dg = _post_bwd_math(tiled[0][...], tiled[1][...], params[0][...], scale)
        outs[0][...] = df.astype(BF16)
        accs[0][...] += dg

    return _row_call("post_bwd", body, T, [dh, f], [g_post], [_sds((T, D), BF16)], [_sds((1, D))])


def pre_bwd(dn, h, g_pre, dres):
    T, D = h.shape

    def body(i, n, tiled, prev, nxt, params, outs, accs):
        dnv, hv = tiled[0][...], tiled[1][...]
        r = _rstd(hv)
        hhat = hv * r
        z = dnv * params[0][...]
        outs[0][...] = tiled[2][...] + r * (z - hhat * jnp.mean(z * hhat, axis=-1, keepdims=True))
        accs[0][...] += _colsum(dnv * hhat)

    return _row_call("pre_bwd", body, T, [dn, h, dres], [g_pre], [_sds((T, D))], [_sds((1, D))])


def _head_ones():
    i = lax.broadcasted_iota(jnp.int32, (LANES, LANES), 0)
    j = lax.broadcasted_iota(jnp.int32, (LANES, LANES), 1)
    return jnp.where((i < HEAD) == (j < HEAD), 1.0, 0.0).astype(F32)


def _headsum(x):
    e = _head_ones()
    parts = [jnp.dot(x[:, s:s + LANES], e, precision=lax.Precision.HIGHEST, preferred_element_type=F32)
             for s in range(0, x.shape[1], LANES)]
    return parts[0] if len(parts) == 1 else jnp.concatenate(parts, axis=1)


def _shift_down(x, before):
    row = lax.broadcasted_iota(jnp.int32, x.shape, 0)
    return jnp.where(row == 0, before, pltpu.roll(x, 1, 0))


def _shift_up(x, after):
    row = lax.broadcasted_iota(jnp.int32, x.shape, 0)
    return jnp.where(row == x.shape[0] - 1, after, pltpu.roll(x, x.shape[0] - 1, 0))


def _last_row(ref, keep):
    r = ref[ref.shape[0] - 1:ref.shape[0], :]
    return jnp.where(keep, r, jnp.zeros_like(r))


def _first_row(ref, keep):
    r = ref[0:1, :]
    return jnp.where(keep, r, jnp.zeros_like(r))


def _softplus(u):
    return jnp.maximum(u, 0.0) + jnp.log(1.0 + jnp.exp(-jnp.abs(u)))


def _dotb(a, b, contract):
    return lax.dot_general(a.astype(BF16), b.astype(BF16), (contract, ((), ())), preferred_element_type=F32)


_NN, _NT, _TN = ((1,), (0,)), ((1,), (1,)), ((0,), (0,))


def _prep_forward(z, zprev_row, zl, zlprev_row, mu, mul, w0, a0, kk_w, ka_w, w2p, a2p, g2p):
    W = w0.shape[1]
    zs = z + (_shift_down(z, zprev_row) - z) * mu
    zls = zl + (_shift_down(zl, zlprev_row) - zl) * mul
    r, k, v = zs[:, :W], zs[:, W:2 * W], zs[:, 2 * W:]
    th, sg = jnp.tanh(zls), jax.nn.sigmoid(zls)
    xw = w0 + _dotb(th, w2p, _NN)
    wlog = -_softplus(-xw) - 0.5
    ew = jnp.exp(wlog)
    decay = jnp.exp(-ew)
    a = jax.nn.sigmoid(a0 + _dotb(zls, a2p, _NN))
    gate = _dotb(sg, g2p, _NN)
    q = k * kk_w
    nrm = jnp.sqrt(_headsum(q * q))
    den = jnp.maximum(nrm, L2_EPS)
    kk = q / den
    kmod = k * (1.0 + (a - 1.0) * ka_w)
    return dict(zs=zs, zls=zls, r=r, k=k, v=v, th=th, sg=sg, xw=xw, ew=ew, decay=decay, a=a, gate=gate,
                nrm=nrm, den=den, kk=kk, kmod=kmod)


def rwkv_prep(p, cols, mu, mul, w0, a0, kk_w, ka_w, w2p, a2p, g2p):
    T = p.shape[0]
    W = w0.shape[1]

    def body(i, n, tiled, prev, nxt, params, outs, accs):
        c = _prep_forward(tiled[0][...], _last_row(prev[0], i > 0), tiled[1][...], _last_row(prev[1], i > 0),
                          *[q[...] for q in params])
        for o, val in zip(outs, (c["r"], c["decay"], c["kmod"], c["v"], -c["kk"], c["kk"] * c["a"], c["gate"])):
            o[...] = val

    return _row_call("rwkv_prep", body, T, [cols["rkv"], cols["lat"]],
                     [mu, mul, w0, a0, kk_w, ka_w, w2p, a2p, g2p], [_sds((T, W))] * 7,
                     prev=[cols["rkv"], cols["lat"]], tile=RWKV_ROW_TILE)


def _post_forward(y, r, kmod, v, gn_w, gn_b, rk):
    mean = _headsum(y) * (1.0 / HEAD)
    yc = y - mean
    rstd = lax.rsqrt(_headsum(yc * yc) * (1.0 / HEAD) + GN_EPS)
    yn = yc * rstd
    s = _headsum(r * kmod * rk)
    return yn, rstd, s, yn * gn_w + gn_b + s * v


def rwkv_post(y, r, kmod, v, gate, gn_w, gn_b, rk):
    T, W = y.shape

    def body(i, n, tiled, prev, nxt, params, outs, accs):
        yv, rv, kv, vv, gv = [t[...] for t in tiled]
        _, _, _, o = _post_forward(yv, rv, kv, vv, *[q[...] for q in params])
        outs[0][...] = (o * gv).astype(BF16)

    return _row_call("rwkv_post", body, T, [y, r, kmod, v, gate], [gn_w, gn_b, rk], [_sds((T, W), BF16)],
                     tile=RWKV_ROW_TILE)[0]


def rwkv_post_bwd(dout, y, r, kmod, v, gate, gn_w, gn_b, rk):
    T, W = y.shape

    def body(i, n, tiled, prev, nxt, params, outs, accs):
        dv_, yv, rv, kv, vv, gv = [t[...] for t in tiled]
        gn_w_, gn_b_, rk_ = [q[...] for q in params]
        yn, rstd, s, o = _post_forward(yv, rv, kv, vv, gn_w_, gn_b_, rk_)
        do = dv_ * gv
        outs[4][...] = dv_ * o
        accs[0][...] += _colsum(do * yn)
        accs[1][...] += _colsum(do)
        dyn = do * gn_w_
        outs[0][...] = rstd * (dyn - _headsum(dyn) * (1.0 / HEAD) - yn * (_headsum(dyn * yn) * (1.0 / HEAD)))
        ds = _headsum(do * vv)
        outs[1][...] = ds * kv * rk_
        outs[2][...] = ds * rv * rk_
        outs[3][...] = do * s
        accs[2][...] += _colsum(ds * rv * kv)

    return _row_call("rwkv_post_bwd", body, T, [dout, y, r, kmod, v, gate], [gn_w, gn_b, rk],
                     [_sds((T, W))] * 5, [_sds((1, W))] * 3, tile=RWKV_ROW_TILE)


def rwkv_prep_bwd(p, cols, grads, mu, mul, w0, a0, kk_w, ka_w, w2p, a2p, g2p):
    T = p.shape[0]
    W = w0.shape[1]
    latp = w2p.shape[0]

    def body(i, n, tiled, prev, nxt, params, outs, accs):
        pv = [q[...] for q in params]
        mu_, mul_, w0_, a0_, kk_w_, ka_w_, w2p_, a2p_, g2p_ = pv
        c = _prep_forward(tiled[0][...], _last_row(prev[0], i > 0), tiled[1][...], _last_row(prev[1], i > 0), *pv)
        dr_s, dr_x, ddecay, dk_s, dk_x, dv_s, dv_x, dneg, db, dgate = [t[...] for t in tiled[2:]]
        k, a, kk = c["k"], c["a"], c["kk"]
        dkmod = dk_s + dk_x
        dk = dkmod * (1.0 + (a - 1.0) * ka_w_)
        da = dkmod * k * ka_w_ + db * kk
        accs[0][...] += _colsum(dkmod * k * (a - 1.0))
        dkk = db * a - dneg
        dq = jnp.where(c["nrm"] > L2_EPS, dkk - kk * _headsum(dkk * kk), dkk) / c["den"]
        dk = dk + dq * kk_w_
        accs[1][...] += _colsum(dq * k)
        dxa = da * a * (1.0 - a)
        accs[2][...] += _colsum(dxa)
        accs[4][...] += _dotb(c["zls"], dxa, _TN)
        dzls = _dotb(dxa, a2p_, _NT)
        dxw = (-ddecay * c["decay"] * c["ew"]) * jax.nn.sigmoid(-c["xw"])
        accs[3][...] += _colsum(dxw)
        accs[5][...] += _dotb(c["th"], dxw, _TN)
        dzls = dzls + _dotb(dxw, w2p_, _NT) * (1.0 - c["th"] * c["th"])
        accs[6][...] += _dotb(c["sg"], dgate, _TN)
        dzls = dzls + _dotb(dgate, g2p_, _NT) * c["sg"] * (1.0 - c["sg"])
        outs[0][...] = jnp.concatenate([dr_s + dr_x, dk, dv_s + dv_x], axis=1)
        outs[1][...] = dzls

    return _row_call("rwkv_prep_bwd", body, T, [cols["rkv"], cols["lat"]] + list(grads),
                     [mu, mul, w0, a0, kk_w, ka_w, w2p, a2p, g2p], [_sds((T, 3 * W)), _sds((T, latp))],
                     [_sds((1, W))] * 4 + [_sds((latp, W))] * 3, prev=[cols["rkv"], cols["lat"]], tile=RWKV_ROW_TILE)


def shift_bwd(cols, dzs, dzls, mu, mul):
    T = dzs.shape[0]

    def body(i, n, tiled, prev, nxt, params, outs, accs):
        for j in range(2):
            z, d, m = tiled[j][...], tiled[2 + j][...], params[j][...]
            zprev = _shift_down(z, _last_row(prev[j], i > 0))
            dnext = _shift_up(d, _first_row(nxt[j], i < n - 1))
            outs[j][...] = (d * (1.0 - m) + dnext * m).astype(BF16)
            accs[j][...] += _colsum(d * (zprev - z))

    return _row_call("shift_bwd", body, T, [cols["rkv"], cols["lat"], dzs, dzls], [mu, mul],
                     [_sds(dzs.shape, BF16), _sds(dzls.shape, BF16)], [_sds(mu.shape), _sds(mul.shape)],
                     prev=[cols["rkv"], cols["lat"]], nxt=[dzs, dzls])


def _window_pick(x, windows):
    gid = lax.broadcasted_iota(jnp.int32, x.shape, 1) // (x.shape[1] // len(windows))
    out = windows[-1]
    for g in range(len(windows) - 2, -1, -1):
        out = jnp.where(gid == g, windows[g], out)
    return out


def _pool_counts(t0, rows, width):
    t = (t0 + lax.broadcasted_iota(jnp.int32, (rows, width), 0) + 1).astype(F32)
    return _window_pick(t, [jnp.minimum(t, float(w)) for w in POOL_WINDOWS])


def _pool_mixed(x, before, t0):
    tm, width = x.shape
    xe = jnp.concatenate([before, x], axis=0)
    sums, s, span = [], xe, 1
    for w in POOL_WINDOWS:
        while span < w:
            s = s + pltpu.roll(s, span, 0)
            span *= 2
        sums.append(s[POOL_HALO:, :])
    return _window_pick(x, sums) / _pool_counts(t0, tm, width) - x


def _group_dot(x, w_ref, contract):
    gd = w_ref.shape[-1]
    parts = [_dotb(x[:, g * gd:(g + 1) * gd], w_ref[g], contract) for g in range(w_ref.shape[0])]
    return jnp.concatenate(parts, axis=1)


def pool_fwd(cols, pool_w, pool_scale):
    T, width = cols["pool"][0].shape[0], cols["pool"][1]
    tm = min(ROW_TILE, T)

    def body(i, n, tiled, prev, nxt, params, outs, accs):
        before = jnp.where(i > 0, prev[0][...], 0.0)
        mixed = _pool_mixed(tiled[0][...], before, i * tm)
        outs[0][...] = (_group_dot(mixed, params[0], _NN) * params[1][...]).astype(BF16)

    return _row_call("pool_fwd", body, T, [cols["pool"]], [pool_w, pool_scale], [_sds((T, width), BF16)],
                     prev=[cols["pool"]], halo=POOL_HALO)[0]


def pool_bwd(cols, dout, pool_w, pool_scale):
    T, width = dout.shape
    tm = min(ROW_TILE, T)

    def body(i, n, tiled, prev, nxt, params, outs, accs):
        w_ref, scale = params[0], params[1][...]
        before = jnp.where(i > 0, prev[0][...], 0.0)
        mixed = _pool_mixed(tiled[0][...], before, i * tm)
        dv = tiled[1][...]
        accs[1][...] += _colsum(dv * _group_dot(mixed, w_ref, _NN))
        after = jnp.where(i < n - 1, nxt[0][...], 0.0)
        dys = jnp.concatenate([dv, after], axis=0) * scale
        gd = w_ref.shape[-1]
        for g in range(w_ref.shape[0]):
            accs[0][g] += _dotb(mixed[:, g * gd:(g + 1) * gd], dys[:tm, g * gd:(g + 1) * gd], _TN)
        dmixed = _group_dot(dys, w_ref, _NT)
        u = dmixed / _pool_counts(i * tm, tm + POOL_HALO, width)
        rows = tm + POOL_HALO
        sums, s, span = [], u, 1
        for w in POOL_WINDOWS:
            while span < w:
                s = s + pltpu.roll(s, rows - span, 0)
                span *= 2
            sums.append(s[:tm, :])
        outs[0][...] = (_window_pick(dv, sums) - dmixed[:tm, :]).astype(BF16)

    return _row_call("pool_bwd", body, T, [cols["pool"], dout], [pool_w, pool_scale], [_sds((T, width), BF16)],
                     [_sds(pool_w.shape), _sds((1, width))], prev=[cols["pool"]], nxt=[dout], halo=POOL_HALO)


def _wkv_consts(pairs):
    lane = lax.broadcasted_iota(jnp.int32, (HEAD, LANES), 1)
    sub = lax.broadcasted_iota(jnp.int32, (pairs * HEAD, LANES), 0)
    lane_all = lax.broadcasted_iota(jnp.int32, (pairs * HEAD, LANES), 1)
    i = lax.broadcasted_iota(jnp.int32, (LANES, LANES), 0)
    j = lax.broadcasted_iota(jnp.int32, (LANES, LANES), 1)
    ones = jnp.where((i < HEAD) == (j < HEAD), 1.0, 0.0).astype(BF16)
    diag = jnp.where((lane_all & (HEAD - 1)) == (sub & (HEAD - 1)), 1.0, 0.0).astype(F32)
    return lane < HEAD, diag, ones


def _segsum(p, in_a):
    sa = jnp.sum(jnp.where(in_a, p, 0.0), axis=1, keepdims=True)
    sb = jnp.sum(jnp.where(in_a, 0.0, p), axis=1, keepdims=True)
    return jnp.where(in_a, sa, sb)


def _hi_lo(p):
    hi = lax.bitcast_convert_type(lax.bitcast_convert_type(p, jnp.uint32) & jnp.uint32(0xFFFF0000), F32)
    return hi, p - hi


def _segsum_mxu(p, ones):
    hi, lo = _hi_lo(p)
    return (jnp.dot(hi.astype(BF16), ones, preferred_element_type=F32)
            + jnp.dot(lo.astype(BF16), ones, preferred_element_type=F32))


def _cat(parts, axis):
    return parts[0] if len(parts) == 1 else jnp.concatenate(parts, axis=axis)


def _tile_rows(row, pairs):
    return _cat([jnp.broadcast_to(row[:, p * LANES:(p + 1) * LANES], (HEAD, LANES)) for p in range(pairs)], 0)


def _spread(row, pairs, diag16, ones):
    hi, lo = _hi_lo(row)
    return (jnp.dot(_tile_rows(hi.astype(BF16), pairs) * diag16, ones, preferred_element_type=F32)
            + jnp.dot(_tile_rows(lo.astype(BF16), pairs) * diag16, ones, preferred_element_type=F32))


def _pair_colsums(x, pairs):
    return _cat([_colsum(x[p * HEAD:(p + 1) * HEAD]) for p in range(pairs)], 1)


def wkv_fwd(r, w, k, v, a, b):
    T, W = r.shape
    P = W // LANES
    PB = min(WKV_PAIRS, P)
    chunk = min(WKV_CHUNK, T)
    NC = T // chunk
    R = PB * HEAD

    def body(r_ref, w_ref, k_ref, v_ref, a_ref, b_ref, y_ref, st_ref, sa_ref, vt_ref, s_ref):
        c = pl.program_id(1)

        @pl.when(c == 0)
        def _():
            s_ref[...] = jnp.zeros_like(s_ref)

        in_a, diag, ones = _wkv_consts(PB)
        diag16 = diag.astype(BF16)

        def spread(t, _):
            vt_ref[t] = _spread(v_ref[pl.ds(t, 1), :], PB, diag16, ones)
            return 0

        lax.fori_loop(0, chunk, spread, 0, unroll=WKV_UNROLL)

        def step(t, _):
            rows = [ref[pl.ds(t, 1), :] for ref in (w_ref, k_ref, a_ref, b_ref)]
            for p in range(PB):
                wt, kt, at, bt = [x[:, p * LANES:(p + 1) * LANES] for x in rows]
                rs = pl.ds(p * HEAD, HEAD)
                S = s_ref[rs]
                sa = _segsum(S * at, in_a)
                sa_ref[t, rs] = sa
                S = S * wt + sa * bt + vt_ref[t, rs] * kt
                st_ref[t, rs] = S
                s_ref[rs] = S
            return 0

        lax.fori_loop(0, chunk, step, 0, unroll=WKV_UNROLL)

        def readout(t, _):
            yt = _segsum_mxu(st_ref[t] * _tile_rows(r_ref[pl.ds(t, 1), :], PB), ones) * diag
            y_ref[pl.ds(t, 1), :] = _pair_colsums(yt, PB)
            return 0

        lax.fori_loop(0, chunk, readout, 0, unroll=WKV_UNROLL)

    spec = pl.BlockSpec((chunk, PB * LANES), lambda g, c: (c, g))
    tiles = pl.BlockSpec((chunk, R, LANES), lambda g, c: (c, g, 0))
    return _pallas(
        body, name="wkv_fwd", grid=(P // PB, NC), in_specs=[spec] * 6, out_specs=[spec, tiles, tiles],
        out_shape=[_sds((T, W)), _sds((T, P * HEAD, LANES)), _sds((T, P * HEAD, LANES))],
        scratch_shapes=[pltpu.VMEM((chunk, R, LANES), F32), pltpu.VMEM((R, LANES), F32)],
        compiler_params=_params("parallel", "arbitrary"),
    )(r, w, k, v, a, b)


def wkv_bwd(r, w, k, v, a, b, dy, st, sa):
    T, W = r.shape
    P = W // LANES
    PB = min(WKV_PAIRS, P)
    chunk = min(WKV_CHUNK, T)
    NC = T // chunk
    R = PB * HEAD

    def body(r_ref, w_ref, k_ref, v_ref, a_ref, b_ref, dy_ref, st_ref, before_ref, sa_ref,
             dr_ref, dw_ref, dk_ref, dv_ref, da_ref, db_ref, ds_ref, vt_ref, dyt_ref, dst_ref, dsa_ref):
        c = pl.program_id(1)

        @pl.when(c == 0)
        def _():
            ds_ref[...] = jnp.zeros_like(ds_ref)

        in_a, diag, ones = _wkv_consts(PB)
        diag16 = diag.astype(BF16)

        def spread(t, _):
            vt_ref[t] = _spread(v_ref[pl.ds(t, 1), :], PB, diag16, ones)
            dyt_ref[t] = _spread(dy_ref[pl.ds(t, 1), :], PB, diag16, ones)
            return 0

        lax.fori_loop(0, chunk, spread, 0, unroll=WKV_UNROLL)

        def bstep(n, _):
            t = chunk - 1 - n
            rows = [ref[pl.ds(t, 1), :] for ref in (r_ref, w_ref, a_ref, b_ref)]
            for p in range(PB):
                rt, wt, at, bt = [x[:, p * LANES:(p + 1) * LANES] for x in rows]
                rs = pl.ds(p * HEAD, HEAD)
                dS = ds_ref[rs] + dyt_ref[t, rs] * rt
                dst_ref[t, rs] = dS
                dsa = _segsum(dS * bt, in_a)
                dsa_ref[t, rs] = dsa
                ds_ref[rs] = dS * wt + dsa * at
            return 0

        lax.fori_loop(0, chunk, bstep, 0, unroll=WKV_UNROLL)

        def collect(t, _):
            sn, dS, dsa = st_ref[t], dst_ref[t], dsa_ref[t]
            first = jnp.where(c == NC - 1, 0.0, before_ref[0])
            sp = jnp.where(t > 0, st_ref[jnp.maximum(t - 1, 0)], first)
            dvt = _segsum_mxu(dS * _tile_rows(k_ref[pl.ds(t, 1), :], PB), ones) * diag
            for ref, val in ((dr_ref, sn * dyt_ref[t]), (dw_ref, dS * sp), (dk_ref, dS * vt_ref[t]), (dv_ref, dvt),
                             (da_ref, sp * dsa), (db_ref, dS * sa_ref[t])):
                ref[pl.ds(t, 1), :] = _pair_colsums(val, PB)
            return 0

        lax.fori_loop(0, chunk, collect, 0, unroll=WKV_UNROLL)

    spec = pl.BlockSpec((chunk, PB * LANES), lambda g, c: (NC - 1 - c, g))
    tiles = pl.BlockSpec((chunk, R, LANES), lambda g, c: (NC - 1 - c, g, 0))
    before = pl.BlockSpec((1, R, LANES), lambda g, c: (jnp.maximum((NC - 1 - c) * chunk - 1, 0), g, 0))

    def scratch(n):
        return pltpu.VMEM((n, R, LANES), F32)

    return _pallas(
        body, name="wkv_bwd", grid=(P // PB, NC), in_specs=[spec] * 7 + [tiles, before, tiles],
        out_specs=[spec] * 6, out_shape=[_sds((T, W))] * 6,
        scratch_shapes=[pltpu.VMEM((R, LANES), F32), scratch(chunk), scratch(chunk), scratch(chunk), scratch(chunk)],
        compiler_params=_params("parallel", "arbitrary"),
    )(r, w, k, v, a, b, dy, st, st, sa)


def _position():
    return lax.axis_index("x"), lax.axis_index("y"), lax.axis_index("c")


def _other_chips(x, y):
    return [(1 - x, y), (x, 1 - y), (1 - x, 1 - y)]


ANY = pl.BlockSpec(memory_space=pl.ANY)


def all_gather(shards):
    n = len(shards)

    def body(*refs):
        x_refs, out_refs = refs[:n], refs[n:2 * n]
        send_sems, recv_sems, local_sems = refs[2 * n:]
        x, y, c = _position()
        me, sibling = (x, y, c), (x, y, 1 - c)
        chips = _other_chips(x, y)

        def slot(ref, pos):
            return ref.at[4 * pos[0] + 2 * pos[1] + pos[2]]

        def copy(t, j, block, to, src=None):
            dst = slot(out_refs[t], block)
            return pltpu.make_async_remote_copy(
                src_ref=dst if src is None else src, dst_ref=dst, send_sem=send_sems.at[t, j],
                recv_sem=recv_sems.at[t, j], device_id=to, device_id_type=MESH)

        mine = [pltpu.make_async_copy(x_refs[t], slot(out_refs[t], me), local_sems.at[t]) for t in range(n)]
        first, passed = [], []
        for t in range(n):
            mine[t].start()
            first.append(copy(t, 0, me, sibling, src=x_refs[t]))
            first += [copy(t, 1 + j, me, (*chip, c), src=x_refs[t]) for j, chip in enumerate(chips)]
        for cp in first:
            cp.start()
        for t in range(n):
            for j, chip in enumerate(chips):
                copy(t, 1 + j, (*chip, c), me).wait_recv()
                fwd = copy(t, 4 + j, (*chip, c), sibling)
                fwd.start()
                passed.append(fwd)
        for t in range(n):
            copy(t, 0, sibling, me).wait_recv()
            for j, chip in enumerate(chips):
                copy(t, 4 + j, (*chip, 1 - c), me).wait_recv()
        for cp in first + passed:
            cp.wait_send()
        for cp in mine:
            cp.wait()

    return _pallas(
        body, name="all_gather", in_specs=[ANY] * n, out_specs=[ANY] * n,
        out_shape=[_sds((N_DEV,) + s.shape, s.dtype) for s in shards],
        scratch_shapes=[pltpu.SemaphoreType.DMA((n, 7)), pltpu.SemaphoreType.DMA((n, 7)), pltpu.SemaphoreType.DMA((n,))],
    )(*shards)


def exchange_sibling(parts):
    n = len(parts)

    def body(*refs):
        p_refs, out_refs = refs[:n], refs[n:2 * n]
        send_sems, recv_sems = refs[2 * n:]
        x, y, c = _position()
        copies = []
        for t in range(n):
            for q in range(N_CHIP):
                cp = pltpu.make_async_remote_copy(
                    src_ref=p_refs[t].at[q, 1 - c], dst_ref=out_refs[t].at[q], send_sem=send_sems.at[t, q],
                    recv_sem=recv_sems.at[t, q], device_id=(x, y, 1 - c), device_id_type=MESH)
                cp.start()
                copies.append(cp)
        for cp in copies:
            cp.wait()

    return _pallas(
        body, name="exchange_sibling", in_specs=[ANY] * n, out_specs=[ANY] * n,
        out_shape=[_sds((N_CHIP,) + p.shape[2:], p.dtype) for p in parts],
        scratch_shapes=[pltpu.SemaphoreType.DMA((n, N_CHIP)), pltpu.SemaphoreType.DMA((n, N_CHIP))],
    )(*parts)


def exchange_chips(parts):
    n = len(parts)

    def body(*refs):
        p_refs, out_refs = refs[:n], refs[n:2 * n]
        send_sems, recv_sems, local_sems = refs[2 * n:]
        x, y, c = _position()
        chips = _other_chips(x, y)
        copies, local = [], []
        for t in range(n):
            own = pltpu.make_async_copy(p_refs[t].at[2 * x + y], out_refs[t].at[3], local_sems.at[t])
            own.start()
            local.append(own)
            for j, (cx, cy) in enumerate(chips):
                cp = pltpu.make_async_remote_copy(
                    src_ref=p_refs[t].at[2 * cx + cy], dst_ref=out_refs[t].at[j], send_sem=send_sems.at[t, j],
                    recv_sem=recv_sems.at[t, j], device_id=(cx, cy, c), device_id_type=MESH)
                cp.start()
                copies.append(cp)
        for cp in copies:
            cp.wait()
        for cp in local:
            cp.wait()

    return _pallas(
        body, name="exchange_chips", in_specs=[ANY] * n, out_specs=[ANY] * n,
        out_shape=[_sds(p.shape, p.dtype) for p in parts],
        scratch_shapes=[pltpu.SemaphoreType.DMA((n, 3)), pltpu.SemaphoreType.DMA((n, 3)), pltpu.SemaphoreType.DMA((n,))],
    )(*parts)


def _flat_tile(rows, cols):
    tr = rows
    for d in range(16, min(rows, 512) + 1, 16):
        if rows % d == 0 and d * cols * 4 <= 2 * 1024 * 1024:
            tr = d
    return tr


def pair_add(part, recv):
    _, _, R, C = part.shape
    tr = _flat_tile(R, C)
    core = jnp.reshape(lax.axis_index("c"), (1,)).astype(jnp.int32)

    def body(core_ref, p_ref, r_ref, o_ref):
        o_ref[...] = (p_ref[...] + r_ref[...]).astype(BF16)

    grid_spec = pltpu.PrefetchScalarGridSpec(
        num_scalar_prefetch=1, grid=(N_CHIP, R // tr),
        in_specs=[pl.BlockSpec((None, None, tr, C), lambda q, i, core_ref: (q, core_ref[0], i, 0)),
                  pl.BlockSpec((None, tr, C), lambda q, i, core_ref: (q, i, 0))],
        out_specs=pl.BlockSpec((None, tr, C), lambda q, i, core_ref: (q, i, 0)))
    return _pallas(body, name="pair_add", grid_spec=grid_spec, out_shape=_sds((N_CHIP, R, C), BF16),
                   compiler_params=_params("parallel", "parallel"))(core, part, recv)


def adamw(w, m, v, slabs):
    R, C = w.shape
    tr = _flat_tile(R, C)
    n = slabs.shape[0]

    def body(w_ref, m_ref, v_ref, s_ref, g_ref, d_ref, nm_ref, nv_ref):
        g = s_ref[0].astype(F32)
        for j in range(1, n):
            g = g + s_ref[j].astype(F32)
        m2 = ADAM_B1 * m_ref[...] + (1.0 - ADAM_B1) * g
        v2 = ADAM_B2 * v_ref[...] + (1.0 - ADAM_B2) * (g * g)
        m_hat = m2 / (1.0 - ADAM_B1 ** ADAM_STEP)
        v_hat = v2 / (1.0 - ADAM_B2 ** ADAM_STEP)
        g_ref[...] = g
        d_ref[...] = -ADAM_LR * (m_hat / (jnp.sqrt(v_hat) + ADAM_EPS) + ADAM_WD * w_ref[...])
        nm_ref[...] = m2
        nv_ref[...] = v2

    spec = pl.BlockSpec((tr, C), lambda i: (i, 0))
    return _pallas(body, name="adamw", grid=(R // tr,),
                   in_specs=[spec] * 3 + [pl.BlockSpec((n, tr, C), lambda i: (0, i, 0))], out_specs=[spec] * 4,
                   out_shape=[_sds((R, C))] * 4, compiler_params=_params("parallel"))(w, m, v, slabs)


def _unshard_cols(g):
    return jnp.transpose(g, (1, 0, 2)).reshape(g.shape[1], -1)


def _unshard_rows(g):
    return g.reshape(-1, g.shape[2])


def _shard_cols(full):
    R, C = full.shape
    return jnp.transpose(full.reshape(R, N_DEV, C // N_DEV), (1, 0, 2)).reshape(N_CHIP, 2, R, C // N_DEV)


def _shard_rows(full):
    R, C = full.shape
    return full.reshape(N_CHIP, 2, R // N_DEV, C)


WEIGHTS = ['ln_ffn1_pre', 'ln_ffn1_post', 'ffn1_gate', 'ffn1_up', 'ffn1_down', 'ln_mix_pre', 'ln_mix_post', 'w_in',
           'rwkv_mu', 'rwkv_w0', 'rwkv_w2', 'rwkv_a0', 'rwkv_a2', 'rwkv_g2', 'rwkv_k_k', 'rwkv_k_a', 'rwkv_r_k',
           'rwkv_gn_w', 'rwkv_gn_b', 'w_proj_a', 'pool_w', 'pool_scale', 'w_proj_b', 'w_out', 'ln_ffn2_pre',
           'ln_ffn2_post', 'ffn2_gate', 'ffn2_up', 'ffn2_down']
COL_SHARDED = ['ffn1_gate', 'ffn1_up', 'ffn2_gate', 'ffn2_up', 'w_in', 'rwkv_w2', 'rwkv_a2', 'rwkv_g2', 'w_proj_a',
               'w_proj_b']
ROW_SHARDED = ['ffn1_down', 'ffn2_down', 'w_out', 'pool_w']
SHARDED = COL_SHARDED + ROW_SHARDED
REPLICATED = [n for n in WEIGHTS if n not in SHARDED]


def _step(args):
    wts = {n: args[n] if args[n].ndim == 2 else args[n][0] for n in WEIGHTS}
    x, target = args["x"][0], args["loss_target"][0]
    T, D = x.shape
    W = wts["rwkv_w0"].shape[1]
    PW = wts["pool_scale"].shape[1]
    LW, LA, LG = wts["rwkv_w2"].shape[0], wts["rwkv_a2"].shape[0], wts["rwkv_g2"].shape[0]
    lat = LW + LA + LG
    latp = _round_up(lat, LAT_ALIGN)
    rc = 3 * W + lat
    base = 3 * W + PW + 2 * D
    n_groups, gshard, gd = wts["pool_w"].shape

    pool_w_shard = wts["pool_w"].reshape(n_groups * gshard, gd)
    shards = {n: (pool_w_shard if n == "pool_w" else wts[n]).astype(BF16) for n in SHARDED}
    gathered = dict(zip(SHARDED, all_gather([shards[n] for n in SHARDED])))
    full = {n: _unshard_cols(gathered[n]) for n in COL_SHARDED}
    full.update({n: _unshard_rows(gathered[n]) for n in ('ffn1_down', 'ffn2_down', 'w_out')})
    pool_w = jnp.transpose(gathered["pool_w"].reshape(N_DEV, n_groups, gshard, gd), (1, 0, 2, 3)).reshape(n_groups, gd, gd)
    w_in = full["w_in"]
    w_in_p = jnp.concatenate([w_in[:, :3 * W], w_in[:, rc:], w_in[:, 3 * W:rc], jnp.zeros((D, latp - lat), BF16)], axis=1)

    def pad_rows(m, at):
        return jnp.zeros((latp, W), BF16).at[at:at + m.shape[0]].set(m)

    w2p, a2p, g2p = pad_rows(full["rwkv_w2"], 0), pad_rows(full["rwkv_a2"], LW), pad_rows(full["rwkv_g2"], LW + LA)
    mu = wts["rwkv_mu"]
    mu_rkv = mu[:, :3 * W]
    mu_lat = jnp.concatenate([mu[:, 3 * W:], jnp.zeros((1, latp - lat), F32)], axis=1)
    rk = wts["rwkv_r_k"].reshape(1, W)
    small = [mu_rkv, mu_lat, wts["rwkv_w0"], wts["rwkv_a0"], wts["rwkv_k_k"], wts["rwkv_k_a"], w2p, a2p, g2p]

    def ffn_fwd(tag, n_in, gate, up, down):
        g, u, act = _mm(tag + "_up", [n_in], [gate, up], "nn", [BF16] * 3, epilogue=_swiglu_fwd_epi)
        return g, u, act, _mm(tag + "_down", [act], [down], "nn", [F32], tm=512)[0]

    n1 = rms_pre(x, wts["ln_ffn1_pre"])
    g1, u1, act1, f1 = ffn_fwd("ffn1", n1, full["ffn1_gate"], full["ffn1_up"], full["ffn1_down"])
    h1, nm = post_pre(x, f1, wts["ln_ffn1_post"], wts["ln_mix_pre"], MACARON)
    p = _mm("in_proj", [nm], [w_in_p], "nn", [F32])[0]
    cols = {"rkv": (p, 3 * W, 0), "pool": (p, PW, 3 * W // PW), "lat": (p, latp, base // latp)}
    r, decay, kmod, v, aneg, bpos, gate = rwkv_prep(p, cols, *small)
    y, states, sdota = wkv_fwd(r, decay, kmod, v, aneg, bpos)
    ya_in = rwkv_post(y, r, kmod, v, gate, wts["rwkv_gn_w"], wts["rwkv_gn_b"], rk)
    yb_in = pool_fwd(cols, pool_w, wts["pool_scale"])
    gates = [(p, 3 * W + PW), (p, 3 * W + PW + D)]
    m, ya, yb = _mm("mix", [ya_in, yb_in], [full["w_proj_a"], full["w_proj_b"]], "nn", [BF16] * 3,
                    extras=gates, epilogue=_mix_fwd_epi)
    mx = _mm("out_proj", [m], [full["w_out"]], "nn", [F32])[0]
    h2, n2 = post_pre(h1, mx, wts["ln_mix_post"], wts["ln_ffn2_pre"], 1.0)
    g2_, u2, act2, f2 = ffn_fwd("ffn2", n2, full["ffn2_gate"], full["ffn2_up"], full["ffn2_down"])

    grads = {}
    dh3, df2, loss_part, grads["ln_ffn2_post"] = loss_post_bwd(h2, f2, wts["ln_ffn2_post"], target, MACARON)

    def ffn_bwd(tag, df, n_in, g, u, act, gate, up, down):
        dg, du = _mm(tag + "_dact", [df], [down], "nt", [BF16] * 2, extras=[(g, 0), (u, 0)], epilogue=_swiglu_bwd_epi)
        grads[tag + "_down"] = _mm(tag + "_ddown", [act], [df], "tn", [F32], tm=512, tn=1024)[0]
        grads[tag + "_gate"], grads[tag + "_up"] = _mm(tag + "_dup", [n_in], [dg, du], "tn", [F32] * 2, tm=512)
        return _mm(tag + "_dn", [dg, du], [gate, up], "nt", [F32], sum_pairs=True, tm=512)[0]

    dn2 = ffn_bwd("ffn2", df2, n2, g2_, u2, act2, full["ffn2_gate"], full["ffn2_up"], full["ffn2_down"])
    dh2, grads["ln_ffn2_pre"] = pre_bwd(dn2, h2, wts["ln_ffn2_pre"], dh3)
    dmx, grads["ln_mix_post"] = post_bwd(dh2, mx, wts["ln_mix_post"], 1.0)
    dya, dyb, dga, dgb = _mm("dmix", [dmx], [full["w_out"]], "nt", [BF16] * 4,
                             extras=gates + [(ya, 0), (yb, 0)], epilogue=_mix_bwd_epi)
    grads["w_out"] = _mm("dw_out", [m], [dmx], "tn", [F32])[0]
    dya_in = _mm("dproj_a", [dya], [full["w_proj_a"]], "nt", [F32])[0]
    dyb_in = _mm("dproj_b", [dyb], [full["w_proj_b"]], "nt", [F32])[0]
    grads["w_proj_a"] = _mm("dw_proj_a", [ya_in], [dya], "tn", [F32])[0]
    grads["w_proj_b"] = _mm("dw_proj_b", [yb_in], [dyb], "tn", [F32])[0]
    dz_pool, dpool_w, grads["pool_scale"] = pool_bwd(cols, dyb_in, pool_w, wts["pool_scale"])
    dy, dr_x, dk_x, dv_x, dgate, grads["rwkv_gn_w"], grads["rwkv_gn_b"], drk = rwkv_post_bwd(
        dya_in, y, r, kmod, v, gate, wts["rwkv_gn_w"], wts["rwkv_gn_b"], rk)
    grads["rwkv_r_k"] = drk.reshape(wts["rwkv_r_k"].shape)
    dr_s, ddecay, dk_s, dv_s, dneg, dbpos = wkv_bwd(r, decay, kmod, v, aneg, bpos, dy, states, sdota)
    (dzs, dzls, grads["rwkv_k_a"], grads["rwkv_k_k"], grads["rwkv_a0"], grads["rwkv_w0"], da2p, dw2p, dg2p) = rwkv_prep_bwd(
        p, cols, [dr_s, dr_x, ddecay, dk_s, dk_x, dv_s, dv_x, dneg, dbpos, dgate], *small)
    grads["rwkv_w2"], grads["rwkv_a2"], grads["rwkv_g2"] = dw2p[:LW], da2p[LW:LW + LA], dg2p[LW + LA:lat]
    dz_rkv, dz_lat, dmu_rkv, dmu_lat = shift_bwd(cols, dzs, dzls, mu_rkv, mu_lat)
    grads["rwkv_mu"] = jnp.concatenate([dmu_rkv, dmu_lat[:, :lat]], axis=1)
    dp = jnp.concatenate([dz_rkv, dz_pool, dga, dgb, dz_lat], axis=1)
    dnm = _mm("din_proj", [dp], [w_in_p], "nt", [F32], tm=512)[0]
    dw_in_p = _mm("dw_in", [nm], [dp], "tn", [F32])[0]
    grads["w_in"] = jnp.concatenate([dw_in_p[:, :3 * W], dw_in_p[:, base:base + lat], dw_in_p[:, 3 * W:base]], axis=1)
    dh1, grads["ln_mix_pre"] = pre_bwd(dnm, h1, wts["ln_mix_pre"], dh2)
    df1, grads["ln_ffn1_post"] = post_bwd(dh1, f1, wts["ln_ffn1_post"], MACARON)
    dn1 = ffn_bwd("ffn1", df1, n1, g1, u1, act1, full["ffn1_gate"], full["ffn1_up"], full["ffn1_down"])
    grad_x, grads["ln_ffn1_pre"] = pre_bwd(dn1, x, wts["ln_ffn1_pre"], dh1)

    parts = {n: _shard_cols(grads[n]) for n in COL_SHARDED}
    parts.update({n: _shard_rows(grads[n]) for n in ('ffn1_down', 'ffn2_down', 'w_out')})
    parts["pool_w"] = jnp.transpose(dpool_w.reshape(n_groups, N_DEV, gshard, gd), (1, 0, 2, 3)).reshape(
        N_CHIP, 2, n_groups * gshard, gd)
    from_sibling = exchange_sibling([parts[n] for n in SHARDED])
    pair_sums = [pair_add(parts[n], rcv) for n, rcv in zip(SHARDED, from_sibling)]
    slabs = dict(zip(SHARDED, exchange_chips(pair_sums)))

    flat = jnp.concatenate([grads[n].reshape(-1) for n in REPLICATED])
    n_small = flat.shape[0]
    rows = _round_up(n_small, 8 * LANES) // LANES
    flat = jnp.concatenate([flat, jnp.zeros((rows * LANES - n_small,), F32)]).reshape(rows, LANES)
    small_slabs = all_gather([flat])[0]

    def packed(prefix):
        vals = jnp.concatenate([args[prefix + n].reshape(-1) for n in REPLICATED])
        return jnp.concatenate([vals, jnp.ones((rows * LANES - n_small,), F32)]).reshape(rows, LANES)

    outs = {}
    small_out = adamw(packed(""), packed("m_"), packed("v_"), small_slabs)
    offset = 0
    for n in REPLICATED:
        size = args[n].size
        outs[n] = [o.reshape(-1)[offset:offset + size].reshape(args[n].shape) for o in small_out]
        offset += size
    for n in SHARDED:
        shard2d = slabs[n].shape[1:]
        res = adamw(*[args[pre + n].reshape(shard2d) for pre in ("", "m_", "v_")], slabs[n])
        outs[n] = [o.reshape(args[n].shape) for o in res]

    loss = lax.psum(loss_part[0, 0], ("x", "y", "c"))
    return (loss, grad_x[None], *[outs[n][0] for n in WEIGHTS], *[outs[n][1] for n in WEIGHTS],
            *[outs[n][2] for n in WEIGHTS], *[outs[n][3] for n in WEIGHTS])


ARG_NAMES = ["x"] + WEIGHTS + ["loss_target"] + ["m_" + n for n in WEIGHTS] + ["v_" + n for n in WEIGHTS]


def kernel(x, ln_ffn1_pre, ln_ffn1_post, ffn1_gate, ffn1_up, ffn1_down, ln_mix_pre, ln_mix_post, w_in, rwkv_mu, rwkv_w0,
           rwkv_w2, rwkv_a0, rwkv_a2, rwkv_g2, rwkv_k_k, rwkv_k_a, rwkv_r_k, rwkv_gn_w, rwkv_gn_b, w_proj_a, pool_w,
           pool_scale, w_proj_b, w_out, ln_ffn2_pre, ln_ffn2_post, ffn2_gate, ffn2_up, ffn2_down, loss_target,
           m_ln_ffn1_pre, m_ln_ffn1_post, m_ffn1_gate, m_ffn1_up, m_ffn1_down, m_ln_mix_pre, m_ln_mix_post, m_w_in,
           m_rwkv_mu, m_rwkv_w0, m_rwkv_w2, m_rwkv_a0, m_rwkv_a2, m_rwkv_g2, m_rwkv_k_k, m_rwkv_k_a, m_rwkv_r_k,
           m_rwkv_gn_w, m_rwkv_gn_b, m_w_proj_a, m_pool_w, m_pool_scale, m_w_proj_b, m_w_out, m_ln_ffn2_pre,
           m_ln_ffn2_post, m_ffn2_gate, m_ffn2_up, m_ffn2_down, v_ln_ffn1_pre, v_ln_ffn1_post, v_ffn1_gate, v_ffn1_up,
           v_ffn1_down, v_ln_mix_pre, v_ln_mix_post, v_w_in, v_rwkv_mu, v_rwkv_w0, v_rwkv_w2, v_rwkv_a0, v_rwkv_a2,
           v_rwkv_g2, v_rwkv_k_k, v_rwkv_k_a, v_rwkv_r_k, v_rwkv_gn_w, v_rwkv_gn_b, v_w_proj_a, v_pool_w, v_pool_scale,
           v_w_proj_b, v_w_out, v_ln_ffn2_pre, v_ln_ffn2_post, v_ffn2_gate, v_ffn2_up, v_ffn2_down):
    given = locals()
    return _step({n: given[n] for n in ARG_NAMES})
```

```python
import jax
import jax.numpy as jnp
from jax import lax
from jax.experimental import pallas as pl
from jax.experimental.pallas import tpu as pltpu

F32, BF16 = jnp.float32, jnp.bfloat16
N_DEV = 8
N_CHIP = 4
HEAD = 64
LANES = 2 * HEAD
NORM_EPS, GN_EPS, L2_EPS = 1e-6, 64e-5, 1e-12
POOL_WINDOWS = (2, 4, 8, 16)
POOL_HALO = 16
MACARON = 0.5
ADAM_LR, ADAM_B1, ADAM_B2, ADAM_EPS, ADAM_WD, ADAM_STEP = 0.001, 0.9, 0.999, 1e-08, 0.01, 10
VMEM_LIMIT = 48 * 1024 * 1024
MM_VMEM_BUDGET = 36 * 1024 * 1024
ROW_TILE = 256
RWKV_ROW_TILE = 128
LAT_ALIGN = 512
WKV_CHUNK, WKV_PAIRS = 16, 8
WKV_UNROLL = 4
MESH = pl.DeviceIdType.MESH


def _pallas(body, **kw):
    return pl.pallas_call(body, **kw)


def _params(*sem):
    return pltpu.CompilerParams(dimension_semantics=sem, vmem_limit_bytes=VMEM_LIMIT)


def _tile(n, target, align=128):
    best = None
    for d in range(align, min(n, target) + 1, align):
        if n % d == 0:
            best = d
    return best if best is not None else n


def _round_up(n, m):
    return (n + m - 1) // m * m


ANY = pl.BlockSpec(memory_space=pl.ANY)


def _mm(name, a_list, b_list, mode, out_dtypes, *, sum_pairs=False, extras=(), epilogue=None, tm=1024, tn=512,
        carry=None):
    n_a, n_b = len(a_list), len(b_list)
    assert n_a in (1, n_b)
    a0, b0 = a_list[0], b_list[0]
    if mode == "nn":
        (M, K), N = a0.shape, b0.shape[1]
    elif mode == "nt":
        (M, K), N = a0.shape, b0.shape[0]
    else:
        (K, M), N = a0.shape, b0.shape[1]
    tm, tn = _tile(M, tm), _tile(N, tn)
    n_acc = 1 if sum_pairs else n_b
    n_ex = len(extras)

    def planned(tk):
        operands = 2 * 2 * tk * (n_a * tm + n_b * tn)
        tiles = 2 * tm * tn * (sum(e.dtype.itemsize for e, _ in extras) + sum(jnp.dtype(d).itemsize for d in out_dtypes))
        return operands + tiles + 4 * tm * tn * (n_acc + len(out_dtypes))

    tk = max([d for d in range(128, K + 1, 128) if K % d == 0 and planned(d) <= MM_VMEM_BUDGET] or [_tile(K, 512)])
    nk = K // tk
    if mode == "tn":
        a_spec = pl.BlockSpec((tk, tm), lambda i, j, k: (k, i))
    else:
        a_spec = pl.BlockSpec((tm, tk), lambda i, j, k: (i, k))
    if mode == "nt":
        b_spec = pl.BlockSpec((tn, tk), lambda i, j, k: (j, k))
    else:
        b_spec = pl.BlockSpec((tk, tn), lambda i, j, k: (k, j))
    contract = {"nn": ((1,), (0,)), "nt": ((1,), (1,)), "tn": ((0,), (0,))}[mode]
    e_specs = []
    for _, col in extras:
        assert col % tn == 0
        e_specs.append(pl.BlockSpec((tm, tn), lambda i, j, k, off=col // tn: (i, j + off)))
    o_spec = pl.BlockSpec((tm, tn), lambda i, j, k: (i, j))

    n_in, n_out, n_scr = n_a + n_b + n_ex, len(out_dtypes), (n_acc if nk > 1 else 0)
    c_in, c_out = (len(carry.inputs), len(carry.out_shapes)) if carry else (0, 0)
    grid = (M // tm, N // tn, nk)

    def body(*refs):
        a_refs, b_refs, e_refs = refs[:n_a], refs[n_a:n_a + n_b], refs[n_a + n_b:n_in]
        o_refs = refs[n_in + c_in:n_in + c_in + n_out]
        acc_refs = refs[n_in + c_in + n_out + c_out:n_in + c_in + n_out + c_out + n_scr]
        carried = (refs[n_in:n_in + c_in], refs[n_in + c_in + n_out:n_in + c_in + n_out + c_out],
                   refs[n_in + c_in + n_out + c_out + n_scr:])
        at = [pl.program_id(d) for d in range(3)]

        if carry:
            @pl.when((at[0] == 0) & (at[1] == 0) & (at[2] == 0))
            def _():
                carry.start(*carried)

        def products():
            a_vals = [a[...] for a in a_refs]
            prods = [lax.dot_general(a_vals[p if n_a > 1 else 0], b_refs[p][...], (contract, ((), ())),
                                     preferred_element_type=F32) for p in range(n_b)]
            return [sum(prods[1:], prods[0])] if sum_pairs else prods

        def finish(results):
            outs = epilogue(results, [e[...] for e in e_refs]) if epilogue else results
            for o_ref, o in zip(o_refs, outs):
                o_ref[...] = o.astype(o_ref.dtype)

        if nk == 1:
            finish(products())
        else:
            @pl.when(at[2] == 0)
            def _():
                for acc in acc_refs:
                    acc[...] = jnp.zeros_like(acc)

            for acc, prod in zip(acc_refs, products()):
                acc[...] += prod

            @pl.when(at[2] == nk - 1)
            def _():
                finish([acc[...] for acc in acc_refs])

        if carry:
            @pl.when((at[0] == grid[0] - 1) & (at[1] == grid[1] - 1) & (at[2] == grid[2] - 1))
            def _():
                carry.finish(*carried)

    res = _pallas(
        body, name=name, grid=grid,
        in_specs=[a_spec] * n_a + [b_spec] * n_b + e_specs + [ANY] * c_in,
        out_specs=[o_spec] * n_out + [ANY] * c_out,
        out_shape=[jax.ShapeDtypeStruct((M, N), dt) for dt in out_dtypes] + (list(carry.out_shapes) if carry else []),
        scratch_shapes=[pltpu.VMEM((tm, tn), F32)] * n_scr + (list(carry.scratch) if carry else []),
        compiler_params=_params("arbitrary", "arbitrary", "arbitrary") if carry else _params("parallel", "parallel", "arbitrary"),
    )(*a_list, *b_list, *[e for e, _ in extras], *(carry.inputs if carry else []))
    return (res[:n_out], res[n_out:]) if carry else res


def _swiglu_fwd_epi(accs, _):
    g, u = accs
    return [g, u, g * jax.nn.sigmoid(g) * u]


def _swiglu_bwd_epi(accs, ex):
    dact = accs[0]
    g, u = ex[0].astype(F32), ex[1].astype(F32)
    sg = jax.nn.sigmoid(g)
    return [dact * u * (sg * (1.0 + g * (1.0 - sg))), dact * (g * sg)]


def _mix_fwd_epi(accs, ex):
    ya, yb = accs
    return [jax.nn.sigmoid(ex[0]) * ya + jax.nn.sigmoid(ex[1]) * yb, ya, yb]


def _mix_bwd_epi(accs, ex):
    dm = accs[0]
    sa, sb = jax.nn.sigmoid(ex[0]), jax.nn.sigmoid(ex[1])
    ya, yb = ex[2].astype(F32), ex[3].astype(F32)
    return [dm * sa, dm * sb, dm * ya * sa * (1.0 - sa), dm * yb * sb * (1.0 - sb)]


def _row_call(name, body, T, tiled, params, outs, accs=(), prev=(), nxt=(), halo=8, tile=ROW_TILE):
    tm = min(tile, T)
    n_tiles = T // tm

    def norm(e):
        return e if isinstance(e, tuple) else (e, e.shape[1], 0)

    tiled, prev, nxt = [norm(e) for e in tiled], [norm(e) for e in prev], [norm(e) for e in nxt]
    per_halo, n_halo = tm // halo, T // halo
    in_specs = [pl.BlockSpec((tm, w), lambda i, cb=cb: (i, cb)) for _, w, cb in tiled]
    in_specs += [pl.BlockSpec((halo, w), lambda i, cb=cb: (jnp.maximum(i * per_halo - 1, 0), cb)) for _, w, cb in prev]
    in_specs += [pl.BlockSpec((halo, w), lambda i, cb=cb: (jnp.minimum((i + 1) * per_halo, n_halo - 1), cb))
                 for _, w, cb in nxt]
    in_specs += [pl.BlockSpec(p.shape, lambda i, nd=p.ndim: (0,) * nd) for p in params]
    out_specs = [pl.BlockSpec((tm, o.shape[1]), lambda i: (i, 0)) for o in outs]
    out_specs += [pl.BlockSpec(a.shape, lambda i, nd=len(a.shape): (0,) * nd) for a in accs]
    n1, n2, n3, n4, n5 = len(tiled), len(prev), len(nxt), len(params), len(outs)

    def kernel_body(*refs):
        i = pl.program_id(0)
        acc_refs = refs[n1 + n2 + n3 + n4 + n5:]

        @pl.when(i == 0)
        def _():
            for a in acc_refs:
                a[...] = jnp.zeros_like(a)

        body(i, n_tiles, refs[:n1], refs[n1:n1 + n2], refs[n1 + n2:n1 + n2 + n3],
             refs[n1 + n2 + n3:n1 + n2 + n3 + n4], refs[n1 + n2 + n3 + n4:n1 + n2 + n3 + n4 + n5], acc_refs)

    return _pallas(
        kernel_body, name=name, grid=(n_tiles,), in_specs=in_specs, out_specs=out_specs,
        out_shape=list(outs) + list(accs),
        compiler_params=_params("arbitrary"),
    )(*[e[0] for e in tiled + prev + nxt], *params)


def _sds(shape, dtype=F32):
    return jax.ShapeDtypeStruct(tuple(shape), dtype)


def _rstd(x):
    return lax.rsqrt(jnp.mean(x * x, axis=-1, keepdims=True) + NORM_EPS)


def _colsum(x):
    return jnp.sum(x, axis=0, keepdims=True)


def rms_pre(x, g):
    T, D = x.shape

    def body(i, n, tiled, prev, nxt, params, outs, accs):
        xv = tiled[0][...]
        outs[0][...] = (xv * _rstd(xv) * params[0][...]).astype(BF16)

    return _row_call("rms_pre", body, T, [x], [g], [_sds((T, D), BF16)])[0]


def post_pre(h, f, g_post, g_pre, scale):
    T, D = h.shape

    def body(i, n, tiled, prev, nxt, params, outs, accs):
        hv, fv = tiled[0][...], tiled[1][...]
        h2 = hv + scale * (fv * _rstd(fv) * params[0][...])
        outs[0][...] = h2
        outs[1][...] = (h2 * _rstd(h2) * params[1][...]).astype(BF16)

    return _row_call("post_pre", body, T, [h, f], [g_post, g_pre], [_sds((T, D)), _sds((T, D), BF16)])


def _post_bwd_math(dh, fv, g, scale):
    r = _rstd(fv)
    fhat = fv * r
    dy = scale * dh
    z = dy * g
    df = r * (z - fhat * jnp.mean(z * fhat, axis=-1, keepdims=True))
    return df, _colsum(dy * fhat)


def loss_post_bwd(h, f, g_post, target, scale):
    T, D = h.shape

    def body(i, n, tiled, prev, nxt, params, outs, accs):
        hv, fv, tv = tiled[0][...], tiled[1][...], tiled[2][...]
        g = params[0][...]
        e = hv + scale * (fv * _rstd(fv) * g) - tv
        accs[0][...] += jnp.full(accs[0].shape, 0.5 / D, F32) * jnp.sum(e * e)
        dh = e * (1.0 / D)
        outs[0][...] = dh
        df, dg = _post_bwd_math(dh, fv, g, scale)
        outs[1][...] = df.astype(BF16)
        accs[1][...] += dg

    return _row_call("loss_post_bwd", body, T, [h, f, target], [g_post],
                     [_sds((T, D)), _sds((T, D), BF16)], [_sds((1, LANES)), _sds((1, D))])


def post_bwd(dh, f, g_post, scale):
    T, D = dh.shape

    def body(i, n, tiled, prev, nxt, params, outs, accs):
        df, dg = _post_bwd_math(tiled[0][...], tiled[1][...], params[0][...], scale)
        outs[0][...] = df.astype(BF16)
        accs[0][...] += dg

    return _row_call("post_bwd", body, T, [dh, f], [g_post], [_sds((T, D), BF16)], [_sds((1, D))])


def pre_bwd(dn, h, g_pre, dres):
    T, D = h.shape

    def body(i, n, tiled, prev, nxt, params, outs, accs):
        dnv, hv = tiled[0][...], tiled[1][...]
        r = _rstd(hv)
        hhat = hv * r
        z = dnv * params[0][...]
        outs[0][...] = tiled[2][...] + r * (z - hhat * jnp.mean(z * hhat, axis=-1, keepdims=True))
        accs[0][...] += _colsum(dnv * hhat)

    return _row_call("pre_bwd", body, T, [dn, h, dres], [g_pre], [_sds((T, D))], [_sds((1, D))])


def _head_ones():
    i = lax.broadcasted_iota(jnp.int32, (LANES, LANES), 0)
    j = lax.broadcasted_iota(jnp.int32, (LANES, LANES), 1)
    return jnp.where((i < HEAD) == (j < HEAD), 1.0, 0.0).astype(F32)


def _headsum(x):
    e = _head_ones()
    parts = [jnp.dot(x[:, s:s + LANES], e, precision=lax.Precision.HIGHEST, preferred_element_type=F32)
             for s in range(0, x.shape[1], LANES)]
    return parts[0] if len(parts) == 1 else jnp.concatenate(parts, axis=1)


def _shift_down(x, before):
    row = lax.broadcasted_iota(jnp.int32, x.shape, 0)
    return jnp.where(row == 0, before, pltpu.roll(x, 1, 0))


def _shift_up(x, after):
    row = lax.broadcasted_iota(jnp.int32, x.shape, 0)
    return jnp.where(row == x.shape[0] - 1, after, pltpu.roll(x, x.shape[0] - 1, 0))


def _last_row(ref, keep):
    r = ref[ref.shape[0] - 1:ref.shape[0], :]
    return jnp.where(keep, r, jnp.zeros_like(r))


def _first_row(ref, keep):
    r = ref[0:1, :]
    return jnp.where(keep, r, jnp.zeros_like(r))


def _softplus(u):
    return jnp.maximum(u, 0.0) + jnp.log(1.0 + jnp.exp(-jnp.abs(u)))


def _dotb(a, b, contract):
    return lax.dot_general(a.astype(BF16), b.astype(BF16), (contract, ((), ())), preferred_element_type=F32)


_NN, _NT, _TN = ((1,), (0,)), ((1,), (1,)), ((0,), (0,))


def _prep_forward(z, zprev_row, zl, zlprev_row, mu, mul, w0, a0, kk_w, ka_w, w2p, a2p, g2p):
    W = w0.shape[1]
    zs = z + (_shift_down(z, zprev_row) - z) * mu
    zls = zl + (_shift_down(zl, zlprev_row) - zl) * mul
    r, k, v = zs[:, :W], zs[:, W:2 * W], zs[:, 2 * W:]
    th, sg = jnp.tanh(zls), jax.nn.sigmoid(zls)
    xw = w0 + _dotb(th, w2p, _NN)
    wlog = -_softplus(-xw) - 0.5
    ew = jnp.exp(wlog)
    decay = jnp.exp(-ew)
    a = jax.nn.sigmoid(a0 + _dotb(zls, a2p, _NN))
    gate = _dotb(sg, g2p, _NN)
    q = k * kk_w
    nrm = jnp.sqrt(_headsum(q * q))
    den = jnp.maximum(nrm, L2_EPS)
    kk = q / den
    kmod = k * (1.0 + (a - 1.0) * ka_w)
    return dict(zs=zs, zls=zls, r=r, k=k, v=v, th=th, sg=sg, xw=xw, ew=ew, decay=decay, a=a, gate=gate,
                nrm=nrm, den=den, kk=kk, kmod=kmod)


def rwkv_prep(p, cols, mu, mul, w0, a0, kk_w, ka_w, w2p, a2p, g2p):
    T = p.shape[0]
    W = w0.shape[1]

    def body(i, n, tiled, prev, nxt, params, outs, accs):
        c = _prep_forward(tiled[0][...], _last_row(prev[0], i > 0), tiled[1][...], _last_row(prev[1], i > 0),
                          *[q[...] for q in params])
        for o, val in zip(outs, (c["r"], c["decay"], c["kmod"], c["v"], -c["kk"], c["kk"] * c["a"], c["gate"])):
            o[...] = val

    return _row_call("rwkv_prep", body, T, [cols["rkv"], cols["lat"]],
                     [mu, mul, w0, a0, kk_w, ka_w, w2p, a2p, g2p], [_sds((T, W))] * 7,
                     prev=[cols["rkv"], cols["lat"]], tile=RWKV_ROW_TILE)


def _post_forward(y, r, kmod, v, gn_w, gn_b, rk):
    mean = _headsum(y) * (1.0 / HEAD)
    yc = y - mean
    rstd = lax.rsqrt(_headsum(yc * yc) * (1.0 / HEAD) + GN_EPS)
    yn = yc * rstd
    s = _headsum(r * kmod * rk)
    return yn, rstd, s, yn * gn_w + gn_b + s * v


def rwkv_post(y, r, kmod, v, gate, gn_w, gn_b, rk):
    T, W = y.shape

    def body(i, n, tiled, prev, nxt, params, outs, accs):
        yv, rv, kv, vv, gv = [t[...] for t in tiled]
        _, _, _, o = _post_forward(yv, rv, kv, vv, *[q[...] for q in params])
        outs[0][...] = (o * gv).astype(BF16)

    return _row_call("rwkv_post", body, T, [y, r, kmod, v, gate], [gn_w, gn_b, rk], [_sds((T, W), BF16)],
                     tile=RWKV_ROW_TILE)[0]


def rwkv_post_bwd(dout, y, r, kmod, v, gate, gn_w, gn_b, rk):
    T, W = y.shape

    def body(i, n, tiled, prev, nxt, params, outs, accs):
        dv_, yv, rv, kv, vv, gv = [t[...] for t in tiled]
        gn_w_, gn_b_, rk_ = [q[...] for q in params]
        yn, rstd, s, o = _post_forward(yv, rv, kv, vv, gn_w_, gn_b_, rk_)
        do = dv_ * gv
        outs[4][...] = dv_ * o
        accs[0][...] += _colsum(do * yn)
        accs[1][...] += _colsum(do)
        dyn = do * gn_w_
        outs[0][...] = rstd * (dyn - _headsum(dyn) * (1.0 / HEAD) - yn * (_headsum(dyn * yn) * (1.0 / HEAD)))
        ds = _headsum(do * vv)
        outs[1][...] = ds * kv * rk_
        outs[2][...] = ds * rv * rk_
        outs[3][...] = do * s
        accs[2][...] += _colsum(ds * rv * kv)

    return _row_call("rwkv_post_bwd", body, T, [dout, y, r, kmod, v, gate], [gn_w, gn_b, rk],
                     [_sds((T, W))] * 5, [_sds((1, W))] * 3, tile=RWKV_ROW_TILE)


def rwkv_prep_bwd(p, cols, grads, mu, mul, w0, a0, kk_w, ka_w, w2p, a2p, g2p):
    T = p.shape[0]
    W = w0.shape[1]
    latp = w2p.shape[0]

    def body(i, n, tiled, prev, nxt, params, outs, accs):
        pv = [q[...] for q in params]
        mu_, mul_, w0_, a0_, kk_w_, ka_w_, w2p_, a2p_, g2p_ = pv
        c = _prep_forward(tiled[0][...], _last_row(prev[0], i > 0), tiled[1][...], _last_row(prev[1], i > 0), *pv)
        dr_s, dr_x, ddecay, dk_s, dk_x, dv_s, dv_x, dneg, db, dgate = [t[...] for t in tiled[2:]]
        k, a, kk = c["k"], c["a"], c["kk"]
        dkmod = dk_s + dk_x
        dk = dkmod * (1.0 + (a - 1.0) * ka_w_)
        da = dkmod * k * ka_w_ + db * kk
        accs[0][...] += _colsum(dkmod * k * (a - 1.0))
        dkk = db * a - dneg
        dq = jnp.where(c["nrm"] > L2_EPS, dkk - kk * _headsum(dkk * kk), dkk) / c["den"]
        dk = dk + dq * kk_w_
        accs[1][...] += _colsum(dq * k)
        dxa = da * a * (1.0 - a)
        accs[2][...] += _colsum(dxa)
        accs[4][...] += _dotb(c["zls"], dxa, _TN)
        dzls = _dotb(dxa, a2p_, _NT)
        dxw = (-ddecay * c["decay"] * c["ew"]) * jax.nn.sigmoid(-c["xw"])
        accs[3][...] += _colsum(dxw)
        accs[5][...] += _dotb(c["th"], dxw, _TN)
        dzls = dzls + _dotb(dxw, w2p_, _NT) * (1.0 - c["th"] * c["th"])
        accs[6][...] += _dotb(c["sg"], dgate, _TN)
        dzls = dzls + _dotb(dgate, g2p_, _NT) * c["sg"] * (1.0 - c["sg"])
        outs[0][...] = jnp.concatenate([dr_s + dr_x, dk, dv_s + dv_x], axis=1)
        outs[1][...] = dzls

    return _row_call("rwkv_prep_bwd", body, T, [cols["rkv"], cols["lat"]] + list(grads),
                     [mu, mul, w0, a0, kk_w, ka_w, w2p, a2p, g2p], [_sds((T, 3 * W)), _sds((T, latp))],
                     [_sds((1, W))] * 4 + [_sds((latp, W))] * 3, prev=[cols["rkv"], cols["lat"]], tile=RWKV_ROW_TILE)


def shift_bwd(cols, dzs, dzls, mu, mul):
    T = dzs.shape[0]

    def body(i, n, tiled, prev, nxt, params, outs, accs):
        for j in range(2):
            z, d, m = tiled[j][...], tiled[2 + j][...], params[j][...]
            zprev = _shift_down(z, _last_row(prev[j], i > 0))
            dnext = _shift_up(d, _first_row(nxt[j], i < n - 1))
            outs[j][...] = (d * (1.0 - m) + dnext * m).astype(BF16)
            accs[j][...] += _colsum(d * (zprev - z))

    return _row_call("shift_bwd", body, T, [cols["rkv"], cols["lat"], dzs, dzls], [mu, mul],
                     [_sds(dzs.shape, BF16), _sds(dzls.shape, BF16)], [_sds(mu.shape), _sds(mul.shape)],
                     prev=[cols["rkv"], cols["lat"]], nxt=[dzs, dzls])


def _window_pick(x, windows):
    gid = lax.broadcasted_iota(jnp.int32, x.shape, 1) // (x.shape[1] // len(windows))
    out = windows[-1]
    for g in range(len(windows) - 2, -1, -1):
        out = jnp.where(gid == g, windows[g], out)
    return out


def _pool_counts(t0, rows, width):
    t = (t0 + lax.broadcasted_iota(jnp.int32, (rows, width), 0) + 1).astype(F32)
    return _window_pick(t, [jnp.minimum(t, float(w)) for w in POOL_WINDOWS])


def _pool_mixed(x, before, t0):
    tm, width = x.shape
    xe = jnp.concatenate([before, x], axis=0)
    sums, s, span = [], xe, 1
    for w in POOL_WINDOWS:
        while span < w:
            s = s + pltpu.roll(s, span, 0)
            span *= 2
        sums.append(s[POOL_HALO:, :])
    return _window_pick(x, sums) / _pool_counts(t0, tm, width) - x


def _group_dot(x, w_ref, contract):
    gd = w_ref.shape[-1]
    parts = [_dotb(x[:, g * gd:(g + 1) * gd], w_ref[g], contract) for g in range(w_ref.shape[0])]
    return jnp.concatenate(parts, axis=1)


def pool_fwd(cols, pool_w, pool_scale):
    T, width = cols["pool"][0].shape[0], cols["pool"][1]
    tm = min(ROW_TILE, T)

    def body(i, n, tiled, prev, nxt, params, outs, accs):
        before = jnp.where(i > 0, prev[0][...], 0.0)
        mixed = _pool_mixed(tiled[0][...], before, i * tm)
        outs[0][...] = (_group_dot(mixed, params[0], _NN) * params[1][...]).astype(BF16)

    return _row_call("pool_fwd", body, T, [cols["pool"]], [pool_w, pool_scale], [_sds((T, width), BF16)],
                     prev=[cols["pool"]], halo=POOL_HALO)[0]


def pool_bwd(cols, dout, pool_w, pool_scale):
    T, width = dout.shape
    tm = min(ROW_TILE, T)

    def body(i, n, tiled, prev, nxt, params, outs, accs):
        w_ref, scale = params[0], params[1][...]
        before = jnp.where(i > 0, prev[0][...], 0.0)
        mixed = _pool_mixed(tiled[0][...], before, i * tm)
        dv = tiled[1][...]
        accs[1][...] += _colsum(dv * _group_dot(mixed, w_ref, _NN))
        after = jnp.where(i < n - 1, nxt[0][...], 0.0)
        dys = jnp.concatenate([dv, after], axis=0) * scale
        gd = w_ref.shape[-1]
        for g in range(w_ref.shape[0]):
            accs[0][g] += _dotb(mixed[:, g * gd:(g + 1) * gd], dys[:tm, g * gd:(g + 1) * gd], _TN)
        dmixed = _group_dot(dys, w_ref, _NT)
        u = dmixed / _pool_counts(i * tm, tm + POOL_HALO, width)
        rows = tm + POOL_HALO
        sums, s, span = [], u, 1
        for w in POOL_WINDOWS:
            while span < w:
                s = s + pltpu.roll(s, rows - span, 0)
                span *= 2
            sums.append(s[:tm, :])
        outs[0][...] = (_window_pick(dv, sums) - dmixed[:tm, :]).astype(BF16)

    return _row_call("pool_bwd", body, T, [cols["pool"], dout], [pool_w, pool_scale], [_sds((T, width), BF16)],
                     [_sds(pool_w.shape), _sds((1, width))], prev=[cols["pool"]], nxt=[dout], halo=POOL_HALO)


def _wkv_consts(pairs):
    lane = lax.broadcasted_iota(jnp.int32, (HEAD, LANES), 1)
    sub = lax.broadcasted_iota(jnp.int32, (pairs * HEAD, LANES), 0)
    lane_all = lax.broadcasted_iota(jnp.int32, (pairs * HEAD, LANES), 1)
    i = lax.broadcasted_iota(jnp.int32, (LANES, LANES), 0)
    j = lax.broadcasted_iota(jnp.int32, (LANES, LANES), 1)
    ones = jnp.where((i < HEAD) == (j < HEAD), 1.0, 0.0).astype(BF16)
    diag = jnp.where((lane_all & (HEAD - 1)) == (sub & (HEAD - 1)), 1.0, 0.0).astype(F32)
    return lane < HEAD, diag, ones


def _segsum(p, in_a):
    sa = jnp.sum(jnp.where(in_a, p, 0.0), axis=1, keepdims=True)
    sb = jnp.sum(jnp.where(in_a, 0.0, p), axis=1, keepdims=True)
    return jnp.where(in_a, sa, sb)


def _hi_lo(p):
    hi = lax.bitcast_convert_type(lax.bitcast_convert_type(p, jnp.uint32) & jnp.uint32(0xFFFF0000), F32)
    return hi, p - hi


def _segsum_mxu(p, ones):
    hi, lo = _hi_lo(p)
    return (jnp.dot(hi.astype(BF16), ones, preferred_element_type=F32)
            + jnp.dot(lo.astype(BF16), ones, preferred_element_type=F32))


def _cat(parts, axis):
    return parts[0] if len(parts) == 1 else jnp.concatenate(parts, axis=axis)


def _tile_rows(row, pairs):
    return _cat([jnp.broadcast_to(row[:, p * LANES:(p + 1) * LANES], (HEAD, LANES)) for p in range(pairs)], 0)


def _spread(row, pairs, diag16, ones):
    hi, lo = _hi_lo(row)
    return (jnp.dot(_tile_rows(hi.astype(BF16), pairs) * diag16, ones, preferred_element_type=F32)
            + jnp.dot(_tile_rows(lo.astype(BF16), pairs) * diag16, ones, preferred_element_type=F32))


def _pair_colsums(x, pairs):
    return _cat([_colsum(x[p * HEAD:(p + 1) * HEAD]) for p in range(pairs)], 1)


def wkv_fwd(r, w, k, v, a, b):
    T, W = r.shape
    P = W // LANES
    PB = min(WKV_PAIRS, P)
    chunk = min(WKV_CHUNK, T)
    NC = T // chunk
    R = PB * HEAD

    def body(r_ref, w_ref, k_ref, v_ref, a_ref, b_ref, y_ref, st_ref, sa_ref, vt_ref, s_ref):
        c = pl.program_id(1)

        @pl.when(c == 0)
        def _():
            s_ref[...] = jnp.zeros_like(s_ref)

        in_a, diag, ones = _wkv_consts(PB)
        diag16 = diag.astype(BF16)

        def spread(t, _):
            vt_ref[t] = _spread(v_ref[pl.ds(t, 1), :], PB, diag16, ones)
            return 0

        lax.fori_loop(0, chunk, spread, 0, unroll=WKV_UNROLL)

        def step(t, _):
            rows = [ref[pl.ds(t, 1), :] for ref in (w_ref, k_ref, a_ref, b_ref)]
            for p in range(PB):
                wt, kt, at, bt = [x[:, p * LANES:(p + 1) * LANES] for x in rows]
                rs = pl.ds(p * HEAD, HEAD)
                S = s_ref[rs]
                sa = _segsum(S * at, in_a)
                sa_ref[t, rs] = sa
                S = S * wt + sa * bt + vt_ref[t, rs] * kt
                st_ref[t, rs] = S
                s_ref[rs] = S
            return 0

        lax.fori_loop(0, chunk, step, 0, unroll=WKV_UNROLL)

        def readout(t, _):
            yt = _segsum_mxu(st_ref[t] * _tile_rows(r_ref[pl.ds(t, 1), :], PB), ones) * diag
            y_ref[pl.ds(t, 1), :] = _pair_colsums(yt, PB)
            return 0

        lax.fori_loop(0, chunk, readout, 0, unroll=WKV_UNROLL)

    spec = pl.BlockSpec((chunk, PB * LANES), lambda g, c: (c, g))
    tiles = pl.BlockSpec((chunk, R, LANES), lambda g, c: (c, g, 0))
    return _pallas(
        body, name="wkv_fwd", grid=(P // PB, NC), in_specs=[spec] * 6, out_specs=[spec, tiles, tiles],
        out_shape=[_sds((T, W)), _sds((T, P * HEAD, LANES)), _sds((T, P * HEAD, LANES))],
        scratch_shapes=[pltpu.VMEM((chunk, R, LANES), F32), pltpu.VMEM((R, LANES), F32)],
        compiler_params=_params("parallel", "arbitrary"),
    )(r, w, k, v, a, b)


def wkv_bwd(r, w, k, v, a, b, dy, st, sa):
    T, W = r.shape
    P = W // LANES
    PB = min(WKV_PAIRS, P)
    chunk = min(WKV_CHUNK, T)
    NC = T // chunk
    R = PB * HEAD

    def body(r_ref, w_ref, k_ref, v_ref, a_ref, b_ref, dy_ref, st_ref, before_ref, sa_ref,
             dr_ref, dw_ref, dk_ref, dv_ref, da_ref, db_ref, ds_ref, vt_ref, dyt_ref, dst_ref, dsa_ref):
        c = pl.program_id(1)

        @pl.when(c == 0)
        def _():
            ds_ref[...] = jnp.zeros_like(ds_ref)

        in_a, diag, ones = _wkv_consts(PB)
        diag16 = diag.astype(BF16)

        def spread(t, _):
            vt_ref[t] = _spread(v_ref[pl.ds(t, 1), :], PB, diag16, ones)
            dyt_ref[t] = _spread(dy_ref[pl.ds(t, 1), :], PB, diag16, ones)
            return 0

        lax.fori_loop(0, chunk, spread, 0, unroll=WKV_UNROLL)

        def bstep(n, _):
            t = chunk - 1 - n
            rows = [ref[pl.ds(t, 1), :] for ref in (r_ref, w_ref, a_ref, b_ref)]
            for p in range(PB):
                rt, wt, at, bt = [x[:, p * LANES:(p + 1) * LANES] for x in rows]
                rs = pl.ds(p * HEAD, HEAD)
                dS = ds_ref[rs] + dyt_ref[t, rs] * rt
                dst_ref[t, rs] = dS
                dsa = _segsum(dS * bt, in_a)
                dsa_ref[t, rs] = dsa
                ds_ref[rs] = dS * wt + dsa * at
            return 0

        lax.fori_loop(0, chunk, bstep, 0, unroll=WKV_UNROLL)

        def collect(t, _):
            sn, dS, dsa = st_ref[t], dst_ref[t], dsa_ref[t]
            first = jnp.where(c == NC - 1, 0.0, before_ref[0])
            sp = jnp.where(t > 0, st_ref[jnp.maximum(t - 1, 0)], first)
            dvt = _segsum_mxu(dS * _tile_rows(k_ref[pl.ds(t, 1), :], PB), ones) * diag
            for ref, val in ((dr_ref, sn * dyt_ref[t]), (dw_ref, dS * sp), (dk_ref, dS * vt_ref[t]), (dv_ref, dvt),
                             (da_ref, sp * dsa), (db_ref, dS * sa_ref[t])):
                ref[pl.ds(t, 1), :] = _pair_colsums(val, PB)
            return 0

        lax.fori_loop(0, chunk, collect, 0, unroll=WKV_UNROLL)

    spec = pl.BlockSpec((chunk, PB * LANES), lambda g, c: (NC - 1 - c, g))
    tiles = pl.BlockSpec((chunk, R, LANES), lambda g, c: (NC - 1 - c, g, 0))
    before = pl.BlockSpec((1, R, LANES), lambda g, c: (jnp.maximum((NC - 1 - c) * chunk - 1, 0), g, 0))

    def scratch(n):
        return pltpu.VMEM((n, R, LANES), F32)

    return _pallas(
        body, name="wkv_bwd", grid=(P // PB, NC), in_specs=[spec] * 7 + [tiles, before, tiles],
        out_specs=[spec] * 6, out_shape=[_sds((T, W))] * 6,
        scratch_shapes=[pltpu.VMEM((R, LANES), F32), scratch(chunk), scratch(chunk), scratch(chunk), scratch(chunk)],
        compiler_params=_params("parallel", "arbitrary"),
    )(r, w, k, v, a, b, dy, st, st, sa)


def _position():
    return lax.axis_index("x"), lax.axis_index("y"), lax.axis_index("c")


def _other_chips(x, y):
    return [(1 - x, y), (x, 1 - y), (1 - x, 1 - y)]


class _Carry:
    def __init__(self, inputs, out_shapes, scratch, start, finish):
        self.inputs, self.out_shapes, self.scratch, self.start, self.finish = inputs, out_shapes, scratch, start, finish


def _run_carry(name, carry):
    n_in, n_out = len(carry.inputs), len(carry.out_shapes)

    def body(*refs):
        parts = refs[:n_in], refs[n_in:n_in + n_out], refs[n_in + n_out:]
        carry.start(*parts)
        carry.finish(*parts)

    return _pallas(body, name=name, in_specs=[ANY] * n_in, out_specs=[ANY] * n_out, out_shape=list(carry.out_shapes),
                   scratch_shapes=list(carry.scratch))(*carry.inputs)


def gather_carry(shards):
    n = len(shards)

    def plan(x_refs, out_refs, sems):
        send_sems, recv_sems, local_sems = sems
        x, y, c = _position()
        me, sibling = (x, y, c), (x, y, 1 - c)
        chips = _other_chips(x, y)

        def slot(ref, pos):
            return ref.at[4 * pos[0] + 2 * pos[1] + pos[2]]

        def copy(t, j, block, to, src=None):
            dst = slot(out_refs[t], block)
            return pltpu.make_async_remote_copy(
                src_ref=dst if src is None else src, dst_ref=dst, send_sem=send_sems.at[t, j],
                recv_sem=recv_sems.at[t, j], device_id=to, device_id_type=MESH)

        mine = [pltpu.make_async_copy(x_refs[t], slot(out_refs[t], me), local_sems.at[t]) for t in range(n)]
        first = []
        for t in range(n):
            first.append(copy(t, 0, me, sibling, src=x_refs[t]))
            first += [copy(t, 1 + j, me, (*chip, c), src=x_refs[t]) for j, chip in enumerate(chips)]
        return c, me, sibling, chips, copy, mine, first

    def start(x_refs, out_refs, sems):
        _, _, _, _, _, mine, first = plan(x_refs, out_refs, sems)
        for cp in mine + first:
            cp.start()

    def finish(x_refs, out_refs, sems):
        c, me, sibling, chips, copy, mine, first = plan(x_refs, out_refs, sems)
        passed = []
        for t in range(n):
            for j, chip in enumerate(chips):
                copy(t, 1 + j, (*chip, c), me).wait_recv()
                fwd = copy(t, 4 + j, (*chip, c), sibling)
                fwd.start()
                passed.append(fwd)
        for t in range(n):
            copy(t, 0, sibling, me).wait_recv()
            for j, chip in enumerate(chips):
                copy(t, 4 + j, (*chip, 1 - c), me).wait_recv()
        for cp in first + passed:
            cp.wait_send()
        for cp in mine:
            cp.wait()

    return _Carry(list(shards), [_sds((N_DEV,) + s.shape, s.dtype) for s in shards],
                  [pltpu.SemaphoreType.DMA((n, 7)), pltpu.SemaphoreType.DMA((n, 7)), pltpu.SemaphoreType.DMA((n,))],
                  start, finish)


def all_gather(shards):
    return _run_carry("all_gather", gather_carry(shards))


def exchange_sibling(parts):
    n = len(parts)

    def body(*refs):
        p_refs, out_refs = refs[:n], refs[n:2 * n]
        send_sems, recv_sems = refs[2 * n:]
        x, y, c = _position()
        copies = []
        for t in range(n):
            for q in range(N_CHIP):
                cp = pltpu.make_async_remote_copy(
                    src_ref=p_refs[t].at[q, 1 - c], dst_ref=out_refs[t].at[q], send_sem=send_sems.at[t, q],
                    recv_sem=recv_sems.at[t, q], device_id=(x, y, 1 - c), device_id_type=MESH)
                cp.start()
                copies.append(cp)
        for cp in copies:
            cp.wait()

    return _pallas(
        body, name="exchange_sibling", in_specs=[ANY] * n, out_specs=[ANY] * n,
        out_shape=[_sds((N_CHIP,) + p.shape[2:], p.dtype) for p in parts],
        scratch_shapes=[pltpu.SemaphoreType.DMA((n, N_CHIP)), pltpu.SemaphoreType.DMA((n, N_CHIP))],
    )(*parts)


def chips_carry(parts):
    n = len(parts)

    def plan(p_refs, out_refs, sems):
        send_sems, recv_sems, local_sems = sems
        x, y, c = _position()
        local = [pltpu.make_async_copy(p_refs[t].at[2 * x + y], out_refs[t].at[3], local_sems.at[t]) for t in range(n)]
        remote = [pltpu.make_async_remote_copy(
            src_ref=p_refs[t].at[2 * cx + cy], dst_ref=out_refs[t].at[j], send_sem=send_sems.at[t, j],
            recv_sem=recv_sems.at[t, j], device_id=(cx, cy, c), device_id_type=MESH)
            for t in range(n) for j, (cx, cy) in enumerate(_other_chips(x, y))]
        return local, remote

    def start(p_refs, out_refs, sems):
        local, remote = plan(p_refs, out_refs, sems)
        for cp in local + remote:
            cp.start()

    def finish(p_refs, out_refs, sems):
        local, remote = plan(p_refs, out_refs, sems)
        for cp in remote + local:
            cp.wait()

    return _Carry(list(parts), [_sds(p.shape, p.dtype) for p in parts],
                  [pltpu.SemaphoreType.DMA((n, 3)), pltpu.SemaphoreType.DMA((n, 3)), pltpu.SemaphoreType.DMA((n,))],
                  start, finish)


def _flat_tile(rows, cols):
    tr = rows
    for d in range(16, min(rows, 512) + 1, 16):
        if rows % d == 0 and d * cols * 4 <= 2 * 1024 * 1024:
            tr = d
    return tr


def pair_add(part, recv):
    _, _, R, C = part.shape
    tr = _flat_tile(R, C)
    core = jnp.reshape(lax.axis_index("c"), (1,)).astype(jnp.int32)

    def body(core_ref, p_ref, r_ref, o_ref):
        o_ref[...] = (p_ref[...] + r_ref[...]).astype(BF16)

    grid_spec = pltpu.PrefetchScalarGridSpec(
        num_scalar_prefetch=1, grid=(N_CHIP, R // tr),
        in_specs=[pl.BlockSpec((None, None, tr, C), lambda q, i, core_ref: (q, core_ref[0], i, 0)),
                  pl.BlockSpec((None, tr, C), lambda q, i, core_ref: (q, i, 0))],
        out_specs=pl.BlockSpec((None, tr, C), lambda q, i, core_ref: (q, i, 0)))
    return _pallas(body, name="pair_add", grid_spec=grid_spec, out_shape=_sds((N_CHIP, R, C), BF16),
                   compiler_params=_params("parallel", "parallel"))(core, part, recv)


def adamw(w, m, v, slabs):
    R, C = w.shape
    tr = _flat_tile(R, C)
    n = slabs.shape[0]

    def body(w_ref, m_ref, v_ref, s_ref, g_ref, d_ref, nm_ref, nv_ref):
        g = s_ref[0].astype(F32)
        for j in range(1, n):
            g = g + s_ref[j].astype(F32)
        m2 = ADAM_B1 * m_ref[...] + (1.0 - ADAM_B1) * g
        v2 = ADAM_B2 * v_ref[...] + (1.0 - ADAM_B2) * (g * g)
        m_hat = m2 / (1.0 - ADAM_B1 ** ADAM_STEP)
        v_hat = v2 / (1.0 - ADAM_B2 ** ADAM_STEP)
        g_ref[...] = g
        d_ref[...] = -ADAM_LR * (m_hat / (jnp.sqrt(v_hat) + ADAM_EPS) + ADAM_WD * w_ref[...])
        nm_ref[...] = m2
        nv_ref[...] = v2

    spec = pl.BlockSpec((tr, C), lambda i: (i, 0))
    return _pallas(body, name="adamw", grid=(R // tr,),
                   in_specs=[spec] * 3 + [pl.BlockSpec((n, tr, C), lambda i: (0, i, 0))], out_specs=[spec] * 4,
                   out_shape=[_sds((R, C))] * 4, compiler_params=_params("parallel"))(w, m, v, slabs)


def _unshard_cols(g):
    return jnp.transpose(g, (1, 0, 2)).reshape(g.shape[1], -1)


def _unshard_rows(g):
    return g.reshape(-1, g.shape[2])


def _shard_cols(full):
    R, C = full.shape
    return jnp.transpose(full.reshape(R, N_DEV, C // N_DEV), (1, 0, 2)).reshape(N_CHIP, 2, R, C // N_DEV)


def _shard_rows(full):
    R, C = full.shape
    return full.reshape(N_CHIP, 2, R // N_DEV, C)


WEIGHTS = ['ln_ffn1_pre', 'ln_ffn1_post', 'ffn1_gate', 'ffn1_up', 'ffn1_down', 'ln_mix_pre', 'ln_mix_post', 'w_in',
           'rwkv_mu', 'rwkv_w0', 'rwkv_w2', 'rwkv_a0', 'rwkv_a2', 'rwkv_g2', 'rwkv_k_k', 'rwkv_k_a', 'rwkv_r_k',
           'rwkv_gn_w', 'rwkv_gn_b', 'w_proj_a', 'pool_w', 'pool_scale', 'w_proj_b', 'w_out', 'ln_ffn2_pre',
           'ln_ffn2_post', 'ffn2_gate', 'ffn2_up', 'ffn2_down']
COL_SHARDED = ['ffn1_gate', 'ffn1_up', 'ffn2_gate', 'ffn2_up', 'w_in', 'rwkv_w2', 'rwkv_a2', 'rwkv_g2', 'w_proj_a',
               'w_proj_b']
ROW_SHARDED = ['ffn1_down', 'ffn2_down', 'w_out', 'pool_w']
SHARDED = COL_SHARDED + ROW_SHARDED
REPLICATED = [n for n in WEIGHTS if n not in SHARDED]


def _step(args):
    wts = {n: args[n] if args[n].ndim == 2 else args[n][0] for n in WEIGHTS}
    x, target = args["x"][0], args["loss_target"][0]
    T, D = x.shape
    W = wts["rwkv_w0"].shape[1]
    PW = wts["pool_scale"].shape[1]
    LW, LA, LG = wts["rwkv_w2"].shape[0], wts["rwkv_a2"].shape[0], wts["rwkv_g2"].shape[0]
    lat = LW + LA + LG
    latp = _round_up(lat, LAT_ALIGN)
    rc = 3 * W + lat
    base = 3 * W + PW + 2 * D
    n_groups, gshard, gd = wts["pool_w"].shape

    pool_w_shard = wts["pool_w"].reshape(n_groups * gshard, gd)
    shards = {n: (pool_w_shard if n == "pool_w" else wts[n]).astype(BF16) for n in SHARDED}
    full = {}

    def fetch(names):
        return gather_carry([shards[n] for n in names])

    def arrived(names, got):
        for n, g in zip(names, got):
            if n == "pool_w":
                full[n] = jnp.transpose(g.reshape(N_DEV, n_groups, gshard, gd), (1, 0, 2, 3)).reshape(n_groups, gd, gd)
            else:
                full[n] = _unshard_cols(g) if n in COL_SHARDED else _unshard_rows(g)

    arrived(["ffn1_gate", "ffn1_up"], all_gather([shards["ffn1_gate"], shards["ffn1_up"]]))
    mu = wts["rwkv_mu"]
    mu_rkv = mu[:, :3 * W]
    mu_lat = jnp.concatenate([mu[:, 3 * W:], jnp.zeros((1, latp - lat), F32)], axis=1)
    rk = wts["rwkv_r_k"].reshape(1, W)

    n1 = rms_pre(x, wts["ln_ffn1_pre"])
    (g1, u1, act1), got = _mm("ffn1_up", [n1], [full["ffn1_gate"], full["ffn1_up"]], "nn", [BF16] * 3,
                              epilogue=_swiglu_fwd_epi, carry=fetch(["ffn1_down"]))
    arrived(["ffn1_down"], got)
    (f1,), got = _mm("ffn1_down", [act1], [full["ffn1_down"]], "nn", [F32], tm=512, carry=fetch(["w_in"]))
    arrived(["w_in"], got)
    w_in = full["w_in"]
    w_in_p = jnp.concatenate([w_in[:, :3 * W], w_in[:, rc:], w_in[:, 3 * W:rc], jnp.zeros((D, latp - lat), BF16)], axis=1)
    h1, nm = post_pre(x, f1, wts["ln_ffn1_post"], wts["ln_mix_pre"], MACARON)
    mixer = ["rwkv_w2", "rwkv_a2", "rwkv_g2", "w_proj_a", "w_proj_b", "pool_w", "w_out"]
    (p,), got = _mm("in_proj", [nm], [w_in_p], "nn", [F32], carry=fetch(mixer))
    arrived(mixer, got)
    pool_w = full["pool_w"]

    def pad_rows(m, at):
        return jnp.zeros((latp, W), BF16).at[at:at + m.shape[0]].set(m)

    w2p, a2p, g2p = pad_rows(full["rwkv_w2"], 0), pad_rows(full["rwkv_a2"], LW), pad_rows(full["rwkv_g2"], LW + LA)
    small = [mu_rkv, mu_lat, wts["rwkv_w0"], wts["rwkv_a0"], wts["rwkv_k_k"], wts["rwkv_k_a"], w2p, a2p, g2p]
    cols = {"rkv": (p, 3 * W, 0), "pool": (p, PW, 3 * W // PW), "lat": (p, latp, base // latp)}
    r, decay, kmod, v, aneg, bpos, gate = rwkv_prep(p, cols, *small)
    y, states, sdota = wkv_fwd(r, decay, kmod, v, aneg, bpos)
    ya_in = rwkv_post(y, r, kmod, v, gate, wts["rwkv_gn_w"], wts["rwkv_gn_b"], rk)
    yb_in = pool_fwd(cols, pool_w, wts["pool_scale"])
    gates = [(p, 3 * W + PW), (p, 3 * W + PW + D)]
    (m, ya, yb), got = _mm("mix", [ya_in, yb_in], [full["w_proj_a"], full["w_proj_b"]], "nn", [BF16] * 3,
                           extras=gates, epilogue=_mix_fwd_epi, carry=fetch(["ffn2_gate"]))
    arrived(["ffn2_gate"], got)
    (mx,), got = _mm("out_proj", [m], [full["w_out"]], "nn", [F32], carry=fetch(["ffn2_up"]))
    arrived(["ffn2_up"], got)
    h2, n2 = post_pre(h1, mx, wts["ln_mix_post"], wts["ln_ffn2_pre"], 1.0)
    (g2_, u2, act2), got = _mm("ffn2_up", [n2], [full["ffn2_gate"], full["ffn2_up"]], "nn", [BF16] * 3,
                               epilogue=_swiglu_fwd_epi, carry=fetch(["ffn2_down"]))
    arrived(["ffn2_down"], got)
    f2 = _mm("ffn2_down", [act2], [full["ffn2_down"]], "nn", [F32], tm=512)[0]

    grads, slabs = {}, {}

    def pair_sums(names):
        parts = []
        for n in names:
            if n == "pool_w":
                parts.append(jnp.transpose(grads[n].reshape(n_groups, N_DEV, gshard, gd), (1, 0, 2, 3)).reshape(
                    N_CHIP, 2, n_groups * gshard, gd))
            else:
                parts.append(_shard_cols(grads[n]) if n in COL_SHARDED else _shard_rows(grads[n]))
        return [pair_add(part, rcv) for part, rcv in zip(parts, exchange_sibling(parts))]

    def landed(names, got):
        slabs.update(zip(names, got))

    dh3, df2, loss_part, grads["ln_ffn2_post"] = loss_post_bwd(h2, f2, wts["ln_ffn2_post"], target, MACARON)
    dg2, du2 = _mm("ffn2_dact", [df2], [full["ffn2_down"]], "nt", [BF16] * 2, extras=[(g2_, 0), (u2, 0)],
                   epilogue=_swiglu_bwd_epi)
    grads["ffn2_down"] = _mm("ffn2_ddown", [act2], [df2], "tn", [F32], tm=512, tn=1024)[0]
    grads["ffn2_gate"], grads["ffn2_up"] = _mm("ffn2_dup", [n2], [dg2, du2], "tn", [F32] * 2, tm=512)
    dn2 = _mm("ffn2_dn", [dg2, du2], [full["ffn2_gate"], full["ffn2_up"]], "nt", [F32], sum_pairs=True, tm=512)[0]
    sums2 = pair_sums(["ffn2_down", "ffn2_gate", "ffn2_up"])
    dh2, grads["ln_ffn2_pre"] = pre_bwd(dn2, h2, wts["ln_ffn2_pre"], dh3)
    dmx, grads["ln_mix_post"] = post_bwd(dh2, mx, wts["ln_mix_post"], 1.0)
    (dya, dyb, dga, dgb), got = _mm("dmix", [dmx], [full["w_out"]], "nt", [BF16] * 4, extras=gates + [(ya, 0), (yb, 0)],
                                    epilogue=_mix_bwd_epi, carry=chips_carry(sums2[:1]))
    landed(["ffn2_down"], got)
    grads["w_out"] = _mm("dw_out", [m], [dmx], "tn", [F32])[0]
    dya_in = _mm("dproj_a", [dya], [full["w_proj_a"]], "nt", [F32])[0]
    dyb_in = _mm("dproj_b", [dyb], [full["w_proj_b"]], "nt", [F32])[0]
    grads["w_proj_a"] = _mm("dw_proj_a", [ya_in], [dya], "tn", [F32])[0]
    grads["w_proj_b"] = _mm("dw_proj_b", [yb_in], [dyb], "tn", [F32])[0]
    dz_pool, grads["pool_w"], grads["pool_scale"] = pool_bwd(cols, dyb_in, pool_w, wts["pool_scale"])
    dy, dr_x, dk_x, dv_x, dgate, grads["rwkv_gn_w"], grads["rwkv_gn_b"], drk = rwkv_post_bwd(
        dya_in, y, r, kmod, v, gate, wts["rwkv_gn_w"], wts["rwkv_gn_b"], rk)
    grads["rwkv_r_k"] = drk.reshape(wts["rwkv_r_k"].shape)
    dr_s, ddecay, dk_s, dv_s, dneg, dbpos = wkv_bwd(r, decay, kmod, v, aneg, bpos, dy, states, sdota)
    (dzs, dzls, grads["rwkv_k_a"], grads["rwkv_k_k"], grads["rwkv_a0"], grads["rwkv_w0"], da2p, dw2p, dg2p) = rwkv_prep_bwd(
        p, cols, [dr_s, dr_x, ddecay, dk_s, dk_x, dv_s, dv_x, dneg, dbpos, dgate], *small)
    grads["rwkv_w2"], grads["rwkv_a2"], grads["rwkv_g2"] = dw2p[:LW], da2p[LW:LW + LA], dg2p[LW + LA:lat]
    dz_rkv, dz_lat, dmu_rkv, dmu_lat = shift_bwd(cols, dzs, dzls, mu_rkv, mu_lat)
    grads["rwkv_mu"] = jnp.concatenate([dmu_rkv, dmu_lat[:, :lat]], axis=1)
    dp = jnp.concatenate([dz_rkv, dz_pool, dga, dgb, dz_lat], axis=1)
    (dnm,), got = _mm("din_proj", [dp], [w_in_p], "nt", [F32], tm=512, carry=chips_carry(sums2[1:2]))
    landed(["ffn2_gate"], got)
    (dw_in_p,), got = _mm("dw_in", [nm], [dp], "tn", [F32], carry=chips_carry(sums2[2:]))
    landed(["ffn2_up"], got)
    grads["w_in"] = jnp.concatenate([dw_in_p[:, :3 * W], dw_in_p[:, base:base + lat], dw_in_p[:, 3 * W:base]], axis=1)
    sums_mix = pair_sums(["w_in"] + mixer)
    dh1, grads["ln_mix_pre"] = pre_bwd(dnm, h1, wts["ln_mix_pre"], dh2)
    df1, grads["ln_ffn1_post"] = post_bwd(dh1, f1, wts["ln_ffn1_post"], MACARON)
    (dg1, du1), got = _mm("ffn1_dact", [df1], [full["ffn1_down"]], "nt", [BF16] * 2, extras=[(g1, 0), (u1, 0)],
                          epilogue=_swiglu_bwd_epi, carry=chips_carry(sums_mix[:1]))
    landed(["w_in"], got)
    (grads["ffn1_down"],), got = _mm("ffn1_ddown", [act1], [df1], "tn", [F32], tm=512, tn=1024,
                                     carry=chips_carry(sums_mix[1:]))
    landed(mixer, got)
    (grads["ffn1_gate"], grads["ffn1_up"]), got = _mm("ffn1_dup", [n1], [dg1, du1], "tn", [F32] * 2, tm=512,
                                                      carry=chips_carry(pair_sums(["ffn1_down"])))
    landed(["ffn1_down"], got)
    (dn1,), got = _mm("ffn1_dn", [dg1, du1], [full["ffn1_gate"], full["ffn1_up"]], "nt", [F32], sum_pairs=True, tm=512,
                      carry=chips_carry(pair_sums(["ffn1_gate", "ffn1_up"])))
    landed(["ffn1_gate", "ffn1_up"], got)
    grad_x, grads["ln_ffn1_pre"] = pre_bwd(dn1, x, wts["ln_ffn1_pre"], dh1)

    flat = jnp.concatenate([grads[n].reshape(-1) for n in REPLICATED])
    n_small = flat.shape[0]
    rows = _round_up(n_small, 8 * LANES) // LANES
    flat = jnp.concatenate([flat, jnp.zeros((rows * LANES - n_small,), F32)]).reshape(rows, LANES)
    small_slabs = all_gather([flat])[0]

    def packed(prefix):
        vals = jnp.concatenate([args[prefix + n].reshape(-1) for n in REPLICATED])
        return jnp.concatenate([vals, jnp.ones((rows * LANES - n_small,), F32)]).reshape(rows, LANES)

    outs = {}
    small_out = adamw(packed(""), packed("m_"), packed("v_"), small_slabs)
    offset = 0
    for n in REPLICATED:
        size = args[n].size
        outs[n] = [o.reshape(-1)[offset:offset + size].reshape(args[n].shape) for o in small_out]
        offset += size
    for n in SHARDED:
        shard2d = slabs[n].shape[1:]
        res = adamw(*[args[pre + n].reshape(shard2d) for pre in ("", "m_", "v_")], slabs[n])
        outs[n] = [o.reshape(args[n].shape) for o in res]

    loss = lax.psum(loss_part[0, 0], ("x", "y", "c"))
    return (loss, grad_x[None], *[outs[n][0] for n in WEIGHTS], *[outs[n][1] for n in WEIGHTS],
            *[outs[n][2] for n in WEIGHTS], *[outs[n][3] for n in WEIGHTS])


ARG_NAMES = ["x"] + WEIGHTS + ["loss_target"] + ["m_" + n for n in WEIGHTS] + ["v_" + n for n in WEIGHTS]


def kernel(x, ln_ffn1_pre, ln_ffn1_post, ffn1_gate, ffn1_up, ffn1_down, ln_mix_pre, ln_mix_post, w_in, rwkv_mu, rwkv_w0,
           rwkv_w2, rwkv_a0, rwkv_a2, rwkv_g2, rwkv_k_k, rwkv_k_a, rwkv_r_k, rwkv_gn_w, rwkv_gn_b, w_proj_a, pool_w,
           pool_scale, w_proj_b, w_out, ln_ffn2_pre, ln_ffn2_post, ffn2_gate, ffn2_up, ffn2_down, loss_target,
           m_ln_ffn1_pre, m_ln_ffn1_post, m_ffn1_gate, m_ffn1_up, m_ffn1_down, m_ln_mix_pre, m_ln_mix_post, m_w_in,
           m_rwkv_mu, m_rwkv_w0, m_rwkv_w2, m_rwkv_a0, m_rwkv_a2, m_rwkv_g2, m_rwkv_k_k, m_rwkv_k_a, m_rwkv_r_k,
           m_rwkv_gn_w, m_rwkv_gn_b, m_w_proj_a, m_pool_w, m_pool_scale, m_w_proj_b, m_w_out, m_ln_ffn2_pre,
           m_ln_ffn2_post, m_ffn2_gate, m_ffn2_up, m_ffn2_down, v_ln_ffn1_pre, v_ln_ffn1_post, v_ffn1_gate, v_ffn1_up,
           v_ffn1_down, v_ln_mix_pre, v_ln_mix_post, v_w_in, v_rwkv_mu, v_rwkv_w0, v_rwkv_w2, v_rwkv_a0, v_rwkv_a2,
           v_rwkv_g2, v_rwkv_k_k, v_rwkv_k_a, v_rwkv_r_k, v_rwkv_gn_w, v_rwkv_gn_b, v_w_proj_a, v_pool_w, v_pool_scale,
           v_w_proj_b, v_w_out, v_ln_ffn2_pre, v_ln_ffn2_post, v_ffn2_gate, v_ffn2_up, v_ffn2_down):
    given = locals()
    return _step({n: given[n] for n in ARG_NAMES})
```

```python
import jax
import jax.numpy as jnp
from jax import lax
from jax.experimental import pallas as pl
from jax.experimental.pallas import tpu as pltpu

F32, BF16 = jnp.float32, jnp.bfloat16
N_DEV = 8
N_CHIP = 4
HEAD = 64
LANES = 2 * HEAD
NORM_EPS, GN_EPS, L2_EPS = 1e-6, 64e-5, 1e-12
POOL_WINDOWS = (2, 4, 8, 16)
POOL_HALO = 16
MACARON = 0.5
ADAM_LR, ADAM_B1, ADAM_B2, ADAM_EPS, ADAM_WD, ADAM_STEP = 0.001, 0.9, 0.999, 1e-08, 0.01, 10
VMEM_LIMIT = 48 * 1024 * 1024
MM_VMEM_BUDGET = 36 * 1024 * 1024
ROW_TILE = 256
RWKV_ROW_TILE = 128
LAT_ALIGN = 512
WKV_CHUNK, WKV_PAIRS = 16, 8
WKV_UNROLL = 4
WKV_MXU_PAIRS = 5
MESH = pl.DeviceIdType.MESH


def _pallas(body, **kw):
    return pl.pallas_call(body, **kw)


def _params(*sem):
    return pltpu.CompilerParams(dimension_semantics=sem, vmem_limit_bytes=VMEM_LIMIT)


def _tile(n, target, align=128):
    best = None
    for d in range(align, min(n, target) + 1, align):
        if n % d == 0:
            best = d
    return best if best is not None else n


def _round_up(n, m):
    return (n + m - 1) // m * m


ANY = pl.BlockSpec(memory_space=pl.ANY)


def _mm(name, a_list, b_list, mode, out_dtypes, *, sum_pairs=False, extras=(), epilogue=None, tm=1024, tn=512,
        carry=None):
    n_a, n_b = len(a_list), len(b_list)
    assert n_a in (1, n_b)
    a0, b0 = a_list[0], b_list[0]
    if mode == "nn":
        (M, K), N = a0.shape, b0.shape[1]
    elif mode == "nt":
        (M, K), N = a0.shape, b0.shape[0]
    else:
        (K, M), N = a0.shape, b0.shape[1]
    tm, tn = _tile(M, tm), _tile(N, tn)
    n_acc = 1 if sum_pairs else n_b
    n_ex = len(extras)

    def planned(tk):
        operands = 2 * 2 * tk * (n_a * tm + n_b * tn)
        tiles = 2 * tm * tn * (sum(e.dtype.itemsize for e, _ in extras) + sum(jnp.dtype(d).itemsize for d in out_dtypes))
        return operands + tiles + 4 * tm * tn * (n_acc + len(out_dtypes))

    tk = max([d for d in range(128, K + 1, 128) if K % d == 0 and planned(d) <= MM_VMEM_BUDGET] or [_tile(K, 512)])
    nk = K // tk
    if mode == "tn":
        a_spec = pl.BlockSpec((tk, tm), lambda i, j, k: (k, i))
    else:
        a_spec = pl.BlockSpec((tm, tk), lambda i, j, k: (i, k))
    if mode == "nt":
        b_spec = pl.BlockSpec((tn, tk), lambda i, j, k: (j, k))
    else:
        b_spec = pl.BlockSpec((tk, tn), lambda i, j, k: (k, j))
    contract = {"nn": ((1,), (0,)), "nt": ((1,), (1,)), "tn": ((0,), (0,))}[mode]
    e_specs = []
    for _, col in extras:
        assert col % tn == 0
        e_specs.append(pl.BlockSpec((tm, tn), lambda i, j, k, off=col // tn: (i, j + off)))
    o_spec = pl.BlockSpec((tm, tn), lambda i, j, k: (i, j))

    n_in, n_out, n_scr = n_a + n_b + n_ex, len(out_dtypes), (n_acc if nk > 1 else 0)
    c_in, c_out = (len(carry.inputs), len(carry.out_shapes)) if carry else (0, 0)
    grid = (M // tm, N // tn, nk)

    def body(*refs):
        a_refs, b_refs, e_refs = refs[:n_a], refs[n_a:n_a + n_b], refs[n_a + n_b:n_in]
        o_refs = refs[n_in + c_in:n_in + c_in + n_out]
        acc_refs = refs[n_in + c_in + n_out + c_out:n_in + c_in + n_out + c_out + n_scr]
        carried = (refs[n_in:n_in + c_in], refs[n_in + c_in + n_out:n_in + c_in + n_out + c_out],
                   refs[n_in + c_in + n_out + c_out + n_scr:])
        at = [pl.program_id(d) for d in range(3)]

        if carry:
            @pl.when((at[0] == 0) & (at[1] == 0) & (at[2] == 0))
            def _():
                carry.start(*carried)

        def products():
            a_vals = [a[...] for a in a_refs]
            prods = [lax.dot_general(a_vals[p if n_a > 1 else 0], b_refs[p][...], (contract, ((), ())),
                                     preferred_element_type=F32) for p in range(n_b)]
            return [sum(prods[1:], prods[0])] if sum_pairs else prods

        def finish(results):
            outs = epilogue(results, [e[...] for e in e_refs]) if epilogue else results
            for o_ref, o in zip(o_refs, outs):
                o_ref[...] = o.astype(o_ref.dtype)

        if nk == 1:
            finish(products())
        else:
            @pl.when(at[2] == 0)
            def _():
                for acc in acc_refs:
                    acc[...] = jnp.zeros_like(acc)

            for acc, prod in zip(acc_refs, products()):
                acc[...] += prod

            @pl.when(at[2] == nk - 1)
            def _():
                finish([acc[...] for acc in acc_refs])

        if carry:
            @pl.when((at[0] == grid[0] - 1) & (at[1] == grid[1] - 1) & (at[2] == grid[2] - 1))
            def _():
                carry.finish(*carried)

    res = _pallas(
        body, name=name, grid=grid,
        in_specs=[a_spec] * n_a + [b_spec] * n_b + e_specs + [ANY] * c_in,
        out_specs=[o_spec] * n_out + [ANY] * c_out,
        out_shape=[jax.ShapeDtypeStruct((M, N), dt) for dt in out_dtypes] + (list(carry.out_shapes) if carry else []),
        scratch_shapes=[pltpu.VMEM((tm, tn), F32)] * n_scr + (list(carry.scratch) if carry else []),
        compiler_params=_params("arbitrary", "arbitrary", "arbitrary") if carry else _params("parallel", "parallel", "arbitrary"),
    )(*a_list, *b_list, *[e for e, _ in extras], *(carry.inputs if carry else []))
    return (res[:n_out], res[n_out:]) if carry else res


def _swiglu_fwd_epi(accs, _):
    g, u = accs
    return [g, u, g * jax.nn.sigmoid(g) * u]


def _swiglu_bwd_epi(accs, ex):
    dact = accs[0]
    g, u = ex[0].astype(F32), ex[1].astype(F32)
    sg = jax.nn.sigmoid(g)
    return [dact * u * (sg * (1.0 + g * (1.0 - sg))), dact * (g * sg)]


def _add_epi(accs, ex):
    return [accs[0] + ex[0]]


def _mix_fwd_epi(accs, ex):
    ya, yb = accs
    return [jax.nn.sigmoid(ex[0]) * ya + jax.nn.sigmoid(ex[1]) * yb, ya, yb]


def _mix_bwd_epi(accs, ex):
    dm = accs[0]
    sa, sb = jax.nn.sigmoid(ex[0]), jax.nn.sigmoid(ex[1])
    ya, yb = ex[2].astype(F32), ex[3].astype(F32)
    return [dm * sa, dm * sb, dm * ya * sa * (1.0 - sa), dm * yb * sb * (1.0 - sb)]


def _row_call(name, body, T, tiled, params, outs, accs=(), prev=(), nxt=(), halo=8, tile=ROW_TILE):
    tm = min(tile, T)
    n_tiles = T // tm

    def norm(e):
        return e if isinstance(e, tuple) else (e, e.shape[1], 0)

    tiled, prev, nxt = [norm(e) for e in tiled], [norm(e) for e in prev], [norm(e) for e in nxt]
    per_halo, n_halo = tm // halo, T // halo
    in_specs = [pl.BlockSpec((tm, w), lambda i, cb=cb: (i, cb)) for _, w, cb in tiled]
    in_specs += [pl.BlockSpec((halo, w), lambda i, cb=cb: (jnp.maximum(i * per_halo - 1, 0), cb)) for _, w, cb in prev]
    in_specs += [pl.BlockSpec((halo, w), lambda i, cb=cb: (jnp.minimum((i + 1) * per_halo, n_halo - 1), cb))
                 for _, w, cb in nxt]
    in_specs += [pl.BlockSpec(p.shape, lambda i, nd=p.ndim: (0,) * nd) for p in params]
    out_specs = [pl.BlockSpec((tm, o.shape[1]), lambda i: (i, 0)) for o in outs]
    out_specs += [pl.BlockSpec(a.shape, lambda i, nd=len(a.shape): (0,) * nd) for a in accs]
    n1, n2, n3, n4, n5 = len(tiled), len(prev), len(nxt), len(params), len(outs)

    def kernel_body(*refs):
        i = pl.program_id(0)
        acc_refs = refs[n1 + n2 + n3 + n4 + n5:]

        @pl.when(i == 0)
        def _():
            for a in acc_refs:
                a[...] = jnp.zeros_like(a)

        body(i, n_tiles, refs[:n1], refs[n1:n1 + n2], refs[n1 + n2:n1 + n2 + n3],
             refs[n1 + n2 + n3:n1 + n2 + n3 + n4], refs[n1 + n2 + n3 + n4:n1 + n2 + n3 + n4 + n5], acc_refs)

    return _pallas(
        kernel_body, name=name, grid=(n_tiles,), in_specs=in_specs, out_specs=out_specs,
        out_shape=list(outs) + list(accs),
        compiler_params=_params("arbitrary"),
    )(*[e[0] for e in tiled + prev + nxt], *params)


def _sds(shape, dtype=F32):
    return jax.ShapeDtypeStruct(tuple(shape), dtype)


def _rstd(x):
    return lax.rsqrt(jnp.mean(x * x, axis=-1, keepdims=True) + NORM_EPS)


def _colsum(x):
    return jnp.sum(x, axis=0, keepdims=True)


def rms_pre(x, g):
    T, D = x.shape

    def body(i, n, tiled, prev, nxt, params, outs, accs):
        xv = tiled[0][...]
        outs[0][...] = (xv * _rstd(xv) * params[0][...]).astype(BF16)

    return _row_call("rms_pre", body, T, [x], [g], [_sds((T, D), BF16)])[0]


def post_pre(h, f, g_post, g_pre, scale):
    T, D = h.shape

    def body(i, n, tiled, prev, nxt, params, outs, accs):
        hv, fv = tiled[0][...], tiled[1][...]
        h2 = hv + scale * (fv * _rstd(fv) * params[0][...])
        outs[0][...] = h2
        outs[1][...] = (h2 * _rstd(h2) * params[1][...]).astype(BF16)

    return _row_call("post_pre", body, T, [h, f], [g_post, g_pre], [_sds((T, D)), _sds((T, D), BF16)])


def _post_bwd_math(dh, fv, g, scale):
    r = _rstd(fv)
    fhat = fv * r
    dy = scale * dh
    z = dy * g
    df = r * (z - fhat * jnp.mean(z * fhat, axis=-1, keepdims=True))
    return df, _colsum(dy * fhat)


def loss_post_bwd(h, f, g_post, target, scale):
    T, D = h.shape

    def body(i, n, tiled, prev, nxt, params, outs, accs):
        hv, fv, tv = tiled[0][...], tiled[1][...], tiled[2][...]
        g = params[0][...]
        e = hv + scale * (fv * _rstd(fv) * g) - tv
        accs[0][...] += jnp.full(accs[0].shape, 0.5 / D, F32) * jnp.sum(e * e)
        dh = e * (1.0 / D)
        outs[0][...] = dh
        df, dg = _post_bwd_math(dh, fv, g, scale)
        outs[1][...] = df.astype(BF16)
        accs[1][...] += dg

    return _row_call("loss_post_bwd", body, T, [h, f, target], [g_post],
                     [_sds((T, D)), _sds((T, D), BF16)], [_sds((1, LANES)), _sds((1, D))])


def post_bwd(dh, f, g_post, scale):
    T, D = dh.shape

    def body(i, n, tiled, prev, nxt, params, outs, accs):
        df, dg = _post_bwd_math(tiled[0][...], tiled[1][...], params[0][...], scale)
        outs[0][...] = df.astype(BF16)
        accs[0][...] += dg

    return _row_call("post_bwd", body, T, [dh, f], [g_post], [_sds((T, D), BF16)], [_sds((1, D))])


def pre_bwd(dn, h, g_pre, dres):
    T, D = h.shape

    def body(i, n, tiled, prev, nxt, params, outs, accs):
        dnv, hv = tiled[0][...], tiled[1][...]
        r = _rstd(hv)
        hhat = hv * r
        z = dnv * params[0][...]
        outs[0][...] = tiled[2][...] + r * (z - hhat * jnp.mean(z * hhat, axis=-1, keepdims=True))
        accs[0][...] += _colsum(dnv * hhat)

    return _row_call("pre_bwd", body, T, [dn, h, dres], [g_pre], [_sds((T, D))], [_sds((1, D))])


def _head_ones():
    i = lax.broadcasted_iota(jnp.int32, (LANES, LANES), 0)
    j = lax.broadcasted_iota(jnp.int32, (LANES, LANES), 1)
    return jnp.where((i < HEAD) == (j < HEAD), 1.0, 0.0).astype(F32)


def _headsum(x):
    e = _head_ones()
    parts = [jnp.dot(x[:, s:s + LANES], e, precision=lax.Precision.HIGHEST, preferred_element_type=F32)
             for s in range(0, x.shape[1], LANES)]
    return parts[0] if len(parts) == 1 else jnp.concatenate(parts, axis=1)


def _shift_down(x, before):
    row = lax.broadcasted_iota(jnp.int32, x.shape, 0)
    return jnp.where(row == 0, before, pltpu.roll(x, 1, 0))


def _shift_up(x, after):
    row = lax.broadcasted_iota(jnp.int32, x.shape, 0)
    return jnp.where(row == x.shape[0] - 1, after, pltpu.roll(x, x.shape[0] - 1, 0))


def _last_row(ref, keep):
    r = ref[ref.shape[0] - 1:ref.shape[0], :]
    return jnp.where(keep, r, jnp.zeros_like(r))


def _first_row(ref, keep):
    r = ref[0:1, :]
    return jnp.where(keep, r, jnp.zeros_like(r))


def _softplus(u):
    return jnp.maximum(u, 0.0) + jnp.log(1.0 + jnp.exp(-jnp.abs(u)))


def _dotb(a, b, contract):
    return lax.dot_general(a.astype(BF16), b.astype(BF16), (contract, ((), ())), preferred_element_type=F32)


_NN, _NT, _TN = ((1,), (0,)), ((1,), (1,)), ((0,), (0,))


def _prep_forward(z, zprev_row, zl, zlprev_row, mu, mul, w0, a0, kk_w, ka_w, w2p, a2p, g2p):
    W = w0.shape[1]
    zs = z + (_shift_down(z, zprev_row) - z) * mu
    zls = zl + (_shift_down(zl, zlprev_row) - zl) * mul
    r, k, v = zs[:, :W], zs[:, W:2 * W], zs[:, 2 * W:]
    th, sg = jnp.tanh(zls), jax.nn.sigmoid(zls)
    xw = w0 + _dotb(th, w2p, _NN)
    wlog = -_softplus(-xw) - 0.5
    ew = jnp.exp(wlog)
    decay = jnp.exp(-ew)
    a = jax.nn.sigmoid(a0 + _dotb(zls, a2p, _NN))
    gate = _dotb(sg, g2p, _NN)
    q = k * kk_w
    nrm = jnp.sqrt(_headsum(q * q))
    den = jnp.maximum(nrm, L2_EPS)
    kk = q / den
    kmod = k * (1.0 + (a - 1.0) * ka_w)
    return dict(zs=zs, zls=zls, r=r, k=k, v=v, th=th, sg=sg, xw=xw, ew=ew, decay=decay, a=a, gate=gate,
                nrm=nrm, den=den, kk=kk, kmod=kmod)


def rwkv_prep(p, cols, mu, mul, w0, a0, kk_w, ka_w, w2p, a2p, g2p):
    T = p.shape[0]
    W = w0.shape[1]

    def body(i, n, tiled, prev, nxt, params, outs, accs):
        c = _prep_forward(tiled[0][...], _last_row(prev[0], i > 0), tiled[1][...], _last_row(prev[1], i > 0),
                          *[q[...] for q in params])
        for o, val in zip(outs, (c["r"], c["decay"], c["kmod"], c["v"], -c["kk"], c["kk"] * c["a"], c["gate"])):
            o[...] = val

    return _row_call("rwkv_prep", body, T, [cols["rkv"], cols["lat"]],
                     [mu, mul, w0, a0, kk_w, ka_w, w2p, a2p, g2p], [_sds((T, W))] * 7,
                     prev=[cols["rkv"], cols["lat"]], tile=RWKV_ROW_TILE)


def _post_forward(y, r, kmod, v, gn_w, gn_b, rk):
    mean = _headsum(y) * (1.0 / HEAD)
    yc = y - mean
    rstd = lax.rsqrt(_headsum(yc * yc) * (1.0 / HEAD) + GN_EPS)
    yn = yc * rstd
    s = _headsum(r * kmod * rk)
    return yn, rstd, s, yn * gn_w + gn_b + s * v


def rwkv_post(y, r, kmod, v, gate, gn_w, gn_b, rk):
    T, W = y.shape

    def body(i, n, tiled, prev, nxt, params, outs, accs):
        yv, rv, kv, vv, gv = [t[...] for t in tiled]
        _, _, _, o = _post_forward(yv, rv, kv, vv, *[q[...] for q in params])
        outs[0][...] = (o * gv).astype(BF16)

    return _row_call("rwkv_post", body, T, [y, r, kmod, v, gate], [gn_w, gn_b, rk], [_sds((T, W), BF16)],
                     tile=RWKV_ROW_TILE)[0]


def rwkv_post_bwd(dout, y, r, kmod, v, gate, gn_w, gn_b, rk):
    T, W = y.shape

    def body(i, n, tiled, prev, nxt, params, outs, accs):
        dv_, yv, rv, kv, vv, gv = [t[...] for t in tiled]
        gn_w_, gn_b_, rk_ = [q[...] for q in params]
        yn, rstd, s, o = _post_forward(yv, rv, kv, vv, gn_w_, gn_b_, rk_)
        do = dv_ * gv
        outs[4][...] = dv_ * o
        accs[0][...] += _colsum(do * yn)
        accs[1][...] += _colsum(do)
        dyn = do * gn_w_
        outs[0][...] = rstd * (dyn - _headsum(dyn) * (1.0 / HEAD) - yn * (_headsum(dyn * yn) * (1.0 / HEAD)))
        ds = _headsum(do * vv)
        outs[1][...] = ds * kv * rk_
        outs[2][...] = ds * rv * rk_
        outs[3][...] = do * s
        accs[2][...] += _colsum(ds * rv * kv)

    return _row_call("rwkv_post_bwd", body, T, [dout, y, r, kmod, v, gate], [gn_w, gn_b, rk],
                     [_sds((T, W))] * 5, [_sds((1, W))] * 3, tile=RWKV_ROW_TILE)


def rwkv_prep_bwd(p, cols, grads, mu, mul, w0, a0, kk_w, ka_w, w2p, a2p, g2p):
    T = p.shape[0]
    W = w0.shape[1]
    latp = w2p.shape[0]

    def body(i, n, tiled, prev, nxt, params, outs, accs):
        pv = [q[...] for q in params]
        mu_, mul_, w0_, a0_, kk_w_, ka_w_, w2p_, a2p_, g2p_ = pv
        c = _prep_forward(tiled[0][...], _last_row(prev[0], i > 0), tiled[1][...], _last_row(prev[1], i > 0), *pv)
        dr_s, dr_x, ddecay, dk_s, dk_x, dv_s, dv_x, dneg, db, dgate = [t[...] for t in tiled[2:]]
        k, a, kk = c["k"], c["a"], c["kk"]
        dkmod = dk_s + dk_x
        dk = dkmod * (1.0 + (a - 1.0) * ka_w_)
        da = dkmod * k * ka_w_ + db * kk
        accs[0][...] += _colsum(dkmod * k * (a - 1.0))
        dkk = db * a - dneg
        dq = jnp.where(c["nrm"] > L2_EPS, dkk - kk * _headsum(dkk * kk), dkk) / c["den"]
        dk = dk + dq * kk_w_
        accs[1][...] += _colsum(dq * k)
        dxa = da * a * (1.0 - a)
        accs[2][...] += _colsum(dxa)
        accs[4][...] += _dotb(c["zls"], dxa, _TN)
        dzls = _dotb(dxa, a2p_, _NT)
        dxw = (-ddecay * c["decay"] * c["ew"]) * jax.nn.sigmoid(-c["xw"])
        accs[3][...] += _colsum(dxw)
        accs[5][...] += _dotb(c["th"], dxw, _TN)
        dzls = dzls + _dotb(dxw, w2p_, _NT) * (1.0 - c["th"] * c["th"])
        accs[6][...] += _dotb(c["sg"], dgate, _TN)
        dzls = dzls + _dotb(dgate, g2p_, _NT) * c["sg"] * (1.0 - c["sg"])
        outs[0][...] = jnp.concatenate([dr_s + dr_x, dk, dv_s + dv_x], axis=1)
        outs[1][...] = dzls

    return _row_call("rwkv_prep_bwd", body, T, [cols["rkv"], cols["lat"]] + list(grads),
                     [mu, mul, w0, a0, kk_w, ka_w, w2p, a2p, g2p], [_sds((T, 3 * W)), _sds((T, latp))],
                     [_sds((1, W))] * 4 + [_sds((latp, W))] * 3, prev=[cols["rkv"], cols["lat"]], tile=RWKV_ROW_TILE)


def shift_bwd(cols, dzs, dzls, mu, mul):
    T = dzs.shape[0]

    def body(i, n, tiled, prev, nxt, params, outs, accs):
        for j in range(2):
            z, d, m = tiled[j][...], tiled[2 + j][...], params[j][...]
            zprev = _shift_down(z, _last_row(prev[j], i > 0))
            dnext = _shift_up(d, _first_row(nxt[j], i < n - 1))
            outs[j][...] = (d * (1.0 - m) + dnext * m).astype(BF16)
            accs[j][...] += _colsum(d * (zprev - z))

    return _row_call("shift_bwd", body, T, [cols["rkv"], cols["lat"], dzs, dzls], [mu, mul],
                     [_sds(dzs.shape, BF16), _sds(dzls.shape, BF16)], [_sds(mu.shape), _sds(mul.shape)],
                     prev=[cols["rkv"], cols["lat"]], nxt=[dzs, dzls])


def _window_pick(x, windows):
    gid = lax.broadcasted_iota(jnp.int32, x.shape, 1) // (x.shape[1] // len(windows))
    out = windows[-1]
    for g in range(len(windows) - 2, -1, -1):
        out = jnp.where(gid == g, windows[g], out)
    return out


def _pool_counts(t0, rows, width):
    t = (t0 + lax.broadcasted_iota(jnp.int32, (rows, width), 0) + 1).astype(F32)
    return _window_pick(t, [jnp.minimum(t, float(w)) for w in POOL_WINDOWS])


def _pool_mixed(x, before, t0):
    tm, width = x.shape
    xe = jnp.concatenate([before, x], axis=0)
    sums, s, span = [], xe, 1
    for w in POOL_WINDOWS:
        while span < w:
            s = s + pltpu.roll(s, span, 0)
            span *= 2
        sums.append(s[POOL_HALO:, :])
    return _window_pick(x, sums) / _pool_counts(t0, tm, width) - x


def _group_dot(x, w_ref, contract):
    gd = w_ref.shape[-1]
    parts = [_dotb(x[:, g * gd:(g + 1) * gd], w_ref[g], contract) for g in range(w_ref.shape[0])]
    return jnp.concatenate(parts, axis=1)


def pool_fwd(cols, pool_w, pool_scale):
    T, width = cols["pool"][0].shape[0], cols["pool"][1]
    tm = min(ROW_TILE, T)

    def body(i, n, tiled, prev, nxt, params, outs, accs):
        before = jnp.where(i > 0, prev[0][...], 0.0)
        mixed = _pool_mixed(tiled[0][...], before, i * tm)
        outs[0][...] = (_group_dot(mixed, params[0], _NN) * params[1][...]).astype(BF16)

    return _row_call("pool_fwd", body, T, [cols["pool"]], [pool_w, pool_scale], [_sds((T, width), BF16)],
                     prev=[cols["pool"]], halo=POOL_HALO)[0]


def pool_bwd(cols, dout, pool_w, pool_scale):
    T, width = dout.shape
    tm = min(ROW_TILE, T)

    def body(i, n, tiled, prev, nxt, params, outs, accs):
        w_ref, scale = params[0], params[1][...]
        before = jnp.where(i > 0, prev[0][...], 0.0)
        mixed = _pool_mixed(tiled[0][...], before, i * tm)
        dv = tiled[1][...]
        accs[1][...] += _colsum(dv * _group_dot(mixed, w_ref, _NN))
        after = jnp.where(i < n - 1, nxt[0][...], 0.0)
        dys = jnp.concatenate([dv, after], axis=0) * scale
        gd = w_ref.shape[-1]
        for g in range(w_ref.shape[0]):
            accs[0][g] += _dotb(mixed[:, g * gd:(g + 1) * gd], dys[:tm, g * gd:(g + 1) * gd], _TN)
        dmixed = _group_dot(dys, w_ref, _NT)
        u = dmixed / _pool_counts(i * tm, tm + POOL_HALO, width)
        rows = tm + POOL_HALO
        sums, s, span = [], u, 1
        for w in POOL_WINDOWS:
            while span < w:
                s = s + pltpu.roll(s, rows - span, 0)
                span *= 2
            sums.append(s[:tm, :])
        outs[0][...] = (_window_pick(dv, sums) - dmixed[:tm, :]).astype(BF16)

    return _row_call("pool_bwd", body, T, [cols["pool"], dout], [pool_w, pool_scale], [_sds((T, width), BF16)],
                     [_sds(pool_w.shape), _sds((1, width))], prev=[cols["pool"]], nxt=[dout], halo=POOL_HALO)


def _wkv_consts(pairs):
    lane = lax.broadcasted_iota(jnp.int32, (HEAD, LANES), 1)
    sub = lax.broadcasted_iota(jnp.int32, (pairs * HEAD, LANES), 0)
    lane_all = lax.broadcasted_iota(jnp.int32, (pairs * HEAD, LANES), 1)
    i = lax.broadcasted_iota(jnp.int32, (LANES, LANES), 0)
    j = lax.broadcasted_iota(jnp.int32, (LANES, LANES), 1)
    ones = jnp.where((i < HEAD) == (j < HEAD), 1.0, 0.0).astype(BF16)
    diag = jnp.where((lane_all & (HEAD - 1)) == (sub & (HEAD - 1)), 1.0, 0.0).astype(F32)
    return lane < HEAD, diag, ones


def _segsum(p, in_a):
    sa = jnp.sum(jnp.where(in_a, p, 0.0), axis=1, keepdims=True)
    sb = jnp.sum(jnp.where(in_a, 0.0, p), axis=1, keepdims=True)
    return jnp.where(in_a, sa, sb)


def _hi_lo(p):
    hi = lax.bitcast_convert_type(lax.bitcast_convert_type(p, jnp.uint32) & jnp.uint32(0xFFFF0000), F32)
    return hi, p - hi


def _segsum_mxu(p, ones):
    hi, lo = _hi_lo(p)
    return (jnp.dot(hi.astype(BF16), ones, preferred_element_type=F32)
            + jnp.dot(lo.astype(BF16), ones, preferred_element_type=F32))


def _cat(parts, axis):
    return parts[0] if len(parts) == 1 else jnp.concatenate(parts, axis=axis)


def _tile_rows(row, pairs):
    return _cat([jnp.broadcast_to(row[:, p * LANES:(p + 1) * LANES], (HEAD, LANES)) for p in range(pairs)], 0)


def _spread(row, pairs, diag16, ones):
    hi, lo = _hi_lo(row)
    return (jnp.dot(_tile_rows(hi.astype(BF16), pairs) * diag16, ones, preferred_element_type=F32)
            + jnp.dot(_tile_rows(lo.astype(BF16), pairs) * diag16, ones, preferred_element_type=F32))


def _pair_colsums(x, pairs):
    return _cat([_colsum(x[p * HEAD:(p + 1) * HEAD]) for p in range(pairs)], 1)


def _spread_split(row, pairs, in_a, diag, diag16, ones):
    n_mxu = min(WKV_MXU_PAIRS, pairs)
    parts = [_spread(row[:, :n_mxu * LANES], n_mxu, diag16[:n_mxu * HEAD], ones)]
    parts += [_segsum(row[:, p * LANES:(p + 1) * LANES] * diag[:HEAD], in_a) for p in range(n_mxu, pairs)]
    return _cat(parts, 0)


def wkv_fwd(r, w, k, v, a, b, carry=None):
    T, W = r.shape
    P = W // LANES
    PB = min(WKV_PAIRS, P)
    chunk = min(WKV_CHUNK, T)
    NC = T // chunk
    R = PB * HEAD
    c_in, c_out = (len(carry.inputs), len(carry.out_shapes)) if carry else (0, 0)

    def body(*refs):
        r_ref, w_ref, k_ref, v_ref, a_ref, b_ref = refs[:6]
        y_ref, st_ref, sa_ref = refs[6 + c_in:9 + c_in]
        vt_ref, s_ref = refs[9 + c_in + c_out:11 + c_in + c_out]
        carried = refs[6:6 + c_in], refs[9 + c_in:9 + c_in + c_out], refs[11 + c_in + c_out:]
        g, c = pl.program_id(0), pl.program_id(1)

        if carry:
            @pl.when((g == 0) & (c == 0))
            def _():
                carry.start(*carried)

        @pl.when(c == 0)
        def _():
            s_ref[...] = jnp.zeros_like(s_ref)

        in_a, diag, ones = _wkv_consts(PB)
        diag16 = diag.astype(BF16)

        def spread(t, _):
            vt_ref[t] = _spread_split(v_ref[pl.ds(t, 1), :], PB, in_a, diag, diag16, ones)
            return 0

        lax.fori_loop(0, chunk, spread, 0, unroll=WKV_UNROLL)

        def step(t, _):
            rows = [ref[pl.ds(t, 1), :] for ref in (w_ref, k_ref, a_ref, b_ref)]
            for p in range(PB):
                wt, kt, at, bt = [x[:, p * LANES:(p + 1) * LANES] for x in rows]
                rs = pl.ds(p * HEAD, HEAD)
                S = s_ref[rs]
                sa = _segsum(S * at, in_a)
                sa_ref[t, rs] = sa
                S = S * wt + sa * bt + vt_ref[t, rs] * kt
                st_ref[t, rs] = S
                s_ref[rs] = S
            return 0

        lax.fori_loop(0, chunk, step, 0, unroll=WKV_UNROLL)

        def readout(t, _):
            yt = _segsum_mxu(st_ref[t] * _tile_rows(r_ref[pl.ds(t, 1), :], PB), ones) * diag
            y_ref[pl.ds(t, 1), :] = _pair_colsums(yt, PB)
            return 0

        lax.fori_loop(0, chunk, readout, 0, unroll=WKV_UNROLL)

        if carry:
            @pl.when((g == P // PB - 1) & (c == NC - 1))
            def _():
                carry.finish(*carried)

    spec = pl.BlockSpec((chunk, PB * LANES), lambda g, c: (c, g))
    tiles = pl.BlockSpec((chunk, R, LANES), lambda g, c: (c, g, 0))
    res = _pallas(
        body, name="wkv_fwd", grid=(P // PB, NC), in_specs=[spec] * 6 + [ANY] * c_in,
        out_specs=[spec, tiles, tiles] + [ANY] * c_out,
        out_shape=[_sds((T, W)), _sds((T, P * HEAD, LANES)), _sds((T, P * HEAD, LANES))]
        + (list(carry.out_shapes) if carry else []),
        scratch_shapes=[pltpu.VMEM((chunk, R, LANES), F32), pltpu.VMEM((R, LANES), F32)]
        + (list(carry.scratch) if carry else []),
        compiler_params=_params("arbitrary", "arbitrary") if carry else _params("parallel", "arbitrary"),
    )(r, w, k, v, a, b, *(carry.inputs if carry else []))
    return (res[:3], res[3:]) if carry else res


def wkv_bwd(r, w, k, v, a, b, dy, st, sa):
    T, W = r.shape
    P = W // LANES
    PB = min(WKV_PAIRS, P)
    chunk = min(WKV_CHUNK, T)
    NC = T // chunk
    R = PB * HEAD

    def body(r_ref, w_ref, k_ref, v_ref, a_ref, b_ref, dy_ref, st_ref, before_ref, sa_ref,
             dr_ref, dw_ref, dk_ref, dv_ref, da_ref, db_ref, ds_ref, vt_ref, dyt_ref, dst_ref, dsa_ref):
        c = pl.program_id(1)

        @pl.when(c == 0)
        def _():
            ds_ref[...] = jnp.zeros_like(ds_ref)

        in_a, diag, ones = _wkv_consts(PB)
        diag16 = diag.astype(BF16)

        def spread(t, _):
            vt_ref[t] = _spread_split(v_ref[pl.ds(t, 1), :], PB, in_a, diag, diag16, ones)
            dyt_ref[t] = _spread_split(dy_ref[pl.ds(t, 1), :], PB, in_a, diag, diag16, ones)
            return 0

        lax.fori_loop(0, chunk, spread, 0, unroll=WKV_UNROLL)

        def bstep(n, _):
            t = chunk - 1 - n
            rows = [ref[pl.ds(t, 1), :] for ref in (r_ref, w_ref, a_ref, b_ref)]
            for p in range(PB):
                rt, wt, at, bt = [x[:, p * LANES:(p + 1) * LANES] for x in rows]
                rs = pl.ds(p * HEAD, HEAD)
                dS = ds_ref[rs] + dyt_ref[t, rs] * rt
                dst_ref[t, rs] = dS
                dsa = _segsum(dS * bt, in_a)
                dsa_ref[t, rs] = dsa
                ds_ref[rs] = dS * wt + dsa * at
            return 0

        lax.fori_loop(0, chunk, bstep, 0, unroll=WKV_UNROLL)

        def collect(t, _):
            sn, dS, dsa = st_ref[t], dst_ref[t], dsa_ref[t]
            sp = st_ref[jnp.maximum(t - 1, 0)]
            dvt = _segsum_mxu(dS * _tile_rows(k_ref[pl.ds(t, 1), :], PB), ones) * diag
            for ref, val in ((dr_ref, sn * dyt_ref[t]), (dw_ref, dS * sp), (dk_ref, dS * vt_ref[t]), (dv_ref, dvt),
                             (da_ref, sp * dsa), (db_ref, dS * sa_ref[t])):
                ref[pl.ds(t, 1), :] = _pair_colsums(val, PB)
            return 0

        lax.fori_loop(0, chunk, collect, 0, unroll=WKV_UNROLL)
        first = jnp.where(c == NC - 1, 0.0, before_ref[0])
        dw_ref[0:1, :] = _pair_colsums(dst_ref[0] * first, PB)
        da_ref[0:1, :] = _pair_colsums(first * dsa_ref[0], PB)

    spec = pl.BlockSpec((chunk, PB * LANES), lambda g, c: (NC - 1 - c, g))
    tiles = pl.BlockSpec((chunk, R, LANES), lambda g, c: (NC - 1 - c, g, 0))
    before = pl.BlockSpec((1, R, LANES), lambda g, c: (jnp.maximum((NC - 1 - c) * chunk - 1, 0), g, 0))

    def scratch(n):
        return pltpu.VMEM((n, R, LANES), F32)

    return _pallas(
        body, name="wkv_bwd", grid=(P // PB, NC), in_specs=[spec] * 7 + [tiles, before, tiles],
        out_specs=[spec] * 6, out_shape=[_sds((T, W))] * 6,
        scratch_shapes=[pltpu.VMEM((R, LANES), F32), scratch(chunk), scratch(chunk), scratch(chunk), scratch(chunk)],
        compiler_params=_params("parallel", "arbitrary"),
    )(r, w, k, v, a, b, dy, st, st, sa)


def _position():
    return lax.axis_index("x"), lax.axis_index("y"), lax.axis_index("c")


def _other_chips(x, y):
    return [(1 - x, y), (x, 1 - y), (1 - x, 1 - y)]


class _Carry:
    def __init__(self, inputs, out_shapes, scratch, start, finish):
        self.inputs, self.out_shapes, self.scratch, self.start, self.finish = inputs, out_shapes, scratch, start, finish


def _run_carry(name, carry):
    n_in, n_out = len(carry.inputs), len(carry.out_shapes)

    def body(*refs):
        parts = refs[:n_in], refs[n_in:n_in + n_out], refs[n_in + n_out:]
        carry.start(*parts)
        carry.finish(*parts)

    return _pallas(body, name=name, in_specs=[ANY] * n_in, out_specs=[ANY] * n_out, out_shape=list(carry.out_shapes),
                   scratch_shapes=list(carry.scratch))(*carry.inputs)


def gather_carry(shards):
    n = len(shards)

    def plan(x_refs, out_refs, sems):
        send_sems, recv_sems, local_sems = sems
        x, y, c = _position()
        me, sibling = (x, y, c), (x, y, 1 - c)
        chips = _other_chips(x, y)

        def slot(ref, pos):
            return ref.at[4 * pos[0] + 2 * pos[1] + pos[2]]

        def copy(t, j, block, to, src=None):
            dst = slot(out_refs[t], block)
            return pltpu.make_async_remote_copy(
                src_ref=dst if src is None else src, dst_ref=dst, send_sem=send_sems.at[t, j],
                recv_sem=recv_sems.at[t, j], device_id=to, device_id_type=MESH)

        mine = [pltpu.make_async_copy(x_refs[t], slot(out_refs[t], me), local_sems.at[t]) for t in range(n)]
        first = []
        for t in range(n):
            first.append(copy(t, 0, me, sibling, src=x_refs[t]))
            first += [copy(t, 1 + j, me, (*chip, c), src=x_refs[t]) for j, chip in enumerate(chips)]
        return c, me, sibling, chips, copy, mine, first

    def start(x_refs, out_refs, sems):
        _, _, _, _, _, mine, first = plan(x_refs, out_refs, sems)
        for cp in mine + first:
            cp.start()

    def finish(x_refs, out_refs, sems):
        c, me, sibling, chips, copy, mine, first = plan(x_refs, out_refs, sems)
        passed = []
        for t in range(n):
            for j, chip in enumerate(chips):
                copy(t, 1 + j, (*chip, c), me).wait_recv()
                fwd = copy(t, 4 + j, (*chip, c), sibling)
                fwd.start()
                passed.append(fwd)
        for t in range(n):
            copy(t, 0, sibling, me).wait_recv()
            for j, chip in enumerate(chips):
                copy(t, 4 + j, (*chip, 1 - c), me).wait_recv()
        for cp in first + passed:
            cp.wait_send()
        for cp in mine:
            cp.wait()

    return _Carry(list(shards), [_sds((N_DEV,) + s.shape, s.dtype) for s in shards],
                  [pltpu.SemaphoreType.DMA((n, 7)), pltpu.SemaphoreType.DMA((n, 7)), pltpu.SemaphoreType.DMA((n,))],
                  start, finish)


def all_gather(shards):
    return _run_carry("all_gather", gather_carry(shards))


def exchange_sibling(parts):
    n = len(parts)

    def body(*refs):
        p_refs, out_refs = refs[:n], refs[n:2 * n]
        send_sems, recv_sems = refs[2 * n:]
        x, y, c = _position()
        copies = []
        for t in range(n):
            for q in range(N_CHIP):
                cp = pltpu.make_async_remote_copy(
                    src_ref=p_refs[t].at[q, 1 - c], dst_ref=out_refs[t].at[q], send_sem=send_sems.at[t, q],
                    recv_sem=recv_sems.at[t, q], device_id=(x, y, 1 - c), device_id_type=MESH)
                cp.start()
                copies.append(cp)
        for cp in copies:
            cp.wait()

    return _pallas(
        body, name="exchange_sibling", in_specs=[ANY] * n, out_specs=[ANY] * n,
        out_shape=[_sds((N_CHIP,) + p.shape[2:], p.dtype) for p in parts],
        scratch_shapes=[pltpu.SemaphoreType.DMA((n, N_CHIP)), pltpu.SemaphoreType.DMA((n, N_CHIP))],
    )(*parts)


def chips_carry(parts):
    n = len(parts)

    def plan(p_refs, out_refs, sems):
        send_sems, recv_sems, local_sems = sems
        x, y, c = _position()
        local = [pltpu.make_async_copy(p_refs[t].at[2 * x + y], out_refs[t].at[3], local_sems.at[t]) for t in range(n)]
        remote = [pltpu.make_async_remote_copy(
            src_ref=p_refs[t].at[2 * cx + cy], dst_ref=out_refs[t].at[j], send_sem=send_sems.at[t, j],
            recv_sem=recv_sems.at[t, j], device_id=(cx, cy, c), device_id_type=MESH)
            for t in range(n) for j, (cx, cy) in enumerate(_other_chips(x, y))]
        return local, remote

    def start(p_refs, out_refs, sems):
        local, remote = plan(p_refs, out_refs, sems)
        for cp in local + remote:
            cp.start()

    def finish(p_refs, out_refs, sems):
        local, remote = plan(p_refs, out_refs, sems)
        for cp in remote + local:
            cp.wait()

    return _Carry(list(parts), [_sds(p.shape, p.dtype) for p in parts],
                  [pltpu.SemaphoreType.DMA((n, 3)), pltpu.SemaphoreType.DMA((n, 3)), pltpu.SemaphoreType.DMA((n,))],
                  start, finish)


def _flat_tile(rows, cols):
    tr = rows
    for d in range(16, min(rows, 512) + 1, 16):
        if rows % d == 0 and d * cols * 4 <= 2 * 1024 * 1024:
            tr = d
    return tr


def pair_add(part, recv):
    _, _, R, C = part.shape
    tr = _flat_tile(R, C)
    core = jnp.reshape(lax.axis_index("c"), (1,)).astype(jnp.int32)

    def body(core_ref, p_ref, r_ref, o_ref):
        o_ref[...] = (p_ref[...] + r_ref[...]).astype(BF16)

    grid_spec = pltpu.PrefetchScalarGridSpec(
        num_scalar_prefetch=1, grid=(N_CHIP, R // tr),
        in_specs=[pl.BlockSpec((None, None, tr, C), lambda q, i, core_ref: (q, core_ref[0], i, 0)),
                  pl.BlockSpec((None, tr, C), lambda q, i, core_ref: (q, i, 0))],
        out_specs=pl.BlockSpec((None, tr, C), lambda q, i, core_ref: (q, i, 0)))
    return _pallas(body, name="pair_add", grid_spec=grid_spec, out_shape=_sds((N_CHIP, R, C), BF16),
                   compiler_params=_params("parallel", "parallel"))(core, part, recv)


def adamw(w, m, v, slabs):
    R, C = w.shape
    tr = _flat_tile(R, C)
    n = slabs.shape[0]

    def body(w_ref, m_ref, v_ref, s_ref, g_ref, d_ref, nm_ref, nv_ref):
        g = s_ref[0].astype(F32)
        for j in range(1, n):
            g = g + s_ref[j].astype(F32)
        m2 = ADAM_B1 * m_ref[...] + (1.0 - ADAM_B1) * g
        v2 = ADAM_B2 * v_ref[...] + (1.0 - ADAM_B2) * (g * g)
        m_hat = m2 / (1.0 - ADAM_B1 ** ADAM_STEP)
        v_hat = v2 / (1.0 - ADAM_B2 ** ADAM_STEP)
        g_ref[...] = g
        d_ref[...] = -ADAM_LR * (m_hat / (jnp.sqrt(v_hat) + ADAM_EPS) + ADAM_WD * w_ref[...])
        nm_ref[...] = m2
        nv_ref[...] = v2

    spec = pl.BlockSpec((tr, C), lambda i: (i, 0))
    return _pallas(body, name="adamw", grid=(R // tr,),
                   in_specs=[spec] * 3 + [pl.BlockSpec((n, tr, C), lambda i: (0, i, 0))], out_specs=[spec] * 4,
                   out_shape=[_sds((R, C))] * 4, compiler_params=_params("parallel"))(w, m, v, slabs)


def _unshard_cols(g):
    return jnp.transpose(g, (1, 0, 2)).reshape(g.shape[1], -1)


def _unshard_rows(g):
    return g.reshape(-1, g.shape[2])


def _shard_cols(full):
    R, C = full.shape
    return jnp.transpose(full.reshape(R, N_DEV, C // N_DEV), (1, 0, 2)).reshape(N_CHIP, 2, R, C // N_DEV)


def _shard_rows(full):
    R, C = full.shape
    return full.reshape(N_CHIP, 2, R // N_DEV, C)


WEIGHTS = ['ln_ffn1_pre', 'ln_ffn1_post', 'ffn1_gate', 'ffn1_up', 'ffn1_down', 'ln_mix_pre', 'ln_mix_post', 'w_in',
           'rwkv_mu', 'rwkv_w0', 'rwkv_w2', 'rwkv_a0', 'rwkv_a2', 'rwkv_g2', 'rwkv_k_k', 'rwkv_k_a', 'rwkv_r_k',
           'rwkv_gn_w', 'rwkv_gn_b', 'w_proj_a', 'pool_w', 'pool_scale', 'w_proj_b', 'w_out', 'ln_ffn2_pre',
           'ln_ffn2_post', 'ffn2_gate', 'ffn2_up', 'ffn2_down']
COL_SHARDED = ['ffn1_gate', 'ffn1_up', 'ffn2_gate', 'ffn2_up', 'w_in', 'rwkv_w2', 'rwkv_a2', 'rwkv_g2', 'w_proj_a',
               'w_proj_b']
ROW_SHARDED = ['ffn1_down', 'ffn2_down', 'w_out', 'pool_w']
SHARDED = COL_SHARDED + ROW_SHARDED
REPLICATED = [n for n in WEIGHTS if n not in SHARDED]


def _step(args):
    wts = {n: args[n] if args[n].ndim == 2 else args[n][0] for n in WEIGHTS}
    x, target = args["x"][0], args["loss_target"][0]
    T, D = x.shape
    W = wts["rwkv_w0"].shape[1]
    PW = wts["pool_scale"].shape[1]
    LW, LA, LG = wts["rwkv_w2"].shape[0], wts["rwkv_a2"].shape[0], wts["rwkv_g2"].shape[0]
    lat = LW + LA + LG
    latp = _round_up(lat, LAT_ALIGN)
    rc = 3 * W + lat
    base = 3 * W + PW + 2 * D
    n_groups, gshard, gd = wts["pool_w"].shape

    pool_w_shard = wts["pool_w"].reshape(n_groups * gshard, gd)
    shards = {n: (pool_w_shard if n == "pool_w" else wts[n]).astype(BF16) for n in SHARDED}
    full = {}

    def fetch(names):
        return gather_carry([shards[n] for n in names])

    def arrived(names, got):
        for n, g in zip(names, got):
            if n == "pool_w":
                full[n] = jnp.transpose(g.reshape(N_DEV, n_groups, gshard, gd), (1, 0, 2, 3)).reshape(n_groups, gd, gd)
            else:
                full[n] = _unshard_cols(g) if n in COL_SHARDED else _unshard_rows(g)

    arrived(["ffn1_gate", "ffn1_up"], all_gather([shards["ffn1_gate"], shards["ffn1_up"]]))
    mu = wts["rwkv_mu"]
    mu_rkv = mu[:, :3 * W]
    mu_lat = jnp.concatenate([mu[:, 3 * W:], jnp.zeros((1, latp - lat), F32)], axis=1)
    rk = wts["rwkv_r_k"].reshape(1, W)

    n1 = rms_pre(x, wts["ln_ffn1_pre"])
    (g1, u1, act1), got = _mm("ffn1_up", [n1], [full["ffn1_gate"], full["ffn1_up"]], "nn", [BF16] * 3,
                              epilogue=_swiglu_fwd_epi, carry=fetch(["ffn1_down"]))
    arrived(["ffn1_down"], got)
    (f1,), got = _mm("ffn1_down", [act1], [full["ffn1_down"]], "nn", [F32], tm=512, carry=fetch(["w_in"]))
    arrived(["w_in"], got)
    w_in = full["w_in"]
    w_in_p = jnp.concatenate([w_in[:, :3 * W], w_in[:, rc:], w_in[:, 3 * W:rc], jnp.zeros((D, latp - lat), BF16)], axis=1)
    h1, nm = post_pre(x, f1, wts["ln_ffn1_post"], wts["ln_mix_pre"], MACARON)
    mixer = ["rwkv_w2", "rwkv_a2", "rwkv_g2", "w_proj_a", "w_proj_b", "pool_w", "w_out"]
    (p,), got = _mm("in_proj", [nm], [w_in_p], "nn", [F32], carry=fetch(mixer))
    arrived(mixer, got)
    pool_w = full["pool_w"]

    def pad_rows(m, at):
        return jnp.zeros((latp, W), BF16).at[at:at + m.shape[0]].set(m)

    w2p, a2p, g2p = pad_rows(full["rwkv_w2"], 0), pad_rows(full["rwkv_a2"], LW), pad_rows(full["rwkv_g2"], LW + LA)
    small = [mu_rkv, mu_lat, wts["rwkv_w0"], wts["rwkv_a0"], wts["rwkv_k_k"], wts["rwkv_k_a"], w2p, a2p, g2p]
    cols = {"rkv": (p, 3 * W, 0), "pool": (p, PW, 3 * W // PW), "lat": (p, latp, base // latp)}
    r, decay, kmod, v, aneg, bpos, gate = rwkv_prep(p, cols, *small)
    (y, states, sdota), got = wkv_fwd(r, decay, kmod, v, aneg, bpos, carry=fetch(["ffn2_gate", "ffn2_up"]))
    arrived(["ffn2_gate", "ffn2_up"], got)
    ya_in = rwkv_post(y, r, kmod, v, gate, wts["rwkv_gn_w"], wts["rwkv_gn_b"], rk)
    yb_in = pool_fwd(cols, pool_w, wts["pool_scale"])
    gates = [(p, 3 * W + PW), (p, 3 * W + PW + D)]
    m, ya, yb = _mm("mix", [ya_in, yb_in], [full["w_proj_a"], full["w_proj_b"]], "nn", [BF16] * 3,
                    extras=gates, epilogue=_mix_fwd_epi)
    mx = _mm("out_proj", [m], [full["w_out"]], "nn", [F32])[0]
    h2, n2 = post_pre(h1, mx, wts["ln_mix_post"], wts["ln_ffn2_pre"], 1.0)
    (g2_, u2, act2), got = _mm("ffn2_up", [n2], [full["ffn2_gate"], full["ffn2_up"]], "nn", [BF16] * 3,
                               epilogue=_swiglu_fwd_epi, carry=fetch(["ffn2_down"]))
    arrived(["ffn2_down"], got)
    f2 = _mm("ffn2_down", [act2], [full["ffn2_down"]], "nn", [F32], tm=512)[0]

    grads, slabs = {}, {}

    def pair_sums(names):
        parts = []
        for n in names:
            if n == "pool_w":
                parts.append(jnp.transpose(grads[n].reshape(n_groups, N_DEV, gshard, gd), (1, 0, 2, 3)).reshape(
                    N_CHIP, 2, n_groups * gshard, gd))
            else:
                parts.append(_shard_cols(grads[n]) if n in COL_SHARDED else _shard_rows(grads[n]))
        return [pair_add(part, rcv) for part, rcv in zip(parts, exchange_sibling(parts))]

    def landed(names, got):
        slabs.update(zip(names, got))

    dh3, df2, loss_part, grads["ln_ffn2_post"] = loss_post_bwd(h2, f2, wts["ln_ffn2_post"], target, MACARON)
    dg2, du2 = _mm("ffn2_dact", [df2], [full["ffn2_down"]], "nt", [BF16] * 2, extras=[(g2_, 0), (u2, 0)],
                   epilogue=_swiglu_bwd_epi)
    grads["ffn2_down"] = _mm("ffn2_ddown", [act2], [df2], "tn", [F32], tm=512, tn=1024)[0]
    grads["ffn2_gate"], grads["ffn2_up"] = _mm("ffn2_dup", [n2], [dg2, du2], "tn", [F32] * 2, tm=512)
    dn2 = _mm("ffn2_dn_gate", [dg2], [full["ffn2_gate"]], "nt", [F32], tm=512)[0]
    dn2 = _mm("ffn2_dn", [du2], [full["ffn2_up"]], "nt", [F32], tm=512, extras=[(dn2, 0)], epilogue=_add_epi)[0]
    sums2 = pair_sums(["ffn2_down", "ffn2_gate", "ffn2_up"])
    dh2, grads["ln_ffn2_pre"] = pre_bwd(dn2, h2, wts["ln_ffn2_pre"], dh3)
    dmx, grads["ln_mix_post"] = post_bwd(dh2, mx, wts["ln_mix_post"], 1.0)
    (dya, dyb, dga, dgb), got = _mm("dmix", [dmx], [full["w_out"]], "nt", [BF16] * 4, extras=gates + [(ya, 0), (yb, 0)],
                                    epilogue=_mix_bwd_epi, carry=chips_carry(sums2[:1]))
    landed(["ffn2_down"], got)
    grads["w_out"] = _mm("dw_out", [m], [dmx], "tn", [F32])[0]
    dya_in = _mm("dproj_a", [dya], [full["w_proj_a"]], "nt", [F32])[0]
    dyb_in = _mm("dproj_b", [dyb], [full["w_proj_b"]], "nt", [F32])[0]
    grads["w_proj_a"] = _mm("dw_proj_a", [ya_in], [dya], "tn", [F32])[0]
    grads["w_proj_b"] = _mm("dw_proj_b", [yb_in], [dyb], "tn", [F32])[0]
    dz_pool, grads["pool_w"], grads["pool_scale"] = pool_bwd(cols, dyb_in, pool_w, wts["pool_scale"])
    dy, dr_x, dk_x, dv_x, dgate, grads["rwkv_gn_w"], grads["rwkv_gn_b"], drk = rwkv_post_bwd(
        dya_in, y, r, kmod, v, gate, wts["rwkv_gn_w"], wts["rwkv_gn_b"], rk)
    grads["rwkv_r_k"] = drk.reshape(wts["rwkv_r_k"].shape)
    dr_s, ddecay, dk_s, dv_s, dneg, dbpos = wkv_bwd(r, decay, kmod, v, aneg, bpos, dy, states, sdota)
    (dzs, dzls, grads["rwkv_k_a"], grads["rwkv_k_k"], grads["rwkv_a0"], grads["rwkv_w0"], da2p, dw2p, dg2p) = rwkv_prep_bwd(
        p, cols, [dr_s, dr_x, ddecay, dk_s, dk_x, dv_s, dv_x, dneg, dbpos, dgate], *small)
    grads["rwkv_w2"], grads["rwkv_a2"], grads["rwkv_g2"] = dw2p[:LW], da2p[LW:LW + LA], dg2p[LW + LA:lat]
    dz_rkv, dz_lat, dmu_rkv, dmu_lat = shift_bwd(cols, dzs, dzls, mu_rkv, mu_lat)
    grads["rwkv_mu"] = jnp.concatenate([dmu_rkv, dmu_lat[:, :lat]], axis=1)
    dp = jnp.concatenate([dz_rkv, dz_pool, dga, dgb, dz_lat], axis=1)
    (dnm,), got = _mm("din_proj", [dp], [w_in_p], "nt", [F32], tm=512, carry=chips_carry(sums2[1:2]))
    landed(["ffn2_gate"], got)
    (dw_in_p,), got = _mm("dw_in", [nm], [dp], "tn", [F32], carry=chips_carry(sums2[2:]))
    landed(["ffn2_up"], got)
    grads["w_in"] = jnp.concatenate([dw_in_p[:, :3 * W], dw_in_p[:, base:base + lat], dw_in_p[:, 3 * W:base]], axis=1)
    sums_mix = pair_sums(["w_in"] + mixer)
    dh1, grads["ln_mix_pre"] = pre_bwd(dnm, h1, wts["ln_mix_pre"], dh2)
    df1, grads["ln_ffn1_post"] = post_bwd(dh1, f1, wts["ln_ffn1_post"], MACARON)
    (dg1, du1), got = _mm("ffn1_dact", [df1], [full["ffn1_down"]], "nt", [BF16] * 2, extras=[(g1, 0), (u1, 0)],
                          epilogue=_swiglu_bwd_epi, carry=chips_carry(sums_mix[:1]))
    landed(["w_in"], got)
    (grads["ffn1_down"],), got = _mm("ffn1_ddown", [act1], [df1], "tn", [F32], tm=512, tn=1024,
                                     carry=chips_carry(sums_mix[1:]))
    landed(mixer, got)
    (grads["ffn1_gate"], grads["ffn1_up"]), got = _mm("ffn1_dup", [n1], [dg1, du1], "tn", [F32] * 2, tm=512,
                                                      carry=chips_carry(pair_sums(["ffn1_down"])))
    landed(["ffn1_down"], got)
    sums1 = pair_sums(["ffn1_gate", "ffn1_up"])
    (dn1,), got = _mm("ffn1_dn_gate", [dg1], [full["ffn1_gate"]], "nt", [F32], tm=512, carry=chips_carry(sums1[:1]))
    landed(["ffn1_gate"], got)
    (dn1,), got = _mm("ffn1_dn", [du1], [full["ffn1_up"]], "nt", [F32], tm=512, extras=[(dn1, 0)], epilogue=_add_epi,
                      carry=chips_carry(sums1[1:]))
    landed(["ffn1_up"], got)
    grad_x, grads["ln_ffn1_pre"] = pre_bwd(dn1, x, wts["ln_ffn1_pre"], dh1)

    flat = jnp.concatenate([grads[n].reshape(-1) for n in REPLICATED])
    n_small = flat.shape[0]
    rows = _round_up(n_small, 8 * LANES) // LANES
    flat = jnp.concatenate([flat, jnp.zeros((rows * LANES - n_small,), F32)]).reshape(rows, LANES)
    small_slabs = all_gather([flat])[0]

    def packed(prefix):
        vals = jnp.concatenate([args[prefix + n].reshape(-1) for n in REPLICATED])
        return jnp.concatenate([vals, jnp.ones((rows * LANES - n_small,), F32)]).reshape(rows, LANES)

    outs = {}
    small_out = adamw(packed(""), packed("m_"), packed("v_"), small_slabs)
    offset = 0
    for n in REPLICATED:
        size = args[n].size
        outs[n] = [o.reshape(-1)[offset:offset + size].reshape(args[n].shape) for o in small_out]
        offset += size
    for n in SHARDED:
        shard2d = slabs[n].shape[1:]
        res = adamw(*[args[pre + n].reshape(shard2d) for pre in ("", "m_", "v_")], slabs[n])
        outs[n] = [o.reshape(args[n].shape) for o in res]

    loss = lax.psum(loss_part[0, 0], ("x", "y", "c"))
    return (loss, grad_x[None], *[outs[n][0] for n in WEIGHTS], *[outs[n][1] for n in WEIGHTS],
            *[outs[n][2] for n in WEIGHTS], *[outs[n][3] for n in WEIGHTS])


ARG_NAMES = ["x"] + WEIGHTS + ["loss_target"] + ["m_" + n for n in WEIGHTS] + ["v_" + n for n in WEIGHTS]


def kernel(x, ln_ffn1_pre, ln_ffn1_post, ffn1_gate, ffn1_up, ffn1_down, ln_mix_pre, ln_mix_post, w_in, rwkv_mu, rwkv_w0,
           rwkv_w2, rwkv_a0, rwkv_a2, rwkv_g2, rwkv_k_k, rwkv_k_a, rwkv_r_k, rwkv_gn_w, rwkv_gn_b, w_proj_a, pool_w,
           pool_scale, w_proj_b, w_out, ln_ffn2_pre, ln_ffn2_post, ffn2_gate, ffn2_up, ffn2_down, loss_target,
           m_ln_ffn1_pre, m_ln_ffn1_post, m_ffn1_gate, m_ffn1_up, m_ffn1_down, m_ln_mix_pre, m_ln_mix_post, m_w_in,
           m_rwkv_mu, m_rwkv_w0, m_rwkv_w2, m_rwkv_a0, m_rwkv_a2, m_rwkv_g2, m_rwkv_k_k, m_rwkv_k_a, m_rwkv_r_k,
           m_rwkv_gn_w, m_rwkv_gn_b, m_w_proj_a, m_pool_w, m_pool_scale, m_w_proj_b, m_w_out, m_ln_ffn2_pre,
           m_ln_ffn2_post, m_ffn2_gate, m_ffn2_up, m_ffn2_down, v_ln_ffn1_pre, v_ln_ffn1_post, v_ffn1_gate, v_ffn1_up,
           v_ffn1_down, v_ln_mix_pre, v_ln_mix_post, v_w_in, v_rwkv_mu, v_rwkv_w0, v_rwkv_w2, v_rwkv_a0, v_rwkv_a2,
           v_rwkv_g2, v_rwkv_k_k, v_rwkv_k_a, v_rwkv_r_k, v_rwkv_gn_w, v_rwkv_gn_b, v_w_proj_a, v_pool_w, v_pool_scale,
           v_w_proj_b, v_w_out, v_ln_ffn2_pre, v_ln_ffn2_post, v_ffn2_gate, v_ffn2_up, v_ffn2_down):
    given = locals()
    return _step({n: given[n] for n in ARG_NAMES})
```

```python
import jax
import jax.numpy as jnp
from jax import lax
from jax.experimental import pallas as pl
from jax.experimental.pallas import tpu as pltpu

F32, BF16 = jnp.float32, jnp.bfloat16
N_DEV = 8
N_CHIP = 4
HEAD = 64
LANES = 2 * HEAD
NORM_EPS, GN_EPS, L2_EPS = 1e-6, 64e-5, 1e-12
POOL_WINDOWS = (2, 4, 8, 16)
POOL_HALO = 16
MACARON = 0.5
ADAM_LR, ADAM_B1, ADAM_B2, ADAM_EPS, ADAM_WD, ADAM_STEP = 0.001, 0.9, 0.999, 1e-08, 0.01, 10
VMEM_LIMIT = 48 * 1024 * 1024
MM_VMEM_BUDGET = 36 * 1024 * 1024
ROW_TILE = 256
RWKV_ROW_TILE = 128
LAT_ALIGN = 512
WKV_CHUNK, WKV_PAIRS = 16, 8
WKV_UNROLL = 4
WKV_MXU_PAIRS = 5
MESH = pl.DeviceIdType.MESH


def _pallas(body, **kw):
    return pl.pallas_call(body, **kw)


def _params(*sem):
    return pltpu.CompilerParams(dimension_semantics=sem, vmem_limit_bytes=VMEM_LIMIT)


def _tile(n, target, align=128):
    best = None
    for d in range(align, min(n, target) + 1, align):
        if n % d == 0:
            best = d
    return best if best is not None else n


def _round_up(n, m):
    return (n + m - 1) // m * m


ANY = pl.BlockSpec(memory_space=pl.ANY)


def _mm(name, a_list, b_list, mode, out_dtypes, *, sum_pairs=False, extras=(), epilogue=None, tm=1024, tn=512,
        carry=None):
    n_a, n_b = len(a_list), len(b_list)
    n_prod = max(n_a, n_b)
    assert n_a in (1, n_prod) and n_b in (1, n_prod)
    a0, b0 = a_list[0], b_list[0]
    if mode == "nn":
        (M, K), N = a0.shape, b0.shape[1]
    elif mode == "nt":
        (M, K), N = a0.shape, b0.shape[0]
    else:
        (K, M), N = a0.shape, b0.shape[1]
    tm, tn = _tile(M, tm), _tile(N, tn)
    n_acc = 1 if sum_pairs else n_prod
    n_ex = len(extras)

    def planned(tk):
        operands = 2 * 2 * tk * (n_a * tm + n_b * tn)
        tiles = 2 * tm * tn * (sum(e.dtype.itemsize for e, _ in extras) + sum(jnp.dtype(d).itemsize for d in out_dtypes))
        return operands + tiles + 4 * tm * tn * (n_acc + len(out_dtypes))

    tk = max([d for d in range(128, K + 1, 128) if K % d == 0 and planned(d) <= MM_VMEM_BUDGET] or [_tile(K, 512)])
    nk = K // tk
    if mode == "tn":
        a_spec = pl.BlockSpec((tk, tm), lambda i, j, k: (k, i))
    else:
        a_spec = pl.BlockSpec((tm, tk), lambda i, j, k: (i, k))
    if mode == "nt":
        b_spec = pl.BlockSpec((tn, tk), lambda i, j, k: (j, k))
    else:
        b_spec = pl.BlockSpec((tk, tn), lambda i, j, k: (k, j))
    contract = {"nn": ((1,), (0,)), "nt": ((1,), (1,)), "tn": ((0,), (0,))}[mode]
    e_specs = []
    for _, col in extras:
        assert col % tn == 0
        e_specs.append(pl.BlockSpec((tm, tn), lambda i, j, k, off=col // tn: (i, j + off)))
    o_spec = pl.BlockSpec((tm, tn), lambda i, j, k: (i, j))

    n_in, n_out, n_scr = n_a + n_b + n_ex, len(out_dtypes), (n_acc if nk > 1 else 0)
    c_in, c_out = (len(carry.inputs), len(carry.out_shapes)) if carry else (0, 0)
    grid = (M // tm, N // tn, nk)

    def body(*refs):
        a_refs, b_refs, e_refs = refs[:n_a], refs[n_a:n_a + n_b], refs[n_a + n_b:n_in]
        o_refs = refs[n_in + c_in:n_in + c_in + n_out]
        acc_refs = refs[n_in + c_in + n_out + c_out:n_in + c_in + n_out + c_out + n_scr]
        carried = (refs[n_in:n_in + c_in], refs[n_in + c_in + n_out:n_in + c_in + n_out + c_out],
                   refs[n_in + c_in + n_out + c_out + n_scr:])
        at = [pl.program_id(d) for d in range(3)]

        if carry:
            @pl.when((at[0] == 0) & (at[1] == 0) & (at[2] == 0))
            def _():
                carry.start(*carried)

        def products():
            a_vals, b_vals = [a[...] for a in a_refs], [b[...] for b in b_refs]
            prods = [lax.dot_general(a_vals[p if n_a > 1 else 0], b_vals[p if n_b > 1 else 0], (contract, ((), ())),
                                     preferred_element_type=F32) for p in range(n_prod)]
            return [sum(prods[1:], prods[0])] if sum_pairs else prods

        def finish(results):
            outs = epilogue(results, [e[...] for e in e_refs]) if epilogue else results
            for o_ref, o in zip(o_refs, outs):
                o_ref[...] = o.astype(o_ref.dtype)

        if nk == 1:
            finish(products())
        else:
            @pl.when(at[2] == 0)
            def _():
                for acc in acc_refs:
                    acc[...] = jnp.zeros_like(acc)

            for acc, prod in zip(acc_refs, products()):
                acc[...] += prod

            @pl.when(at[2] == nk - 1)
            def _():
                finish([acc[...] for acc in acc_refs])

        if carry:
            @pl.when((at[0] == grid[0] - 1) & (at[1] == grid[1] - 1) & (at[2] == grid[2] - 1))
            def _():
                carry.finish(*carried)

    res = _pallas(
        body, name=name, grid=grid,
        in_specs=[a_spec] * n_a + [b_spec] * n_b + e_specs + [ANY] * c_in,
        out_specs=[o_spec] * n_out + [ANY] * c_out,
        out_shape=[jax.ShapeDtypeStruct((M, N), dt) for dt in out_dtypes] + (list(carry.out_shapes) if carry else []),
        scratch_shapes=[pltpu.VMEM((tm, tn), F32)] * n_scr + (list(carry.scratch) if carry else []),
        compiler_params=_params("arbitrary", "arbitrary", "arbitrary") if carry else _params("parallel", "parallel", "arbitrary"),
    )(*a_list, *b_list, *[e for e, _ in extras], *(carry.inputs if carry else []))
    return (res[:n_out], res[n_out:]) if carry else res


def _swiglu_fwd_epi(accs, _):
    g, u = accs
    return [g, u, g * jax.nn.sigmoid(g) * u]


def _swiglu_bwd_epi(accs, ex):
    dact = accs[0]
    g, u = ex[0].astype(F32), ex[1].astype(F32)
    sg = jax.nn.sigmoid(g)
    return [dact * u * (sg * (1.0 + g * (1.0 - sg))), dact * (g * sg)]


def _add_epi(accs, ex):
    return [accs[0] + ex[0]]


def _mix_fwd_epi(accs, ex):
    ya, yb = accs
    return [jax.nn.sigmoid(ex[0]) * ya + jax.nn.sigmoid(ex[1]) * yb, ya, yb]


def _mix_bwd_epi(accs, ex):
    dm = accs[0]
    sa, sb = jax.nn.sigmoid(ex[0]), jax.nn.sigmoid(ex[1])
    ya, yb = ex[2].astype(F32), ex[3].astype(F32)
    return [dm * sa, dm * sb, dm * ya * sa * (1.0 - sa), dm * yb * sb * (1.0 - sb)]


def _row_call(name, body, T, tiled, params, outs, accs=(), prev=(), nxt=(), halo=8, tile=ROW_TILE):
    tm = min(tile, T)
    n_tiles = T // tm

    def norm(e):
        return e if isinstance(e, tuple) else (e, e.shape[1], 0)

    tiled, prev, nxt = [norm(e) for e in tiled], [norm(e) for e in prev], [norm(e) for e in nxt]
    per_halo, n_halo = tm // halo, T // halo
    in_specs = [pl.BlockSpec((tm, w), lambda i, cb=cb: (i, cb)) for _, w, cb in tiled]
    in_specs += [pl.BlockSpec((halo, w), lambda i, cb=cb: (jnp.maximum(i * per_halo - 1, 0), cb)) for _, w, cb in prev]
    in_specs += [pl.BlockSpec((halo, w), lambda i, cb=cb: (jnp.minimum((i + 1) * per_halo, n_halo - 1), cb))
                 for _, w, cb in nxt]
    in_specs += [pl.BlockSpec(p.shape, lambda i, nd=p.ndim: (0,) * nd) for p in params]
    out_specs = [pl.BlockSpec((tm, o.shape[1]), lambda i: (i, 0)) for o in outs]
    out_specs += [pl.BlockSpec(a.shape, lambda i, nd=len(a.shape): (0,) * nd) for a in accs]
    n1, n2, n3, n4, n5 = len(tiled), len(prev), len(nxt), len(params), len(outs)

    def kernel_body(*refs):
        i = pl.program_id(0)
        acc_refs = refs[n1 + n2 + n3 + n4 + n5:]

        @pl.when(i == 0)
        def _():
            for a in acc_refs:
                a[...] = jnp.zeros_like(a)

        body(i, n_tiles, refs[:n1], refs[n1:n1 + n2], refs[n1 + n2:n1 + n2 + n3],
             refs[n1 + n2 + n3:n1 + n2 + n3 + n4], refs[n1 + n2 + n3 + n4:n1 + n2 + n3 + n4 + n5], acc_refs)

    return _pallas(
        kernel_body, name=name, grid=(n_tiles,), in_specs=in_specs, out_specs=out_specs,
        out_shape=list(outs) + list(accs),
        compiler_params=_params("arbitrary"),
    )(*[e[0] for e in tiled + prev + nxt], *params)


def _sds(shape, dtype=F32):
    return jax.ShapeDtypeStruct(tuple(shape), dtype)


def _rstd(x):
    return lax.rsqrt(jnp.mean(x * x, axis=-1, keepdims=True) + NORM_EPS)


def _colsum(x):
    return jnp.sum(x, axis=0, keepdims=True)


def rms_pre(x, g):
    T, D = x.shape

    def body(i, n, tiled, prev, nxt, params, outs, accs):
        xv = tiled[0][...]
        outs[0][...] = (xv * _rstd(xv) * params[0][...]).astype(BF16)

    return _row_call("rms_pre", body, T, [x], [g], [_sds((T, D), BF16)])[0]


def post_pre(h, f, g_post, g_pre, scale):
    T, D = h.shape

    def body(i, n, tiled, prev, nxt, params, outs, accs):
        hv, fv = tiled[0][...], tiled[1][...]
        h2 = hv + scale * (fv * _rstd(fv) * params[0][...])
        outs[0][...] = h2
        outs[1][...] = (h2 * _rstd(h2) * params[1][...]).astype(BF16)

    return _row_call("post_pre", body, T, [h, f], [g_post, g_pre], [_sds((T, D)), _sds((T, D), BF16)])


def _post_bwd_math(dh, fv, g, scale):
    r = _rstd(fv)
    fhat = fv * r
    dy = scale * dh
    z = dy * g
    df = r * (z - fhat * jnp.mean(z * fhat, axis=-1, keepdims=True))
    return df, _colsum(dy * fhat)


def loss_post_bwd(h, f, g_post, target, scale):
    T, D = h.shape

    def body(i, n, tiled, prev, nxt, params, outs, accs):
        hv, fv, tv = tiled[0][...], tiled[1][...], tiled[2][...]
        g = params[0][...]
        e = hv + scale * (fv * _rstd(fv) * g) - tv
        accs[0][...] += jnp.full(accs[0].shape, 0.5 / D, F32) * jnp.sum(e * e)
        dh = e * (1.0 / D)
        outs[0][...] = dh
        df, dg = _post_bwd_math(dh, fv, g, scale)
        outs[1][...] = df.astype(BF16)
        accs[1][...] += dg

    return _row_call("loss_post_bwd", body, T, [h, f, target], [g_post],
                     [_sds((T, D)), _sds((T, D), BF16)], [_sds((1, LANES)), _sds((1, D))])


def post_bwd(dh, f, g_post, scale):
    T, D = dh.shape

    def body(i, n, tiled, prev, nxt, params, outs, accs):
        df, dg = _post_bwd_math(tiled[0][...], tiled[1][...], params[0][...], scale)
        outs[0][...] = df.astype(BF16)
        accs[0][...] += dg

    return _row_call("post_bwd", body, T, [dh, f], [g_post], [_sds((T, D), BF16)], [_sds((1, D))])


def pre_bwd(dn, h, g_pre, dres):
    T, D = h.shape

    def body(i, n, tiled, prev, nxt, params, outs, accs):
        dnv, hv = tiled[0][...], tiled[1][...]
        r = _rstd(hv)
        hhat = hv * r
        z = dnv * params[0][...]
        outs[0][...] = tiled[2][...] + r * (z - hhat * jnp.mean(z * hhat, axis=-1, keepdims=True))
        accs[0][...] += _colsum(dnv * hhat)

    return _row_call("pre_bwd", body, T, [dn, h, dres], [g_pre], [_sds((T, D))], [_sds((1, D))])


def _head_ones():
    i = lax.broadcasted_iota(jnp.int32, (LANES, LANES), 0)
    j = lax.broadcasted_iota(jnp.int32, (LANES, LANES), 1)
    return jnp.where((i < HEAD) == (j < HEAD), 1.0, 0.0).astype(F32)


def _headsum(x):
    e = _head_ones()
    parts = [jnp.dot(x[:, s:s + LANES], e, precision=lax.Precision.HIGHEST, preferred_element_type=F32)
             for s in range(0, x.shape[1], LANES)]
    return parts[0] if len(parts) == 1 else jnp.concatenate(parts, axis=1)


def _shift_down(x, before):
    row = lax.broadcasted_iota(jnp.int32, x.shape, 0)
    return jnp.where(row == 0, before, pltpu.roll(x, 1, 0))


def _shift_up(x, after):
    row = lax.broadcasted_iota(jnp.int32, x.shape, 0)
    return jnp.where(row == x.shape[0] - 1, after, pltpu.roll(x, x.shape[0] - 1, 0))


def _last_row(ref, keep):
    r = ref[ref.shape[0] - 1:ref.shape[0], :]
    return jnp.where(keep, r, jnp.zeros_like(r))


def _first_row(ref, keep):
    r = ref[0:1, :]
    return jnp.where(keep, r, jnp.zeros_like(r))


def _softplus(u):
    return jnp.maximum(u, 0.0) + jnp.log(1.0 + jnp.exp(-jnp.abs(u)))


def _dotb(a, b, contract):
    return lax.dot_general(a.astype(BF16), b.astype(BF16), (contract, ((), ())), preferred_element_type=F32)


_NN, _NT, _TN = ((1,), (0,)), ((1,), (1,)), ((0,), (0,))


def _prep_forward(z, zprev_row, zl, zlprev_row, mu, mul, w0, a0, kk_w, ka_w, w2p, a2p, g2p):
    W = w0.shape[1]
    zs = z + (_shift_down(z, zprev_row) - z) * mu
    zls = zl + (_shift_down(zl, zlprev_row) - zl) * mul
    r, k, v = zs[:, :W], zs[:, W:2 * W], zs[:, 2 * W:]
    th, sg = jnp.tanh(zls), jax.nn.sigmoid(zls)
    xw = w0 + _dotb(th, w2p, _NN)
    wlog = -_softplus(-xw) - 0.5
    ew = jnp.exp(wlog)
    decay = jnp.exp(-ew)
    a = jax.nn.sigmoid(a0 + _dotb(zls, a2p, _NN))
    gate = _dotb(sg, g2p, _NN)
    q = k * kk_w
    nrm = jnp.sqrt(_headsum(q * q))
    den = jnp.maximum(nrm, L2_EPS)
    kk = q / den
    kmod = k * (1.0 + (a - 1.0) * ka_w)
    return dict(zs=zs, zls=zls, r=r, k=k, v=v, th=th, sg=sg, xw=xw, ew=ew, decay=decay, a=a, gate=gate,
                nrm=nrm, den=den, kk=kk, kmod=kmod)


def rwkv_prep(p, cols, mu, mul, w0, a0, kk_w, ka_w, w2p, a2p, g2p):
    T = p.shape[0]
    W = w0.shape[1]

    def body(i, n, tiled, prev, nxt, params, outs, accs):
        c = _prep_forward(tiled[0][...], _last_row(prev[0], i > 0), tiled[1][...], _last_row(prev[1], i > 0),
                          *[q[...] for q in params])
        for o, val in zip(outs, (c["r"], c["decay"], c["kmod"], c["v"], -c["kk"], c["kk"] * c["a"], c["gate"])):
            o[...] = val

    return _row_call("rwkv_prep", body, T, [cols["rkv"], cols["lat"]],
                     [mu, mul, w0, a0, kk_w, ka_w, w2p, a2p, g2p], [_sds((T, W))] * 7,
                     prev=[cols["rkv"], cols["lat"]], tile=RWKV_ROW_TILE)


def _post_forward(y, r, kmod, v, gn_w, gn_b, rk):
    mean = _headsum(y) * (1.0 / HEAD)
    yc = y - mean
    rstd = lax.rsqrt(_headsum(yc * yc) * (1.0 / HEAD) + GN_EPS)
    yn = yc * rstd
    s = _headsum(r * kmod * rk)
    return yn, rstd, s, yn * gn_w + gn_b + s * v


def rwkv_post(y, r, kmod, v, gate, gn_w, gn_b, rk):
    T, W = y.shape

    def body(i, n, tiled, prev, nxt, params, outs, accs):
        yv, rv, kv, vv, gv = [t[...] for t in tiled]
        _, _, _, o = _post_forward(yv, rv, kv, vv, *[q[...] for q in params])
        outs[0][...] = (o * gv).astype(BF16)

    return _row_call("rwkv_post", body, T, [y, r, kmod, v, gate], [gn_w, gn_b, rk], [_sds((T, W), BF16)],
                     tile=RWKV_ROW_TILE)[0]


def rwkv_post_bwd(dout, y, r, kmod, v, gate, gn_w, gn_b, rk):
    T, W = y.shape

    def body(i, n, tiled, prev, nxt, params, outs, accs):
        dv_, yv, rv, kv, vv, gv = [t[...] for t in tiled]
        gn_w_, gn_b_, rk_ = [q[...] for q in params]
        yn, rstd, s, o = _post_forward(yv, rv, kv, vv, gn_w_, gn_b_, rk_)
        do = dv_ * gv
        outs[4][...] = dv_ * o
        accs[0][...] += _colsum(do * yn)
        accs[1][...] += _colsum(do)
        dyn = do * gn_w_
        outs[0][...] = rstd * (dyn - _headsum(dyn) * (1.0 / HEAD) - yn * (_headsum(dyn * yn) * (1.0 / HEAD)))
        ds = _headsum(do * vv)
        outs[1][...] = ds * kv * rk_
        outs[2][...] = ds * rv * rk_
        outs[3][...] = do * s
        accs[2][...] += _colsum(ds * rv * kv)

    return _row_call("rwkv_post_bwd", body, T, [dout, y, r, kmod, v, gate], [gn_w, gn_b, rk],
                     [_sds((T, W))] * 5, [_sds((1, W))] * 3, tile=RWKV_ROW_TILE)


def rwkv_prep_bwd(p, cols, grads, mu, mul, w0, a0, kk_w, ka_w, w2p, a2p, g2p):
    T = p.shape[0]
    W = w0.shape[1]
    latp = w2p.shape[0]

    def body(i, n, tiled, prev, nxt, params, outs, accs):
        pv = [q[...] for q in params]
        mu_, mul_, w0_, a0_, kk_w_, ka_w_, w2p_, a2p_, g2p_ = pv
        c = _prep_forward(tiled[0][...], _last_row(prev[0], i > 0), tiled[1][...], _last_row(prev[1], i > 0), *pv)
        dr_s, dr_x, ddecay, dk_s, dk_x, dv_s, dv_x, dneg, db, dgate = [t[...] for t in tiled[2:]]
        k, a, kk = c["k"], c["a"], c["kk"]
        dkmod = dk_s + dk_x
        dk = dkmod * (1.0 + (a - 1.0) * ka_w_)
        da = dkmod * k * ka_w_ + db * kk
        accs[0][...] += _colsum(dkmod * k * (a - 1.0))
        dkk = db * a - dneg
        dq = jnp.where(c["nrm"] > L2_EPS, dkk - kk * _headsum(dkk * kk), dkk) / c["den"]
        dk = dk + dq * kk_w_
        accs[1][...] += _colsum(dq * k)
        dxa = da * a * (1.0 - a)
        accs[2][...] += _colsum(dxa)
        accs[4][...] += _dotb(c["zls"], dxa, _TN)
        dzls = _dotb(dxa, a2p_, _NT)
        dxw = (-ddecay * c["decay"] * c["ew"]) * jax.nn.sigmoid(-c["xw"])
        accs[3][...] += _colsum(dxw)
        accs[5][...] += _dotb(c["th"], dxw, _TN)
        dzls = dzls + _dotb(dxw, w2p_, _NT) * (1.0 - c["th"] * c["th"])
        accs[6][...] += _dotb(c["sg"], dgate, _TN)
        dzls = dzls + _dotb(dgate, g2p_, _NT) * c["sg"] * (1.0 - c["sg"])
        outs[0][...] = jnp.concatenate([dr_s + dr_x, dk, dv_s + dv_x], axis=1)
        outs[1][...] = dzls

    return _row_call("rwkv_prep_bwd", body, T, [cols["rkv"], cols["lat"]] + list(grads),
                     [mu, mul, w0, a0, kk_w, ka_w, w2p, a2p, g2p], [_sds((T, 3 * W)), _sds((T, latp))],
                     [_sds((1, W))] * 4 + [_sds((latp, W))] * 3, prev=[cols["rkv"], cols["lat"]], tile=RWKV_ROW_TILE)


def shift_bwd(cols, dzs, dzls, mu, mul):
    T = dzs.shape[0]

    def body(i, n, tiled, prev, nxt, params, outs, accs):
        for j in range(2):
            z, d, m = tiled[j][...], tiled[2 + j][...], params[j][...]
            zprev = _shift_down(z, _last_row(prev[j], i > 0))
            dnext = _shift_up(d, _first_row(nxt[j], i < n - 1))
            outs[j][...] = (d * (1.0 - m) + dnext * m).astype(BF16)
            accs[j][...] += _colsum(d * (zprev - z))

    return _row_call("shift_bwd", body, T, [cols["rkv"], cols["lat"], dzs, dzls], [mu, mul],
                     [_sds(dzs.shape, BF16), _sds(dzls.shape, BF16)], [_sds(mu.shape), _sds(mul.shape)],
                     prev=[cols["rkv"], cols["lat"]], nxt=[dzs, dzls])


def _window_pick(x, windows):
    gid = lax.broadcasted_iota(jnp.int32, x.shape, 1) // (x.shape[1] // len(windows))
    out = windows[-1]
    for g in range(len(windows) - 2, -1, -1):
        out = jnp.where(gid == g, windows[g], out)
    return out


def _pool_counts(t0, rows, width):
    t = (t0 + lax.broadcasted_iota(jnp.int32, (rows, width), 0) + 1).astype(F32)
    return _window_pick(t, [jnp.minimum(t, float(w)) for w in POOL_WINDOWS])


def _pool_mixed(x, before, t0):
    tm, width = x.shape
    xe = jnp.concatenate([before, x], axis=0)
    sums, s, span = [], xe, 1
    for w in POOL_WINDOWS:
        while span < w:
            s = s + pltpu.roll(s, span, 0)
            span *= 2
        sums.append(s[POOL_HALO:, :])
    return _window_pick(x, sums) / _pool_counts(t0, tm, width) - x


def _group_dot(x, w_ref, contract):
    gd = w_ref.shape[-1]
    parts = [_dotb(x[:, g * gd:(g + 1) * gd], w_ref[g], contract) for g in range(w_ref.shape[0])]
    return jnp.concatenate(parts, axis=1)


def pool_fwd(cols, pool_w, pool_scale):
    T, width = cols["pool"][0].shape[0], cols["pool"][1]
    tm = min(ROW_TILE, T)

    def body(i, n, tiled, prev, nxt, params, outs, accs):
        before = jnp.where(i > 0, prev[0][...], 0.0)
        mixed = _pool_mixed(tiled[0][...], before, i * tm)
        outs[0][...] = (_group_dot(mixed, params[0], _NN) * params[1][...]).astype(BF16)

    return _row_call("pool_fwd", body, T, [cols["pool"]], [pool_w, pool_scale], [_sds((T, width), BF16)],
                     prev=[cols["pool"]], halo=POOL_HALO)[0]


def pool_bwd(cols, dout, pool_w, pool_scale):
    T, width = dout.shape
    tm = min(ROW_TILE, T)

    def body(i, n, tiled, prev, nxt, params, outs, accs):
        w_ref, scale = params[0], params[1][...]
        before = jnp.where(i > 0, prev[0][...], 0.0)
        mixed = _pool_mixed(tiled[0][...], before, i * tm)
        dv = tiled[1][...]
        accs[1][...] += _colsum(dv * _group_dot(mixed, w_ref, _NN))
        after = jnp.where(i < n - 1, nxt[0][...], 0.0)
        dys = jnp.concatenate([dv, after], axis=0) * scale
        gd = w_ref.shape[-1]
        for g in range(w_ref.shape[0]):
            accs[0][g] += _dotb(mixed[:, g * gd:(g + 1) * gd], dys[:tm, g * gd:(g + 1) * gd], _TN)
        dmixed = _group_dot(dys, w_ref, _NT)
        u = dmixed / _pool_counts(i * tm, tm + POOL_HALO, width)
        rows = tm + POOL_HALO
        sums, s, span = [], u, 1
        for w in POOL_WINDOWS:
            while span < w:
                s = s + pltpu.roll(s, rows - span, 0)
                span *= 2
            sums.append(s[:tm, :])
        outs[0][...] = (_window_pick(dv, sums) - dmixed[:tm, :]).astype(BF16)

    return _row_call("pool_bwd", body, T, [cols["pool"], dout], [pool_w, pool_scale], [_sds((T, width), BF16)],
                     [_sds(pool_w.shape), _sds((1, width))], prev=[cols["pool"]], nxt=[dout], halo=POOL_HALO)


def _wkv_consts(pairs):
    lane = lax.broadcasted_iota(jnp.int32, (HEAD, LANES), 1)
    sub = lax.broadcasted_iota(jnp.int32, (pairs * HEAD, LANES), 0)
    lane_all = lax.broadcasted_iota(jnp.int32, (pairs * HEAD, LANES), 1)
    i = lax.broadcasted_iota(jnp.int32, (LANES, LANES), 0)
    j = lax.broadcasted_iota(jnp.int32, (LANES, LANES), 1)
    ones = jnp.where((i < HEAD) == (j < HEAD), 1.0, 0.0).astype(BF16)
    diag = jnp.where((lane_all & (HEAD - 1)) == (sub & (HEAD - 1)), 1.0, 0.0).astype(F32)
    return lane < HEAD, diag, ones


def _segsum(p, in_a):
    sa = jnp.sum(jnp.where(in_a, p, 0.0), axis=1, keepdims=True)
    sb = jnp.sum(jnp.where(in_a, 0.0, p), axis=1, keepdims=True)
    return jnp.where(in_a, sa, sb)


def _hi_lo(p):
    hi = lax.bitcast_convert_type(lax.bitcast_convert_type(p, jnp.uint32) & jnp.uint32(0xFFFF0000), F32)
    return hi, p - hi


def _segsum_mxu(p, ones):
    hi, lo = _hi_lo(p)
    return (jnp.dot(hi.astype(BF16), ones, preferred_element_type=F32)
            + jnp.dot(lo.astype(BF16), ones, preferred_element_type=F32))


def _cat(parts, axis):
    return parts[0] if len(parts) == 1 else jnp.concatenate(parts, axis=axis)


def _tile_rows(row, pairs):
    return _cat([jnp.broadcast_to(row[:, p * LANES:(p + 1) * LANES], (HEAD, LANES)) for p in range(pairs)], 0)


def _spread(row, pairs, diag16, ones):
    hi, lo = _hi_lo(row)
    return (jnp.dot(_tile_rows(hi.astype(BF16), pairs) * diag16, ones, preferred_element_type=F32)
            + jnp.dot(_tile_rows(lo.astype(BF16), pairs) * diag16, ones, preferred_element_type=F32))


def _pair_colsums(x, pairs):
    return _cat([_colsum(x[p * HEAD:(p + 1) * HEAD]) for p in range(pairs)], 1)


def _spread_split(row, pairs, in_a, diag, diag16, ones):
    n_mxu = min(WKV_MXU_PAIRS, pairs)
    parts = [_spread(row[:, :n_mxu * LANES], n_mxu, diag16[:n_mxu * HEAD], ones)]
    parts += [_segsum(row[:, p * LANES:(p + 1) * LANES] * diag[:HEAD], in_a) for p in range(n_mxu, pairs)]
    return _cat(parts, 0)


def wkv_fwd(r, w, k, v, a, b, carry=None):
    T, W = r.shape
    P = W // LANES
    PB = min(WKV_PAIRS, P)
    chunk = min(WKV_CHUNK, T)
    NC = T // chunk
    R = PB * HEAD
    c_in, c_out = (len(carry.inputs), len(carry.out_shapes)) if carry else (0, 0)

    def body(*refs):
        r_ref, w_ref, k_ref, v_ref, a_ref, b_ref = refs[:6]
        y_ref, st_ref, sa_ref = refs[6 + c_in:9 + c_in]
        vt_ref, s_ref = refs[9 + c_in + c_out:11 + c_in + c_out]
        carried = refs[6:6 + c_in], refs[9 + c_in:9 + c_in + c_out], refs[11 + c_in + c_out:]
        g, c = pl.program_id(0), pl.program_id(1)

        if carry:
            @pl.when((g == 0) & (c == 0))
            def _():
                carry.start(*carried)

        @pl.when(c == 0)
        def _():
            s_ref[...] = jnp.zeros_like(s_ref)

        in_a, diag, ones = _wkv_consts(PB)
        diag16 = diag.astype(BF16)

        def spread(t, _):
            vt_ref[t] = _spread_split(v_ref[pl.ds(t, 1), :], PB, in_a, diag, diag16, ones)
            return 0

        lax.fori_loop(0, chunk, spread, 0, unroll=WKV_UNROLL)

        def step(t, _):
            rows = [ref[pl.ds(t, 1), :] for ref in (w_ref, k_ref, a_ref, b_ref)]
            for p in range(PB):
                wt, kt, at, bt = [x[:, p * LANES:(p + 1) * LANES] for x in rows]
                rs = pl.ds(p * HEAD, HEAD)
                S = s_ref[rs]
                sa = _segsum(S * at, in_a)
                sa_ref[t, rs] = sa
                S = S * wt + sa * bt + vt_ref[t, rs] * kt
                st_ref[t, rs] = S
                s_ref[rs] = S
            return 0

        lax.fori_loop(0, chunk, step, 0, unroll=WKV_UNROLL)

        def readout(t, _):
            yt = _segsum_mxu(st_ref[t] * _tile_rows(r_ref[pl.ds(t, 1), :], PB), ones) * diag
            y_ref[pl.ds(t, 1), :] = _pair_colsums(yt, PB)
            return 0

        lax.fori_loop(0, chunk, readout, 0, unroll=WKV_UNROLL)

        if carry:
            @pl.when((g == P // PB - 1) & (c == NC - 1))
            def _():
                carry.finish(*carried)

    spec = pl.BlockSpec((chunk, PB * LANES), lambda g, c: (c, g))
    tiles = pl.BlockSpec((chunk, R, LANES), lambda g, c: (c, g, 0))
    res = _pallas(
        body, name="wkv_fwd", grid=(P // PB, NC), in_specs=[spec] * 6 + [ANY] * c_in,
        out_specs=[spec, tiles, tiles] + [ANY] * c_out,
        out_shape=[_sds((T, W)), _sds((T, P * HEAD, LANES)), _sds((T, P * HEAD, LANES))]
        + (list(carry.out_shapes) if carry else []),
        scratch_shapes=[pltpu.VMEM((chunk, R, LANES), F32), pltpu.VMEM((R, LANES), F32)]
        + (list(carry.scratch) if carry else []),
        compiler_params=_params("arbitrary", "arbitrary") if carry else _params("parallel", "arbitrary"),
    )(r, w, k, v, a, b, *(carry.inputs if carry else []))
    return (res[:3], res[3:]) if carry else res


def wkv_bwd(r, w, k, v, a, b, dy, st, sa):
    T, W = r.shape
    P = W // LANES
    PB = min(WKV_PAIRS, P)
    chunk = min(WKV_CHUNK, T)
    NC = T // chunk
    R = PB * HEAD

    def body(r_ref, w_ref, k_ref, v_ref, a_ref, b_ref, dy_ref, st_ref, before_ref, sa_ref,
             dr_ref, dw_ref, dk_ref, dv_ref, da_ref, db_ref, ds_ref, vt_ref, dyt_ref, dst_ref, dsa_ref):
        c = pl.program_id(1)

        @pl.when(c == 0)
        def _():
            ds_ref[...] = jnp.zeros_like(ds_ref)

        in_a, diag, ones = _wkv_consts(PB)
        diag16 = diag.astype(BF16)

        def spread(t, _):
            vt_ref[t] = _spread_split(v_ref[pl.ds(t, 1), :], PB, in_a, diag, diag16, ones)
            dyt_ref[t] = _spread_split(dy_ref[pl.ds(t, 1), :], PB, in_a, diag, diag16, ones)
            return 0

        lax.fori_loop(0, chunk, spread, 0, unroll=WKV_UNROLL)

        def bstep(n, _):
            t = chunk - 1 - n
            rows = [ref[pl.ds(t, 1), :] for ref in (r_ref, w_ref, a_ref, b_ref)]
            for p in range(PB):
                rt, wt, at, bt = [x[:, p * LANES:(p + 1) * LANES] for x in rows]
                rs = pl.ds(p * HEAD, HEAD)
                dS = ds_ref[rs] + dyt_ref[t, rs] * rt
                dst_ref[t, rs] = dS
                dsa = _segsum(dS * bt, in_a)
                dsa_ref[t, rs] = dsa
                ds_ref[rs] = dS * wt + dsa * at
            return 0

        lax.fori_loop(0, chunk, bstep, 0, unroll=WKV_UNROLL)

        def collect(t, _):
            sn, dS, dsa = st_ref[t], dst_ref[t], dsa_ref[t]
            sp = st_ref[jnp.maximum(t - 1, 0)]
            dvt = _segsum_mxu(dS * _tile_rows(k_ref[pl.ds(t, 1), :], PB), ones) * diag
            for ref, val in ((dr_ref, sn * dyt_ref[t]), (dw_ref, dS * sp), (dk_ref, dS * vt_ref[t]), (dv_ref, dvt),
                             (da_ref, sp * dsa), (db_ref, dS * sa_ref[t])):
                ref[pl.ds(t, 1), :] = _pair_colsums(val, PB)
            return 0

        lax.fori_loop(0, chunk, collect, 0, unroll=WKV_UNROLL)
        first = jnp.where(c == NC - 1, 0.0, before_ref[0])
        dw_ref[0:1, :] = _pair_colsums(dst_ref[0] * first, PB)
        da_ref[0:1, :] = _pair_colsums(first * dsa_ref[0], PB)

    spec = pl.BlockSpec((chunk, PB * LANES), lambda g, c: (NC - 1 - c, g))
    tiles = pl.BlockSpec((chunk, R, LANES), lambda g, c: (NC - 1 - c, g, 0))
    before = pl.BlockSpec((1, R, LANES), lambda g, c: (jnp.maximum((NC - 1 - c) * chunk - 1, 0), g, 0))

    def scratch(n):
        return pltpu.VMEM((n, R, LANES), F32)

    return _pallas(
        body, name="wkv_bwd", grid=(P // PB, NC), in_specs=[spec] * 7 + [tiles, before, tiles],
        out_specs=[spec] * 6, out_shape=[_sds((T, W))] * 6,
        scratch_shapes=[pltpu.VMEM((R, LANES), F32), scratch(chunk), scratch(chunk), scratch(chunk), scratch(chunk)],
        compiler_params=_params("parallel", "arbitrary"),
    )(r, w, k, v, a, b, dy, st, st, sa)


def _position():
    return lax.axis_index("x"), lax.axis_index("y"), lax.axis_index("c")


def _other_chips(x, y):
    return [(1 - x, y), (x, 1 - y), (1 - x, 1 - y)]


class _Carry:
    def __init__(self, inputs, out_shapes, scratch, start, finish):
        self.inputs, self.out_shapes, self.scratch, self.start, self.finish = inputs, out_shapes, scratch, start, finish


def _run_carry(name, carry):
    n_in, n_out = len(carry.inputs), len(carry.out_shapes)

    def body(*refs):
        parts = refs[:n_in], refs[n_in:n_in + n_out], refs[n_in + n_out:]
        carry.start(*parts)
        carry.finish(*parts)

    return _pallas(body, name=name, in_specs=[ANY] * n_in, out_specs=[ANY] * n_out, out_shape=list(carry.out_shapes),
                   scratch_shapes=list(carry.scratch))(*carry.inputs)


def gather_carry(shards):
    n = len(shards)

    def plan(x_refs, out_refs, sems):
        send_sems, recv_sems, local_sems = sems
        x, y, c = _position()
        me, sibling = (x, y, c), (x, y, 1 - c)
        chips = _other_chips(x, y)

        def slot(ref, pos):
            return ref.at[4 * pos[0] + 2 * pos[1] + pos[2]]

        def copy(t, j, block, to, src=None):
            dst = slot(out_refs[t], block)
            return pltpu.make_async_remote_copy(
                src_ref=dst if src is None else src, dst_ref=dst, send_sem=send_sems.at[t, j],
                recv_sem=recv_sems.at[t, j], device_id=to, device_id_type=MESH)

        mine = [pltpu.make_async_copy(x_refs[t], slot(out_refs[t], me), local_sems.at[t]) for t in range(n)]
        first = []
        for t in range(n):
            first.append(copy(t, 0, me, sibling, src=x_refs[t]))
            first += [copy(t, 1 + j, me, (*chip, c), src=x_refs[t]) for j, chip in enumerate(chips)]
        return c, me, sibling, chips, copy, mine, first

    def start(x_refs, out_refs, sems):
        _, _, _, _, _, mine, first = plan(x_refs, out_refs, sems)
        for cp in mine + first:
            cp.start()

    def finish(x_refs, out_refs, sems):
        c, me, sibling, chips, copy, mine, first = plan(x_refs, out_refs, sems)
        passed = []
        for t in range(n):
            for j, chip in enumerate(chips):
                copy(t, 1 + j, (*chip, c), me).wait_recv()
                fwd = copy(t, 4 + j, (*chip, c), sibling)
                fwd.start()
                passed.append(fwd)
        for t in range(n):
            copy(t, 0, sibling, me).wait_recv()
            for j, chip in enumerate(chips):
                copy(t, 4 + j, (*chip, 1 - c), me).wait_recv()
        for cp in first + passed:
            cp.wait_send()
        for cp in mine:
            cp.wait()

    return _Carry(list(shards), [_sds((N_DEV,) + s.shape, s.dtype) for s in shards],
                  [pltpu.SemaphoreType.DMA((n, 7)), pltpu.SemaphoreType.DMA((n, 7)), pltpu.SemaphoreType.DMA((n,))],
                  start, finish)


def all_gather(shards):
    return _run_carry("all_gather", gather_carry(shards))


def exchange_sibling(parts):
    n = len(parts)

    def body(*refs):
        p_refs, out_refs = refs[:n], refs[n:2 * n]
        send_sems, recv_sems = refs[2 * n:]
        x, y, c = _position()
        copies = []
        for t in range(n):
            for q in range(N_CHIP):
                cp = pltpu.make_async_remote_copy(
                    src_ref=p_refs[t].at[q, 1 - c], dst_ref=out_refs[t].at[q], send_sem=send_sems.at[t, q],
                    recv_sem=recv_sems.at[t, q], device_id=(x, y, 1 - c), device_id_type=MESH)
                cp.start()
                copies.append(cp)
        for cp in copies:
            cp.wait()

    return _pallas(
        body, name="exchange_sibling", in_specs=[ANY] * n, out_specs=[ANY] * n,
        out_shape=[_sds((N_CHIP,) + p.shape[2:], p.dtype) for p in parts],
        scratch_shapes=[pltpu.SemaphoreType.DMA((n, N_CHIP)), pltpu.SemaphoreType.DMA((n, N_CHIP))],
    )(*parts)


def chips_carry(parts):
    n = len(parts)

    def plan(p_refs, out_refs, sems):
        send_sems, recv_sems, local_sems = sems
        x, y, c = _position()
        local = [pltpu.make_async_copy(p_refs[t].at[2 * x + y], out_refs[t].at[3], local_sems.at[t]) for t in range(n)]
        remote = [pltpu.make_async_remote_copy(
            src_ref=p_refs[t].at[2 * cx + cy], dst_ref=out_refs[t].at[j], send_sem=send_sems.at[t, j],
            recv_sem=recv_sems.at[t, j], device_id=(cx, cy, c), device_id_type=MESH)
            for t in range(n) for j, (cx, cy) in enumerate(_other_chips(x, y))]
        return local, remote

    def start(p_refs, out_refs, sems):
        local, remote = plan(p_refs, out_refs, sems)
        for cp in local + remote:
            cp.start()

    def finish(p_refs, out_refs, sems):
        local, remote = plan(p_refs, out_refs, sems)
        for cp in remote + local:
            cp.wait()

    return _Carry(list(parts), [_sds(p.shape, p.dtype) for p in parts],
                  [pltpu.SemaphoreType.DMA((n, 3)), pltpu.SemaphoreType.DMA((n, 3)), pltpu.SemaphoreType.DMA((n,))],
                  start, finish)


def _flat_tile(rows, cols):
    tr = rows
    for d in range(16, min(rows, 512) + 1, 16):
        if rows % d == 0 and d * cols * 4 <= 2 * 1024 * 1024:
            tr = d
    return tr


def pair_add(part, recv):
    _, _, R, C = part.shape
    tr = _flat_tile(R, C)
    core = jnp.reshape(lax.axis_index("c"), (1,)).astype(jnp.int32)

    def body(core_ref, p_ref, r_ref, o_ref):
        o_ref[...] = (p_ref[...] + r_ref[...]).astype(BF16)

    grid_spec = pltpu.PrefetchScalarGridSpec(
        num_scalar_prefetch=1, grid=(N_CHIP, R // tr),
        in_specs=[pl.BlockSpec((None, None, tr, C), lambda q, i, core_ref: (q, core_ref[0], i, 0)),
                  pl.BlockSpec((None, tr, C), lambda q, i, core_ref: (q, i, 0))],
        out_specs=pl.BlockSpec((None, tr, C), lambda q, i, core_ref: (q, i, 0)))
    return _pallas(body, name="pair_add", grid_spec=grid_spec, out_shape=_sds((N_CHIP, R, C), BF16),
                   compiler_params=_params("parallel", "parallel"))(core, part, recv)


def adamw(w, m, v, slabs):
    R, C = w.shape
    tr = _flat_tile(R, C)
    n = slabs.shape[0]

    def body(w_ref, m_ref, v_ref, s_ref, g_ref, d_ref, nm_ref, nv_ref):
        g = s_ref[0].astype(F32)
        for j in range(1, n):
            g = g + s_ref[j].astype(F32)
        m2 = ADAM_B1 * m_ref[...] + (1.0 - ADAM_B1) * g
        v2 = ADAM_B2 * v_ref[...] + (1.0 - ADAM_B2) * (g * g)
        m_hat = m2 / (1.0 - ADAM_B1 ** ADAM_STEP)
        v_hat = v2 / (1.0 - ADAM_B2 ** ADAM_STEP)
        g_ref[...] = g
        d_ref[...] = -ADAM_LR * (m_hat / (jnp.sqrt(v_hat) + ADAM_EPS) + ADAM_WD * w_ref[...])
        nm_ref[...] = m2
        nv_ref[...] = v2

    spec = pl.BlockSpec((tr, C), lambda i: (i, 0))
    return _pallas(body, name="adamw", grid=(R // tr,),
                   in_specs=[spec] * 3 + [pl.BlockSpec((n, tr, C), lambda i: (0, i, 0))], out_specs=[spec] * 4,
                   out_shape=[_sds((R, C))] * 4, compiler_params=_params("parallel"))(w, m, v, slabs)


def _unshard_cols(g):
    return jnp.transpose(g, (1, 0, 2)).reshape(g.shape[1], -1)


def _unshard_rows(g):
    return g.reshape(-1, g.shape[2])


def _shard_cols(full):
    R, C = full.shape
    return jnp.transpose(full.reshape(R, N_DEV, C // N_DEV), (1, 0, 2)).reshape(N_CHIP, 2, R, C // N_DEV)


def _shard_rows(full):
    R, C = full.shape
    return full.reshape(N_CHIP, 2, R // N_DEV, C)


WEIGHTS = ['ln_ffn1_pre', 'ln_ffn1_post', 'ffn1_gate', 'ffn1_up', 'ffn1_down', 'ln_mix_pre', 'ln_mix_post', 'w_in',
           'rwkv_mu', 'rwkv_w0', 'rwkv_w2', 'rwkv_a0', 'rwkv_a2', 'rwkv_g2', 'rwkv_k_k', 'rwkv_k_a', 'rwkv_r_k',
           'rwkv_gn_w', 'rwkv_gn_b', 'w_proj_a', 'pool_w', 'pool_scale', 'w_proj_b', 'w_out', 'ln_ffn2_pre',
           'ln_ffn2_post', 'ffn2_gate', 'ffn2_up', 'ffn2_down']
COL_SHARDED = ['ffn1_gate', 'ffn1_up', 'ffn2_gate', 'ffn2_up', 'w_in', 'rwkv_w2', 'rwkv_a2', 'rwkv_g2', 'w_proj_a',
               'w_proj_b']
ROW_SHARDED = ['ffn1_down', 'ffn2_down', 'w_out', 'pool_w']
TRANSPOSED = ['ffn1_gate', 'ffn1_up', 'ffn2_gate', 'ffn2_up', 'w_in', 'w_proj_a', 'w_proj_b']
SHARDED = COL_SHARDED + ROW_SHARDED
REPLICATED = [n for n in WEIGHTS if n not in SHARDED]


def _step(args):
    wts = {n: args[n] if args[n].ndim == 2 else args[n][0] for n in WEIGHTS}
    x, target = args["x"][0], args["loss_target"][0]
    T, D = x.shape
    W = wts["rwkv_w0"].shape[1]
    PW = wts["pool_scale"].shape[1]
    LW, LA, LG = wts["rwkv_w2"].shape[0], wts["rwkv_a2"].shape[0], wts["rwkv_g2"].shape[0]
    lat = LW + LA + LG
    latp = _round_up(lat, LAT_ALIGN)
    rc = 3 * W + lat
    base = 3 * W + PW + 2 * D
    n_groups, gshard, gd = wts["pool_w"].shape

    pool_w_shard = wts["pool_w"].reshape(n_groups * gshard, gd)
    shards = {n: (pool_w_shard if n == "pool_w" else wts[n]).astype(BF16) for n in SHARDED}
    shards.update({n: shards[n].T for n in TRANSPOSED})
    full = {}

    def fetch(names):
        return gather_carry([shards[n] for n in names])

    def arrived(names, got):
        for n, g in zip(names, got):
            if n == "pool_w":
                full[n] = jnp.transpose(g.reshape(N_DEV, n_groups, gshard, gd), (1, 0, 2, 3)).reshape(n_groups, gd, gd)
            elif n in COL_SHARDED and n not in TRANSPOSED:
                full[n] = _unshard_cols(g)
            else:
                full[n] = _unshard_rows(g)

    arrived(["ffn1_gate", "ffn1_up"], all_gather([shards["ffn1_gate"], shards["ffn1_up"]]))
    mu = wts["rwkv_mu"]
    mu_rkv = mu[:, :3 * W]
    mu_lat = jnp.concatenate([mu[:, 3 * W:], jnp.zeros((1, latp - lat), F32)], axis=1)
    rk = wts["rwkv_r_k"].reshape(1, W)

    n1 = rms_pre(x, wts["ln_ffn1_pre"])
    (g1, u1, act1), got = _mm("ffn1_up", [n1], [full["ffn1_gate"], full["ffn1_up"]], "nt", [BF16] * 3,
                              epilogue=_swiglu_fwd_epi, carry=fetch(["ffn1_down"]))
    arrived(["ffn1_down"], got)
    (f1,), got = _mm("ffn1_down", [act1], [full["ffn1_down"]], "nn", [F32], tm=512, carry=fetch(["w_in"]))
    arrived(["w_in"], got)
    w_in = full["w_in"]
    w_in_p = jnp.concatenate([w_in[:3 * W], w_in[rc:], w_in[3 * W:rc], jnp.zeros((latp - lat, D), BF16)], axis=0)
    h1, nm = post_pre(x, f1, wts["ln_ffn1_post"], wts["ln_mix_pre"], MACARON)
    mixer = ["rwkv_w2", "rwkv_a2", "rwkv_g2", "w_proj_a", "w_proj_b", "pool_w", "w_out"]
    (p,), got = _mm("in_proj", [nm], [w_in_p], "nt", [F32], carry=fetch(mixer))
    arrived(mixer, got)
    pool_w = full["pool_w"]

    def pad_rows(m, at):
        return jnp.zeros((latp, W), BF16).at[at:at + m.shape[0]].set(m)

    w2p, a2p, g2p = pad_rows(full["rwkv_w2"], 0), pad_rows(full["rwkv_a2"], LW), pad_rows(full["rwkv_g2"], LW + LA)
    small = [mu_rkv, mu_lat, wts["rwkv_w0"], wts["rwkv_a0"], wts["rwkv_k_k"], wts["rwkv_k_a"], w2p, a2p, g2p]
    cols = {"rkv": (p, 3 * W, 0), "pool": (p, PW, 3 * W // PW), "lat": (p, latp, base // latp)}
    r, decay, kmod, v, aneg, bpos, gate = rwkv_prep(p, cols, *small)
    (y, states, sdota), got = wkv_fwd(r, decay, kmod, v, aneg, bpos, carry=fetch(["ffn2_gate", "ffn2_up"]))
    arrived(["ffn2_gate", "ffn2_up"], got)
    ya_in = rwkv_post(y, r, kmod, v, gate, wts["rwkv_gn_w"], wts["rwkv_gn_b"], rk)
    yb_in = pool_fwd(cols, pool_w, wts["pool_scale"])
    gates = [(p, 3 * W + PW), (p, 3 * W + PW + D)]
    m, ya, yb = _mm("mix", [ya_in, yb_in], [full["w_proj_a"], full["w_proj_b"]], "nt", [BF16] * 3,
                    extras=gates, epilogue=_mix_fwd_epi)
    mx = _mm("out_proj", [m], [full["w_out"]], "nn", [F32])[0]
    h2, n2 = post_pre(h1, mx, wts["ln_mix_post"], wts["ln_ffn2_pre"], 1.0)
    (g2_, u2, act2), got = _mm("ffn2_up", [n2], [full["ffn2_gate"], full["ffn2_up"]], "nt", [BF16] * 3,
                               epilogue=_swiglu_fwd_epi, carry=fetch(["ffn2_down"]))
    arrived(["ffn2_down"], got)
    f2 = _mm("ffn2_down", [act2], [full["ffn2_down"]], "nn", [F32], tm=512)[0]

    grads, slabs = {}, {}

    def pair_sums(names):
        parts = []
        for n in names:
            if n == "pool_w":
                parts.append(jnp.transpose(grads[n].reshape(n_groups, N_DEV, gshard, gd), (1, 0, 2, 3)).reshape(
                    N_CHIP, 2, n_groups * gshard, gd))
            elif n in COL_SHARDED and n not in TRANSPOSED:
                parts.append(_shard_cols(grads[n]))
            else:
                parts.append(_shard_rows(grads[n]))
        return [pair_add(part, rcv) for part, rcv in zip(parts, exchange_sibling(parts))]

    def landed(names, got):
        slabs.update(zip(names, got))

    dh3, df2, loss_part, grads["ln_ffn2_post"] = loss_post_bwd(h2, f2, wts["ln_ffn2_post"], target, MACARON)
    dg2, du2 = _mm("ffn2_dact", [df2], [full["ffn2_down"]], "nt", [BF16] * 2, extras=[(g2_, 0), (u2, 0)],
                   epilogue=_swiglu_bwd_epi)
    grads["ffn2_down"] = _mm("ffn2_ddown", [act2], [df2], "tn", [F32], tm=512, tn=1024)[0]
    grads["ffn2_gate"], grads["ffn2_up"] = _mm("ffn2_dup", [dg2, du2], [n2], "tn", [F32] * 2, tm=512)
    dn2 = _mm("ffn2_dn_gate", [dg2], [full["ffn2_gate"]], "nn", [F32], tm=512)[0]
    dn2 = _mm("ffn2_dn", [du2], [full["ffn2_up"]], "nn", [F32], tm=512, extras=[(dn2, 0)], epilogue=_add_epi)[0]
    sums2 = pair_sums(["ffn2_down", "ffn2_gate", "ffn2_up"])
    dh2, grads["ln_ffn2_pre"] = pre_bwd(dn2, h2, wts["ln_ffn2_pre"], dh3)
    dmx, grads["ln_mix_post"] = post_bwd(dh2, mx, wts["ln_mix_post"], 1.0)
    (dya, dyb, dga, dgb), got = _mm("dmix", [dmx], [full["w_out"]], "nt", [BF16] * 4, extras=gates + [(ya, 0), (yb, 0)],
                                    epilogue=_mix_bwd_epi, carry=chips_carry(sums2[:1]))
    landed(["ffn2_down"], got)
    grads["w_out"] = _mm("dw_out", [m], [dmx], "tn", [F32])[0]
    dya_in = _mm("dproj_a", [dya], [full["w_proj_a"]], "nn", [F32])[0]
    dyb_in = _mm("dproj_b", [dyb], [full["w_proj_b"]], "nn", [F32])[0]
    grads["w_proj_a"] = _mm("dw_proj_a", [dya], [ya_in], "tn", [F32])[0]
    grads["w_proj_b"] = _mm("dw_proj_b", [dyb], [yb_in], "tn", [F32])[0]
    dz_pool, grads["pool_w"], grads["pool_scale"] = pool_bwd(cols, dyb_in, pool_w, wts["pool_scale"])
    dy, dr_x, dk_x, dv_x, dgate, grads["rwkv_gn_w"], grads["rwkv_gn_b"], drk = rwkv_post_bwd(
        dya_in, y, r, kmod, v, gate, wts["rwkv_gn_w"], wts["rwkv_gn_b"], rk)
    grads["rwkv_r_k"] = drk.reshape(wts["rwkv_r_k"].shape)
    dr_s, ddecay, dk_s, dv_s, dneg, dbpos = wkv_bwd(r, decay, kmod, v, aneg, bpos, dy, states, sdota)
    (dzs, dzls, grads["rwkv_k_a"], grads["rwkv_k_k"], grads["rwkv_a0"], grads["rwkv_w0"], da2p, dw2p, dg2p) = rwkv_prep_bwd(
        p, cols, [dr_s, dr_x, ddecay, dk_s, dk_x, dv_s, dv_x, dneg, dbpos, dgate], *small)
    grads["rwkv_w2"], grads["rwkv_a2"], grads["rwkv_g2"] = dw2p[:LW], da2p[LW:LW + LA], dg2p[LW + LA:lat]
    dz_rkv, dz_lat, dmu_rkv, dmu_lat = shift_bwd(cols, dzs, dzls, mu_rkv, mu_lat)
    grads["rwkv_mu"] = jnp.concatenate([dmu_rkv, dmu_lat[:, :lat]], axis=1)
    dp = jnp.concatenate([dz_rkv, dz_pool, dga, dgb, dz_lat], axis=1)
    (dnm,), got = _mm("din_proj", [dp], [w_in_p], "nn", [F32], tm=512, carry=chips_carry(sums2[1:2]))
    landed(["ffn2_gate"], got)
    (dw_in_p,), got = _mm("dw_in", [dp], [nm], "tn", [F32], tm=512, tn=1024, carry=chips_carry(sums2[2:]))
    landed(["ffn2_up"], got)
    grads["w_in"] = jnp.concatenate([dw_in_p[:3 * W], dw_in_p[base:base + lat], dw_in_p[3 * W:base]], axis=0)
    sums_mix = pair_sums(["w_in"] + mixer)
    dh1, grads["ln_mix_pre"] = pre_bwd(dnm, h1, wts["ln_mix_pre"], dh2)
    df1, grads["ln_ffn1_post"] = post_bwd(dh1, f1, wts["ln_ffn1_post"], MACARON)
    (dg1, du1), got = _mm("ffn1_dact", [df1], [full["ffn1_down"]], "nt", [BF16] * 2, extras=[(g1, 0), (u1, 0)],
                          epilogue=_swiglu_bwd_epi, carry=chips_carry(sums_mix[:1]))
    landed(["w_in"], got)
    (grads["ffn1_down"],), got = _mm("ffn1_ddown", [act1], [df1], "tn", [F32], tm=512, tn=1024,
                                     carry=chips_carry(sums_mix[1:]))
    landed(mixer, got)
    (grads["ffn1_gate"], grads["ffn1_up"]), got = _mm("ffn1_dup", [dg1, du1], [n1], "tn", [F32] * 2, tm=512,
                                                      carry=chips_carry(pair_sums(["ffn1_down"])))
    landed(["ffn1_down"], got)
    sums1 = pair_sums(["ffn1_gate", "ffn1_up"])
    (dn1,), got = _mm("ffn1_dn_gate", [dg1], [full["ffn1_gate"]], "nn", [F32], tm=512, carry=chips_carry(sums1[:1]))
    landed(["ffn1_gate"], got)
    (dn1,), got = _mm("ffn1_dn", [du1], [full["ffn1_up"]], "nn", [F32], tm=512, extras=[(dn1, 0)], epilogue=_add_epi,
                      carry=chips_carry(sums1[1:]))
    landed(["ffn1_up"], got)
    grad_x, grads["ln_ffn1_pre"] = pre_bwd(dn1, x, wts["ln_ffn1_pre"], dh1)

    flat = jnp.concatenate([grads[n].reshape(-1) for n in REPLICATED])
    n_small = flat.shape[0]
    rows = _round_up(n_small, 8 * LANES) // LANES
    flat = jnp.concatenate([flat, jnp.zeros((rows * LANES - n_small,), F32)]).reshape(rows, LANES)
    small_slabs = all_gather([flat])[0]

    def packed(prefix):
        vals = jnp.concatenate([args[prefix + n].reshape(-1) for n in REPLICATED])
        return jnp.concatenate([vals, jnp.ones((rows * LANES - n_small,), F32)]).reshape(rows, LANES)

    outs = {}
    small_out = adamw(packed(""), packed("m_"), packed("v_"), small_slabs)
    offset = 0
    for n in REPLICATED:
        size = args[n].size
        outs[n] = [o.reshape(-1)[offset:offset + size].reshape(args[n].shape) for o in small_out]
        offset += size
    for n in SHARDED:
        slab = jnp.swapaxes(slabs[n], 1, 2) if n in TRANSPOSED else slabs[n]
        shard2d = slab.shape[1:]
        res = adamw(*[args[pre + n].reshape(shard2d) for pre in ("", "m_", "v_")], slab)
        outs[n] = [o.reshape(args[n].shape) for o in res]

    loss = lax.psum(loss_part[0, 0], ("x", "y", "c"))
    return (loss, grad_x[None], *[outs[n][0] for n in WEIGHTS], *[outs[n][1] for n in WEIGHTS],
            *[outs[n][2] for n in WEIGHTS], *[outs[n][3] for n in WEIGHTS])


ARG_NAMES = ["x"] + WEIGHTS + ["loss_target"] + ["m_" + n for n in WEIGHTS] + ["v_" + n for n in WEIGHTS]


def kernel(x, ln_ffn1_pre, ln_ffn1_post, ffn1_gate, ffn1_up, ffn1_down, ln_mix_pre, ln_mix_post, w_in, rwkv_mu, rwkv_w0,
           rwkv_w2, rwkv_a0, rwkv_a2, rwkv_g2, rwkv_k_k, rwkv_k_a, rwkv_r_k, rwkv_gn_w, rwkv_gn_b, w_proj_a, pool_w,
           pool_scale, w_proj_b, w_out, ln_ffn2_pre, ln_ffn2_post, ffn2_gate, ffn2_up, ffn2_down, loss_target,
           m_ln_ffn1_pre, m_ln_ffn1_post, m_ffn1_gate, m_ffn1_up, m_ffn1_down, m_ln_mix_pre, m_ln_mix_post, m_w_in,
           m_rwkv_mu, m_rwkv_w0, m_rwkv_w2, m_rwkv_a0, m_rwkv_a2, m_rwkv_g2, m_rwkv_k_k, m_rwkv_k_a, m_rwkv_r_k,
           m_rwkv_gn_w, m_rwkv_gn_b, m_w_proj_a, m_pool_w, m_pool_scale, m_w_proj_b, m_w_out, m_ln_ffn2_pre,
           m_ln_ffn2_post, m_ffn2_gate, m_ffn2_up, m_ffn2_down, v_ln_ffn1_pre, v_ln_ffn1_post, v_ffn1_gate, v_ffn1_up,
           v_ffn1_down, v_ln_mix_pre, v_ln_mix_post, v_w_in, v_rwkv_mu, v_rwkv_w0, v_rwkv_w2, v_rwkv_a0, v_rwkv_a2,
           v_rwkv_g2, v_rwkv_k_k, v_rwkv_k_a, v_rwkv_r_k, v_rwkv_gn_w, v_rwkv_gn_b, v_w_proj_a, v_pool_w, v_pool_scale,
           v_w_proj_b, v_w_out, v_ln_ffn2_pre, v_ln_ffn2_post, v_ffn2_gate, v_ffn2_up, v_ffn2_down):
    given = locals()
    return _step({n: given[n] for n in ARG_NAMES})
```

```python
import jax
import jax.numpy as jnp
from jax import lax
from jax.experimental import pallas as pl
from jax.experimental.pallas import tpu as pltpu

F32, BF16 = jnp.float32, jnp.bfloat16
N_DEV = 8
N_CHIP = 4
HEAD = 64
LANES = 2 * HEAD
NORM_EPS, GN_EPS, L2_EPS = 1e-6, 64e-5, 1e-12
POOL_WINDOWS = (2, 4, 8, 16)
POOL_HALO = 16
MACARON = 0.5
ADAM_LR, ADAM_B1, ADAM_B2, ADAM_EPS, ADAM_WD, ADAM_STEP = 0.001, 0.9, 0.999, 1e-08, 0.01, 10
VMEM_LIMIT = 48 * 1024 * 1024
MM_VMEM_BUDGET = 36 * 1024 * 1024
ROW_TILE = 256
RWKV_ROW_TILE = 128
LAT_ALIGN = 512
WKV_CHUNK, WKV_PAIRS = 16, 8
WKV_UNROLL = 4
WKV_MXU_PAIRS = 5
MESH = pl.DeviceIdType.MESH


def _pallas(body, **kw):
    return pl.pallas_call(body, **kw)


def _params(*sem):
    return pltpu.CompilerParams(dimension_semantics=sem, vmem_limit_bytes=VMEM_LIMIT)


def _tile(n, target, align=128):
    best = None
    for d in range(align, min(n, target) + 1, align):
        if n % d == 0:
            best = d
    return best if best is not None else n


def _round_up(n, m):
    return (n + m - 1) // m * m


ANY = pl.BlockSpec(memory_space=pl.ANY)


def _mm(name, a_list, b_list, mode, out_dtypes, *, sum_pairs=False, extras=(), epilogue=None, tm=1024, tn=512,
        carry=None):
    n_a, n_b = len(a_list), len(b_list)
    n_prod = max(n_a, n_b)
    assert n_a in (1, n_prod) and n_b in (1, n_prod)
    a0, b0 = a_list[0], b_list[0]
    if mode == "nn":
        (M, K), N = a0.shape, b0.shape[1]
    elif mode == "nt":
        (M, K), N = a0.shape, b0.shape[0]
    else:
        (K, M), N = a0.shape, b0.shape[1]
    tm, tn = _tile(M, tm), _tile(N, tn)
    n_acc = 1 if sum_pairs else n_prod
    n_ex = len(extras)

    def planned(tk):
        operands = 2 * 2 * tk * (n_a * tm + n_b * tn)
        tiles = 2 * tm * tn * (sum(e.dtype.itemsize for e, _ in extras) + sum(jnp.dtype(d).itemsize for d in out_dtypes))
        return operands + tiles + 4 * tm * tn * (n_acc + len(out_dtypes))

    tk = max([d for d in range(128, K + 1, 128) if K % d == 0 and planned(d) <= MM_VMEM_BUDGET] or [_tile(K, 512)])
    nk = K // tk
    if mode == "tn":
        a_spec = pl.BlockSpec((tk, tm), lambda i, j, k: (k, i))
    else:
        a_spec = pl.BlockSpec((tm, tk), lambda i, j, k: (i, k))
    if mode == "nt":
        b_spec = pl.BlockSpec((tn, tk), lambda i, j, k: (j, k))
    else:
        b_spec = pl.BlockSpec((tk, tn), lambda i, j, k: (k, j))
    contract = {"nn": ((1,), (0,)), "nt": ((1,), (1,)), "tn": ((0,), (0,))}[mode]
    e_specs = []
    for _, col in extras:
        assert col % tn == 0
        e_specs.append(pl.BlockSpec((tm, tn), lambda i, j, k, off=col // tn: (i, j + off)))
    o_spec = pl.BlockSpec((tm, tn), lambda i, j, k: (i, j))

    n_in, n_out, n_scr = n_a + n_b + n_ex, len(out_dtypes), (n_acc if nk > 1 else 0)
    c_in, c_out = (len(carry.inputs), len(carry.out_shapes)) if carry else (0, 0)
    grid = (M // tm, N // tn, nk)

    def body(*refs):
        a_refs, b_refs, e_refs = refs[:n_a], refs[n_a:n_a + n_b], refs[n_a + n_b:n_in]
        o_refs = refs[n_in + c_in:n_in + c_in + n_out]
        acc_refs = refs[n_in + c_in + n_out + c_out:n_in + c_in + n_out + c_out + n_scr]
        carried = (refs[n_in:n_in + c_in], refs[n_in + c_in + n_out:n_in + c_in + n_out + c_out],
                   refs[n_in + c_in + n_out + c_out + n_scr:])
        at = [pl.program_id(d) for d in range(3)]

        if carry:
            @pl.when((at[0] == 0) & (at[1] == 0) & (at[2] == 0))
            def _():
                carry.start(*carried)

        def products():
            a_vals, b_vals = [a[...] for a in a_refs], [b[...] for b in b_refs]
            prods = [lax.dot_general(a_vals[p if n_a > 1 else 0], b_vals[p if n_b > 1 else 0], (contract, ((), ())),
                                     preferred_element_type=F32) for p in range(n_prod)]
            return [sum(prods[1:], prods[0])] if sum_pairs else prods

        def finish(results):
            outs = epilogue(results, [e[...] for e in e_refs]) if epilogue else results
            for o_ref, o in zip(o_refs, outs):
                o_ref[...] = o.astype(o_ref.dtype)

        if nk == 1:
            finish(products())
        else:
            @pl.when(at[2] == 0)
            def _():
                for acc in acc_refs:
                    acc[...] = jnp.zeros_like(acc)

            for acc, prod in zip(acc_refs, products()):
                acc[...] += prod

            @pl.when(at[2] == nk - 1)
            def _():
                finish([acc[...] for acc in acc_refs])

        if carry:
            @pl.when((at[0] == grid[0] - 1) & (at[1] == grid[1] - 1) & (at[2] == grid[2] - 1))
            def _():
                carry.finish(*carried)

    res = _pallas(
        body, name=name, grid=grid,
        in_specs=[a_spec] * n_a + [b_spec] * n_b + e_specs + [ANY] * c_in,
        out_specs=[o_spec] * n_out + [ANY] * c_out,
        out_shape=[jax.ShapeDtypeStruct((M, N), dt) for dt in out_dtypes] + (list(carry.out_shapes) if carry else []),
        scratch_shapes=[pltpu.VMEM((tm, tn), F32)] * n_scr + (list(carry.scratch) if carry else []),
        compiler_params=_params("arbitrary", "arbitrary", "arbitrary") if carry else _params("parallel", "parallel", "arbitrary"),
    )(*a_list, *b_list, *[e for e, _ in extras], *(carry.inputs if carry else []))
    return (res[:n_out], res[n_out:]) if carry else res


def _swiglu_fwd_epi(accs, _):
    g, u = accs
    return [g, u, g * jax.nn.sigmoid(g) * u]


def _swiglu_bwd_epi(accs, ex):
    dact = accs[0]
    g, u = ex[0].astype(F32), ex[1].astype(F32)
    sg = jax.nn.sigmoid(g)
    return [dact * u * (sg * (1.0 + g * (1.0 - sg))), dact * (g * sg)]


def _add_epi(accs, ex):
    return [accs[0] + ex[0]]


def _mix_fwd_epi(accs, ex):
    ya, yb = accs
    return [jax.nn.sigmoid(ex[0]) * ya + jax.nn.sigmoid(ex[1]) * yb, ya, yb]


def _mix_bwd_epi(accs, ex):
    dm = accs[0]
    sa, sb = jax.nn.sigmoid(ex[0]), jax.nn.sigmoid(ex[1])
    ya, yb = ex[2].astype(F32), ex[3].astype(F32)
    return [dm * sa, dm * sb, dm * ya * sa * (1.0 - sa), dm * yb * sb * (1.0 - sb)]


def _row_call(name, body, T, tiled, params, outs, accs=(), prev=(), nxt=(), halo=8, tile=ROW_TILE):
    tm = min(tile, T)
    n_tiles = T // tm

    def norm(e):
        return e if isinstance(e, tuple) else (e, e.shape[1], 0)

    tiled, prev, nxt = [norm(e) for e in tiled], [norm(e) for e in prev], [norm(e) for e in nxt]
    per_halo, n_halo = tm // halo, T // halo
    in_specs = [pl.BlockSpec((tm, w), lambda i, cb=cb: (i, cb)) for _, w, cb in tiled]
    in_specs += [pl.BlockSpec((halo, w), lambda i, cb=cb: (jnp.maximum(i * per_halo - 1, 0), cb)) for _, w, cb in prev]
    in_specs += [pl.BlockSpec((halo, w), lambda i, cb=cb: (jnp.minimum((i + 1) * per_halo, n_halo - 1), cb))
                 for _, w, cb in nxt]
    in_specs += [pl.BlockSpec(p.shape, lambda i, nd=p.ndim: (0,) * nd) for p in params]
    out_specs = [pl.BlockSpec((tm, o.shape[1]), lambda i: (i, 0)) for o in outs]
    out_specs += [pl.BlockSpec(a.shape, lambda i, nd=len(a.shape): (0,) * nd) for a in accs]
    n1, n2, n3, n4, n5 = len(tiled), len(prev), len(nxt), len(params), len(outs)

    def kernel_body(*refs):
        i = pl.program_id(0)
        acc_refs = refs[n1 + n2 + n3 + n4 + n5:]

        @pl.when(i == 0)
        def _():
            for a in acc_refs:
                a[...] = jnp.zeros_like(a)

        body(i, n_tiles, refs[:n1], refs[n1:n1 + n2], refs[n1 + n2:n1 + n2 + n3],
             refs[n1 + n2 + n3:n1 + n2 + n3 + n4], refs[n1 + n2 + n3 + n4:n1 + n2 + n3 + n4 + n5], acc_refs)

    return _pallas(
        kernel_body, name=name, grid=(n_tiles,), in_specs=in_specs, out_specs=out_specs,
        out_shape=list(outs) + list(accs),
        compiler_params=_params("arbitrary"),
    )(*[e[0] for e in tiled + prev + nxt], *params)


def _sds(shape, dtype=F32):
    return jax.ShapeDtypeStruct(tuple(shape), dtype)


def _rstd(x):
    return lax.rsqrt(jnp.mean(x * x, axis=-1, keepdims=True) + NORM_EPS)


def _colsum(x):
    return jnp.sum(x, axis=0, keepdims=True)


def rms_pre(x, g):
    T, D = x.shape

    def body(i, n, tiled, prev, nxt, params, outs, accs):
        xv = tiled[0][...]
        outs[0][...] = (xv * _rstd(xv) * params[0][...]).astype(BF16)

    return _row_call("rms_pre", body, T, [x], [g], [_sds((T, D), BF16)])[0]


def post_pre(h, f, g_post, g_pre, scale):
    T, D = h.shape

    def body(i, n, tiled, prev, nxt, params, outs, accs):
        hv, fv = tiled[0][...], tiled[1][...]
        h2 = hv + scale * (fv * _rstd(fv) * params[0][...])
        outs[0][...] = h2
        outs[1][...] = (h2 * _rstd(h2) * params[1][...]).astype(BF16)

    return _row_call("post_pre", body, T, [h, f], [g_post, g_pre], [_sds((T, D)), _sds((T, D), BF16)])


def _post_bwd_math(dh, fv, g, scale):
    r = _rstd(fv)
    fhat = fv * r
    dy = scale * dh
    z = dy * g
    df = r * (z - fhat * jnp.mean(z * fhat, axis=-1, keepdims=True))
    return df, _colsum(dy * fhat)


def loss_post_bwd(h, f, g_post, target, scale):
    T, D = h.shape

    def body(i, n, tiled, prev, nxt, params, outs, accs):
        hv, fv, tv = tiled[0][...], tiled[1][...], tiled[2][...]
        g = params[0][...]
        e = hv + scale * (fv * _rstd(fv) * g) - tv
        accs[0][...] += jnp.full(accs[0].shape, 0.5 / D, F32) * jnp.sum(e * e)
        dh = e * (1.0 / D)
        outs[0][...] = dh
        df, dg = _post_bwd_math(dh, fv, g, scale)
        outs[1][...] = df.astype(BF16)
        accs[1][...] += dg

    return _row_call("loss_post_bwd", body, T, [h, f, target], [g_post],
                     [_sds((T, D)), _sds((T, D), BF16)], [_sds((1, LANES)), _sds((1, D))])


def post_bwd(dh, f, g_post, scale):
    T, D = dh.shape

    def body(i, n, tiled, prev, nxt, params, outs, accs):
        df, dg = _post_bwd_math(tiled[0][...], tiled[1][...], params[0][...], scale)
        outs[0][...] = df.astype(BF16)
        accs[0][...] += dg

    return _row_call("post_bwd", body, T, [dh, f], [g_post], [_sds((T, D), BF16)], [_sds((1, D))])


def pre_bwd(dn, h, g_pre, dres):
    T, D = h.shape

    def body(i, n, tiled, prev, nxt, params, outs, accs):
        dnv, hv = tiled[0][...], tiled[1][...]
        r = _rstd(hv)
        hhat = hv * r
        z = dnv * params[0][...]
        outs[0][...] = tiled[2][...] + r * (z - hhat * jnp.mean(z * hhat, axis=-1, keepdims=True))
        accs[0][...] += _colsum(dnv * hhat)

    return _row_call("pre_bwd", body, T, [dn, h, dres], [g_pre], [_sds((T, D))], [_sds((1, D))])


def _head_ones():
    i = lax.broadcasted_iota(jnp.int32, (LANES, LANES), 0)
    j = lax.broadcasted_iota(jnp.int32, (LANES, LANES), 1)
    return jnp.where((i < HEAD) == (j < HEAD), 1.0, 0.0).astype(F32)


def _headsum(x):
    e = _head_ones()
    parts = [jnp.dot(x[:, s:s + LANES], e, precision=lax.Precision.HIGHEST, preferred_element_type=F32)
             for s in range(0, x.shape[1], LANES)]
    return parts[0] if len(parts) == 1 else jnp.concatenate(parts, axis=1)


def _shift_down(x, before):
    row = lax.broadcasted_iota(jnp.int32, x.shape, 0)
    return jnp.where(row == 0, before, pltpu.roll(x, 1, 0))


def _shift_up(x, after):
    row = lax.broadcasted_iota(jnp.int32, x.shape, 0)
    return jnp.where(row == x.shape[0] - 1, after, pltpu.roll(x, x.shape[0] - 1, 0))


def _last_row(ref, keep):
    r = ref[ref.shape[0] - 1:ref.shape[0], :]
    return jnp.where(keep, r, jnp.zeros_like(r))


def _first_row(ref, keep):
    r = ref[0:1, :]
    return jnp.where(keep, r, jnp.zeros_like(r))


def _softplus(u):
    return jnp.maximum(u, 0.0) + jnp.log(1.0 + jnp.exp(-jnp.abs(u)))


def _dotb(a, b, contract):
    return lax.dot_general(a.astype(BF16), b.astype(BF16), (contract, ((), ())), preferred_element_type=F32)


_NN, _NT, _TN = ((1,), (0,)), ((1,), (1,)), ((0,), (0,))


def _prep_forward(z, zprev_row, zl, zlprev_row, mu, mul, w0, a0, kk_w, ka_w, w2p, a2p, g2p):
    W = w0.shape[1]
    zs = z + (_shift_down(z, zprev_row) - z) * mu
    zls = zl + (_shift_down(zl, zlprev_row) - zl) * mul
    r, k, v = zs[:, :W], zs[:, W:2 * W], zs[:, 2 * W:]
    th, sg = jnp.tanh(zls), jax.nn.sigmoid(zls)
    xw = w0 + _dotb(th, w2p, _NN)
    wlog = -_softplus(-xw) - 0.5
    ew = jnp.exp(wlog)
    decay = jnp.exp(-ew)
    a = jax.nn.sigmoid(a0 + _dotb(zls, a2p, _NN))
    gate = _dotb(sg, g2p, _NN)
    q = k * kk_w
    nrm = jnp.sqrt(_headsum(q * q))
    den = jnp.maximum(nrm, L2_EPS)
    kk = q / den
    kmod = k * (1.0 + (a - 1.0) * ka_w)
    return dict(zs=zs, zls=zls, r=r, k=k, v=v, th=th, sg=sg, xw=xw, ew=ew, decay=decay, a=a, gate=gate,
                nrm=nrm, den=den, kk=kk, kmod=kmod)


def rwkv_prep(p, cols, mu, mul, w0, a0, kk_w, ka_w, w2p, a2p, g2p):
    T = p.shape[0]
    W = w0.shape[1]

    def body(i, n, tiled, prev, nxt, params, outs, accs):
        c = _prep_forward(tiled[0][...], _last_row(prev[0], i > 0), tiled[1][...], _last_row(prev[1], i > 0),
                          *[q[...] for q in params])
        for o, val in zip(outs, (c["r"], c["decay"], c["kmod"], c["v"], -c["kk"], c["kk"] * c["a"], c["gate"])):
            o[...] = val

    return _row_call("rwkv_prep", body, T, [cols["rkv"], cols["lat"]],
                     [mu, mul, w0, a0, kk_w, ka_w, w2p, a2p, g2p], [_sds((T, W))] * 7,
                     prev=[cols["rkv"], cols["lat"]], tile=RWKV_ROW_TILE)


def _post_forward(y, r, kmod, v, gn_w, gn_b, rk):
    mean = _headsum(y) * (1.0 / HEAD)
    yc = y - mean
    rstd = lax.rsqrt(_headsum(yc * yc) * (1.0 / HEAD) + GN_EPS)
    yn = yc * rstd
    s = _headsum(r * kmod * rk)
    return yn, rstd, s, yn * gn_w + gn_b + s * v


def rwkv_post(y, r, kmod, v, gate, gn_w, gn_b, rk):
    T, W = y.shape

    def body(i, n, tiled, prev, nxt, params, outs, accs):
        yv, rv, kv, vv, gv = [t[...] for t in tiled]
        _, _, _, o = _post_forward(yv, rv, kv, vv, *[q[...] for q in params])
        outs[0][...] = (o * gv).astype(BF16)

    return _row_call("rwkv_post", body, T, [y, r, kmod, v, gate], [gn_w, gn_b, rk], [_sds((T, W), BF16)],
                     tile=RWKV_ROW_TILE)[0]


def rwkv_post_bwd(dout, y, r, kmod, v, gate, gn_w, gn_b, rk):
    T, W = y.shape

    def body(i, n, tiled, prev, nxt, params, outs, accs):
        dv_, yv, rv, kv, vv, gv = [t[...] for t in tiled]
        gn_w_, gn_b_, rk_ = [q[...] for q in params]
        yn, rstd, s, o = _post_forward(yv, rv, kv, vv, gn_w_, gn_b_, rk_)
        do = dv_ * gv
        outs[4][...] = dv_ * o
        accs[0][...] += _colsum(do * yn)
        accs[1][...] += _colsum(do)
        dyn = do * gn_w_
        outs[0][...] = rstd * (dyn - _headsum(dyn) * (1.0 / HEAD) - yn * (_headsum(dyn * yn) * (1.0 / HEAD)))
        ds = _headsum(do * vv)
        outs[1][...] = ds * kv * rk_
        outs[2][...] = ds * rv * rk_
        outs[3][...] = do * s
        accs[2][...] += _colsum(ds * rv * kv)

    return _row_call("rwkv_post_bwd", body, T, [dout, y, r, kmod, v, gate], [gn_w, gn_b, rk],
                     [_sds((T, W))] * 5, [_sds((1, W))] * 3, tile=RWKV_ROW_TILE)


def rwkv_prep_bwd(p, cols, grads, mu, mul, w0, a0, kk_w, ka_w, w2p, a2p, g2p):
    T = p.shape[0]
    W = w0.shape[1]
    latp = w2p.shape[0]

    def body(i, n, tiled, prev, nxt, params, outs, accs):
        pv = [q[...] for q in params]
        mu_, mul_, w0_, a0_, kk_w_, ka_w_, w2p_, a2p_, g2p_ = pv
        c = _prep_forward(tiled[0][...], _last_row(prev[0], i > 0), tiled[1][...], _last_row(prev[1], i > 0), *pv)
        dr_s, dr_x, ddecay, dk_s, dk_x, dv_s, dv_x, dneg, db, dgate = [t[...] for t in tiled[2:]]
        k, a, kk = c["k"], c["a"], c["kk"]
        dkmod = dk_s + dk_x
        dk = dkmod * (1.0 + (a - 1.0) * ka_w_)
        da = dkmod * k * ka_w_ + db * kk
        accs[0][...] += _colsum(dkmod * k * (a - 1.0))
        dkk = db * a - dneg
        dq = jnp.where(c["nrm"] > L2_EPS, dkk - kk * _headsum(dkk * kk), dkk) / c["den"]
        dk = dk + dq * kk_w_
        accs[1][...] += _colsum(dq * k)
        dxa = da * a * (1.0 - a)
        accs[2][...] += _colsum(dxa)
        accs[4][...] += _dotb(c["zls"], dxa, _TN)
        dzls = _dotb(dxa, a2p_, _NT)
        dxw = (-ddecay * c["decay"] * c["ew"]) * jax.nn.sigmoid(-c["xw"])
        accs[3][...] += _colsum(dxw)
        accs[5][...] += _dotb(c["th"], dxw, _TN)
        dzls = dzls + _dotb(dxw, w2p_, _NT) * (1.0 - c["th"] * c["th"])
        accs[6][...] += _dotb(c["sg"], dgate, _TN)
        dzls = dzls + _dotb(dgate, g2p_, _NT) * c["sg"] * (1.0 - c["sg"])
        outs[0][...] = jnp.concatenate([dr_s + dr_x, dk, dv_s + dv_x], axis=1)
        outs[1][...] = dzls

    return _row_call("rwkv_prep_bwd", body, T, [cols["rkv"], cols["lat"]] + list(grads),
                     [mu, mul, w0, a0, kk_w, ka_w, w2p, a2p, g2p], [_sds((T, 3 * W)), _sds((T, latp))],
                     [_sds((1, W))] * 4 + [_sds((latp, W))] * 3, prev=[cols["rkv"], cols["lat"]], tile=RWKV_ROW_TILE)


def shift_bwd(cols, dzs, dzls, mu, mul):
    T = dzs.shape[0]

    def body(i, n, tiled, prev, nxt, params, outs, accs):
        for j in range(2):
            z, d, m = tiled[j][...], tiled[2 + j][...], params[j][...]
            zprev = _shift_down(z, _last_row(prev[j], i > 0))
            dnext = _shift_up(d, _first_row(nxt[j], i < n - 1))
            outs[j][...] = (d * (1.0 - m) + dnext * m).astype(BF16)
            accs[j][...] += _colsum(d * (zprev - z))

    return _row_call("shift_bwd", body, T, [cols["rkv"], cols["lat"], dzs, dzls], [mu, mul],
                     [_sds(dzs.shape, BF16), _sds(dzls.shape, BF16)], [_sds(mu.shape), _sds(mul.shape)],
                     prev=[cols["rkv"], cols["lat"]], nxt=[dzs, dzls])


def _window_pick(x, windows):
    gid = lax.broadcasted_iota(jnp.int32, x.shape, 1) // (x.shape[1] // len(windows))
    out = windows[-1]
    for g in range(len(windows) - 2, -1, -1):
        out = jnp.where(gid == g, windows[g], out)
    return out


def _pool_counts(t0, rows, width):
    t = (t0 + lax.broadcasted_iota(jnp.int32, (rows, width), 0) + 1).astype(F32)
    return _window_pick(t, [jnp.minimum(t, float(w)) for w in POOL_WINDOWS])


def _pool_mixed(x, before, t0):
    tm, width = x.shape
    xe = jnp.concatenate([before, x], axis=0)
    sums, s, span = [], xe, 1
    for w in POOL_WINDOWS:
        while span < w:
            s = s + pltpu.roll(s, span, 0)
            span *= 2
        sums.append(s[POOL_HALO:, :])
    return _window_pick(x, sums) / _pool_counts(t0, tm, width) - x


def _group_dot(x, w_ref, contract):
    gd = w_ref.shape[-1]
    parts = [_dotb(x[:, g * gd:(g + 1) * gd], w_ref[g], contract) for g in range(w_ref.shape[0])]
    return jnp.concatenate(parts, axis=1)


def pool_fwd(cols, pool_w, pool_scale):
    T, width = cols["pool"][0].shape[0], cols["pool"][1]
    tm = min(ROW_TILE, T)

    def body(i, n, tiled, prev, nxt, params, outs, accs):
        before = jnp.where(i > 0, prev[0][...], 0.0)
        mixed = _pool_mixed(tiled[0][...], before, i * tm)
        outs[0][...] = (_group_dot(mixed, params[0], _NN) * params[1][...]).astype(BF16)

    return _row_call("pool_fwd", body, T, [cols["pool"]], [pool_w, pool_scale], [_sds((T, width), BF16)],
                     prev=[cols["pool"]], halo=POOL_HALO)[0]


def pool_bwd(cols, dout, pool_w, pool_scale):
    T, width = dout.shape
    tm = min(ROW_TILE, T)

    def body(i, n, tiled, prev, nxt, params, outs, accs):
        w_ref, scale = params[0], params[1][...]
        before = jnp.where(i > 0, prev[0][...], 0.0)
        mixed = _pool_mixed(tiled[0][...], before, i * tm)
        dv = tiled[1][...]
        accs[1][...] += _colsum(dv * _group_dot(mixed, w_ref, _NN))
        after = jnp.where(i < n - 1, nxt[0][...], 0.0)
        dys = jnp.concatenate([dv, after], axis=0) * scale
        gd = w_ref.shape[-1]
        for g in range(w_ref.shape[0]):
            accs[0][g] += _dotb(mixed[:, g * gd:(g + 1) * gd], dys[:tm, g * gd:(g + 1) * gd], _TN)
        dmixed = _group_dot(dys, w_ref, _NT)
        u = dmixed / _pool_counts(i * tm, tm + POOL_HALO, width)
        rows = tm + POOL_HALO
        sums, s, span = [], u, 1
        for w in POOL_WINDOWS:
            while span < w:
                s = s + pltpu.roll(s, rows - span, 0)
                span *= 2
            sums.append(s[:tm, :])
        outs[0][...] = (_window_pick(dv, sums) - dmixed[:tm, :]).astype(BF16)

    return _row_call("pool_bwd", body, T, [cols["pool"], dout], [pool_w, pool_scale], [_sds((T, width), BF16)],
                     [_sds(pool_w.shape), _sds((1, width))], prev=[cols["pool"]], nxt=[dout], halo=POOL_HALO)


def _wkv_consts(pairs):
    lane = lax.broadcasted_iota(jnp.int32, (HEAD, LANES), 1)
    sub = lax.broadcasted_iota(jnp.int32, (pairs * HEAD, LANES), 0)
    lane_all = lax.broadcasted_iota(jnp.int32, (pairs * HEAD, LANES), 1)
    i = lax.broadcasted_iota(jnp.int32, (LANES, LANES), 0)
    j = lax.broadcasted_iota(jnp.int32, (LANES, LANES), 1)
    ones = jnp.where((i < HEAD) == (j < HEAD), 1.0, 0.0).astype(BF16)
    diag = jnp.where((lane_all & (HEAD - 1)) == (sub & (HEAD - 1)), 1.0, 0.0).astype(F32)
    return lane < HEAD, diag, ones


def _segsum(p, in_a):
    sa = jnp.sum(jnp.where(in_a, p, 0.0), axis=1, keepdims=True)
    sb = jnp.sum(jnp.where(in_a, 0.0, p), axis=1, keepdims=True)
    return jnp.where(in_a, sa, sb)


def _hi_lo(p):
    hi = lax.bitcast_convert_type(lax.bitcast_convert_type(p, jnp.uint32) & jnp.uint32(0xFFFF0000), F32)
    return hi, p - hi


def _segsum_mxu(p, ones):
    hi, lo = _hi_lo(p)
    return (jnp.dot(hi.astype(BF16), ones, preferred_element_type=F32)
            + jnp.dot(lo.astype(BF16), ones, preferred_element_type=F32))


def _cat(parts, axis):
    return parts[0] if len(parts) == 1 else jnp.concatenate(parts, axis=axis)


def _tile_rows(row, pairs):
    return _cat([jnp.broadcast_to(row[:, p * LANES:(p + 1) * LANES], (HEAD, LANES)) for p in range(pairs)], 0)


def _spread(row, pairs, diag16, ones):
    hi, lo = _hi_lo(row)
    return (jnp.dot(_tile_rows(hi.astype(BF16), pairs) * diag16, ones, preferred_element_type=F32)
            + jnp.dot(_tile_rows(lo.astype(BF16), pairs) * diag16, ones, preferred_element_type=F32))


def _pair_colsums(x, pairs):
    return _cat([_colsum(x[p * HEAD:(p + 1) * HEAD]) for p in range(pairs)], 1)


def _spread_split(row, pairs, in_a, diag, diag16, ones):
    n_mxu = min(WKV_MXU_PAIRS, pairs)
    parts = [_spread(row[:, :n_mxu * LANES], n_mxu, diag16[:n_mxu * HEAD], ones)]
    parts += [_segsum(row[:, p * LANES:(p + 1) * LANES] * diag[:HEAD], in_a) for p in range(n_mxu, pairs)]
    return _cat(parts, 0)


def _segsum_split(x, pairs, in_a, ones):
    n_mxu = min(WKV_MXU_PAIRS, pairs)
    parts = [_segsum_mxu(x[:n_mxu * HEAD], ones)]
    parts += [_segsum(x[p * HEAD:(p + 1) * HEAD], in_a) for p in range(n_mxu, pairs)]
    return _cat(parts, 0)


def wkv_fwd(r, w, k, v, a, b, carry=None):
    T, W = r.shape
    P = W // LANES
    PB = min(WKV_PAIRS, P)
    chunk = min(WKV_CHUNK, T)
    NC = T // chunk
    R = PB * HEAD
    ahead = min(WKV_UNROLL, chunk // 2)
    c_in, c_out = (len(carry.inputs), len(carry.out_shapes)) if carry else (0, 0)

    def body(*refs):
        r_ref, w_ref, k_ref, v_ref, a_ref, b_ref = refs[:6]
        y_ref, st_ref, sa_ref = refs[6 + c_in:9 + c_in]
        vt_ref, s_ref = refs[9 + c_in + c_out:11 + c_in + c_out]
        carried = refs[6:6 + c_in], refs[9 + c_in:9 + c_in + c_out], refs[11 + c_in + c_out:]
        g, c = pl.program_id(0), pl.program_id(1)

        if carry:
            @pl.when((g == 0) & (c == 0))
            def _():
                carry.start(*carried)

        @pl.when(c == 0)
        def _():
            s_ref[...] = jnp.zeros_like(s_ref)

        in_a, diag, ones = _wkv_consts(PB)
        diag16 = diag.astype(BF16)

        def step(t, _):
            rows = [ref[pl.ds(t, 1), :] for ref in (w_ref, k_ref, a_ref, b_ref)]
            for p in range(PB):
                wt, kt, at, bt = [x[:, p * LANES:(p + 1) * LANES] for x in rows]
                rs = pl.ds(p * HEAD, HEAD)
                S = s_ref[rs]
                sa = _segsum(S * at, in_a)
                sa_ref[t, rs] = sa
                S = S * wt + sa * bt + vt_ref[t, rs] * kt
                st_ref[t, rs] = S
                s_ref[rs] = S
            return 0

        def spread_step(t, _):
            vt_ref[t + ahead] = _spread(v_ref[pl.ds(t + ahead, 1), :], PB, diag16, ones)
            return step(t, 0)

        for t in range(ahead):
            vt_ref[t] = _spread_split(v_ref[t:t + 1, :], PB, in_a, diag, diag16, ones)
        lax.fori_loop(0, chunk - ahead, spread_step, 0, unroll=WKV_UNROLL)
        lax.fori_loop(chunk - ahead, chunk, step, 0, unroll=WKV_UNROLL)

        def readout(t, _):
            yt = _segsum_split(st_ref[t] * _tile_rows(r_ref[pl.ds(t, 1), :], PB), PB, in_a, ones) * diag
            y_ref[pl.ds(t, 1), :] = _pair_colsums(yt, PB)
            return 0

        lax.fori_loop(0, chunk, readout, 0, unroll=WKV_UNROLL)

        if carry:
            @pl.when((g == P // PB - 1) & (c == NC - 1))
            def _():
                carry.finish(*carried)

    spec = pl.BlockSpec((chunk, PB * LANES), lambda g, c: (c, g))
    tiles = pl.BlockSpec((chunk, R, LANES), lambda g, c: (c, g, 0))
    res = _pallas(
        body, name="wkv_fwd", grid=(P // PB, NC), in_specs=[spec] * 6 + [ANY] * c_in,
        out_specs=[spec, tiles, tiles] + [ANY] * c_out,
        out_shape=[_sds((T, W)), _sds((T, P * HEAD, LANES)), _sds((T, P * HEAD, LANES))]
        + (list(carry.out_shapes) if carry else []),
        scratch_shapes=[pltpu.VMEM((chunk, R, LANES), F32), pltpu.VMEM((R, LANES), F32)]
        + (list(carry.scratch) if carry else []),
        compiler_params=_params("arbitrary", "arbitrary") if carry else _params("parallel", "arbitrary"),
    )(r, w, k, v, a, b, *(carry.inputs if carry else []))
    return (res[:3], res[3:]) if carry else res


def wkv_bwd(r, w, k, v, a, b, dy, st, sa):
    T, W = r.shape
    P = W // LANES
    PB = min(WKV_PAIRS, P)
    chunk = min(WKV_CHUNK, T)
    NC = T // chunk
    R = PB * HEAD
    ahead = min(WKV_UNROLL, chunk // 2)

    def body(r_ref, w_ref, k_ref, v_ref, a_ref, b_ref, dy_ref, st_ref, before_ref, sa_ref,
             dr_ref, dw_ref, dk_ref, dv_ref, da_ref, db_ref, ds_ref, dyt_ref, dst_ref, dsa_ref):
        c = pl.program_id(1)

        @pl.when(c == 0)
        def _():
            ds_ref[...] = jnp.zeros_like(ds_ref)

        in_a, diag, ones = _wkv_consts(PB)
        diag16 = diag.astype(BF16)

        def bstep(n, _):
            t = chunk - 1 - n
            rows = [ref[pl.ds(t, 1), :] for ref in (r_ref, w_ref, a_ref, b_ref)]
            for p in range(PB):
                rt, wt, at, bt = [x[:, p * LANES:(p + 1) * LANES] for x in rows]
                rs = pl.ds(p * HEAD, HEAD)
                dS = ds_ref[rs] + dyt_ref[t, rs] * rt
                dst_ref[t, rs] = dS
                dsa = _segsum(dS * bt, in_a)
                dsa_ref[t, rs] = dsa
                ds_ref[rs] = dS * wt + dsa * at
            return 0

        def spread_bstep(n, _):
            t = chunk - 1 - ahead - n
            dyt_ref[t] = _spread(dy_ref[pl.ds(t, 1), :], PB, diag16, ones)
            return bstep(n, 0)

        for t in range(chunk - ahead, chunk):
            dyt_ref[t] = _spread_split(dy_ref[t:t + 1, :], PB, in_a, diag, diag16, ones)
        lax.fori_loop(0, chunk - ahead, spread_bstep, 0, unroll=WKV_UNROLL)
        lax.fori_loop(chunk - ahead, chunk, bstep, 0, unroll=WKV_UNROLL)

        def collect(t, _):
            sn, dS, dsa = st_ref[t], dst_ref[t], dsa_ref[t]
            sp = st_ref[jnp.maximum(t - 1, 0)]
            dvt = _segsum_split(dS * _tile_rows(k_ref[pl.ds(t, 1), :], PB), PB, in_a, ones) * diag
            vt = _spread_split(v_ref[pl.ds(t, 1), :], PB, in_a, diag, diag16, ones)
            for ref, val in ((dr_ref, sn * dyt_ref[t]), (dw_ref, dS * sp), (dk_ref, dS * vt), (dv_ref, dvt),
                             (da_ref, sp * dsa), (db_ref, dS * sa_ref[t])):
                ref[pl.ds(t, 1), :] = _pair_colsums(val, PB)
            return 0

        lax.fori_loop(0, chunk, collect, 0, unroll=WKV_UNROLL)
        first = jnp.where(c == NC - 1, 0.0, before_ref[0])
        dw_ref[0:1, :] = _pair_colsums(dst_ref[0] * first, PB)
        da_ref[0:1, :] = _pair_colsums(first * dsa_ref[0], PB)

    spec = pl.BlockSpec((chunk, PB * LANES), lambda g, c: (NC - 1 - c, g))
    tiles = pl.BlockSpec((chunk, R, LANES), lambda g, c: (NC - 1 - c, g, 0))
    before = pl.BlockSpec((1, R, LANES), lambda g, c: (jnp.maximum((NC - 1 - c) * chunk - 1, 0), g, 0))

    def scratch(n):
        return pltpu.VMEM((n, R, LANES), F32)

    return _pallas(
        body, name="wkv_bwd", grid=(P // PB, NC), in_specs=[spec] * 7 + [tiles, before, tiles],
        out_specs=[spec] * 6, out_shape=[_sds((T, W))] * 6,
        scratch_shapes=[pltpu.VMEM((R, LANES), F32), scratch(chunk), scratch(chunk), scratch(chunk)],
        compiler_params=_params("parallel", "arbitrary"),
    )(r, w, k, v, a, b, dy, st, st, sa)


def _position():
    return lax.axis_index("x"), lax.axis_index("y"), lax.axis_index("c")


def _other_chips(x, y):
    return [(1 - x, y), (x, 1 - y), (1 - x, 1 - y)]


class _Carry:
    def __init__(self, inputs, out_shapes, scratch, start, finish):
        self.inputs, self.out_shapes, self.scratch, self.start, self.finish = inputs, out_shapes, scratch, start, finish


def _run_carry(name, carry):
    n_in, n_out = len(carry.inputs), len(carry.out_shapes)

    def body(*refs):
        parts = refs[:n_in], refs[n_in:n_in + n_out], refs[n_in + n_out:]
        carry.start(*parts)
        carry.finish(*parts)

    return _pallas(body, name=name, in_specs=[ANY] * n_in, out_specs=[ANY] * n_out, out_shape=list(carry.out_shapes),
                   scratch_shapes=list(carry.scratch))(*carry.inputs)


def gather_carry(shards):
    n = len(shards)

    def plan(x_refs, out_refs, sems):
        send_sems, recv_sems, local_sems = sems
        x, y, c = _position()
        me, sibling = (x, y, c), (x, y, 1 - c)
        chips = _other_chips(x, y)

        def slot(ref, pos):
            return ref.at[4 * pos[0] + 2 * pos[1] + pos[2]]

        def copy(t, j, block, to, src=None):
            dst = slot(out_refs[t], block)
            return pltpu.make_async_remote_copy(
                src_ref=dst if src is None else src, dst_ref=dst, send_sem=send_sems.at[t, j],
                recv_sem=recv_sems.at[t, j], device_id=to, device_id_type=MESH)

        mine = [pltpu.make_async_copy(x_refs[t], slot(out_refs[t], me), local_sems.at[t]) for t in range(n)]
        first = []
        for t in range(n):
            first.append(copy(t, 0, me, sibling, src=x_refs[t]))
            first += [copy(t, 1 + j, me, (*chip, c), src=x_refs[t]) for j, chip in enumerate(chips)]
        return c, me, sibling, chips, copy, mine, first

    def start(x_refs, out_refs, sems):
        _, _, _, _, _, mine, first = plan(x_refs, out_refs, sems)
        for cp in mine + first:
            cp.start()

    def finish(x_refs, out_refs, sems):
        c, me, sibling, chips, copy, mine, first = plan(x_refs, out_refs, sems)
        passed = []
        for t in range(n):
            for j, chip in enumerate(chips):
                copy(t, 1 + j, (*chip, c), me).wait_recv()
                fwd = copy(t, 4 + j, (*chip, c), sibling)
                fwd.start()
                passed.append(fwd)
        for t in range(n):
            copy(t, 0, sibling, me).wait_recv()
            for j, chip in enumerate(chips):
                copy(t, 4 + j, (*chip, 1 - c), me).wait_recv()
        for cp in first + passed:
            cp.wait_send()
        for cp in mine:
            cp.wait()

    return _Carry(list(shards), [_sds((N_DEV,) + s.shape, s.dtype) for s in shards],
                  [pltpu.SemaphoreType.DMA((n, 7)), pltpu.SemaphoreType.DMA((n, 7)), pltpu.SemaphoreType.DMA((n,))],
                  start, finish)


def all_gather(shards):
    return _run_carry("all_gather", gather_carry(shards))


def exchange_sibling(parts):
    n = len(parts)

    def body(*refs):
        p_refs, out_refs = refs[:n], refs[n:2 * n]
        send_sems, recv_sems = refs[2 * n:]
        x, y, c = _position()
        copies = []
        for t in range(n):
            for q in range(N_CHIP):
                cp = pltpu.make_async_remote_copy(
                    src_ref=p_refs[t].at[q, 1 - c], dst_ref=out_refs[t].at[q], send_sem=send_sems.at[t, q],
                    recv_sem=recv_sems.at[t, q], device_id=(x, y, 1 - c), device_id_type=MESH)
                cp.start()
                copies.append(cp)
        for cp in copies:
            cp.wait()

    return _pallas(
        body, name="exchange_sibling", in_specs=[ANY] * n, out_specs=[ANY] * n,
        out_shape=[_sds((N_CHIP,) + p.shape[2:], p.dtype) for p in parts],
        scratch_shapes=[pltpu.SemaphoreType.DMA((n, N_CHIP)), pltpu.SemaphoreType.DMA((n, N_CHIP))],
    )(*parts)


def chips_carry(parts):
    n = len(parts)

    def plan(p_refs, out_refs, sems):
        send_sems, recv_sems, local_sems = sems
        x, y, c = _position()
        local = [pltpu.make_async_copy(p_refs[t].at[2 * x + y], out_refs[t].at[3], local_sems.at[t]) for t in range(n)]
        remote = [pltpu.make_async_remote_copy(
            src_ref=p_refs[t].at[2 * cx + cy], dst_ref=out_refs[t].at[j], send_sem=send_sems.at[t, j],
            recv_sem=recv_sems.at[t, j], device_id=(cx, cy, c), device_id_type=MESH)
            for t in range(n) for j, (cx, cy) in enumerate(_other_chips(x, y))]
        return local, remote

    def start(p_refs, out_refs, sems):
        local, remote = plan(p_refs, out_refs, sems)
        for cp in local + remote:
            cp.start()

    def finish(p_refs, out_refs, sems):
        local, remote = plan(p_refs, out_refs, sems)
        for cp in remote + local:
            cp.wait()

    return _Carry(list(parts), [_sds(p.shape, p.dtype) for p in parts],
                  [pltpu.SemaphoreType.DMA((n, 3)), pltpu.SemaphoreType.DMA((n, 3)), pltpu.SemaphoreType.DMA((n,))],
                  start, finish)


def _flat_tile(rows, cols):
    tr = rows
    for d in range(16, min(rows, 512) + 1, 16):
        if rows % d == 0 and d * cols * 4 <= 2 * 1024 * 1024:
            tr = d
    return tr


def pair_add(part, recv):
    _, _, R, C = part.shape
    tr = _flat_tile(R, C)
    core = jnp.reshape(lax.axis_index("c"), (1,)).astype(jnp.int32)

    def body(core_ref, p_ref, r_ref, o_ref):
        o_ref[...] = (p_ref[...] + r_ref[...]).astype(BF16)

    grid_spec = pltpu.PrefetchScalarGridSpec(
        num_scalar_prefetch=1, grid=(N_CHIP, R // tr),
        in_specs=[pl.BlockSpec((None, None, tr, C), lambda q, i, core_ref: (q, core_ref[0], i, 0)),
                  pl.BlockSpec((None, tr, C), lambda q, i, core_ref: (q, i, 0))],
        out_specs=pl.BlockSpec((None, tr, C), lambda q, i, core_ref: (q, i, 0)))
    return _pallas(body, name="pair_add", grid_spec=grid_spec, out_shape=_sds((N_CHIP, R, C), BF16),
                   compiler_params=_params("parallel", "parallel"))(core, part, recv)


def adamw(w, m, v, slabs):
    R, C = w.shape
    tr = _flat_tile(R, C)
    n = slabs.shape[0]

    def body(w_ref, m_ref, v_ref, s_ref, g_ref, d_ref, nm_ref, nv_ref):
        g = s_ref[0].astype(F32)
        for j in range(1, n):
            g = g + s_ref[j].astype(F32)
        m2 = ADAM_B1 * m_ref[...] + (1.0 - ADAM_B1) * g
        v2 = ADAM_B2 * v_ref[...] + (1.0 - ADAM_B2) * (g * g)
        m_hat = m2 / (1.0 - ADAM_B1 ** ADAM_STEP)
        v_hat = v2 / (1.0 - ADAM_B2 ** ADAM_STEP)
        g_ref[...] = g
        d_ref[...] = -ADAM_LR * (m_hat / (jnp.sqrt(v_hat) + ADAM_EPS) + ADAM_WD * w_ref[...])
        nm_ref[...] = m2
        nv_ref[...] = v2

    spec = pl.BlockSpec((tr, C), lambda i: (i, 0))
    return _pallas(body, name="adamw", grid=(R // tr,),
                   in_specs=[spec] * 3 + [pl.BlockSpec((n, tr, C), lambda i: (0, i, 0))], out_specs=[spec] * 4,
                   out_shape=[_sds((R, C))] * 4, compiler_params=_params("parallel"))(w, m, v, slabs)


def _unshard_cols(g):
    return jnp.transpose(g, (1, 0, 2)).reshape(g.shape[1], -1)


def _unshard_rows(g):
    return g.reshape(-1, g.shape[2])


def _shard_cols(full):
    R, C = full.shape
    return jnp.transpose(full.reshape(R, N_DEV, C // N_DEV), (1, 0, 2)).reshape(N_CHIP, 2, R, C // N_DEV)


def _shard_rows(full):
    R, C = full.shape
    return full.reshape(N_CHIP, 2, R // N_DEV, C)


WEIGHTS = ['ln_ffn1_pre', 'ln_ffn1_post', 'ffn1_gate', 'ffn1_up', 'ffn1_down', 'ln_mix_pre', 'ln_mix_post', 'w_in',
           'rwkv_mu', 'rwkv_w0', 'rwkv_w2', 'rwkv_a0', 'rwkv_a2', 'rwkv_g2', 'rwkv_k_k', 'rwkv_k_a', 'rwkv_r_k',
           'rwkv_gn_w', 'rwkv_gn_b', 'w_proj_a', 'pool_w', 'pool_scale', 'w_proj_b', 'w_out', 'ln_ffn2_pre',
           'ln_ffn2_post', 'ffn2_gate', 'ffn2_up', 'ffn2_down']
COL_SHARDED = ['ffn1_gate', 'ffn1_up', 'ffn2_gate', 'ffn2_up', 'w_in', 'rwkv_w2', 'rwkv_a2', 'rwkv_g2', 'w_proj_a',
               'w_proj_b']
ROW_SHARDED = ['ffn1_down', 'ffn2_down', 'w_out', 'pool_w']
TRANSPOSED = ['ffn1_gate', 'ffn1_up', 'ffn2_gate', 'ffn2_up', 'w_in', 'w_proj_a', 'w_proj_b']
SHARDED = COL_SHARDED + ROW_SHARDED
REPLICATED = [n for n in WEIGHTS if n not in SHARDED]


def _step(args):
    wts = {n: args[n] if args[n].ndim == 2 else args[n][0] for n in WEIGHTS}
    x, target = args["x"][0], args["loss_target"][0]
    T, D = x.shape
    W = wts["rwkv_w0"].shape[1]
    PW = wts["pool_scale"].shape[1]
    LW, LA, LG = wts["rwkv_w2"].shape[0], wts["rwkv_a2"].shape[0], wts["rwkv_g2"].shape[0]
    lat = LW + LA + LG
    latp = _round_up(lat, LAT_ALIGN)
    rc = 3 * W + lat
    base = 3 * W + PW + 2 * D
    n_groups, gshard, gd = wts["pool_w"].shape

    pool_w_shard = wts["pool_w"].reshape(n_groups * gshard, gd)
    shards = {n: (pool_w_shard if n == "pool_w" else wts[n]).astype(BF16) for n in SHARDED}
    shards.update({n: shards[n].T for n in TRANSPOSED})
    full = {}

    def fetch(names):
        return gather_carry([shards[n] for n in names])

    def arrived(names, got):
        for n, g in zip(names, got):
            if n == "pool_w":
                full[n] = jnp.transpose(g.reshape(N_DEV, n_groups, gshard, gd), (1, 0, 2, 3)).reshape(n_groups, gd, gd)
            elif n in COL_SHARDED and n not in TRANSPOSED:
                full[n] = _unshard_cols(g)
            else:
                full[n] = _unshard_rows(g)

    arrived(["ffn1_gate", "ffn1_up"], all_gather([shards["ffn1_gate"], shards["ffn1_up"]]))
    mu = wts["rwkv_mu"]
    mu_rkv = mu[:, :3 * W]
    mu_lat = jnp.concatenate([mu[:, 3 * W:], jnp.zeros((1, latp - lat), F32)], axis=1)
    rk = wts["rwkv_r_k"].reshape(1, W)

    n1 = rms_pre(x, wts["ln_ffn1_pre"])
    (g1, u1, act1), got = _mm("ffn1_up", [n1], [full["ffn1_gate"], full["ffn1_up"]], "nt", [BF16] * 3,
                              epilogue=_swiglu_fwd_epi, carry=fetch(["ffn1_down"]))
    arrived(["ffn1_down"], got)
    (f1,), got = _mm("ffn1_down", [act1], [full["ffn1_down"]], "nn", [F32], tm=512, carry=fetch(["w_in"]))
    arrived(["w_in"], got)
    w_in = full["w_in"]
    w_in_p = jnp.concatenate([w_in[:3 * W], w_in[rc:], w_in[3 * W:rc], jnp.zeros((latp - lat, D), BF16)], axis=0)
    h1, nm = post_pre(x, f1, wts["ln_ffn1_post"], wts["ln_mix_pre"], MACARON)
    mixer = ["rwkv_w2", "rwkv_a2", "rwkv_g2", "w_proj_a", "w_proj_b", "pool_w", "w_out"]
    (p,), got = _mm("in_proj", [nm], [w_in_p], "nt", [F32], carry=fetch(mixer))
    arrived(mixer, got)
    pool_w = full["pool_w"]

    def pad_rows(m, at):
        return jnp.zeros((latp, W), BF16).at[at:at + m.shape[0]].set(m)

    w2p, a2p, g2p = pad_rows(full["rwkv_w2"], 0), pad_rows(full["rwkv_a2"], LW), pad_rows(full["rwkv_g2"], LW + LA)
    small = [mu_rkv, mu_lat, wts["rwkv_w0"], wts["rwkv_a0"], wts["rwkv_k_k"], wts["rwkv_k_a"], w2p, a2p, g2p]
    cols = {"rkv": (p, 3 * W, 0), "pool": (p, PW, 3 * W // PW), "lat": (p, latp, base // latp)}
    r, decay, kmod, v, aneg, bpos, gate = rwkv_prep(p, cols, *small)
    (y, states, sdota), got = wkv_fwd(r, decay, kmod, v, aneg, bpos, carry=fetch(["ffn2_gate", "ffn2_up"]))
    arrived(["ffn2_gate", "ffn2_up"], got)
    ya_in = rwkv_post(y, r, kmod, v, gate, wts["rwkv_gn_w"], wts["rwkv_gn_b"], rk)
    yb_in = pool_fwd(cols, pool_w, wts["pool_scale"])
    gates = [(p, 3 * W + PW), (p, 3 * W + PW + D)]
    m, ya, yb = _mm("mix", [ya_in, yb_in], [full["w_proj_a"], full["w_proj_b"]], "nt", [BF16] * 3,
                    extras=gates, epilogue=_mix_fwd_epi)
    mx = _mm("out_proj", [m], [full["w_out"]], "nn", [F32])[0]
    h2, n2 = post_pre(h1, mx, wts["ln_mix_post"], wts["ln_ffn2_pre"], 1.0)
    (g2_, u2, act2), got = _mm("ffn2_up", [n2], [full["ffn2_gate"], full["ffn2_up"]], "nt", [BF16] * 3,
                               epilogue=_swiglu_fwd_epi, carry=fetch(["ffn2_down"]))
    arrived(["ffn2_down"], got)
    f2 = _mm("ffn2_down", [act2], [full["ffn2_down"]], "nn", [F32], tm=512)[0]

    grads, slabs = {}, {}

    def pair_sums(names):
        parts = []
        for n in names:
            if n == "pool_w":
                parts.append(jnp.transpose(grads[n].reshape(n_groups, N_DEV, gshard, gd), (1, 0, 2, 3)).reshape(
                    N_CHIP, 2, n_groups * gshard, gd))
            elif n in COL_SHARDED and n not in TRANSPOSED:
                parts.append(_shard_cols(grads[n]))
            else:
                parts.append(_shard_rows(grads[n]))
        return [pair_add(part, rcv) for part, rcv in zip(parts, exchange_sibling(parts))]

    def landed(names, got):
        slabs.update(zip(names, got))

    dh3, df2, loss_part, grads["ln_ffn2_post"] = loss_post_bwd(h2, f2, wts["ln_ffn2_post"], target, MACARON)
    dg2, du2 = _mm("ffn2_dact", [df2], [full["ffn2_down"]], "nt", [BF16] * 2, extras=[(g2_, 0), (u2, 0)],
                   epilogue=_swiglu_bwd_epi)
    grads["ffn2_down"] = _mm("ffn2_ddown", [act2], [df2], "tn", [F32], tm=512, tn=1024)[0]
    grads["ffn2_gate"], grads["ffn2_up"] = _mm("ffn2_dup", [dg2, du2], [n2], "tn", [F32] * 2, tm=512)
    dn2 = _mm("ffn2_dn_gate", [dg2], [full["ffn2_gate"]], "nn", [F32], tm=512)[0]
    dn2 = _mm("ffn2_dn", [du2], [full["ffn2_up"]], "nn", [F32], tm=512, extras=[(dn2, 0)], epilogue=_add_epi)[0]
    sums2 = pair_sums(["ffn2_down", "ffn2_gate", "ffn2_up"])
    dh2, grads["ln_ffn2_pre"] = pre_bwd(dn2, h2, wts["ln_ffn2_pre"], dh3)
    dmx, grads["ln_mix_post"] = post_bwd(dh2, mx, wts["ln_mix_post"], 1.0)
    (dya, dyb, dga, dgb), got = _mm("dmix", [dmx], [full["w_out"]], "nt", [BF16] * 4, extras=gates + [(ya, 0), (yb, 0)],
                                    epilogue=_mix_bwd_epi, carry=chips_carry(sums2[:1]))
    landed(["ffn2_down"], got)
    grads["w_out"] = _mm("dw_out", [m], [dmx], "tn", [F32])[0]
    dya_in = _mm("dproj_a", [dya], [full["w_proj_a"]], "nn", [F32])[0]
    dyb_in = _mm("dproj_b", [dyb], [full["w_proj_b"]], "nn", [F32])[0]
    grads["w_proj_a"] = _mm("dw_proj_a", [dya], [ya_in], "tn", [F32])[0]
    grads["w_proj_b"] = _mm("dw_proj_b", [dyb], [yb_in], "tn", [F32])[0]
    dz_pool, grads["pool_w"], grads["pool_scale"] = pool_bwd(cols, dyb_in, pool_w, wts["pool_scale"])
    dy, dr_x, dk_x, dv_x, dgate, grads["rwkv_gn_w"], grads["rwkv_gn_b"], drk = rwkv_post_bwd(
        dya_in, y, r, kmod, v, gate, wts["rwkv_gn_w"], wts["rwkv_gn_b"], rk)
    grads["rwkv_r_k"] = drk.reshape(wts["rwkv_r_k"].shape)
    dr_s, ddecay, dk_s, dv_s, dneg, dbpos = wkv_bwd(r, decay, kmod, v, aneg, bpos, dy, states, sdota)
    (dzs, dzls, grads["rwkv_k_a"], grads["rwkv_k_k"], grads["rwkv_a0"], grads["rwkv_w0"], da2p, dw2p, dg2p) = rwkv_prep_bwd(
        p, cols, [dr_s, dr_x, ddecay, dk_s, dk_x, dv_s, dv_x, dneg, dbpos, dgate], *small)
    grads["rwkv_w2"], grads["rwkv_a2"], grads["rwkv_g2"] = dw2p[:LW], da2p[LW:LW + LA], dg2p[LW + LA:lat]
    dz_rkv, dz_lat, dmu_rkv, dmu_lat = shift_bwd(cols, dzs, dzls, mu_rkv, mu_lat)
    grads["rwkv_mu"] = jnp.concatenate([dmu_rkv, dmu_lat[:, :lat]], axis=1)
    dp = jnp.concatenate([dz_rkv, dz_pool, dga, dgb, dz_lat], axis=1)
    (dnm,), got = _mm("din_proj", [dp], [w_in_p], "nn", [F32], tm=512, carry=chips_carry(sums2[1:2]))
    landed(["ffn2_gate"], got)
    (dw_in_p,), got = _mm("dw_in", [dp], [nm], "tn", [F32], tm=512, tn=1024, carry=chips_carry(sums2[2:]))
    landed(["ffn2_up"], got)
    grads["w_in"] = jnp.concatenate([dw_in_p[:3 * W], dw_in_p[base:base + lat], dw_in_p[3 * W:base]], axis=0)
    sums_mix = pair_sums(["w_in"] + mixer)
    dh1, grads["ln_mix_pre"] = pre_bwd(dnm, h1, wts["ln_mix_pre"], dh2)
    df1, grads["ln_ffn1_post"] = post_bwd(dh1, f1, wts["ln_ffn1_post"], MACARON)
    (dg1, du1), got = _mm("ffn1_dact", [df1], [full["ffn1_down"]], "nt", [BF16] * 2, extras=[(g1, 0), (u1, 0)],
                          epilogue=_swiglu_bwd_epi, carry=chips_carry(sums_mix[:1]))
    landed(["w_in"], got)
    (grads["ffn1_down"],), got = _mm("ffn1_ddown", [act1], [df1], "tn", [F32], tm=512, tn=1024,
                                     carry=chips_carry(sums_mix[1:]))
    landed(mixer, got)
    (grads["ffn1_gate"], grads["ffn1_up"]), got = _mm("ffn1_dup", [dg1, du1], [n1], "tn", [F32] * 2, tm=512,
                                                      carry=chips_carry(pair_sums(["ffn1_down"])))
    landed(["ffn1_down"], got)
    sums1 = pair_sums(["ffn1_gate", "ffn1_up"])
    (dn1,), got = _mm("ffn1_dn_gate", [dg1], [full["ffn1_gate"]], "nn", [F32], tm=512, carry=chips_carry(sums1[:1]))
    landed(["ffn1_gate"], got)
    (dn1,), got = _mm("ffn1_dn", [du1], [full["ffn1_up"]], "nn", [F32], tm=512, extras=[(dn1, 0)], epilogue=_add_epi,
                      carry=chips_carry(sums1[1:]))
    landed(["ffn1_up"], got)
    grad_x, grads["ln_ffn1_pre"] = pre_bwd(dn1, x, wts["ln_ffn1_pre"], dh1)

    flat = jnp.concatenate([grads[n].reshape(-1) for n in REPLICATED])
    n_small = flat.shape[0]
    rows = _round_up(n_small, 8 * LANES) // LANES
    flat = jnp.concatenate([flat, jnp.zeros((rows * LANES - n_small,), F32)]).reshape(rows, LANES)
    small_slabs = all_gather([flat])[0]

    def packed(prefix):
        vals = jnp.concatenate([args[prefix + n].reshape(-1) for n in REPLICATED])
        return jnp.concatenate([vals, jnp.ones((rows * LANES - n_small,), F32)]).reshape(rows, LANES)

    outs = {}
    small_out = adamw(packed(""), packed("m_"), packed("v_"), small_slabs)
    offset = 0
    for n in REPLICATED:
        size = args[n].size
        outs[n] = [o.reshape(-1)[offset:offset + size].reshape(args[n].shape) for o in small_out]
        offset += size
    for n in SHARDED:
        slab = jnp.swapaxes(slabs[n], 1, 2) if n in TRANSPOSED else slabs[n]
        shard2d = slab.shape[1:]
        res = adamw(*[args[pre + n].reshape(shard2d) for pre in ("", "m_", "v_")], slab)
        outs[n] = [o.reshape(args[n].shape) for o in res]

    loss = lax.psum(loss_part[0, 0], ("x", "y", "c"))
    return (loss, grad_x[None], *[outs[n][0] for n in WEIGHTS], *[outs[n][1] for n in WEIGHTS],
            *[outs[n][2] for n in WEIGHTS], *[outs[n][3] for n in WEIGHTS])


ARG_NAMES = ["x"] + WEIGHTS + ["loss_target"] + ["m_" + n for n in WEIGHTS] + ["v_" + n for n in WEIGHTS]


def kernel(x, ln_ffn1_pre, ln_ffn1_post, ffn1_gate, ffn1_up, ffn1_down, ln_mix_pre, ln_mix_post, w_in, rwkv_mu, rwkv_w0,
           rwkv_w2, rwkv_a0, rwkv_a2, rwkv_g2, rwkv_k_k, rwkv_k_a, rwkv_r_k, rwkv_gn_w, rwkv_gn_b, w_proj_a, pool_w,
           pool_scale, w_proj_b, w_out, ln_ffn2_pre, ln_ffn2_post, ffn2_gate, ffn2_up, ffn2_down, loss_target,
           m_ln_ffn1_pre, m_ln_ffn1_post, m_ffn1_gate, m_ffn1_up, m_ffn1_down, m_ln_mix_pre, m_ln_mix_post, m_w_in,
           m_rwkv_mu, m_rwkv_w0, m_rwkv_w2, m_rwkv_a0, m_rwkv_a2, m_rwkv_g2, m_rwkv_k_k, m_rwkv_k_a, m_rwkv_r_k,
           m_rwkv_gn_w, m_rwkv_gn_b, m_w_proj_a, m_pool_w, m_pool_scale, m_w_proj_b, m_w_out, m_ln_ffn2_pre,
           m_ln_ffn2_post, m_ffn2_gate, m_ffn2_up, m_ffn2_down, v_ln_ffn1_pre, v_ln_ffn1_post, v_ffn1_gate, v_ffn1_up,
           v_ffn1_down, v_ln_mix_pre, v_ln_mix_post, v_w_in, v_rwkv_mu, v_rwkv_w0, v_rwkv_w2, v_rwkv_a0, v_rwkv_a2,
           v_rwkv_g2, v_rwkv_k_k, v_rwkv_k_a, v_rwkv_r_k, v_rwkv_gn_w, v_rwkv_gn_b, v_w_proj_a, v_pool_w, v_pool_scale,
           v_w_proj_b, v_w_out, v_ln_ffn2_pre, v_ln_ffn2_post, v_ffn2_gate, v_ffn2_up, v_ffn2_down):
    given = locals()
    return _step({n: given[n] for n in ARG_NAMES})
```

```python
import jax
import jax.numpy as jnp
from jax import lax
from jax.experimental import pallas as pl
from jax.experimental.pallas import tpu as pltpu

F32, BF16 = jnp.float32, jnp.bfloat16
N_DEV = 8
N_CHIP = 4
HEAD = 64
LANES = 2 * HEAD
NORM_EPS, GN_EPS, L2_EPS = 1e-6, 64e-5, 1e-12
POOL_WINDOWS = (2, 4, 8, 16)
POOL_HALO = 16
MACARON = 0.5
ADAM_LR, ADAM_B1, ADAM_B2, ADAM_EPS, ADAM_WD, ADAM_STEP = 0.001, 0.9, 0.999, 1e-08, 0.01, 10
VMEM_LIMIT = 48 * 1024 * 1024
MM_VMEM_BUDGET = 40 * 1024 * 1024
ROW_TILE = 256
RWKV_ROW_TILE = 128
LAT_ALIGN = 512
WKV_CHUNK, WKV_PAIRS = 16, 8
WKV_UNROLL = 4
WKV_MXU_PAIRS = 5
MESH = pl.DeviceIdType.MESH


def _pallas(body, **kw):
    return pl.pallas_call(body, **kw)


def _params(*sem):
    return pltpu.CompilerParams(dimension_semantics=sem, vmem_limit_bytes=VMEM_LIMIT)


def _tile(n, target, align=128):
    best = None
    for d in range(align, min(n, target) + 1, align):
        if n % d == 0:
            best = d
    return best if best is not None else n


def _round_up(n, m):
    return (n + m - 1) // m * m


ANY = pl.BlockSpec(memory_space=pl.ANY)


def _mm(name, a_list, b_list, mode, out_dtypes, *, sum_pairs=False, extras=(), epilogue=None, tm=1024, tn=512,
        carry=None):
    n_a, n_b = len(a_list), len(b_list)
    n_prod = max(n_a, n_b)
    assert n_a in (1, n_prod) and n_b in (1, n_prod)
    a0, b0 = a_list[0], b_list[0]
    if mode == "nn":
        (M, K), N = a0.shape, b0.shape[1]
    elif mode == "nt":
        (M, K), N = a0.shape, b0.shape[0]
    else:
        (K, M), N = a0.shape, b0.shape[1]
    tm, tn = _tile(M, tm), _tile(N, tn)
    n_acc = 1 if sum_pairs else n_prod
    n_ex = len(extras)

    def planned(tk):
        operands = 2 * 2 * tk * (n_a * tm + n_b * tn)
        tiles = 2 * tm * tn * (sum(e.dtype.itemsize for e, _ in extras) + sum(jnp.dtype(d).itemsize for d in out_dtypes))
        return operands + tiles + 4 * tm * tn * (n_acc + len(out_dtypes))

    tk = max([d for d in range(128, K + 1, 128) if K % d == 0 and planned(d) <= MM_VMEM_BUDGET] or [_tile(K, 512)])
    nk = K // tk
    if mode == "tn":
        a_spec = pl.BlockSpec((tk, tm), lambda i, j, k: (k, i))
    else:
        a_spec = pl.BlockSpec((tm, tk), lambda i, j, k: (i, k))
    if mode == "nt":
        b_spec = pl.BlockSpec((tn, tk), lambda i, j, k: (j, k))
    else:
        b_spec = pl.BlockSpec((tk, tn), lambda i, j, k: (k, j))
    contract = {"nn": ((1,), (0,)), "nt": ((1,), (1,)), "tn": ((0,), (0,))}[mode]
    e_specs = []
    for _, col in extras:
        assert col % tn == 0
        e_specs.append(pl.BlockSpec((tm, tn), lambda i, j, k, off=col // tn: (i, j + off)))
    o_spec = pl.BlockSpec((tm, tn), lambda i, j, k: (i, j))

    n_in, n_out, n_scr = n_a + n_b + n_ex, len(out_dtypes), (n_acc if nk > 1 else 0)
    c_in, c_out = (len(carry.inputs), len(carry.out_shapes)) if carry else (0, 0)
    grid = (M // tm, N // tn, nk)

    def body(*refs):
        a_refs, b_refs, e_refs = refs[:n_a], refs[n_a:n_a + n_b], refs[n_a + n_b:n_in]
        o_refs = refs[n_in + c_in:n_in + c_in + n_out]
        acc_refs = refs[n_in + c_in + n_out + c_out:n_in + c_in + n_out + c_out + n_scr]
        carried = (refs[n_in:n_in + c_in], refs[n_in + c_in + n_out:n_in + c_in + n_out + c_out],
                   refs[n_in + c_in + n_out + c_out + n_scr:])
        at = [pl.program_id(d) for d in range(3)]

        if carry:
            @pl.when((at[0] == 0) & (at[1] == 0) & (at[2] == 0))
            def _():
                carry.start(*carried)

        def products():
            a_vals, b_vals = [a[...] for a in a_refs], [b[...] for b in b_refs]
            prods = [lax.dot_general(a_vals[p if n_a > 1 else 0], b_vals[p if n_b > 1 else 0], (contract, ((), ())),
                                     preferred_element_type=F32) for p in range(n_prod)]
            return [sum(prods[1:], prods[0])] if sum_pairs else prods

        def finish(results):
            outs = epilogue(results, [e[...] for e in e_refs]) if epilogue else results
            for o_ref, o in zip(o_refs, outs):
                o_ref[...] = o.astype(o_ref.dtype)

        if nk == 1:
            finish(products())
        else:
            @pl.when(at[2] == 0)
            def _():
                for acc in acc_refs:
                    acc[...] = jnp.zeros_like(acc)

            for acc, prod in zip(acc_refs, products()):
                acc[...] += prod

            @pl.when(at[2] == nk - 1)
            def _():
                finish([acc[...] for acc in acc_refs])

        if carry:
            @pl.when((at[0] == grid[0] - 1) & (at[1] == grid[1] - 1) & (at[2] == grid[2] - 1))
            def _():
                carry.finish(*carried)

    res = _pallas(
        body, name=name, grid=grid,
        in_specs=[a_spec] * n_a + [b_spec] * n_b + e_specs + [ANY] * c_in,
        out_specs=[o_spec] * n_out + [ANY] * c_out,
        out_shape=[jax.ShapeDtypeStruct((M, N), dt) for dt in out_dtypes] + (list(carry.out_shapes) if carry else []),
        scratch_shapes=[pltpu.VMEM((tm, tn), F32)] * n_scr + (list(carry.scratch) if carry else []),
        compiler_params=_params("arbitrary", "arbitrary", "arbitrary") if carry else _params("parallel", "parallel", "arbitrary"),
    )(*a_list, *b_list, *[e for e, _ in extras], *(carry.inputs if carry else []))
    return (res[:n_out], res[n_out:]) if carry else res


def _swiglu_fwd_epi(accs, _):
    g, u = accs
    return [g, u, g * jax.nn.sigmoid(g) * u]


def _swiglu_bwd_epi(accs, ex):
    dact = accs[0]
    g, u = ex[0].astype(F32), ex[1].astype(F32)
    sg = jax.nn.sigmoid(g)
    return [dact * u * (sg * (1.0 + g * (1.0 - sg))), dact * (g * sg)]


def _add_epi(accs, ex):
    return [accs[0] + ex[0]]


def _twice_epi(accs, _):
    return list(accs) + list(accs)


def _mix_fwd_epi(accs, ex):
    ya, yb = accs
    return [jax.nn.sigmoid(ex[0]) * ya + jax.nn.sigmoid(ex[1]) * yb, ya, yb]


def _mix_bwd_epi(accs, ex):
    dm = accs[0]
    sa, sb = jax.nn.sigmoid(ex[0]), jax.nn.sigmoid(ex[1])
    ya, yb = ex[2].astype(F32), ex[3].astype(F32)
    return [dm * sa, dm * sb, dm * ya * sa * (1.0 - sa), dm * yb * sb * (1.0 - sb)]


def _row_call(name, body, T, tiled, params, outs, accs=(), prev=(), nxt=(), halo=8, tile=ROW_TILE):
    tm = min(tile, T)
    n_tiles = T // tm

    def norm(e):
        return e if isinstance(e, tuple) else (e, e.shape[1], 0)

    tiled, prev, nxt = [norm(e) for e in tiled], [norm(e) for e in prev], [norm(e) for e in nxt]
    per_halo, n_halo = tm // halo, T // halo
    in_specs = [pl.BlockSpec((tm, w), lambda i, cb=cb: (i, cb)) for _, w, cb in tiled]
    in_specs += [pl.BlockSpec((halo, w), lambda i, cb=cb: (jnp.maximum(i * per_halo - 1, 0), cb)) for _, w, cb in prev]
    in_specs += [pl.BlockSpec((halo, w), lambda i, cb=cb: (jnp.minimum((i + 1) * per_halo, n_halo - 1), cb))
                 for _, w, cb in nxt]
    in_specs += [pl.BlockSpec(p.shape, lambda i, nd=p.ndim: (0,) * nd) for p in params]
    out_specs = [pl.BlockSpec((tm, o.shape[1]), lambda i: (i, 0)) for o in outs]
    out_specs += [pl.BlockSpec(a.shape, lambda i, nd=len(a.shape): (0,) * nd) for a in accs]
    n1, n2, n3, n4, n5 = len(tiled), len(prev), len(nxt), len(params), len(outs)

    def kernel_body(*refs):
        i = pl.program_id(0)
        acc_refs = refs[n1 + n2 + n3 + n4 + n5:]

        @pl.when(i == 0)
        def _():
            for a in acc_refs:
                a[...] = jnp.zeros_like(a)

        body(i, n_tiles, refs[:n1], refs[n1:n1 + n2], refs[n1 + n2:n1 + n2 + n3],
             refs[n1 + n2 + n3:n1 + n2 + n3 + n4], refs[n1 + n2 + n3 + n4:n1 + n2 + n3 + n4 + n5], acc_refs)

    return _pallas(
        kernel_body, name=name, grid=(n_tiles,), in_specs=in_specs, out_specs=out_specs,
        out_shape=list(outs) + list(accs),
        compiler_params=_params("arbitrary"),
    )(*[e[0] for e in tiled + prev + nxt], *params)


def _sds(shape, dtype=F32):
    return jax.ShapeDtypeStruct(tuple(shape), dtype)


def _rstd(x):
    return lax.rsqrt(jnp.mean(x * x, axis=-1, keepdims=True) + NORM_EPS)


def _colsum(x):
    return jnp.sum(x, axis=0, keepdims=True)


def rms_pre(x, g):
    T, D = x.shape

    def body(i, n, tiled, prev, nxt, params, outs, accs):
        xv = tiled[0][...]
        outs[0][...] = (xv * _rstd(xv) * params[0][...]).astype(BF16)

    return _row_call("rms_pre", body, T, [x], [g], [_sds((T, D), BF16)])[0]


def post_pre(h, f, g_post, g_pre, scale):
    T, D = h.shape

    def body(i, n, tiled, prev, nxt, params, outs, accs):
        hv, fv = tiled[0][...], tiled[1][...]
        h2 = hv + scale * (fv * _rstd(fv) * params[0][...])
        outs[0][...] = h2
        outs[1][...] = (h2 * _rstd(h2) * params[1][...]).astype(BF16)

    return _row_call("post_pre", body, T, [h, f], [g_post, g_pre], [_sds((T, D)), _sds((T, D), BF16)])


def _post_bwd_math(dh, fv, g, scale):
    r = _rstd(fv)
    fhat = fv * r
    dy = scale * dh
    z = dy * g
    df = r * (z - fhat * jnp.mean(z * fhat, axis=-1, keepdims=True))
    return df, _colsum(dy * fhat)


def loss_post_bwd(h, f, g_post, target, scale):
    T, D = h.shape

    def body(i, n, tiled, prev, nxt, params, outs, accs):
        hv, fv, tv = tiled[0][...], tiled[1][...], tiled[2][...]
        g = params[0][...]
        e = hv + scale * (fv * _rstd(fv) * g) - tv
        accs[0][...] += jnp.full(accs[0].shape, 0.5 / D, F32) * jnp.sum(e * e)
        dh = e * (1.0 / D)
        outs[0][...] = dh
        df, dg = _post_bwd_math(dh, fv, g, scale)
        outs[1][...] = df.astype(BF16)
        accs[1][...] += dg

    return _row_call("loss_post_bwd", body, T, [h, f, target], [g_post],
                     [_sds((T, D)), _sds((T, D), BF16)], [_sds((1, LANES)), _sds((1, D))])


def post_bwd(dh, f, g_post, scale):
    T, D = dh.shape

    def body(i, n, tiled, prev, nxt, params, outs, accs):
        df, dg = _post_bwd_math(tiled[0][...], tiled[1][...], params[0][...], scale)
        outs[0][...] = df.astype(BF16)
        accs[0][...] += dg

    return _row_call("post_bwd", body, T, [dh, f], [g_post], [_sds((T, D), BF16)], [_sds((1, D))])


def pre_bwd(dn, h, g_pre, dres):
    T, D = h.shape

    def body(i, n, tiled, prev, nxt, params, outs, accs):
        dnv, hv = tiled[0][...], tiled[1][...]
        r = _rstd(hv)
        hhat = hv * r
        z = dnv * params[0][...]
        outs[0][...] = tiled[2][...] + r * (z - hhat * jnp.mean(z * hhat, axis=-1, keepdims=True))
        accs[0][...] += _colsum(dnv * hhat)

    return _row_call("pre_bwd", body, T, [dn, h, dres], [g_pre], [_sds((T, D))], [_sds((1, D))])


def _head_ones():
    i = lax.broadcasted_iota(jnp.int32, (LANES, LANES), 0)
    j = lax.broadcasted_iota(jnp.int32, (LANES, LANES), 1)
    return jnp.where((i < HEAD) == (j < HEAD), 1.0, 0.0).astype(F32)


def _headsum(x):
    e = _head_ones()
    parts = [jnp.dot(x[:, s:s + LANES], e, precision=lax.Precision.HIGHEST, preferred_element_type=F32)
             for s in range(0, x.shape[1], LANES)]
    return parts[0] if len(parts) == 1 else jnp.concatenate(parts, axis=1)


def _shift_down(x, before):
    row = lax.broadcasted_iota(jnp.int32, x.shape, 0)
    return jnp.where(row == 0, before, pltpu.roll(x, 1, 0))


def _shift_up(x, after):
    row = lax.broadcasted_iota(jnp.int32, x.shape, 0)
    return jnp.where(row == x.shape[0] - 1, after, pltpu.roll(x, x.shape[0] - 1, 0))


def _last_row(ref, keep):
    r = ref[ref.shape[0] - 1:ref.shape[0], :]
    return jnp.where(keep, r, jnp.zeros_like(r))


def _first_row(ref, keep):
    r = ref[0:1, :]
    return jnp.where(keep, r, jnp.zeros_like(r))


def _softplus(u):
    return jnp.maximum(u, 0.0) + jnp.log(1.0 + jnp.exp(-jnp.abs(u)))


def _dotb(a, b, contract):
    return lax.dot_general(a.astype(BF16), b.astype(BF16), (contract, ((), ())), preferred_element_type=F32)


_NN, _NT, _TN = ((1,), (0,)), ((1,), (1,)), ((0,), (0,))


def _prep_forward(z, zprev_row, zl, zlprev_row, mu, mul, w0, a0, kk_w, ka_w, w2p, a2p, g2p):
    W = w0.shape[1]
    zs = z + (_shift_down(z, zprev_row) - z) * mu
    zls = zl + (_shift_down(zl, zlprev_row) - zl) * mul
    r, k, v = zs[:, :W], zs[:, W:2 * W], zs[:, 2 * W:]
    th, sg = jnp.tanh(zls), jax.nn.sigmoid(zls)
    xw = w0 + _dotb(th, w2p, _NN)
    wlog = -_softplus(-xw) - 0.5
    ew = jnp.exp(wlog)
    decay = jnp.exp(-ew)
    a = jax.nn.sigmoid(a0 + _dotb(zls, a2p, _NN))
    gate = _dotb(sg, g2p, _NN)
    q = k * kk_w
    nrm = jnp.sqrt(_headsum(q * q))
    den = jnp.maximum(nrm, L2_EPS)
    kk = q / den
    kmod = k * (1.0 + (a - 1.0) * ka_w)
    return dict(zs=zs, zls=zls, r=r, k=k, v=v, th=th, sg=sg, xw=xw, ew=ew, decay=decay, a=a, gate=gate,
                nrm=nrm, den=den, kk=kk, kmod=kmod)


def rwkv_prep(p, cols, mu, mul, w0, a0, kk_w, ka_w, w2p, a2p, g2p):
    T = p.shape[0]
    W = w0.shape[1]

    def body(i, n, tiled, prev, nxt, params, outs, accs):
        c = _prep_forward(tiled[0][...], _last_row(prev[0], i > 0), tiled[1][...], _last_row(prev[1], i > 0),
                          *[q[...] for q in params])
        for o, val in zip(outs, (c["r"], c["decay"], c["kmod"], c["v"], -c["kk"], c["kk"] * c["a"], c["gate"])):
            o[...] = val

    return _row_call("rwkv_prep", body, T, [cols["rkv"], cols["lat"]],
                     [mu, mul, w0, a0, kk_w, ka_w, w2p, a2p, g2p], [_sds((T, W))] * 7,
                     prev=[cols["rkv"], cols["lat"]], tile=RWKV_ROW_TILE)


def _post_forward(y, r, kmod, v, gn_w, gn_b, rk):
    mean = _headsum(y) * (1.0 / HEAD)
    yc = y - mean
    rstd = lax.rsqrt(_headsum(yc * yc) * (1.0 / HEAD) + GN_EPS)
    yn = yc * rstd
    s = _headsum(r * kmod * rk)
    return yn, rstd, s, yn * gn_w + gn_b + s * v


def rwkv_post(y, r, kmod, v, gate, gn_w, gn_b, rk):
    T, W = y.shape

    def body(i, n, tiled, prev, nxt, params, outs, accs):
        yv, rv, kv, vv, gv = [t[...] for t in tiled]
        _, _, _, o = _post_forward(yv, rv, kv, vv, *[q[...] for q in params])
        outs[0][...] = (o * gv).astype(BF16)

    return _row_call("rwkv_post", body, T, [y, r, kmod, v, gate], [gn_w, gn_b, rk], [_sds((T, W), BF16)],
                     tile=RWKV_ROW_TILE)[0]


def rwkv_post_bwd(dout, y, r, kmod, v, gate, gn_w, gn_b, rk):
    T, W = y.shape

    def body(i, n, tiled, prev, nxt, params, outs, accs):
        dv_, yv, rv, kv, vv, gv = [t[...] for t in tiled]
        gn_w_, gn_b_, rk_ = [q[...] for q in params]
        yn, rstd, s, o = _post_forward(yv, rv, kv, vv, gn_w_, gn_b_, rk_)
        do = dv_ * gv
        outs[4][...] = dv_ * o
        accs[0][...] += _colsum(do * yn)
        accs[1][...] += _colsum(do)
        dyn = do * gn_w_
        outs[0][...] = rstd * (dyn - _headsum(dyn) * (1.0 / HEAD) - yn * (_headsum(dyn * yn) * (1.0 / HEAD)))
        ds = _headsum(do * vv)
        outs[1][...] = ds * kv * rk_
        outs[2][...] = ds * rv * rk_
        outs[3][...] = do * s
        accs[2][...] += _colsum(ds * rv * kv)

    return _row_call("rwkv_post_bwd", body, T, [dout, y, r, kmod, v, gate], [gn_w, gn_b, rk],
                     [_sds((T, W))] * 5, [_sds((1, W))] * 3, tile=RWKV_ROW_TILE)


def rwkv_prep_bwd(p, cols, grads, mu, mul, w0, a0, kk_w, ka_w, w2p, a2p, g2p):
    T = p.shape[0]
    W = w0.shape[1]
    latp = w2p.shape[0]

    def body(i, n, tiled, prev, nxt, params, outs, accs):
        pv = [q[...] for q in params]
        mu_, mul_, w0_, a0_, kk_w_, ka_w_, w2p_, a2p_, g2p_ = pv
        c = _prep_forward(tiled[0][...], _last_row(prev[0], i > 0), tiled[1][...], _last_row(prev[1], i > 0), *pv)
        dr_s, dr_x, ddecay, dk_s, dk_x, dv_s, dv_x, dneg, db, dgate = [t[...] for t in tiled[2:]]
        k, a, kk = c["k"], c["a"], c["kk"]
        dkmod = dk_s + dk_x
        dk = dkmod * (1.0 + (a - 1.0) * ka_w_)
        da = dkmod * k * ka_w_ + db * kk
        accs[0][...] += _colsum(dkmod * k * (a - 1.0))
        dkk = db * a - dneg
        dq = jnp.where(c["nrm"] > L2_EPS, dkk - kk * _headsum(dkk * kk), dkk) / c["den"]
        dk = dk + dq * kk_w_
        accs[1][...] += _colsum(dq * k)
        dxa = da * a * (1.0 - a)
        accs[2][...] += _colsum(dxa)
        accs[4][...] += _dotb(c["zls"], dxa, _TN)
        dzls = _dotb(dxa, a2p_, _NT)
        dxw = (-ddecay * c["decay"] * c["ew"]) * jax.nn.sigmoid(-c["xw"])
        accs[3][...] += _colsum(dxw)
        accs[5][...] += _dotb(c["th"], dxw, _TN)
        dzls = dzls + _dotb(dxw, w2p_, _NT) * (1.0 - c["th"] * c["th"])
        accs[6][...] += _dotb(c["sg"], dgate, _TN)
        dzls = dzls + _dotb(dgate, g2p_, _NT) * c["sg"] * (1.0 - c["sg"])
        outs[0][...] = jnp.concatenate([dr_s + dr_x, dk, dv_s + dv_x], axis=1)
        outs[1][...] = dzls

    return _row_call("rwkv_prep_bwd", body, T, [cols["rkv"], cols["lat"]] + list(grads),
                     [mu, mul, w0, a0, kk_w, ka_w, w2p, a2p, g2p], [_sds((T, 3 * W)), _sds((T, latp))],
                     [_sds((1, W))] * 4 + [_sds((latp, W))] * 3, prev=[cols["rkv"], cols["lat"]], tile=RWKV_ROW_TILE)


def shift_bwd(cols, dzs, dzls, mu, mul):
    T = dzs.shape[0]

    def body(i, n, tiled, prev, nxt, params, outs, accs):
        for j in range(2):
            z, d, m = tiled[j][...], tiled[2 + j][...], params[j][...]
            zprev = _shift_down(z, _last_row(prev[j], i > 0))
            dnext = _shift_up(d, _first_row(nxt[j], i < n - 1))
            outs[j][...] = (d * (1.0 - m) + dnext * m).astype(BF16)
            accs[j][...] += _colsum(d * (zprev - z))

    return _row_call("shift_bwd", body, T, [cols["rkv"], cols["lat"], dzs, dzls], [mu, mul],
                     [_sds(dzs.shape, BF16), _sds(dzls.shape, BF16)], [_sds(mu.shape), _sds(mul.shape)],
                     prev=[cols["rkv"], cols["lat"]], nxt=[dzs, dzls])


def _window_pick(x, windows):
    gid = lax.broadcasted_iota(jnp.int32, x.shape, 1) // (x.shape[1] // len(windows))
    out = windows[-1]
    for g in range(len(windows) - 2, -1, -1):
        out = jnp.where(gid == g, windows[g], out)
    return out


def _pool_counts(t0, rows, width):
    t = (t0 + lax.broadcasted_iota(jnp.int32, (rows, width), 0) + 1).astype(F32)
    return _window_pick(t, [jnp.minimum(t, float(w)) for w in POOL_WINDOWS])


def _pool_mixed(x, before, t0):
    tm, width = x.shape
    xe = jnp.concatenate([before, x], axis=0)
    sums, s, span = [], xe, 1
    for w in POOL_WINDOWS:
        while span < w:
            s = s + pltpu.roll(s, span, 0)
            span *= 2
        sums.append(s[POOL_HALO:, :])
    return _window_pick(x, sums) / _pool_counts(t0, tm, width) - x


def _group_dot(x, w_ref, contract):
    gd = w_ref.shape[-1]
    parts = [_dotb(x[:, g * gd:(g + 1) * gd], w_ref[g], contract) for g in range(w_ref.shape[0])]
    return jnp.concatenate(parts, axis=1)


def pool_fwd(cols, pool_w, pool_scale):
    T, width = cols["pool"][0].shape[0], cols["pool"][1]
    tm = min(ROW_TILE, T)

    def body(i, n, tiled, prev, nxt, params, outs, accs):
        before = jnp.where(i > 0, prev[0][...], 0.0)
        mixed = _pool_mixed(tiled[0][...], before, i * tm)
        outs[0][...] = (_group_dot(mixed, params[0], _NN) * params[1][...]).astype(BF16)

    return _row_call("pool_fwd", body, T, [cols["pool"]], [pool_w, pool_scale], [_sds((T, width), BF16)],
                     prev=[cols["pool"]], halo=POOL_HALO)[0]


def pool_bwd(cols, dout, pool_w, pool_scale):
    T, width = dout.shape
    tm = min(ROW_TILE, T)

    def body(i, n, tiled, prev, nxt, params, outs, accs):
        w_ref, scale = params[0], params[1][...]
        before = jnp.where(i > 0, prev[0][...], 0.0)
        mixed = _pool_mixed(tiled[0][...], before, i * tm)
        dv = tiled[1][...]
        accs[1][...] += _colsum(dv * _group_dot(mixed, w_ref, _NN))
        after = jnp.where(i < n - 1, nxt[0][...], 0.0)
        dys = jnp.concatenate([dv, after], axis=0) * scale
        gd = w_ref.shape[-1]
        for g in range(w_ref.shape[0]):
            accs[0][g] += _dotb(mixed[:, g * gd:(g + 1) * gd], dys[:tm, g * gd:(g + 1) * gd], _TN)
        dmixed = _group_dot(dys, w_ref, _NT)
        u = dmixed / _pool_counts(i * tm, tm + POOL_HALO, width)
        rows = tm + POOL_HALO
        sums, s, span = [], u, 1
        for w in POOL_WINDOWS:
            while span < w:
                s = s + pltpu.roll(s, rows - span, 0)
                span *= 2
            sums.append(s[:tm, :])
        outs[0][...] = (_window_pick(dv, sums) - dmixed[:tm, :]).astype(BF16)

    return _row_call("pool_bwd", body, T, [cols["pool"], dout], [pool_w, pool_scale], [_sds((T, width), BF16)],
                     [_sds(pool_w.shape), _sds((1, width))], prev=[cols["pool"]], nxt=[dout], halo=POOL_HALO)


def _wkv_consts(pairs):
    lane = lax.broadcasted_iota(jnp.int32, (HEAD, LANES), 1)
    sub = lax.broadcasted_iota(jnp.int32, (pairs * HEAD, LANES), 0)
    lane_all = lax.broadcasted_iota(jnp.int32, (pairs * HEAD, LANES), 1)
    i = lax.broadcasted_iota(jnp.int32, (LANES, LANES), 0)
    j = lax.broadcasted_iota(jnp.int32, (LANES, LANES), 1)
    ones = jnp.where((i < HEAD) == (j < HEAD), 1.0, 0.0).astype(BF16)
    diag = jnp.where((lane_all & (HEAD - 1)) == (sub & (HEAD - 1)), 1.0, 0.0).astype(F32)
    return lane < HEAD, diag, ones


def _segsum(p, in_a):
    sa = jnp.sum(jnp.where(in_a, p, 0.0), axis=1, keepdims=True)
    sb = jnp.sum(jnp.where(in_a, 0.0, p), axis=1, keepdims=True)
    return jnp.where(in_a, sa, sb)


def _hi_lo(p):
    hi = lax.bitcast_convert_type(lax.bitcast_convert_type(p, jnp.uint32) & jnp.uint32(0xFFFF0000), F32)
    return hi, p - hi


def _segsum_mxu(p, ones):
    hi, lo = _hi_lo(p)
    return (jnp.dot(hi.astype(BF16), ones, preferred_element_type=F32)
            + jnp.dot(lo.astype(BF16), ones, preferred_element_type=F32))


def _cat(parts, axis):
    return parts[0] if len(parts) == 1 else jnp.concatenate(parts, axis=axis)


def _tile_rows(row, pairs):
    return _cat([jnp.broadcast_to(row[:, p * LANES:(p + 1) * LANES], (HEAD, LANES)) for p in range(pairs)], 0)


def _spread(row, pairs, diag16, ones):
    hi, lo = _hi_lo(row)
    return (jnp.dot(_tile_rows(hi.astype(BF16), pairs) * diag16, ones, preferred_element_type=F32)
            + jnp.dot(_tile_rows(lo.astype(BF16), pairs) * diag16, ones, preferred_element_type=F32))


def _pair_colsums(x, pairs):
    return _cat([_colsum(x[p * HEAD:(p + 1) * HEAD]) for p in range(pairs)], 1)


def _spread_split(row, pairs, in_a, diag, diag16, ones):
    n_mxu = min(WKV_MXU_PAIRS, pairs)
    parts = [_spread(row[:, :n_mxu * LANES], n_mxu, diag16[:n_mxu * HEAD], ones)]
    parts += [_segsum(row[:, p * LANES:(p + 1) * LANES] * diag[:HEAD], in_a) for p in range(n_mxu, pairs)]
    return _cat(parts, 0)


def _segsum_split(x, pairs, in_a, ones):
    n_mxu = min(WKV_MXU_PAIRS, pairs)
    parts = [_segsum_mxu(x[:n_mxu * HEAD], ones)]
    parts += [_segsum(x[p * HEAD:(p + 1) * HEAD], in_a) for p in range(n_mxu, pairs)]
    return _cat(parts, 0)


def wkv_fwd(r, w, k, v, a, b, carry=None):
    T, W = r.shape
    P = W // LANES
    PB = min(WKV_PAIRS, P)
    chunk = min(WKV_CHUNK, T)
    NC = T // chunk
    R = PB * HEAD
    ahead = min(WKV_UNROLL, chunk // 2)
    c_in, c_out = (len(carry.inputs), len(carry.out_shapes)) if carry else (0, 0)

    def body(*refs):
        r_ref, w_ref, k_ref, v_ref, a_ref, b_ref = refs[:6]
        y_ref, st_ref, sa_ref = refs[6 + c_in:9 + c_in]
        vt_ref, s_ref = refs[9 + c_in + c_out:11 + c_in + c_out]
        carried = refs[6:6 + c_in], refs[9 + c_in:9 + c_in + c_out], refs[11 + c_in + c_out:]
        g, c = pl.program_id(0), pl.program_id(1)

        if carry:
            @pl.when((g == 0) & (c == 0))
            def _():
                carry.start(*carried)

        @pl.when(c == 0)
        def _():
            s_ref[...] = jnp.zeros_like(s_ref)

        in_a, diag, ones = _wkv_consts(PB)
        diag16 = diag.astype(BF16)

        def step(t, _):
            rows = [ref[pl.ds(t, 1), :] for ref in (w_ref, k_ref, a_ref, b_ref)]
            for p in range(PB):
                wt, kt, at, bt = [x[:, p * LANES:(p + 1) * LANES] for x in rows]
                rs = pl.ds(p * HEAD, HEAD)
                S = s_ref[rs]
                sa = _segsum(S * at, in_a)
                sa_ref[t, rs] = sa
                S = S * wt + sa * bt + vt_ref[t, rs] * kt
                st_ref[t, rs] = S
                s_ref[rs] = S
            return 0

        def spread_step(t, _):
            vt_ref[t + ahead] = _spread(v_ref[pl.ds(t + ahead, 1), :], PB, diag16, ones)
            return step(t, 0)

        for t in range(ahead):
            vt_ref[t] = _spread_split(v_ref[t:t + 1, :], PB, in_a, diag, diag16, ones)
        lax.fori_loop(0, chunk - ahead, spread_step, 0, unroll=WKV_UNROLL)
        lax.fori_loop(chunk - ahead, chunk, step, 0, unroll=WKV_UNROLL)

        def readout(t, _):
            yt = _segsum_split(st_ref[t] * _tile_rows(r_ref[pl.ds(t, 1), :], PB), PB, in_a, ones) * diag
            y_ref[pl.ds(t, 1), :] = _pair_colsums(yt, PB)
            return 0

        lax.fori_loop(0, chunk, readout, 0, unroll=WKV_UNROLL)

        if carry:
            @pl.when((g == P // PB - 1) & (c == NC - 1))
            def _():
                carry.finish(*carried)

    spec = pl.BlockSpec((chunk, PB * LANES), lambda g, c: (c, g))
    tiles = pl.BlockSpec((chunk, R, LANES), lambda g, c: (c, g, 0))
    res = _pallas(
        body, name="wkv_fwd", grid=(P // PB, NC), in_specs=[spec] * 6 + [ANY] * c_in,
        out_specs=[spec, tiles, tiles] + [ANY] * c_out,
        out_shape=[_sds((T, W)), _sds((T, P * HEAD, LANES)), _sds((T, P * HEAD, LANES))]
        + (list(carry.out_shapes) if carry else []),
        scratch_shapes=[pltpu.VMEM((chunk, R, LANES), F32), pltpu.VMEM((R, LANES), F32)]
        + (list(carry.scratch) if carry else []),
        compiler_params=_params("arbitrary", "arbitrary") if carry else _params("parallel", "arbitrary"),
    )(r, w, k, v, a, b, *(carry.inputs if carry else []))
    return (res[:3], res[3:]) if carry else res


def wkv_bwd(r, w, k, v, a, b, dy, st, sa):
    T, W = r.shape
    P = W // LANES
    PB = min(WKV_PAIRS, P)
    chunk = min(WKV_CHUNK, T)
    NC = T // chunk
    R = PB * HEAD
    ahead = min(WKV_UNROLL, chunk // 2)

    def body(r_ref, w_ref, k_ref, v_ref, a_ref, b_ref, dy_ref, st_ref, before_ref, sa_ref,
             dr_ref, dw_ref, dk_ref, dv_ref, da_ref, db_ref, ds_ref, dyt_ref, dst_ref, dsa_ref):
        c = pl.program_id(1)

        @pl.when(c == 0)
        def _():
            ds_ref[...] = jnp.zeros_like(ds_ref)

        in_a, diag, ones = _wkv_consts(PB)
        diag16 = diag.astype(BF16)

        def bstep(n, _):
            t = chunk - 1 - n
            rows = [ref[pl.ds(t, 1), :] for ref in (r_ref, w_ref, a_ref, b_ref)]
            for p in range(PB):
                rt, wt, at, bt = [x[:, p * LANES:(p + 1) * LANES] for x in rows]
                rs = pl.ds(p * HEAD, HEAD)
                dS = ds_ref[rs] + dyt_ref[t, rs] * rt
                dst_ref[t, rs] = dS
                dsa = _segsum(dS * bt, in_a)
                dsa_ref[t, rs] = dsa
                ds_ref[rs] = dS * wt + dsa * at
            return 0

        def spread_bstep(n, _):
            t = chunk - 1 - ahead - n
            dyt_ref[t] = _spread(dy_ref[pl.ds(t, 1), :], PB, diag16, ones)
            return bstep(n, 0)

        for t in range(chunk - ahead, chunk):
            dyt_ref[t] = _spread_split(dy_ref[t:t + 1, :], PB, in_a, diag, diag16, ones)
        lax.fori_loop(0, chunk - ahead, spread_bstep, 0, unroll=WKV_UNROLL)
        lax.fori_loop(chunk - ahead, chunk, bstep, 0, unroll=WKV_UNROLL)

        def collect(t, _):
            sn, dS, dsa = st_ref[t], dst_ref[t], dsa_ref[t]
            sp = st_ref[jnp.maximum(t - 1, 0)]
            dvt = _segsum_split(dS * _tile_rows(k_ref[pl.ds(t, 1), :], PB), PB, in_a, ones) * diag
            vt = _spread_split(v_ref[pl.ds(t, 1), :], PB, in_a, diag, diag16, ones)
            for ref, val in ((dr_ref, sn * dyt_ref[t]), (dw_ref, dS * sp), (dk_ref, dS * vt), (dv_ref, dvt),
                             (da_ref, sp * dsa), (db_ref, dS * sa_ref[t])):
                ref[pl.ds(t, 1), :] = _pair_colsums(val, PB)
            return 0

        lax.fori_loop(0, chunk, collect, 0, unroll=WKV_UNROLL)
        first = jnp.where(c == NC - 1, 0.0, before_ref[0])
        dw_ref[0:1, :] = _pair_colsums(dst_ref[0] * first, PB)
        da_ref[0:1, :] = _pair_colsums(first * dsa_ref[0], PB)

    spec = pl.BlockSpec((chunk, PB * LANES), lambda g, c: (NC - 1 - c, g))
    tiles = pl.BlockSpec((chunk, R, LANES), lambda g, c: (NC - 1 - c, g, 0))
    before = pl.BlockSpec((1, R, LANES), lambda g, c: (jnp.maximum((NC - 1 - c) * chunk - 1, 0), g, 0))

    def scratch(n):
        return pltpu.VMEM((n, R, LANES), F32)

    return _pallas(
        body, name="wkv_bwd", grid=(P // PB, NC), in_specs=[spec] * 7 + [tiles, before, tiles],
        out_specs=[spec] * 6, out_shape=[_sds((T, W))] * 6,
        scratch_shapes=[pltpu.VMEM((R, LANES), F32), scratch(chunk), scratch(chunk), scratch(chunk)],
        compiler_params=_params("parallel", "arbitrary"),
    )(r, w, k, v, a, b, dy, st, st, sa)


def _position():
    return lax.axis_index("x"), lax.axis_index("y"), lax.axis_index("c")


def _other_chips(x, y):
    return [(1 - x, y), (x, 1 - y), (1 - x, 1 - y)]


class _Carry:
    def __init__(self, inputs, out_shapes, scratch, start, finish):
        self.inputs, self.out_shapes, self.scratch, self.start, self.finish = inputs, out_shapes, scratch, start, finish


def _run_carry(name, carry):
    n_in, n_out = len(carry.inputs), len(carry.out_shapes)

    def body(*refs):
        parts = refs[:n_in], refs[n_in:n_in + n_out], refs[n_in + n_out:]
        carry.start(*parts)
        carry.finish(*parts)

    return _pallas(body, name=name, in_specs=[ANY] * n_in, out_specs=[ANY] * n_out, out_shape=list(carry.out_shapes),
                   scratch_shapes=list(carry.scratch))(*carry.inputs)


def gather_carry(shards):
    n = len(shards)

    def plan(x_refs, out_refs, sems):
        send_sems, recv_sems, local_sems = sems
        x, y, c = _position()
        me, sibling = (x, y, c), (x, y, 1 - c)
        chips = _other_chips(x, y)

        def slot(ref, pos):
            return ref.at[4 * pos[0] + 2 * pos[1] + pos[2]]

        def copy(t, j, block, to, src=None):
            dst = slot(out_refs[t], block)
            return pltpu.make_async_remote_copy(
                src_ref=dst if src is None else src, dst_ref=dst, send_sem=send_sems.at[t, j],
                recv_sem=recv_sems.at[t, j], device_id=to, device_id_type=MESH)

        mine = [pltpu.make_async_copy(x_refs[t], slot(out_refs[t], me), local_sems.at[t]) for t in range(n)]
        first = []
        for t in range(n):
            first.append(copy(t, 0, me, sibling, src=x_refs[t]))
            first += [copy(t, 1 + j, me, (*chip, c), src=x_refs[t]) for j, chip in enumerate(chips)]
        return c, me, sibling, chips, copy, mine, first

    def start(x_refs, out_refs, sems):
        _, _, _, _, _, mine, first = plan(x_refs, out_refs, sems)
        for cp in mine + first:
            cp.start()

    def finish(x_refs, out_refs, sems):
        c, me, sibling, chips, copy, mine, first = plan(x_refs, out_refs, sems)
        passed = []
        for t in range(n):
            for j, chip in enumerate(chips):
                copy(t, 1 + j, (*chip, c), me).wait_recv()
                fwd = copy(t, 4 + j, (*chip, c), sibling)
                fwd.start()
                passed.append(fwd)
        for t in range(n):
            copy(t, 0, sibling, me).wait_recv()
            for j, chip in enumerate(chips):
                copy(t, 4 + j, (*chip, 1 - c), me).wait_recv()
        for cp in first + passed:
            cp.wait_send()
        for cp in mine:
            cp.wait()

    return _Carry(list(shards), [_sds((N_DEV,) + s.shape, s.dtype) for s in shards],
                  [pltpu.SemaphoreType.DMA((n, 7)), pltpu.SemaphoreType.DMA((n, 7)), pltpu.SemaphoreType.DMA((n,))],
                  start, finish)


def all_gather(shards):
    return _run_carry("all_gather", gather_carry(shards))


def exchange_sibling(parts):
    n = len(parts)

    def body(*refs):
        p_refs, out_refs = refs[:n], refs[n:2 * n]
        send_sems, recv_sems = refs[2 * n:]
        x, y, c = _position()
        copies = []
        for t in range(n):
            for q in range(N_CHIP):
                cp = pltpu.make_async_remote_copy(
                    src_ref=p_refs[t].at[q, 1 - c], dst_ref=out_refs[t].at[q], send_sem=send_sems.at[t, q],
                    recv_sem=recv_sems.at[t, q], device_id=(x, y, 1 - c), device_id_type=MESH)
                cp.start()
                copies.append(cp)
        for cp in copies:
            cp.wait()

    return _pallas(
        body, name="exchange_sibling", in_specs=[ANY] * n, out_specs=[ANY] * n,
        out_shape=[_sds((N_CHIP,) + p.shape[2:], p.dtype) for p in parts],
        scratch_shapes=[pltpu.SemaphoreType.DMA((n, N_CHIP)), pltpu.SemaphoreType.DMA((n, N_CHIP))],
    )(*parts)


def chips_carry(parts):
    n = len(parts)

    def plan(p_refs, out_refs, sems):
        send_sems, recv_sems, local_sems = sems
        x, y, c = _position()
        local = [pltpu.make_async_copy(p_refs[t].at[2 * x + y], out_refs[t].at[3], local_sems.at[t]) for t in range(n)]
        remote = [pltpu.make_async_remote_copy(
            src_ref=p_refs[t].at[2 * cx + cy], dst_ref=out_refs[t].at[j], send_sem=send_sems.at[t, j],
            recv_sem=recv_sems.at[t, j], device_id=(cx, cy, c), device_id_type=MESH)
            for t in range(n) for j, (cx, cy) in enumerate(_other_chips(x, y))]
        return local, remote

    def start(p_refs, out_refs, sems):
        local, remote = plan(p_refs, out_refs, sems)
        for cp in local + remote:
            cp.start()

    def finish(p_refs, out_refs, sems):
        local, remote = plan(p_refs, out_refs, sems)
        for cp in remote + local:
            cp.wait()

    return _Carry(list(parts), [_sds(p.shape, p.dtype) for p in parts],
                  [pltpu.SemaphoreType.DMA((n, 3)), pltpu.SemaphoreType.DMA((n, 3)), pltpu.SemaphoreType.DMA((n,))],
                  start, finish)


def _flat_tile(rows, cols):
    tr = rows
    for d in range(16, min(rows, 512) + 1, 16):
        if rows % d == 0 and d * cols * 4 <= 2 * 1024 * 1024:
            tr = d
    return tr


def pair_add(part, recv):
    _, _, R, C = part.shape
    tr = _flat_tile(R, C)
    core = jnp.reshape(lax.axis_index("c"), (1,)).astype(jnp.int32)

    def body(core_ref, p_ref, r_ref, o_ref):
        o_ref[...] = (p_ref[...] + r_ref[...]).astype(BF16)

    grid_spec = pltpu.PrefetchScalarGridSpec(
        num_scalar_prefetch=1, grid=(N_CHIP, R // tr),
        in_specs=[pl.BlockSpec((None, None, tr, C), lambda q, i, core_ref: (q, core_ref[0], i, 0)),
                  pl.BlockSpec((None, tr, C), lambda q, i, core_ref: (q, i, 0))],
        out_specs=pl.BlockSpec((None, tr, C), lambda q, i, core_ref: (q, i, 0)))
    return _pallas(body, name="pair_add", grid_spec=grid_spec, out_shape=_sds((N_CHIP, R, C), BF16),
                   compiler_params=_params("parallel", "parallel"))(core, part, recv)


def adamw(w, m, v, slabs):
    R, C = w.shape
    tr = _flat_tile(R, C)
    n = slabs.shape[0]

    def body(w_ref, m_ref, v_ref, s_ref, g_ref, d_ref, nm_ref, nv_ref):
        g = s_ref[0].astype(F32)
        for j in range(1, n):
            g = g + s_ref[j].astype(F32)
        m2 = ADAM_B1 * m_ref[...] + (1.0 - ADAM_B1) * g
        v2 = ADAM_B2 * v_ref[...] + (1.0 - ADAM_B2) * (g * g)
        m_hat = m2 / (1.0 - ADAM_B1 ** ADAM_STEP)
        v_hat = v2 / (1.0 - ADAM_B2 ** ADAM_STEP)
        g_ref[...] = g
        d_ref[...] = -ADAM_LR * (m_hat / (jnp.sqrt(v_hat) + ADAM_EPS) + ADAM_WD * w_ref[...])
        nm_ref[...] = m2
        nv_ref[...] = v2

    spec = pl.BlockSpec((tr, C), lambda i: (i, 0))
    return _pallas(body, name="adamw", grid=(R // tr,),
                   in_specs=[spec] * 3 + [pl.BlockSpec((n, tr, C), lambda i: (0, i, 0))], out_specs=[spec] * 4,
                   out_shape=[_sds((R, C))] * 4, compiler_params=_params("parallel"))(w, m, v, slabs)


def _unshard_cols(g):
    return jnp.transpose(g, (1, 0, 2)).reshape(g.shape[1], -1)


def _unshard_rows(g):
    return g.reshape(-1, g.shape[2])


def _shard_cols(full):
    R, C = full.shape
    return jnp.transpose(full.reshape(R, N_DEV, C // N_DEV), (1, 0, 2)).reshape(N_CHIP, 2, R, C // N_DEV)


def _shard_rows(full):
    R, C = full.shape
    return full.reshape(N_CHIP, 2, R // N_DEV, C)


WEIGHTS = ['ln_ffn1_pre', 'ln_ffn1_post', 'ffn1_gate', 'ffn1_up', 'ffn1_down', 'ln_mix_pre', 'ln_mix_post', 'w_in',
           'rwkv_mu', 'rwkv_w0', 'rwkv_w2', 'rwkv_a0', 'rwkv_a2', 'rwkv_g2', 'rwkv_k_k', 'rwkv_k_a', 'rwkv_r_k',
           'rwkv_gn_w', 'rwkv_gn_b', 'w_proj_a', 'pool_w', 'pool_scale', 'w_proj_b', 'w_out', 'ln_ffn2_pre',
           'ln_ffn2_post', 'ffn2_gate', 'ffn2_up', 'ffn2_down']
COL_SHARDED = ['ffn1_gate', 'ffn1_up', 'ffn2_gate', 'ffn2_up', 'w_in', 'rwkv_w2', 'rwkv_a2', 'rwkv_g2', 'w_proj_a',
               'w_proj_b']
ROW_SHARDED = ['ffn1_down', 'ffn2_down', 'w_out', 'pool_w']
TRANSPOSED = ['ffn1_gate', 'ffn1_up', 'ffn2_gate', 'ffn2_up', 'w_in', 'w_proj_a', 'w_proj_b']
SHARDED = COL_SHARDED + ROW_SHARDED
REPLICATED = [n for n in WEIGHTS if n not in SHARDED]


def _step(args):
    wts = {n: args[n] if args[n].ndim == 2 else args[n][0] for n in WEIGHTS}
    x, target = args["x"][0], args["loss_target"][0]
    T, D = x.shape
    W = wts["rwkv_w0"].shape[1]
    PW = wts["pool_scale"].shape[1]
    LW, LA, LG = wts["rwkv_w2"].shape[0], wts["rwkv_a2"].shape[0], wts["rwkv_g2"].shape[0]
    lat = LW + LA + LG
    latp = _round_up(lat, LAT_ALIGN)
    rc = 3 * W + lat
    base = 3 * W + PW + 2 * D
    n_groups, gshard, gd = wts["pool_w"].shape

    pool_w_shard = wts["pool_w"].reshape(n_groups * gshard, gd)
    shards = {n: (pool_w_shard if n == "pool_w" else wts[n]).astype(BF16) for n in SHARDED}
    shards.update({n: shards[n].T for n in TRANSPOSED})
    full = {}

    def fetch(names):
        return gather_carry([shards[n] for n in names])

    def arrived(names, got):
        for n, g in zip(names, got):
            if n == "pool_w":
                full[n] = jnp.transpose(g.reshape(N_DEV, n_groups, gshard, gd), (1, 0, 2, 3)).reshape(n_groups, gd, gd)
            elif n in COL_SHARDED and n not in TRANSPOSED:
                full[n] = _unshard_cols(g)
            else:
                full[n] = _unshard_rows(g)

    arrived(["ffn1_gate", "ffn1_up"], all_gather([shards["ffn1_gate"], shards["ffn1_up"]]))
    mu = wts["rwkv_mu"]
    mu_rkv = mu[:, :3 * W]
    mu_lat = jnp.concatenate([mu[:, 3 * W:], jnp.zeros((1, latp - lat), F32)], axis=1)
    rk = wts["rwkv_r_k"].reshape(1, W)

    n1 = rms_pre(x, wts["ln_ffn1_pre"])
    (g1, u1, act1), got = _mm("ffn1_up", [n1], [full["ffn1_gate"], full["ffn1_up"]], "nt", [BF16] * 3,
                              epilogue=_swiglu_fwd_epi, carry=fetch(["ffn1_down"]))
    arrived(["ffn1_down"], got)
    (f1,), got = _mm("ffn1_down", [act1], [full["ffn1_down"]], "nn", [F32], tm=512, carry=fetch(["w_in"]))
    arrived(["w_in"], got)
    w_in = full["w_in"]
    w_in_p = jnp.concatenate([w_in[:3 * W], w_in[rc:], w_in[3 * W:rc], jnp.zeros((latp - lat, D), BF16)], axis=0)
    h1, nm = post_pre(x, f1, wts["ln_ffn1_post"], wts["ln_mix_pre"], MACARON)
    mixer = ["rwkv_w2", "rwkv_a2", "rwkv_g2", "w_proj_a", "w_proj_b", "pool_w", "w_out"]
    (p,), got = _mm("in_proj", [nm], [w_in_p], "nt", [F32], carry=fetch(mixer))
    arrived(mixer, got)
    pool_w = full["pool_w"]

    def pad_rows(m, at):
        return jnp.zeros((latp, W), BF16).at[at:at + m.shape[0]].set(m)

    w2p, a2p, g2p = pad_rows(full["rwkv_w2"], 0), pad_rows(full["rwkv_a2"], LW), pad_rows(full["rwkv_g2"], LW + LA)
    small = [mu_rkv, mu_lat, wts["rwkv_w0"], wts["rwkv_a0"], wts["rwkv_k_k"], wts["rwkv_k_a"], w2p, a2p, g2p]
    cols = {"rkv": (p, 3 * W, 0), "pool": (p, PW, 3 * W // PW), "lat": (p, latp, base // latp)}
    r, decay, kmod, v, aneg, bpos, gate = rwkv_prep(p, cols, *small)
    (y, states, sdota), got = wkv_fwd(r, decay, kmod, v, aneg, bpos, carry=fetch(["ffn2_gate", "ffn2_up"]))
    arrived(["ffn2_gate", "ffn2_up"], got)
    ya_in = rwkv_post(y, r, kmod, v, gate, wts["rwkv_gn_w"], wts["rwkv_gn_b"], rk)
    yb_in = pool_fwd(cols, pool_w, wts["pool_scale"])
    gates = [(p, 3 * W + PW), (p, 3 * W + PW + D)]
    m, ya, yb = _mm("mix", [ya_in, yb_in], [full["w_proj_a"], full["w_proj_b"]], "nt", [BF16] * 3,
                    extras=gates, epilogue=_mix_fwd_epi)
    mx = _mm("out_proj", [m], [full["w_out"]], "nn", [F32])[0]
    h2, n2 = post_pre(h1, mx, wts["ln_mix_post"], wts["ln_ffn2_pre"], 1.0)
    (g2_, u2, act2), got = _mm("ffn2_up", [n2], [full["ffn2_gate"], full["ffn2_up"]], "nt", [BF16] * 3,
                               epilogue=_swiglu_fwd_epi, carry=fetch(["ffn2_down"]))
    arrived(["ffn2_down"], got)
    f2 = _mm("ffn2_down", [act2], [full["ffn2_down"]], "nn", [F32], tm=512)[0]

    grads, slabs = {}, {}
    to_sibling = {}

    def pair_sums(names):
        parts = []
        for n in names:
            if n == "pool_w":
                parts.append(jnp.transpose(grads[n].reshape(n_groups, N_DEV, gshard, gd), (1, 0, 2, 3)).reshape(
                    N_CHIP, 2, n_groups * gshard, gd))
            elif n in COL_SHARDED and n not in TRANSPOSED:
                parts.append(_shard_cols(grads[n]))
            else:
                parts.append(_shard_rows(grads[n]))
        sent = [_shard_rows(to_sibling[n]) if n in to_sibling else part for n, part in zip(names, parts)]
        return [pair_add(part, rcv) for part, rcv in zip(parts, exchange_sibling(sent))]

    def landed(names, got):
        slabs.update(zip(names, got))

    dh3, df2, loss_part, grads["ln_ffn2_post"] = loss_post_bwd(h2, f2, wts["ln_ffn2_post"], target, MACARON)
    dg2, du2 = _mm("ffn2_dact", [df2], [full["ffn2_down"]], "nt", [BF16] * 2, extras=[(g2_, 0), (u2, 0)],
                   epilogue=_swiglu_bwd_epi)
    grads["ffn2_down"], to_sibling["ffn2_down"] = _mm("ffn2_ddown", [act2], [df2], "tn", [F32, BF16], tm=512, tn=1024,
                                                      epilogue=_twice_epi)
    grads["ffn2_gate"], grads["ffn2_up"], to_sibling["ffn2_gate"], to_sibling["ffn2_up"] = _mm(
        "ffn2_dup", [dg2, du2], [n2], "tn", [F32, F32, BF16, BF16], tm=512, epilogue=_twice_epi)
    dn2 = _mm("ffn2_dn_gate", [dg2], [full["ffn2_gate"]], "nn", [F32], tm=512)[0]
    dn2 = _mm("ffn2_dn", [du2], [full["ffn2_up"]], "nn", [F32], tm=512, extras=[(dn2, 0)], epilogue=_add_epi)[0]
    sums2 = pair_sums(["ffn2_down", "ffn2_gate", "ffn2_up"])
    dh2, grads["ln_ffn2_pre"] = pre_bwd(dn2, h2, wts["ln_ffn2_pre"], dh3)
    dmx, grads["ln_mix_post"] = post_bwd(dh2, mx, wts["ln_mix_post"], 1.0)
    (dya, dyb, dga, dgb), got = _mm("dmix", [dmx], [full["w_out"]], "nt", [BF16] * 4, extras=gates + [(ya, 0), (yb, 0)],
                                    epilogue=_mix_bwd_epi, carry=chips_carry(sums2[:1]))
    landed(["ffn2_down"], got)
    grads["w_out"] = _mm("dw_out", [m], [dmx], "tn", [F32])[0]
    dya_in = _mm("dproj_a", [dya], [full["w_proj_a"]], "nn", [F32])[0]
    dyb_in = _mm("dproj_b", [dyb], [full["w_proj_b"]], "nn", [F32])[0]
    grads["w_proj_a"] = _mm("dw_proj_a", [dya], [ya_in], "tn", [F32])[0]
    grads["w_proj_b"] = _mm("dw_proj_b", [dyb], [yb_in], "tn", [F32])[0]
    dz_pool, grads["pool_w"], grads["pool_scale"] = pool_bwd(cols, dyb_in, pool_w, wts["pool_scale"])
    dy, dr_x, dk_x, dv_x, dgate, grads["rwkv_gn_w"], grads["rwkv_gn_b"], drk = rwkv_post_bwd(
        dya_in, y, r, kmod, v, gate, wts["rwkv_gn_w"], wts["rwkv_gn_b"], rk)
    grads["rwkv_r_k"] = drk.reshape(wts["rwkv_r_k"].shape)
    dr_s, ddecay, dk_s, dv_s, dneg, dbpos = wkv_bwd(r, decay, kmod, v, aneg, bpos, dy, states, sdota)
    (dzs, dzls, grads["rwkv_k_a"], grads["rwkv_k_k"], grads["rwkv_a0"], grads["rwkv_w0"], da2p, dw2p, dg2p) = rwkv_prep_bwd(
        p, cols, [dr_s, dr_x, ddecay, dk_s, dk_x, dv_s, dv_x, dneg, dbpos, dgate], *small)
    grads["rwkv_w2"], grads["rwkv_a2"], grads["rwkv_g2"] = dw2p[:LW], da2p[LW:LW + LA], dg2p[LW + LA:lat]
    dz_rkv, dz_lat, dmu_rkv, dmu_lat = shift_bwd(cols, dzs, dzls, mu_rkv, mu_lat)
    grads["rwkv_mu"] = jnp.concatenate([dmu_rkv, dmu_lat[:, :lat]], axis=1)
    dp = jnp.concatenate([dz_rkv, dz_pool, dga, dgb, dz_lat], axis=1)
    (dnm,), got = _mm("din_proj", [dp], [w_in_p], "nn", [F32], tm=512, carry=chips_carry(sums2[1:2]))
    landed(["ffn2_gate"], got)
    (dw_in_p,), got = _mm("dw_in", [dp], [nm], "tn", [F32], tm=512, tn=1024, carry=chips_carry(sums2[2:]))
    landed(["ffn2_up"], got)
    grads["w_in"] = jnp.concatenate([dw_in_p[:3 * W], dw_in_p[base:base + lat], dw_in_p[3 * W:base]], axis=0)
    sums_mix = pair_sums(["w_in"] + mixer)
    dh1, grads["ln_mix_pre"] = pre_bwd(dnm, h1, wts["ln_mix_pre"], dh2)
    df1, grads["ln_ffn1_post"] = post_bwd(dh1, f1, wts["ln_ffn1_post"], MACARON)
    (dg1, du1), got = _mm("ffn1_dact", [df1], [full["ffn1_down"]], "nt", [BF16] * 2, extras=[(g1, 0), (u1, 0)],
                          epilogue=_swiglu_bwd_epi, carry=chips_carry(sums_mix[:1]))
    landed(["w_in"], got)
    (grads["ffn1_down"], to_sibling["ffn1_down"]), got = _mm(
        "ffn1_ddown", [act1], [df1], "tn", [F32, BF16], tm=512, tn=1024, epilogue=_twice_epi,
        carry=chips_carry(sums_mix[1:]))
    landed(mixer, got)
    (grads["ffn1_gate"], grads["ffn1_up"], to_sibling["ffn1_gate"], to_sibling["ffn1_up"]), got = _mm(
        "ffn1_dup", [dg1, du1], [n1], "tn", [F32, F32, BF16, BF16], tm=512, epilogue=_twice_epi,
        carry=chips_carry(pair_sums(["ffn1_down"])))
    landed(["ffn1_down"], got)
    sums1 = pair_sums(["ffn1_gate", "ffn1_up"])
    (dn1,), got = _mm("ffn1_dn_gate", [dg1], [full["ffn1_gate"]], "nn", [F32], tm=512, carry=chips_carry(sums1[:1]))
    landed(["ffn1_gate"], got)
    (dn1,), got = _mm("ffn1_dn", [du1], [full["ffn1_up"]], "nn", [F32], tm=512, extras=[(dn1, 0)], epilogue=_add_epi,
                      carry=chips_carry(sums1[1:]))
    landed(["ffn1_up"], got)
    grad_x, grads["ln_ffn1_pre"] = pre_bwd(dn1, x, wts["ln_ffn1_pre"], dh1)

    flat = jnp.concatenate([grads[n].reshape(-1) for n in REPLICATED])
    n_small = flat.shape[0]
    rows = _round_up(n_small, 8 * LANES) // LANES
    flat = jnp.concatenate([flat, jnp.zeros((rows * LANES - n_small,), F32)]).reshape(rows, LANES)
    small_slabs = all_gather([flat])[0]

    def packed(prefix):
        vals = jnp.concatenate([args[prefix + n].reshape(-1) for n in REPLICATED])
        return jnp.concatenate([vals, jnp.ones((rows * LANES - n_small,), F32)]).reshape(rows, LANES)

    outs = {}
    small_out = adamw(packed(""), packed("m_"), packed("v_"), small_slabs)
    offset = 0
    for n in REPLICATED:
        size = args[n].size
        outs[n] = [o.reshape(-1)[offset:offset + size].reshape(args[n].shape) for o in small_out]
        offset += size
    for n in SHARDED:
        slab = jnp.swapaxes(slabs[n], 1, 2) if n in TRANSPOSED else slabs[n]
        shard2d = slab.shape[1:]
        res = adamw(*[args[pre + n].reshape(shard2d) for pre in ("", "m_", "v_")], slab)
        outs[n] = [o.reshape(args[n].shape) for o in res]

    loss = lax.psum(loss_part[0, 0], ("x", "y", "c"))
    return (loss, grad_x[None], *[outs[n][0] for n in WEIGHTS], *[outs[n][1] for n in WEIGHTS],
            *[outs[n][2] for n in WEIGHTS], *[outs[n][3] for n in WEIGHTS])


ARG_NAMES = ["x"] + WEIGHTS + ["loss_target"] + ["m_" + n for n in WEIGHTS] + ["v_" + n for n in WEIGHTS]


def kernel(x, ln_ffn1_pre, ln_ffn1_post, ffn1_gate, ffn1_up, ffn1_down, ln_mix_pre, ln_mix_post, w_in, rwkv_mu, rwkv_w0,
           rwkv_w2, rwkv_a0, rwkv_a2, rwkv_g2, rwkv_k_k, rwkv_k_a, rwkv_r_k, rwkv_gn_w, rwkv_gn_b, w_proj_a, pool_w,
           pool_scale, w_proj_b, w_out, ln_ffn2_pre, ln_ffn2_post, ffn2_gate, ffn2_up, ffn2_down, loss_target,
           m_ln_ffn1_pre, m_ln_ffn1_post, m_ffn1_gate, m_ffn1_up, m_ffn1_down, m_ln_mix_pre, m_ln_mix_post, m_w_in,
           m_rwkv_mu, m_rwkv_w0, m_rwkv_w2, m_rwkv_a0, m_rwkv_a2, m_rwkv_g2, m_rwkv_k_k, m_rwkv_k_a, m_rwkv_r_k,
           m_rwkv_gn_w, m_rwkv_gn_b, m_w_proj_a, m_pool_w, m_pool_scale, m_w_proj_b, m_w_out, m_ln_ffn2_pre,
           m_ln_ffn2_post, m_ffn2_gate, m_ffn2_up, m_ffn2_down, v_ln_ffn1_pre, v_ln_ffn1_post, v_ffn1_gate, v_ffn1_up,
           v_ffn1_down, v_ln_mix_pre, v_ln_mix_post, v_w_in, v_rwkv_mu, v_rwkv_w0, v_rwkv_w2, v_rwkv_a0, v_rwkv_a2,
           v_rwkv_g2, v_rwkv_k_k, v_rwkv_k_a, v_rwkv_r_k, v_rwkv_gn_w, v_rwkv_gn_b, v_w_proj_a, v_pool_w, v_pool_scale,
           v_w_proj_b, v_w_out, v_ln_ffn2_pre, v_ln_ffn2_post, v_ffn2_gate, v_ffn2_up, v_ffn2_down):
    given = locals()
    return _step({n: given[n] for n in ARG_NAMES})
```

```python
import jax
import jax.numpy as jnp
from jax import lax
from jax.experimental import pallas as pl
from jax.experimental.pallas import tpu as pltpu

F32, BF16 = jnp.float32, jnp.bfloat16
N_DEV = 8
N_CHIP = 4
HEAD = 64
LANES = 2 * HEAD
NORM_EPS, GN_EPS, L2_EPS = 1e-6, 64e-5, 1e-12
POOL_WINDOWS = (2, 4, 8, 16)
POOL_HALO = 16
MACARON = 0.5
ADAM_LR, ADAM_B1, ADAM_B2, ADAM_EPS, ADAM_WD, ADAM_STEP = 0.001, 0.9, 0.999, 1e-08, 0.01, 10
VMEM_LIMIT = 48 * 1024 * 1024
MM_VMEM_BUDGET = 40 * 1024 * 1024
ROW_TILE = 256
RWKV_ROW_TILE = 128
LAT_ALIGN = 512
WKV_CHUNK, WKV_PAIRS = 16, 8
WKV_CHUNK_FWD = 32
WKV_UNROLL = 4
WKV_MXU_PAIRS = 5
MESH = pl.DeviceIdType.MESH


def _pallas(body, **kw):
    return pl.pallas_call(body, **kw)


def _params(*sem):
    return pltpu.CompilerParams(dimension_semantics=sem, vmem_limit_bytes=VMEM_LIMIT)


def _tile(n, target, align=128):
    best = None
    for d in range(align, min(n, target) + 1, align):
        if n % d == 0:
            best = d
    return best if best is not None else n


def _round_up(n, m):
    return (n + m - 1) // m * m


ANY = pl.BlockSpec(memory_space=pl.ANY)


def _mm(name, a_list, b_list, mode, out_dtypes, *, sum_pairs=False, extras=(), epilogue=None, tm=1024, tn=512,
        carry=None):
    n_a, n_b = len(a_list), len(b_list)
    n_prod = max(n_a, n_b)
    assert n_a in (1, n_prod) and n_b in (1, n_prod)
    a0, b0 = a_list[0], b_list[0]
    if mode == "nn":
        (M, K), N = a0.shape, b0.shape[1]
    elif mode == "nt":
        (M, K), N = a0.shape, b0.shape[0]
    else:
        (K, M), N = a0.shape, b0.shape[1]
    tm, tn = _tile(M, tm), _tile(N, tn)
    n_acc = 1 if sum_pairs else n_prod
    n_ex = len(extras)

    def planned(tk):
        operands = 2 * 2 * tk * (n_a * tm + n_b * tn)
        tiles = 2 * tm * tn * (sum(e.dtype.itemsize for e, _ in extras) + sum(jnp.dtype(d).itemsize for d in out_dtypes))
        return operands + tiles + 4 * tm * tn * (n_acc + len(out_dtypes))

    tk = max([d for d in range(128, K + 1, 128) if K % d == 0 and planned(d) <= MM_VMEM_BUDGET] or [_tile(K, 512)])
    nk = K // tk
    if mode == "tn":
        a_spec = pl.BlockSpec((tk, tm), lambda i, j, k: (k, i))
    else:
        a_spec = pl.BlockSpec((tm, tk), lambda i, j, k: (i, k))
    if mode == "nt":
        b_spec = pl.BlockSpec((tn, tk), lambda i, j, k: (j, k))
    else:
        b_spec = pl.BlockSpec((tk, tn), lambda i, j, k: (k, j))
    contract = {"nn": ((1,), (0,)), "nt": ((1,), (1,)), "tn": ((0,), (0,))}[mode]
    e_specs = []
    for _, col in extras:
        assert col % tn == 0
        e_specs.append(pl.BlockSpec((tm, tn), lambda i, j, k, off=col // tn: (i, j + off)))
    o_spec = pl.BlockSpec((tm, tn), lambda i, j, k: (i, j))

    n_in, n_out, n_scr = n_a + n_b + n_ex, len(out_dtypes), (n_acc if nk > 1 else 0)
    c_in, c_out = (len(carry.inputs), len(carry.out_shapes)) if carry else (0, 0)
    grid = (M // tm, N // tn, nk)

    def body(*refs):
        a_refs, b_refs, e_refs = refs[:n_a], refs[n_a:n_a + n_b], refs[n_a + n_b:n_in]
        o_refs = refs[n_in + c_in:n_in + c_in + n_out]
        acc_refs = refs[n_in + c_in + n_out + c_out:n_in + c_in + n_out + c_out + n_scr]
        carried = (refs[n_in:n_in + c_in], refs[n_in + c_in + n_out:n_in + c_in + n_out + c_out],
                   refs[n_in + c_in + n_out + c_out + n_scr:])
        at = [pl.program_id(d) for d in range(3)]

        if carry:
            @pl.when((at[0] == 0) & (at[1] == 0) & (at[2] == 0))
            def _():
                carry.start(*carried)

        def products():
            a_vals, b_vals = [a[...] for a in a_refs], [b[...] for b in b_refs]
            prods = [lax.dot_general(a_vals[p if n_a > 1 else 0], b_vals[p if n_b > 1 else 0], (contract, ((), ())),
                                     preferred_element_type=F32) for p in range(n_prod)]
            return [sum(prods[1:], prods[0])] if sum_pairs else prods

        def finish(results):
            outs = epilogue(results, [e[...] for e in e_refs]) if epilogue else results
            for o_ref, o in zip(o_refs, outs):
                o_ref[...] = o.astype(o_ref.dtype)

        if nk == 1:
            finish(products())
        else:
            @pl.when(at[2] == 0)
            def _():
                for acc in acc_refs:
                    acc[...] = jnp.zeros_like(acc)

            for acc, prod in zip(acc_refs, products()):
                acc[...] += prod

            @pl.when(at[2] == nk - 1)
            def _():
                finish([acc[...] for acc in acc_refs])

        if carry:
            @pl.when((at[0] == grid[0] - 1) & (at[1] == grid[1] - 1) & (at[2] == grid[2] - 1))
            def _():
                carry.finish(*carried)

    res = _pallas(
        body, name=name, grid=grid,
        in_specs=[a_spec] * n_a + [b_spec] * n_b + e_specs + [ANY] * c_in,
        out_specs=[o_spec] * n_out + [ANY] * c_out,
        out_shape=[jax.ShapeDtypeStruct((M, N), dt) for dt in out_dtypes] + (list(carry.out_shapes) if carry else []),
        scratch_shapes=[pltpu.VMEM((tm, tn), F32)] * n_scr + (list(carry.scratch) if carry else []),
        compiler_params=_params("arbitrary", "arbitrary", "arbitrary") if carry else _params("parallel", "parallel", "arbitrary"),
    )(*a_list, *b_list, *[e for e, _ in extras], *(carry.inputs if carry else []))
    return (res[:n_out], res[n_out:]) if carry else res


def _swiglu_fwd_epi(accs, _):
    g, u = accs
    return [g, u, g * jax.nn.sigmoid(g) * u]


def _swiglu_bwd_epi(accs, ex):
    dact = accs[0]
    g, u = ex[0].astype(F32), ex[1].astype(F32)
    sg = jax.nn.sigmoid(g)
    return [dact * u * (sg * (1.0 + g * (1.0 - sg))), dact * (g * sg)]


def _add_epi(accs, ex):
    return [accs[0] + ex[0]]


def _twice_epi(accs, _):
    return list(accs) + list(accs)


def _mix_fwd_epi(accs, ex):
    ya, yb = accs
    return [jax.nn.sigmoid(ex[0]) * ya + jax.nn.sigmoid(ex[1]) * yb, ya, yb]


def _mix_bwd_epi(accs, ex):
    dm = accs[0]
    sa, sb = jax.nn.sigmoid(ex[0]), jax.nn.sigmoid(ex[1])
    ya, yb = ex[2].astype(F32), ex[3].astype(F32)
    return [dm * sa, dm * sb, dm * ya * sa * (1.0 - sa), dm * yb * sb * (1.0 - sb)]


def _row_call(name, body, T, tiled, params, outs, accs=(), prev=(), nxt=(), halo=8, tile=ROW_TILE):
    tm = min(tile, T)
    n_tiles = T // tm

    def norm(e):
        return e if isinstance(e, tuple) else (e, e.shape[1], 0)

    tiled, prev, nxt = [norm(e) for e in tiled], [norm(e) for e in prev], [norm(e) for e in nxt]
    per_halo, n_halo = tm // halo, T // halo
    in_specs = [pl.BlockSpec((tm, w), lambda i, cb=cb: (i, cb)) for _, w, cb in tiled]
    in_specs += [pl.BlockSpec((halo, w), lambda i, cb=cb: (jnp.maximum(i * per_halo - 1, 0), cb)) for _, w, cb in prev]
    in_specs += [pl.BlockSpec((halo, w), lambda i, cb=cb: (jnp.minimum((i + 1) * per_halo, n_halo - 1), cb))
                 for _, w, cb in nxt]
    in_specs += [pl.BlockSpec(p.shape, lambda i, nd=p.ndim: (0,) * nd) for p in params]
    out_specs = [pl.BlockSpec((tm, o.shape[1]), lambda i: (i, 0)) for o in outs]
    out_specs += [pl.BlockSpec(a.shape, lambda i, nd=len(a.shape): (0,) * nd) for a in accs]
    n1, n2, n3, n4, n5 = len(tiled), len(prev), len(nxt), len(params), len(outs)

    def kernel_body(*refs):
        i = pl.program_id(0)
        acc_refs = refs[n1 + n2 + n3 + n4 + n5:]

        @pl.when(i == 0)
        def _():
            for a in acc_refs:
                a[...] = jnp.zeros_like(a)

        body(i, n_tiles, refs[:n1], refs[n1:n1 + n2], refs[n1 + n2:n1 + n2 + n3],
             refs[n1 + n2 + n3:n1 + n2 + n3 + n4], refs[n1 + n2 + n3 + n4:n1 + n2 + n3 + n4 + n5], acc_refs)

    return _pallas(
        kernel_body, name=name, grid=(n_tiles,), in_specs=in_specs, out_specs=out_specs,
        out_shape=list(outs) + list(accs),
        compiler_params=_params("arbitrary"),
    )(*[e[0] for e in tiled + prev + nxt], *params)


def _sds(shape, dtype=F32):
    return jax.ShapeDtypeStruct(tuple(shape), dtype)


def _rstd(x):
    return lax.rsqrt(jnp.mean(x * x, axis=-1, keepdims=True) + NORM_EPS)


def _colsum(x):
    return jnp.sum(x, axis=0, keepdims=True)


def rms_pre(x, g):
    T, D = x.shape

    def body(i, n, tiled, prev, nxt, params, outs, accs):
        xv = tiled[0][...]
        outs[0][...] = (xv * _rstd(xv) * params[0][...]).astype(BF16)

    return _row_call("rms_pre", body, T, [x], [g], [_sds((T, D), BF16)])[0]


def post_pre(h, f, g_post, g_pre, scale):
    T, D = h.shape

    def body(i, n, tiled, prev, nxt, params, outs, accs):
        hv, fv = tiled[0][...], tiled[1][...]
        h2 = hv + scale * (fv * _rstd(fv) * params[0][...])
        outs[0][...] = h2
        outs[1][...] = (h2 * _rstd(h2) * params[1][...]).astype(BF16)

    return _row_call("post_pre", body, T, [h, f], [g_post, g_pre], [_sds((T, D)), _sds((T, D), BF16)])


def _post_bwd_math(dh, fv, g, scale):
    r = _rstd(fv)
    fhat = fv * r
    dy = scale * dh
    z = dy * g
    df = r * (z - fhat * jnp.mean(z * fhat, axis=-1, keepdims=True))
    return df, _colsum(dy * fhat)


def loss_post_bwd(h, f, g_post, target, scale):
    T, D = h.shape

    def body(i, n, tiled, prev, nxt, params, outs, accs):
        hv, fv, tv = tiled[0][...], tiled[1][...], tiled[2][...]
        g = params[0][...]
        e = hv + scale * (fv * _rstd(fv) * g) - tv
        accs[0][...] += jnp.full(accs[0].shape, 0.5 / D, F32) * jnp.sum(e * e)
        dh = e * (1.0 / D)
        outs[0][...] = dh
        df, dg = _post_bwd_math(dh, fv, g, scale)
        outs[1][...] = df.astype(BF16)
        accs[1][...] += dg

    return _row_call("loss_post_bwd", body, T, [h, f, target], [g_post],
                     [_sds((T, D)), _sds((T, D), BF16)], [_sds((1, LANES)), _sds((1, D))])


def post_bwd(dh, f, g_post, scale):
    T, D = dh.shape

    def body(i, n, tiled, prev, nxt, params, outs, accs):
        df, dg = _post_bwd_math(tiled[0][...], tiled[1][...], params[0][...], scale)
        outs[0][...] = df.astype(BF16)
        accs[0][...] += dg

    return _row_call("post_bwd", body, T, [dh, f], [g_post], [_sds((T, D), BF16)], [_sds((1, D))])


def pre_bwd(dn, h, g_pre, dres):
    T, D = h.shape

    def body(i, n, tiled, prev, nxt, params, outs, accs):
        dnv, hv = tiled[0][...], tiled[1][...]
        r = _rstd(hv)
        hhat = hv * r
        z = dnv * params[0][...]
        outs[0][...] = tiled[2][...] + r * (z - hhat * jnp.mean(z * hhat, axis=-1, keepdims=True))
        accs[0][...] += _colsum(dnv * hhat)

    return _row_call("pre_bwd", body, T, [dn, h, dres], [g_pre], [_sds((T, D))], [_sds((1, D))])


def _head_ones():
    i = lax.broadcasted_iota(jnp.int32, (LANES, LANES), 0)
    j = lax.broadcasted_iota(jnp.int32, (LANES, LANES), 1)
    return jnp.where((i < HEAD) == (j < HEAD), 1.0, 0.0).astype(F32)


def _headsum(x):
    e = _head_ones()
    parts = [jnp.dot(x[:, s:s + LANES], e, precision=lax.Precision.HIGHEST, preferred_element_type=F32)
             for s in range(0, x.shape[1], LANES)]
    return parts[0] if len(parts) == 1 else jnp.concatenate(parts, axis=1)


def _shift_down(x, before):
    row = lax.broadcasted_iota(jnp.int32, x.shape, 0)
    return jnp.where(row == 0, before, pltpu.roll(x, 1, 0))


def _shift_up(x, after):
    row = lax.broadcasted_iota(jnp.int32, x.shape, 0)
    return jnp.where(row == x.shape[0] - 1, after, pltpu.roll(x, x.shape[0] - 1, 0))


def _last_row(ref, keep):
    r = ref[ref.shape[0] - 1:ref.shape[0], :]
    return jnp.where(keep, r, jnp.zeros_like(r))


def _first_row(ref, keep):
    r = ref[0:1, :]
    return jnp.where(keep, r, jnp.zeros_like(r))


def _softplus(u):
    return jnp.maximum(u, 0.0) + jnp.log(1.0 + jnp.exp(-jnp.abs(u)))


def _dotb(a, b, contract):
    return lax.dot_general(a.astype(BF16), b.astype(BF16), (contract, ((), ())), preferred_element_type=F32)


_NN, _NT, _TN = ((1,), (0,)), ((1,), (1,)), ((0,), (0,))


def _prep_forward(z, zprev_row, zl, zlprev_row, mu, mul, w0, a0, kk_w, ka_w, w2p, a2p, g2p):
    W = w0.shape[1]
    zs = z + (_shift_down(z, zprev_row) - z) * mu
    zls = zl + (_shift_down(zl, zlprev_row) - zl) * mul
    r, k, v = zs[:, :W], zs[:, W:2 * W], zs[:, 2 * W:]
    th, sg = jnp.tanh(zls), jax.nn.sigmoid(zls)
    xw = w0 + _dotb(th, w2p, _NN)
    wlog = -_softplus(-xw) - 0.5
    ew = jnp.exp(wlog)
    decay = jnp.exp(-ew)
    a = jax.nn.sigmoid(a0 + _dotb(zls, a2p, _NN))
    gate = _dotb(sg, g2p, _NN)
    q = k * kk_w
    nrm = jnp.sqrt(_headsum(q * q))
    den = jnp.maximum(nrm, L2_EPS)
    kk = q / den
    kmod = k * (1.0 + (a - 1.0) * ka_w)
    return dict(zs=zs, zls=zls, r=r, k=k, v=v, th=th, sg=sg, xw=xw, ew=ew, decay=decay, a=a, gate=gate,
                nrm=nrm, den=den, kk=kk, kmod=kmod)


def rwkv_prep(p, cols, mu, mul, w0, a0, kk_w, ka_w, w2p, a2p, g2p):
    T = p.shape[0]
    W = w0.shape[1]

    def body(i, n, tiled, prev, nxt, params, outs, accs):
        c = _prep_forward(tiled[0][...], _last_row(prev[0], i > 0), tiled[1][...], _last_row(prev[1], i > 0),
                          *[q[...] for q in params])
        for o, val in zip(outs, (c["r"], c["decay"], c["kmod"], c["v"], -c["kk"], c["kk"] * c["a"], c["gate"])):
            o[...] = val

    return _row_call("rwkv_prep", body, T, [cols["rkv"], cols["lat"]],
                     [mu, mul, w0, a0, kk_w, ka_w, w2p, a2p, g2p], [_sds((T, W))] * 7,
                     prev=[cols["rkv"], cols["lat"]], tile=RWKV_ROW_TILE)


def _post_forward(y, r, kmod, v, gn_w, gn_b, rk):
    mean = _headsum(y) * (1.0 / HEAD)
    yc = y - mean
    rstd = lax.rsqrt(_headsum(yc * yc) * (1.0 / HEAD) + GN_EPS)
    yn = yc * rstd
    s = _headsum(r * kmod * rk)
    return yn, rstd, s, yn * gn_w + gn_b + s * v


def rwkv_post(y, r, kmod, v, gate, gn_w, gn_b, rk):
    T, W = y.shape

    def body(i, n, tiled, prev, nxt, params, outs, accs):
        yv, rv, kv, vv, gv = [t[...] for t in tiled]
        _, _, _, o = _post_forward(yv, rv, kv, vv, *[q[...] for q in params])
        outs[0][...] = (o * gv).astype(BF16)

    return _row_call("rwkv_post", body, T, [y, r, kmod, v, gate], [gn_w, gn_b, rk], [_sds((T, W), BF16)],
                     tile=RWKV_ROW_TILE)[0]


def rwkv_post_bwd(dout, y, r, kmod, v, gate, gn_w, gn_b, rk):
    T, W = y.shape

    def body(i, n, tiled, prev, nxt, params, outs, accs):
        dv_, yv, rv, kv, vv, gv = [t[...] for t in tiled]
        gn_w_, gn_b_, rk_ = [q[...] for q in params]
        yn, rstd, s, o = _post_forward(yv, rv, kv, vv, gn_w_, gn_b_, rk_)
        do = dv_ * gv
        outs[4][...] = dv_ * o
        accs[0][...] += _colsum(do * yn)
        accs[1][...] += _colsum(do)
        dyn = do * gn_w_
        outs[0][...] = rstd * (dyn - _headsum(dyn) * (1.0 / HEAD) - yn * (_headsum(dyn * yn) * (1.0 / HEAD)))
        ds = _headsum(do * vv)
        outs[1][...] = ds * kv * rk_
        outs[2][...] = ds * rv * rk_
        outs[3][...] = do * s
        accs[2][...] += _colsum(ds * rv * kv)

    return _row_call("rwkv_post_bwd", body, T, [dout, y, r, kmod, v, gate], [gn_w, gn_b, rk],
                     [_sds((T, W))] * 5, [_sds((1, W))] * 3, tile=RWKV_ROW_TILE)


def rwkv_prep_bwd(p, cols, grads, mu, mul, w0, a0, kk_w, ka_w, w2p, a2p, g2p):
    T = p.shape[0]
    W = w0.shape[1]
    latp = w2p.shape[0]

    def body(i, n, tiled, prev, nxt, params, outs, accs):
        pv = [q[...] for q in params]
        mu_, mul_, w0_, a0_, kk_w_, ka_w_, w2p_, a2p_, g2p_ = pv
        c = _prep_forward(tiled[0][...], _last_row(prev[0], i > 0), tiled[1][...], _last_row(prev[1], i > 0), *pv)
        dr_s, dr_x, ddecay, dk_s, dk_x, dv_s, dv_x, dneg, db, dgate = [t[...] for t in tiled[2:]]
        k, a, kk = c["k"], c["a"], c["kk"]
        dkmod = dk_s + dk_x
        dk = dkmod * (1.0 + (a - 1.0) * ka_w_)
        da = dkmod * k * ka_w_ + db * kk
        accs[0][...] += _colsum(dkmod * k * (a - 1.0))
        dkk = db * a - dneg
        dq = jnp.where(c["nrm"] > L2_EPS, dkk - kk * _headsum(dkk * kk), dkk) / c["den"]
        dk = dk + dq * kk_w_
        accs[1][...] += _colsum(dq * k)
        dxa = da * a * (1.0 - a)
        accs[2][...] += _colsum(dxa)
        accs[4][...] += _dotb(c["zls"], dxa, _TN)
        dzls = _dotb(dxa, a2p_, _NT)
        dxw = (-ddecay * c["decay"] * c["ew"]) * jax.nn.sigmoid(-c["xw"])
        accs[3][...] += _colsum(dxw)
        accs[5][...] += _dotb(c["th"], dxw, _TN)
        dzls = dzls + _dotb(dxw, w2p_, _NT) * (1.0 - c["th"] * c["th"])
        accs[6][...] += _dotb(c["sg"], dgate, _TN)
        dzls = dzls + _dotb(dgate, g2p_, _NT) * c["sg"] * (1.0 - c["sg"])
        outs[0][...] = jnp.concatenate([dr_s + dr_x, dk, dv_s + dv_x], axis=1)
        outs[1][...] = dzls

    return _row_call("rwkv_prep_bwd", body, T, [cols["rkv"], cols["lat"]] + list(grads),
                     [mu, mul, w0, a0, kk_w, ka_w, w2p, a2p, g2p], [_sds((T, 3 * W)), _sds((T, latp))],
                     [_sds((1, W))] * 4 + [_sds((latp, W))] * 3, prev=[cols["rkv"], cols["lat"]], tile=RWKV_ROW_TILE)


def shift_bwd(cols, dzs, dzls, mu, mul):
    T = dzs.shape[0]

    def body(i, n, tiled, prev, nxt, params, outs, accs):
        for j in range(2):
            z, d, m = tiled[j][...], tiled[2 + j][...], params[j][...]
            zprev = _shift_down(z, _last_row(prev[j], i > 0))
            dnext = _shift_up(d, _first_row(nxt[j], i < n - 1))
            outs[j][...] = (d * (1.0 - m) + dnext * m).astype(BF16)
            accs[j][...] += _colsum(d * (zprev - z))

    return _row_call("shift_bwd", body, T, [cols["rkv"], cols["lat"], dzs, dzls], [mu, mul],
                     [_sds(dzs.shape, BF16), _sds(dzls.shape, BF16)], [_sds(mu.shape), _sds(mul.shape)],
                     prev=[cols["rkv"], cols["lat"]], nxt=[dzs, dzls])


def _window_pick(x, windows):
    gid = lax.broadcasted_iota(jnp.int32, x.shape, 1) // (x.shape[1] // len(windows))
    out = windows[-1]
    for g in range(len(windows) - 2, -1, -1):
        out = jnp.where(gid == g, windows[g], out)
    return out


def _pool_counts(t0, rows, width):
    t = (t0 + lax.broadcasted_iota(jnp.int32, (rows, width), 0) + 1).astype(F32)
    return _window_pick(t, [jnp.minimum(t, float(w)) for w in POOL_WINDOWS])


def _pool_mixed(x, before, t0):
    tm, width = x.shape
    xe = jnp.concatenate([before, x], axis=0)
    sums, s, span = [], xe, 1
    for w in POOL_WINDOWS:
        while span < w:
            s = s + pltpu.roll(s, span, 0)
            span *= 2
        sums.append(s[POOL_HALO:, :])
    return _window_pick(x, sums) / _pool_counts(t0, tm, width) - x


def _group_dot(x, w_ref, contract):
    gd = w_ref.shape[-1]
    parts = [_dotb(x[:, g * gd:(g + 1) * gd], w_ref[g], contract) for g in range(w_ref.shape[0])]
    return jnp.concatenate(parts, axis=1)


def pool_fwd(cols, pool_w, pool_scale):
    T, width = cols["pool"][0].shape[0], cols["pool"][1]
    tm = min(ROW_TILE, T)

    def body(i, n, tiled, prev, nxt, params, outs, accs):
        before = jnp.where(i > 0, prev[0][...], 0.0)
        mixed = _pool_mixed(tiled[0][...], before, i * tm)
        outs[0][...] = (_group_dot(mixed, params[0], _NN) * params[1][...]).astype(BF16)

    return _row_call("pool_fwd", body, T, [cols["pool"]], [pool_w, pool_scale], [_sds((T, width), BF16)],
                     prev=[cols["pool"]], halo=POOL_HALO)[0]


def pool_bwd(cols, dout, pool_w, pool_scale):
    T, width = dout.shape
    tm = min(ROW_TILE, T)

    def body(i, n, tiled, prev, nxt, params, outs, accs):
        w_ref, scale = params[0], params[1][...]
        before = jnp.where(i > 0, prev[0][...], 0.0)
        mixed = _pool_mixed(tiled[0][...], before, i * tm)
        dv = tiled[1][...]
        accs[1][...] += _colsum(dv * _group_dot(mixed, w_ref, _NN))
        after = jnp.where(i < n - 1, nxt[0][...], 0.0)
        dys = jnp.concatenate([dv, after], axis=0) * scale
        gd = w_ref.shape[-1]
        for g in range(w_ref.shape[0]):
            accs[0][g] += _dotb(mixed[:, g * gd:(g + 1) * gd], dys[:tm, g * gd:(g + 1) * gd], _TN)
        dmixed = _group_dot(dys, w_ref, _NT)
        u = dmixed / _pool_counts(i * tm, tm + POOL_HALO, width)
        rows = tm + POOL_HALO
        sums, s, span = [], u, 1
        for w in POOL_WINDOWS:
            while span < w:
                s = s + pltpu.roll(s, rows - span, 0)
                span *= 2
            sums.append(s[:tm, :])
        outs[0][...] = (_window_pick(dv, sums) - dmixed[:tm, :]).astype(BF16)

    return _row_call("pool_bwd", body, T, [cols["pool"], dout], [pool_w, pool_scale], [_sds((T, width), BF16)],
                     [_sds(pool_w.shape), _sds((1, width))], prev=[cols["pool"]], nxt=[dout], halo=POOL_HALO)


def _wkv_consts(pairs):
    lane = lax.broadcasted_iota(jnp.int32, (HEAD, LANES), 1)
    sub = lax.broadcasted_iota(jnp.int32, (pairs * HEAD, LANES), 0)
    lane_all = lax.broadcasted_iota(jnp.int32, (pairs * HEAD, LANES), 1)
    i = lax.broadcasted_iota(jnp.int32, (LANES, LANES), 0)
    j = lax.broadcasted_iota(jnp.int32, (LANES, LANES), 1)
    ones = jnp.where((i < HEAD) == (j < HEAD), 1.0, 0.0).astype(BF16)
    diag = jnp.where((lane_all & (HEAD - 1)) == (sub & (HEAD - 1)), 1.0, 0.0).astype(F32)
    return lane < HEAD, diag, ones


def _segsum(p, in_a):
    sa = jnp.sum(jnp.where(in_a, p, 0.0), axis=1, keepdims=True)
    sb = jnp.sum(jnp.where(in_a, 0.0, p), axis=1, keepdims=True)
    return jnp.where(in_a, sa, sb)


def _hi_lo(p):
    hi = lax.bitcast_convert_type(lax.bitcast_convert_type(p, jnp.uint32) & jnp.uint32(0xFFFF0000), F32)
    return hi, p - hi


def _segsum_mxu(p, ones):
    hi, lo = _hi_lo(p)
    return (jnp.dot(hi.astype(BF16), ones, preferred_element_type=F32)
            + jnp.dot(lo.astype(BF16), ones, preferred_element_type=F32))


def _cat(parts, axis):
    return parts[0] if len(parts) == 1 else jnp.concatenate(parts, axis=axis)


def _tile_rows(row, pairs):
    return _cat([jnp.broadcast_to(row[:, p * LANES:(p + 1) * LANES], (HEAD, LANES)) for p in range(pairs)], 0)


def _spread(row, pairs, diag16, ones):
    hi, lo = _hi_lo(row)
    return (jnp.dot(_tile_rows(hi.astype(BF16), pairs) * diag16, ones, preferred_element_type=F32)
            + jnp.dot(_tile_rows(lo.astype(BF16), pairs) * diag16, ones, preferred_element_type=F32))


def _pair_colsums(x, pairs):
    return _cat([_colsum(x[p * HEAD:(p + 1) * HEAD]) for p in range(pairs)], 1)


def _spread_split(row, pairs, in_a, diag, diag16, ones):
    n_mxu = min(WKV_MXU_PAIRS, pairs)
    parts = [_spread(row[:, :n_mxu * LANES], n_mxu, diag16[:n_mxu * HEAD], ones)]
    parts += [_segsum(row[:, p * LANES:(p + 1) * LANES] * diag[:HEAD], in_a) for p in range(n_mxu, pairs)]
    return _cat(parts, 0)


def _segsum_split(x, pairs, in_a, ones):
    n_mxu = min(WKV_MXU_PAIRS, pairs)
    parts = [_segsum_mxu(x[:n_mxu * HEAD], ones)]
    parts += [_segsum(x[p * HEAD:(p + 1) * HEAD], in_a) for p in range(n_mxu, pairs)]
    return _cat(parts, 0)


def wkv_fwd(r, w, k, v, a, b, carry=None):
    T, W = r.shape
    P = W // LANES
    PB = min(WKV_PAIRS, P)
    chunk = min(WKV_CHUNK_FWD, T)
    NC = T // chunk
    R = PB * HEAD
    ahead = min(WKV_UNROLL, chunk // 2)
    c_in, c_out = (len(carry.inputs), len(carry.out_shapes)) if carry else (0, 0)

    def body(*refs):
        r_ref, w_ref, k_ref, v_ref, a_ref, b_ref = refs[:6]
        y_ref, st_ref, sa_ref = refs[6 + c_in:9 + c_in]
        vt_ref, s_ref = refs[9 + c_in + c_out:11 + c_in + c_out]
        carried = refs[6:6 + c_in], refs[9 + c_in:9 + c_in + c_out], refs[11 + c_in + c_out:]
        g, c = pl.program_id(0), pl.program_id(1)

        if carry:
            @pl.when((g == 0) & (c == 0))
            def _():
                carry.start(*carried)

        @pl.when(c == 0)
        def _():
            s_ref[...] = jnp.zeros_like(s_ref)

        in_a, diag, ones = _wkv_consts(PB)
        diag16 = diag.astype(BF16)

        def step(t, _):
            rows = [ref[pl.ds(t, 1), :] for ref in (w_ref, k_ref, a_ref, b_ref)]
            for p in range(PB):
                wt, kt, at, bt = [x[:, p * LANES:(p + 1) * LANES] for x in rows]
                rs = pl.ds(p * HEAD, HEAD)
                S = s_ref[rs]
                sa = _segsum(S * at, in_a)
                sa_ref[t, rs] = sa
                S = S * wt + sa * bt + vt_ref[t, rs] * kt
                st_ref[t, rs] = S
                s_ref[rs] = S
            return 0

        def spread_step(t, _):
            vt_ref[t + ahead] = _spread(v_ref[pl.ds(t + ahead, 1), :], PB, diag16, ones)
            return step(t, 0)

        for t in range(ahead):
            vt_ref[t] = _spread_split(v_ref[t:t + 1, :], PB, in_a, diag, diag16, ones)
        lax.fori_loop(0, chunk - ahead, spread_step, 0, unroll=WKV_UNROLL)
        lax.fori_loop(chunk - ahead, chunk, step, 0, unroll=WKV_UNROLL)

        def readout(t, _):
            yt = _segsum_split(st_ref[t] * _tile_rows(r_ref[pl.ds(t, 1), :], PB), PB, in_a, ones) * diag
            y_ref[pl.ds(t, 1), :] = _pair_colsums(yt, PB)
            return 0

        lax.fori_loop(0, chunk, readout, 0, unroll=WKV_UNROLL)

        if carry:
            @pl.when((g == P // PB - 1) & (c == NC - 1))
            def _():
                carry.finish(*carried)

    spec = pl.BlockSpec((chunk, PB * LANES), lambda g, c: (c, g))
    tiles = pl.BlockSpec((chunk, R, LANES), lambda g, c: (c, g, 0))
    res = _pallas(
        body, name="wkv_fwd", grid=(P // PB, NC), in_specs=[spec] * 6 + [ANY] * c_in,
        out_specs=[spec, tiles, tiles] + [ANY] * c_out,
        out_shape=[_sds((T, W)), _sds((T, P * HEAD, LANES)), _sds((T, P * HEAD, LANES))]
        + (list(carry.out_shapes) if carry else []),
        scratch_shapes=[pltpu.VMEM((chunk, R, LANES), F32), pltpu.VMEM((R, LANES), F32)]
        + (list(carry.scratch) if carry else []),
        compiler_params=_params("arbitrary", "arbitrary") if carry else _params("parallel", "arbitrary"),
    )(r, w, k, v, a, b, *(carry.inputs if carry else []))
    return (res[:3], res[3:]) if carry else res


def wkv_bwd(r, w, k, v, a, b, dy, st, sa):
    T, W = r.shape
    P = W // LANES
    PB = min(WKV_PAIRS, P)
    chunk = min(WKV_CHUNK, T)
    NC = T // chunk
    R = PB * HEAD
    ahead = min(WKV_UNROLL, chunk // 2)

    def body(r_ref, w_ref, k_ref, v_ref, a_ref, b_ref, dy_ref, st_ref, before_ref, sa_ref,
             dr_ref, dw_ref, dk_ref, dv_ref, da_ref, db_ref, ds_ref, dyt_ref, dst_ref, dsa_ref):
        c = pl.program_id(1)

        @pl.when(c == 0)
        def _():
            ds_ref[...] = jnp.zeros_like(ds_ref)

        in_a, diag, ones = _wkv_consts(PB)
        diag16 = diag.astype(BF16)

        def bstep(n, _):
            t = chunk - 1 - n
            rows = [ref[pl.ds(t, 1), :] for ref in (r_ref, w_ref, a_ref, b_ref)]
            for p in range(PB):
                rt, wt, at, bt = [x[:, p * LANES:(p + 1) * LANES] for x in rows]
                rs = pl.ds(p * HEAD, HEAD)
                dS = ds_ref[rs] + dyt_ref[t, rs] * rt
                dst_ref[t, rs] = dS
                dsa = _segsum(dS * bt, in_a)
                dsa_ref[t, rs] = dsa
                ds_ref[rs] = dS * wt + dsa * at
            return 0

        def spread_bstep(n, _):
            t = chunk - 1 - ahead - n
            dyt_ref[t] = _spread(dy_ref[pl.ds(t, 1), :], PB, diag16, ones)
            return bstep(n, 0)

        for t in range(chunk - ahead, chunk):
            dyt_ref[t] = _spread_split(dy_ref[t:t + 1, :], PB, in_a, diag, diag16, ones)
        lax.fori_loop(0, chunk - ahead, spread_bstep, 0, unroll=WKV_UNROLL)
        lax.fori_loop(chunk - ahead, chunk, bstep, 0, unroll=WKV_UNROLL)

        def collect(t, _):
            sn, dS, dsa = st_ref[t], dst_ref[t], dsa_ref[t]
            sp = st_ref[jnp.maximum(t - 1, 0)]
            dvt = _segsum_split(dS * _tile_rows(k_ref[pl.ds(t, 1), :], PB), PB, in_a, ones) * diag
            vt = _spread_split(v_ref[pl.ds(t, 1), :], PB, in_a, diag, diag16, ones)
            for ref, val in ((dr_ref, sn * dyt_ref[t]), (dw_ref, dS * sp), (dk_ref, dS * vt), (dv_ref, dvt),
                             (da_ref, sp * dsa), (db_ref, dS * sa_ref[t])):
                ref[pl.ds(t, 1), :] = _pair_colsums(val, PB)
            return 0

        lax.fori_loop(0, chunk, collect, 0, unroll=WKV_UNROLL)
        first = jnp.where(c == NC - 1, 0.0, before_ref[0])
        dw_ref[0:1, :] = _pair_colsums(dst_ref[0] * first, PB)
        da_ref[0:1, :] = _pair_colsums(first * dsa_ref[0], PB)

    spec = pl.BlockSpec((chunk, PB * LANES), lambda g, c: (NC - 1 - c, g))
    tiles = pl.BlockSpec((chunk, R, LANES), lambda g, c: (NC - 1 - c, g, 0))
    before = pl.BlockSpec((1, R, LANES), lambda g, c: (jnp.maximum((NC - 1 - c) * chunk - 1, 0), g, 0))

    def scratch(n):
        return pltpu.VMEM((n, R, LANES), F32)

    return _pallas(
        body, name="wkv_bwd", grid=(P // PB, NC), in_specs=[spec] * 7 + [tiles, before, tiles],
        out_specs=[spec] * 6, out_shape=[_sds((T, W))] * 6,
        scratch_shapes=[pltpu.VMEM((R, LANES), F32), scratch(chunk), scratch(chunk), scratch(chunk)],
        compiler_params=_params("parallel", "arbitrary"),
    )(r, w, k, v, a, b, dy, st, st, sa)


def _position():
    return lax.axis_index("x"), lax.axis_index("y"), lax.axis_index("c")


def _other_chips(x, y):
    return [(1 - x, y), (x, 1 - y), (1 - x, 1 - y)]


class _Carry:
    def __init__(self, inputs, out_shapes, scratch, start, finish):
        self.inputs, self.out_shapes, self.scratch, self.start, self.finish = inputs, out_shapes, scratch, start, finish


def _run_carry(name, carry):
    n_in, n_out = len(carry.inputs), len(carry.out_shapes)

    def body(*refs):
        parts = refs[:n_in], refs[n_in:n_in + n_out], refs[n_in + n_out:]
        carry.start(*parts)
        carry.finish(*parts)

    return _pallas(body, name=name, in_specs=[ANY] * n_in, out_specs=[ANY] * n_out, out_shape=list(carry.out_shapes),
                   scratch_shapes=list(carry.scratch))(*carry.inputs)


def gather_carry(shards):
    n = len(shards)

    def plan(x_refs, out_refs, sems):
        send_sems, recv_sems, local_sems = sems
        x, y, c = _position()
        me, sibling = (x, y, c), (x, y, 1 - c)
        chips = _other_chips(x, y)

        def slot(ref, pos):
            return ref.at[4 * pos[0] + 2 * pos[1] + pos[2]]

        def copy(t, j, block, to, src=None):
            dst = slot(out_refs[t], block)
            return pltpu.make_async_remote_copy(
                src_ref=dst if src is None else src, dst_ref=dst, send_sem=send_sems.at[t, j],
                recv_sem=recv_sems.at[t, j], device_id=to, device_id_type=MESH)

        mine = [pltpu.make_async_copy(x_refs[t], slot(out_refs[t], me), local_sems.at[t]) for t in range(n)]
        first = []
        for t in range(n):
            first.append(copy(t, 0, me, sibling, src=x_refs[t]))
            first += [copy(t, 1 + j, me, (*chip, c), src=x_refs[t]) for j, chip in enumerate(chips)]
        return c, me, sibling, chips, copy, mine, first

    def start(x_refs, out_refs, sems):
        _, _, _, _, _, mine, first = plan(x_refs, out_refs, sems)
        for cp in mine + first:
            cp.start()

    def finish(x_refs, out_refs, sems):
        c, me, sibling, chips, copy, mine, first = plan(x_refs, out_refs, sems)
        passed = []
        for t in range(n):
            for j, chip in enumerate(chips):
                copy(t, 1 + j, (*chip, c), me).wait_recv()
                fwd = copy(t, 4 + j, (*chip, c), sibling)
                fwd.start()
                passed.append(fwd)
        for t in range(n):
            copy(t, 0, sibling, me).wait_recv()
            for j, chip in enumerate(chips):
                copy(t, 4 + j, (*chip, 1 - c), me).wait_recv()
        for cp in first + passed:
            cp.wait_send()
        for cp in mine:
            cp.wait()

    return _Carry(list(shards), [_sds((N_DEV,) + s.shape, s.dtype) for s in shards],
                  [pltpu.SemaphoreType.DMA((n, 7)), pltpu.SemaphoreType.DMA((n, 7)), pltpu.SemaphoreType.DMA((n,))],
                  start, finish)


def all_gather(shards):
    return _run_carry("all_gather", gather_carry(shards))


def exchange_sibling(parts):
    n = len(parts)

    def body(*refs):
        p_refs, out_refs = refs[:n], refs[n:2 * n]
        send_sems, recv_sems = refs[2 * n:]
        x, y, c = _position()
        copies = []
        for t in range(n):
            for q in range(N_CHIP):
                cp = pltpu.make_async_remote_copy(
                    src_ref=p_refs[t].at[q, 1 - c], dst_ref=out_refs[t].at[q], send_sem=send_sems.at[t, q],
                    recv_sem=recv_sems.at[t, q], device_id=(x, y, 1 - c), device_id_type=MESH)
                cp.start()
                copies.append(cp)
        for cp in copies:
            cp.wait()

    return _pallas(
        body, name="exchange_sibling", in_specs=[ANY] * n, out_specs=[ANY] * n,
        out_shape=[_sds((N_CHIP,) + p.shape[2:], p.dtype) for p in parts],
        scratch_shapes=[pltpu.SemaphoreType.DMA((n, N_CHIP)), pltpu.SemaphoreType.DMA((n, N_CHIP))],
    )(*parts)


def chips_carry(parts):
    n = len(parts)

    def plan(p_refs, out_refs, sems):
        send_sems, recv_sems, local_sems = sems
        x, y, c = _position()
        local = [pltpu.make_async_copy(p_refs[t].at[2 * x + y], out_refs[t].at[3], local_sems.at[t]) for t in range(n)]
        remote = [pltpu.make_async_remote_copy(
            src_ref=p_refs[t].at[2 * cx + cy], dst_ref=out_refs[t].at[j], send_sem=send_sems.at[t, j],
            recv_sem=recv_sems.at[t, j], device_id=(cx, cy, c), device_id_type=MESH)
            for t in range(n) for j, (cx, cy) in enumerate(_other_chips(x, y))]
        return local, remote

    def start(p_refs, out_refs, sems):
        local, remote = plan(p_refs, out_refs, sems)
        for cp in local + remote:
            cp.start()

    def finish(p_refs, out_refs, sems):
        local, remote = plan(p_refs, out_refs, sems)
        for cp in remote + local:
            cp.wait()

    return _Carry(list(parts), [_sds(p.shape, p.dtype) for p in parts],
                  [pltpu.SemaphoreType.DMA((n, 3)), pltpu.SemaphoreType.DMA((n, 3)), pltpu.SemaphoreType.DMA((n,))],
                  start, finish)


def _flat_tile(rows, cols):
    tr = rows
    for d in range(16, min(rows, 512) + 1, 16):
        if rows % d == 0 and d * cols * 4 <= 2 * 1024 * 1024:
            tr = d
    return tr


def pair_add(part, recv):
    _, _, R, C = part.shape
    tr = _flat_tile(R, C)
    core = jnp.reshape(lax.axis_index("c"), (1,)).astype(jnp.int32)

    def body(core_ref, p_ref, r_ref, o_ref):
        o_ref[...] = (p_ref[...] + r_ref[...]).astype(BF16)

    grid_spec = pltpu.PrefetchScalarGridSpec(
        num_scalar_prefetch=1, grid=(N_CHIP, R // tr),
        in_specs=[pl.BlockSpec((None, None, tr, C), lambda q, i, core_ref: (q, core_ref[0], i, 0)),
                  pl.BlockSpec((None, tr, C), lambda q, i, core_ref: (q, i, 0))],
        out_specs=pl.BlockSpec((None, tr, C), lambda q, i, core_ref: (q, i, 0)))
    return _pallas(body, name="pair_add", grid_spec=grid_spec, out_shape=_sds((N_CHIP, R, C), BF16),
                   compiler_params=_params("parallel", "parallel"))(core, part, recv)


def adamw(w, m, v, slabs):
    R, C = w.shape
    tr = _flat_tile(R, C)
    n = slabs.shape[0]

    def body(w_ref, m_ref, v_ref, s_ref, g_ref, d_ref, nm_ref, nv_ref):
        g = s_ref[0].astype(F32)
        for j in range(1, n):
            g = g + s_ref[j].astype(F32)
        m2 = ADAM_B1 * m_ref[...] + (1.0 - ADAM_B1) * g
        v2 = ADAM_B2 * v_ref[...] + (1.0 - ADAM_B2) * (g * g)
        m_hat = m2 / (1.0 - ADAM_B1 ** ADAM_STEP)
        v_hat = v2 / (1.0 - ADAM_B2 ** ADAM_STEP)
        g_ref[...] = g
        d_ref[...] = -ADAM_LR * (m_hat / (jnp.sqrt(v_hat) + ADAM_EPS) + ADAM_WD * w_ref[...])
        nm_ref[...] = m2
        nv_ref[...] = v2

    spec = pl.BlockSpec((tr, C), lambda i: (i, 0))
    return _pallas(body, name="adamw", grid=(R // tr,),
                   in_specs=[spec] * 3 + [pl.BlockSpec((n, tr, C), lambda i: (0, i, 0))], out_specs=[spec] * 4,
                   out_shape=[_sds((R, C))] * 4, compiler_params=_params("parallel"))(w, m, v, slabs)


def _unshard_cols(g):
    return jnp.transpose(g, (1, 0, 2)).reshape(g.shape[1], -1)


def _unshard_rows(g):
    return g.reshape(-1, g.shape[2])


def _shard_cols(full):
    R, C = full.shape
    return jnp.transpose(full.reshape(R, N_DEV, C // N_DEV), (1, 0, 2)).reshape(N_CHIP, 2, R, C // N_DEV)


def _shard_rows(full):
    R, C = full.shape
    return full.reshape(N_CHIP, 2, R // N_DEV, C)


WEIGHTS = ['ln_ffn1_pre', 'ln_ffn1_post', 'ffn1_gate', 'ffn1_up', 'ffn1_down', 'ln_mix_pre', 'ln_mix_post', 'w_in',
           'rwkv_mu', 'rwkv_w0', 'rwkv_w2', 'rwkv_a0', 'rwkv_a2', 'rwkv_g2', 'rwkv_k_k', 'rwkv_k_a', 'rwkv_r_k',
           'rwkv_gn_w', 'rwkv_gn_b', 'w_proj_a', 'pool_w', 'pool_scale', 'w_proj_b', 'w_out', 'ln_ffn2_pre',
           'ln_ffn2_post', 'ffn2_gate', 'ffn2_up', 'ffn2_down']
COL_SHARDED = ['ffn1_gate', 'ffn1_up', 'ffn2_gate', 'ffn2_up', 'w_in', 'rwkv_w2', 'rwkv_a2', 'rwkv_g2', 'w_proj_a',
               'w_proj_b']
ROW_SHARDED = ['ffn1_down', 'ffn2_down', 'w_out', 'pool_w']
TRANSPOSED = ['ffn1_gate', 'ffn1_up', 'ffn2_gate', 'ffn2_up', 'w_in', 'w_proj_a', 'w_proj_b']
SHARDED = COL_SHARDED + ROW_SHARDED
REPLICATED = [n for n in WEIGHTS if n not in SHARDED]


def _step(args):
    wts = {n: args[n] if args[n].ndim == 2 else args[n][0] for n in WEIGHTS}
    x, target = args["x"][0], args["loss_target"][0]
    T, D = x.shape
    W = wts["rwkv_w0"].shape[1]
    PW = wts["pool_scale"].shape[1]
    LW, LA, LG = wts["rwkv_w2"].shape[0], wts["rwkv_a2"].shape[0], wts["rwkv_g2"].shape[0]
    lat = LW + LA + LG
    latp = _round_up(lat, LAT_ALIGN)
    rc = 3 * W + lat
    base = 3 * W + PW + 2 * D
    n_groups, gshard, gd = wts["pool_w"].shape

    pool_w_shard = wts["pool_w"].reshape(n_groups * gshard, gd)
    shards = {n: (pool_w_shard if n == "pool_w" else wts[n]).astype(BF16) for n in SHARDED}
    shards.update({n: shards[n].T for n in TRANSPOSED})
    full = {}

    def fetch(names):
        return gather_carry([shards[n] for n in names])

    def arrived(names, got):
        for n, g in zip(names, got):
            if n == "pool_w":
                full[n] = jnp.transpose(g.reshape(N_DEV, n_groups, gshard, gd), (1, 0, 2, 3)).reshape(n_groups, gd, gd)
            elif n in COL_SHARDED and n not in TRANSPOSED:
                full[n] = _unshard_cols(g)
            else:
                full[n] = _unshard_rows(g)

    arrived(["ffn1_gate", "ffn1_up"], all_gather([shards["ffn1_gate"], shards["ffn1_up"]]))
    mu = wts["rwkv_mu"]
    mu_rkv = mu[:, :3 * W]
    mu_lat = jnp.concatenate([mu[:, 3 * W:], jnp.zeros((1, latp - lat), F32)], axis=1)
    rk = wts["rwkv_r_k"].reshape(1, W)

    n1 = rms_pre(x, wts["ln_ffn1_pre"])
    (g1, u1, act1), got = _mm("ffn1_up", [n1], [full["ffn1_gate"], full["ffn1_up"]], "nt", [BF16] * 3,
                              epilogue=_swiglu_fwd_epi, carry=fetch(["ffn1_down"]))
    arrived(["ffn1_down"], got)
    (f1,), got = _mm("ffn1_down", [act1], [full["ffn1_down"]], "nn", [F32], tm=512, carry=fetch(["w_in"]))
    arrived(["w_in"], got)
    w_in = full["w_in"]
    w_in_p = jnp.concatenate([w_in[:3 * W], w_in[rc:], w_in[3 * W:rc], jnp.zeros((latp - lat, D), BF16)], axis=0)
    h1, nm = post_pre(x, f1, wts["ln_ffn1_post"], wts["ln_mix_pre"], MACARON)
    mixer = ["rwkv_w2", "rwkv_a2", "rwkv_g2", "w_proj_a", "w_proj_b", "pool_w", "w_out"]
    (p,), got = _mm("in_proj", [nm], [w_in_p], "nt", [F32], carry=fetch(mixer))
    arrived(mixer, got)
    pool_w = full["pool_w"]

    def pad_rows(m, at):
        return jnp.zeros((latp, W), BF16).at[at:at + m.shape[0]].set(m)

    w2p, a2p, g2p = pad_rows(full["rwkv_w2"], 0), pad_rows(full["rwkv_a2"], LW), pad_rows(full["rwkv_g2"], LW + LA)
    small = [mu_rkv, mu_lat, wts["rwkv_w0"], wts["rwkv_a0"], wts["rwkv_k_k"], wts["rwkv_k_a"], w2p, a2p, g2p]
    cols = {"rkv": (p, 3 * W, 0), "pool": (p, PW, 3 * W // PW), "lat": (p, latp, base // latp)}
    r, decay, kmod, v, aneg, bpos, gate = rwkv_prep(p, cols, *small)
    (y, states, sdota), got = wkv_fwd(r, decay, kmod, v, aneg, bpos, carry=fetch(["ffn2_gate", "ffn2_up"]))
    arrived(["ffn2_gate", "ffn2_up"], got)
    ya_in = rwkv_post(y, r, kmod, v, gate, wts["rwkv_gn_w"], wts["rwkv_gn_b"], rk)
    yb_in = pool_fwd(cols, pool_w, wts["pool_scale"])
    gates = [(p, 3 * W + PW), (p, 3 * W + PW + D)]
    m, ya, yb = _mm("mix", [ya_in, yb_in], [full["w_proj_a"], full["w_proj_b"]], "nt", [BF16] * 3,
                    extras=gates, epilogue=_mix_fwd_epi)
    mx = _mm("out_proj", [m], [full["w_out"]], "nn", [F32])[0]
    h2, n2 = post_pre(h1, mx, wts["ln_mix_post"], wts["ln_ffn2_pre"], 1.0)
    (g2_, u2, act2), got = _mm("ffn2_up", [n2], [full["ffn2_gate"], full["ffn2_up"]], "nt", [BF16] * 3,
                               epilogue=_swiglu_fwd_epi, carry=fetch(["ffn2_down"]))
    arrived(["ffn2_down"], got)
    f2 = _mm("ffn2_down", [act2], [full["ffn2_down"]], "nn", [F32], tm=512)[0]

    grads, slabs = {}, {}
    to_sibling = {}

    def pair_sums(names):
        parts = []
        for n in names:
            if n == "pool_w":
                parts.append(jnp.transpose(grads[n].reshape(n_groups, N_DEV, gshard, gd), (1, 0, 2, 3)).reshape(
                    N_CHIP, 2, n_groups * gshard, gd))
            elif n in COL_SHARDED and n not in TRANSPOSED:
                parts.append(_shard_cols(grads[n]))
            else:
                parts.append(_shard_rows(grads[n]))
        sent = [_shard_rows(to_sibling[n]) if n in to_sibling else part for n, part in zip(names, parts)]
        return [pair_add(part, rcv) for part, rcv in zip(parts, exchange_sibling(sent))]

    def landed(names, got):
        slabs.update(zip(names, got))

    dh3, df2, loss_part, grads["ln_ffn2_post"] = loss_post_bwd(h2, f2, wts["ln_ffn2_post"], target, MACARON)
    dg2, du2 = _mm("ffn2_dact", [df2], [full["ffn2_down"]], "nt", [BF16] * 2, extras=[(g2_, 0), (u2, 0)],
                   epilogue=_swiglu_bwd_epi)
    grads["ffn2_down"], to_sibling["ffn2_down"] = _mm("ffn2_ddown", [act2], [df2], "tn", [F32, BF16], tm=512, tn=1024,
                                                      epilogue=_twice_epi)
    grads["ffn2_gate"], grads["ffn2_up"], to_sibling["ffn2_gate"], to_sibling["ffn2_up"] = _mm(
        "ffn2_dup", [dg2, du2], [n2], "tn", [F32, F32, BF16, BF16], tm=512, epilogue=_twice_epi)
    dn2 = _mm("ffn2_dn_gate", [dg2], [full["ffn2_gate"]], "nn", [F32], tm=512)[0]
    dn2 = _mm("ffn2_dn", [du2], [full["ffn2_up"]], "nn", [F32], tm=512, extras=[(dn2, 0)], epilogue=_add_epi)[0]
    sums2 = pair_sums(["ffn2_down", "ffn2_gate", "ffn2_up"])
    dh2, grads["ln_ffn2_pre"] = pre_bwd(dn2, h2, wts["ln_ffn2_pre"], dh3)
    dmx, grads["ln_mix_post"] = post_bwd(dh2, mx, wts["ln_mix_post"], 1.0)
    (dya, dyb, dga, dgb), got = _mm("dmix", [dmx], [full["w_out"]], "nt", [BF16] * 4, extras=gates + [(ya, 0), (yb, 0)],
                                    epilogue=_mix_bwd_epi, carry=chips_carry(sums2[:1]))
    landed(["ffn2_down"], got)
    grads["w_out"] = _mm("dw_out", [m], [dmx], "tn", [F32])[0]
    dya_in = _mm("dproj_a", [dya], [full["w_proj_a"]], "nn", [F32])[0]
    dyb_in = _mm("dproj_b", [dyb], [full["w_proj_b"]], "nn", [F32])[0]
    grads["w_proj_a"] = _mm("dw_proj_a", [dya], [ya_in], "tn", [F32])[0]
    grads["w_proj_b"] = _mm("dw_proj_b", [dyb], [yb_in], "tn", [F32])[0]
    dz_pool, grads["pool_w"], grads["pool_scale"] = pool_bwd(cols, dyb_in, pool_w, wts["pool_scale"])
    dy, dr_x, dk_x, dv_x, dgate, grads["rwkv_gn_w"], grads["rwkv_gn_b"], drk = rwkv_post_bwd(
        dya_in, y, r, kmod, v, gate, wts["rwkv_gn_w"], wts["rwkv_gn_b"], rk)
    grads["rwkv_r_k"] = drk.reshape(wts["rwkv_r_k"].shape)
    dr_s, ddecay, dk_s, dv_s, dneg, dbpos = wkv_bwd(r, decay, kmod, v, aneg, bpos, dy, states, sdota)
    (dzs, dzls, grads["rwkv_k_a"], grads["rwkv_k_k"], grads["rwkv_a0"], grads["rwkv_w0"], da2p, dw2p, dg2p) = rwkv_prep_bwd(
        p, cols, [dr_s, dr_x, ddecay, dk_s, dk_x, dv_s, dv_x, dneg, dbpos, dgate], *small)
    grads["rwkv_w2"], grads["rwkv_a2"], grads["rwkv_g2"] = dw2p[:LW], da2p[LW:LW + LA], dg2p[LW + LA:lat]
    dz_rkv, dz_lat, dmu_rkv, dmu_lat = shift_bwd(cols, dzs, dzls, mu_rkv, mu_lat)
    grads["rwkv_mu"] = jnp.concatenate([dmu_rkv, dmu_lat[:, :lat]], axis=1)
    dp = jnp.concatenate([dz_rkv, dz_pool, dga, dgb, dz_lat], axis=1)
    (dnm,), got = _mm("din_proj", [dp], [w_in_p], "nn", [F32], tm=512, carry=chips_carry(sums2[1:2]))
    landed(["ffn2_gate"], got)
    (dw_in_p,), got = _mm("dw_in", [dp], [nm], "tn", [F32], tm=512, tn=1024, carry=chips_carry(sums2[2:]))
    landed(["ffn2_up"], got)
    grads["w_in"] = jnp.concatenate([dw_in_p[:3 * W], dw_in_p[base:base + lat], dw_in_p[3 * W:base]], axis=0)
    sums_mix = pair_sums(["w_in"] + mixer)
    dh1, grads["ln_mix_pre"] = pre_bwd(dnm, h1, wts["ln_mix_pre"], dh2)
    df1, grads["ln_ffn1_post"] = post_bwd(dh1, f1, wts["ln_ffn1_post"], MACARON)
    (dg1, du1), got = _mm("ffn1_dact", [df1], [full["ffn1_down"]], "nt", [BF16] * 2, extras=[(g1, 0), (u1, 0)],
                          epilogue=_swiglu_bwd_epi, carry=chips_carry(sums_mix[:1]))
    landed(["w_in"], got)
    (grads["ffn1_down"], to_sibling["ffn1_down"]), got = _mm(
        "ffn1_ddown", [act1], [df1], "tn", [F32, BF16], tm=512, tn=1024, epilogue=_twice_epi,
        carry=chips_carry(sums_mix[1:]))
    landed(mixer, got)
    (grads["ffn1_gate"], grads["ffn1_up"], to_sibling["ffn1_gate"], to_sibling["ffn1_up"]), got = _mm(
        "ffn1_dup", [dg1, du1], [n1], "tn", [F32, F32, BF16, BF16], tm=512, epilogue=_twice_epi,
        carry=chips_carry(pair_sums(["ffn1_down"])))
    landed(["ffn1_down"], got)
    sums1 = pair_sums(["ffn1_gate", "ffn1_up"])
    (dn1,), got = _mm("ffn1_dn_gate", [dg1], [full["ffn1_gate"]], "nn", [F32], tm=512, carry=chips_carry(sums1[:1]))
    landed(["ffn1_gate"], got)
    (dn1,), got = _mm("ffn1_dn", [du1], [full["ffn1_up"]], "nn", [F32], tm=512, extras=[(dn1, 0)], epilogue=_add_epi,
                      carry=chips_carry(sums1[1:]))
    landed(["ffn1_up"], got)
    grad_x, grads["ln_ffn1_pre"] = pre_bwd(dn1, x, wts["ln_ffn1_pre"], dh1)

    flat = jnp.concatenate([grads[n].reshape(-1) for n in REPLICATED])
    n_small = flat.shape[0]
    rows = _round_up(n_small, 8 * LANES) // LANES
    flat = jnp.concatenate([flat, jnp.zeros((rows * LANES - n_small,), F32)]).reshape(rows, LANES)
    small_slabs = all_gather([flat])[0]

    def packed(prefix):
        vals = jnp.concatenate([args[prefix + n].reshape(-1) for n in REPLICATED])
        return jnp.concatenate([vals, jnp.ones((rows * LANES - n_small,), F32)]).reshape(rows, LANES)

    outs = {}
    small_out = adamw(packed(""), packed("m_"), packed("v_"), small_slabs)
    offset = 0
    for n in REPLICATED:
        size = args[n].size
        outs[n] = [o.reshape(-1)[offset:offset + size].reshape(args[n].shape) for o in small_out]
        offset += size
    for n in SHARDED:
        slab = jnp.swapaxes(slabs[n], 1, 2) if n in TRANSPOSED else slabs[n]
        shard2d = slab.shape[1:]
        res = adamw(*[args[pre + n].reshape(shard2d) for pre in ("", "m_", "v_")], slab)
        outs[n] = [o.reshape(args[n].shape) for o in res]

    loss = lax.psum(loss_part[0, 0], ("x", "y", "c"))
    return (loss, grad_x[None], *[outs[n][0] for n in WEIGHTS], *[outs[n][1] for n in WEIGHTS],
            *[outs[n][2] for n in WEIGHTS], *[outs[n][3] for n in WEIGHTS])


ARG_NAMES = ["x"] + WEIGHTS + ["loss_target"] + ["m_" + n for n in WEIGHTS] + ["v_" + n for n in WEIGHTS]


def kernel(x, ln_ffn1_pre, ln_ffn1_post, ffn1_gate, ffn1_up, ffn1_down, ln_mix_pre, ln_mix_post, w_in, rwkv_mu, rwkv_w0,
           rwkv_w2, rwkv_a0, rwkv_a2, rwkv_g2, rwkv_k_k, rwkv_k_a, rwkv_r_k, rwkv_gn_w, rwkv_gn_b, w_proj_a, pool_w,
           pool_scale, w_proj_b, w_out, ln_ffn2_pre, ln_ffn2_post, ffn2_gate, ffn2_up, ffn2_down, loss_target,
           m_ln_ffn1_pre, m_ln_ffn1_post, m_ffn1_gate, m_ffn1_up, m_ffn1_down, m_ln_mix_pre, m_ln_mix_post, m_w_in,
           m_rwkv_mu, m_rwkv_w0, m_rwkv_w2, m_rwkv_a0, m_rwkv_a2, m_rwkv_g2, m_rwkv_k_k, m_rwkv_k_a, m_rwkv_r_k,
           m_rwkv_gn_w, m_rwkv_gn_b, m_w_proj_a, m_pool_w, m_pool_scale, m_w_proj_b, m_w_out, m_ln_ffn2_pre,
           m_ln_ffn2_post, m_ffn2_gate, m_ffn2_up, m_ffn2_down, v_ln_ffn1_pre, v_ln_ffn1_post, v_ffn1_gate, v_ffn1_up,
           v_ffn1_down, v_ln_mix_pre, v_ln_mix_post, v_w_in, v_rwkv_mu, v_rwkv_w0, v_rwkv_w2, v_rwkv_a0, v_rwkv_a2,
           v_rwkv_g2, v_rwkv_k_k, v_rwkv_k_a, v_rwkv_r_k, v_rwkv_gn_w, v_rwkv_gn_b, v_w_proj_a, v_pool_w, v_pool_scale,
           v_w_proj_b, v_w_out, v_ln_ffn2_pre, v_ln_ffn2_post, v_ffn2_gate, v_ffn2_up, v_ffn2_down):
    given = locals()
    return _step({n: given[n] for n in ARG_NAMES})
```

```python
import jax
import jax.numpy as jnp
from jax import lax
from jax.experimental import pallas as pl
from jax.experimental.pallas import tpu as pltpu

F32, BF16 = jnp.float32, jnp.bfloat16
N_DEV = 8
N_CHIP = 4
HEAD = 64
LANES = 2 * HEAD
NORM_EPS, GN_EPS, L2_EPS = 1e-6, 64e-5, 1e-12
POOL_WINDOWS = (2, 4, 8, 16)
POOL_HALO = 16
MACARON = 0.5
ADAM_LR, ADAM_B1, ADAM_B2, ADAM_EPS, ADAM_WD, ADAM_STEP = 0.001, 0.9, 0.999, 1e-08, 0.01, 10
VMEM_LIMIT = 48 * 1024 * 1024
MM_VMEM_BUDGET = 40 * 1024 * 1024
ROW_TILE = 256
RWKV_ROW_TILE = 128
LAT_ALIGN = 512
WKV_CHUNK, WKV_PAIRS = 16, 8
WKV_CHUNK_FWD = 32
WKV_UNROLL = 4
WKV_MXU_PAIRS = 5
MESH = pl.DeviceIdType.MESH


def _pallas(body, **kw):
    return pl.pallas_call(body, **kw)


def _params(*sem):
    return pltpu.CompilerParams(dimension_semantics=sem, vmem_limit_bytes=VMEM_LIMIT)


def _tile(n, target, align=128):
    best = None
    for d in range(align, min(n, target) + 1, align):
        if n % d == 0:
            best = d
    return best if best is not None else n


def _round_up(n, m):
    return (n + m - 1) // m * m


ANY = pl.BlockSpec(memory_space=pl.ANY)


def _mm(name, a_list, b_list, mode, out_dtypes, *, sum_pairs=False, extras=(), epilogue=None, tm=1024, tn=512,
        carry=None):
    n_a, n_b = len(a_list), len(b_list)
    n_prod = max(n_a, n_b)
    assert n_a in (1, n_prod) and n_b in (1, n_prod)
    a0, b0 = a_list[0], b_list[0]
    if mode == "nn":
        (M, K), N = a0.shape, b0.shape[1]
    elif mode == "nt":
        (M, K), N = a0.shape, b0.shape[0]
    else:
        (K, M), N = a0.shape, b0.shape[1]
    tm, tn = _tile(M, tm), _tile(N, tn)
    n_acc = 1 if sum_pairs else n_prod
    n_ex = len(extras)

    def planned(tk):
        operands = 2 * 2 * tk * (n_a * tm + n_b * tn)
        tiles = 2 * tm * tn * (sum(e.dtype.itemsize for e, _ in extras) + sum(jnp.dtype(d).itemsize for d in out_dtypes))
        return operands + tiles + 4 * tm * tn * (n_acc + len(out_dtypes))

    tk = max([d for d in range(128, K + 1, 128) if K % d == 0 and planned(d) <= MM_VMEM_BUDGET] or [_tile(K, 512)])
    nk = K // tk
    if mode == "tn":
        a_spec = pl.BlockSpec((tk, tm), lambda i, j, k: (k, i))
    else:
        a_spec = pl.BlockSpec((tm, tk), lambda i, j, k: (i, k))
    if mode == "nt":
        b_spec = pl.BlockSpec((tn, tk), lambda i, j, k: (j, k))
    else:
        b_spec = pl.BlockSpec((tk, tn), lambda i, j, k: (k, j))
    contract = {"nn": ((1,), (0,)), "nt": ((1,), (1,)), "tn": ((0,), (0,))}[mode]
    e_specs = []
    for _, col in extras:
        assert col % tn == 0
        e_specs.append(pl.BlockSpec((tm, tn), lambda i, j, k, off=col // tn: (i, j + off)))
    o_spec = pl.BlockSpec((tm, tn), lambda i, j, k: (i, j))

    n_in, n_out, n_scr = n_a + n_b + n_ex, len(out_dtypes), (n_acc if nk > 1 else 0)
    c_in, c_out = (len(carry.inputs), len(carry.out_shapes)) if carry else (0, 0)
    grid = (M // tm, N // tn, nk)

    def body(*refs):
        a_refs, b_refs, e_refs = refs[:n_a], refs[n_a:n_a + n_b], refs[n_a + n_b:n_in]
        o_refs = refs[n_in + c_in:n_in + c_in + n_out]
        acc_refs = refs[n_in + c_in + n_out + c_out:n_in + c_in + n_out + c_out + n_scr]
        carried = (refs[n_in:n_in + c_in], refs[n_in + c_in + n_out:n_in + c_in + n_out + c_out],
                   refs[n_in + c_in + n_out + c_out + n_scr:])
        at = [pl.program_id(d) for d in range(3)]

        if carry:
            @pl.when((at[0] == 0) & (at[1] == 0) & (at[2] == 0))
            def _():
                carry.start(*carried)

        def products():
            a_vals, b_vals = [a[...] for a in a_refs], [b[...] for b in b_refs]
            prods = [lax.dot_general(a_vals[p if n_a > 1 else 0], b_vals[p if n_b > 1 else 0], (contract, ((), ())),
                                     preferred_element_type=F32) for p in range(n_prod)]
            return [sum(prods[1:], prods[0])] if sum_pairs else prods

        def finish(results):
            outs = epilogue(results, [e[...] for e in e_refs]) if epilogue else results
            for o_ref, o in zip(o_refs, outs):
                o_ref[...] = o.astype(o_ref.dtype)

        if nk == 1:
            finish(products())
        else:
            @pl.when(at[2] == 0)
            def _():
                for acc in acc_refs:
                    acc[...] = jnp.zeros_like(acc)

            for acc, prod in zip(acc_refs, products()):
                acc[...] += prod

            @pl.when(at[2] == nk - 1)
            def _():
                finish([acc[...] for acc in acc_refs])

        if carry:
            @pl.when((at[0] == grid[0] - 1) & (at[1] == grid[1] - 1) & (at[2] == grid[2] - 1))
            def _():
                carry.finish(*carried)

    res = _pallas(
        body, name=name, grid=grid,
        in_specs=[a_spec] * n_a + [b_spec] * n_b + e_specs + [ANY] * c_in,
        out_specs=[o_spec] * n_out + [ANY] * c_out,
        out_shape=[jax.ShapeDtypeStruct((M, N), dt) for dt in out_dtypes] + (list(carry.out_shapes) if carry else []),
        scratch_shapes=[pltpu.VMEM((tm, tn), F32)] * n_scr + (list(carry.scratch) if carry else []),
        compiler_params=_params("arbitrary", "arbitrary", "arbitrary") if carry else _params("parallel", "parallel", "arbitrary"),
    )(*a_list, *b_list, *[e for e, _ in extras], *(carry.inputs if carry else []))
    return (res[:n_out], res[n_out:]) if carry else res


def _swiglu_fwd_epi(accs, _):
    g, u = accs
    return [g, u, g * jax.nn.sigmoid(g) * u]


def _swiglu_bwd_epi(accs, ex):
    dact = accs[0]
    g, u = ex[0].astype(F32), ex[1].astype(F32)
    sg = jax.nn.sigmoid(g)
    return [dact * u * (sg * (1.0 + g * (1.0 - sg))), dact * (g * sg)]


def _add_epi(accs, ex):
    return [accs[0] + ex[0]]


def _twice_epi(accs, _):
    return list(accs) + list(accs)


def _mix_fwd_epi(accs, ex):
    ya, yb = accs
    return [jax.nn.sigmoid(ex[0]) * ya + jax.nn.sigmoid(ex[1]) * yb, ya, yb]


def _mix_bwd_epi(accs, ex):
    dm = accs[0]
    sa, sb = jax.nn.sigmoid(ex[0]), jax.nn.sigmoid(ex[1])
    ya, yb = ex[2].astype(F32), ex[3].astype(F32)
    return [dm * sa, dm * sb, dm * ya * sa * (1.0 - sa), dm * yb * sb * (1.0 - sb)]


def _row_call(name, body, T, tiled, params, outs, accs=(), prev=(), nxt=(), halo=8, tile=ROW_TILE):
    tm = min(tile, T)
    n_tiles = T // tm

    def norm(e):
        return e if isinstance(e, tuple) else (e, e.shape[1], 0)

    tiled, prev, nxt = [norm(e) for e in tiled], [norm(e) for e in prev], [norm(e) for e in nxt]
    per_halo, n_halo = tm // halo, T // halo
    in_specs = [pl.BlockSpec((tm, w), lambda i, cb=cb: (i, cb)) for _, w, cb in tiled]
    in_specs += [pl.BlockSpec((halo, w), lambda i, cb=cb: (jnp.maximum(i * per_halo - 1, 0), cb)) for _, w, cb in prev]
    in_specs += [pl.BlockSpec((halo, w), lambda i, cb=cb: (jnp.minimum((i + 1) * per_halo, n_halo - 1), cb))
                 for _, w, cb in nxt]
    in_specs += [pl.BlockSpec(p.shape, lambda i, nd=p.ndim: (0,) * nd) for p in params]
    out_specs = [pl.BlockSpec((tm, o.shape[1]), lambda i: (i, 0)) for o in outs]
    out_specs += [pl.BlockSpec(a.shape, lambda i, nd=len(a.shape): (0,) * nd) for a in accs]
    n1, n2, n3, n4, n5 = len(tiled), len(prev), len(nxt), len(params), len(outs)

    def kernel_body(*refs):
        i = pl.program_id(0)
        acc_refs = refs[n1 + n2 + n3 + n4 + n5:]

        @pl.when(i == 0)
        def _():
            for a in acc_refs:
                a[...] = jnp.zeros_like(a)

        body(i, n_tiles, refs[:n1], refs[n1:n1 + n2], refs[n1 + n2:n1 + n2 + n3],
             refs[n1 + n2 + n3:n1 + n2 + n3 + n4], refs[n1 + n2 + n3 + n4:n1 + n2 + n3 + n4 + n5], acc_refs)

    return _pallas(
        kernel_body, name=name, grid=(n_tiles,), in_specs=in_specs, out_specs=out_specs,
        out_shape=list(outs) + list(accs),
        compiler_params=_params("arbitrary"),
    )(*[e[0] for e in tiled + prev + nxt], *params)


def _sds(shape, dtype=F32):
    return jax.ShapeDtypeStruct(tuple(shape), dtype)


def _rstd(x):
    return lax.rsqrt(jnp.mean(x * x, axis=-1, keepdims=True) + NORM_EPS)


def _colsum(x):
    return jnp.sum(x, axis=0, keepdims=True)


def rms_pre(x, g):
    T, D = x.shape

    def body(i, n, tiled, prev, nxt, params, outs, accs):
        xv = tiled[0][...]
        outs[0][...] = (xv * _rstd(xv) * params[0][...]).astype(BF16)

    return _row_call("rms_pre", body, T, [x], [g], [_sds((T, D), BF16)])[0]


def post_pre(h, f, g_post, g_pre, scale):
    T, D = h.shape

    def body(i, n, tiled, prev, nxt, params, outs, accs):
        hv, fv = tiled[0][...], tiled[1][...]
        h2 = hv + scale * (fv * _rstd(fv) * params[0][...])
        outs[0][...] = h2
        outs[1][...] = (h2 * _rstd(h2) * params[1][...]).astype(BF16)

    return _row_call("post_pre", body, T, [h, f], [g_post, g_pre], [_sds((T, D)), _sds((T, D), BF16)])


def _post_bwd_math(dh, fv, g, scale):
    r = _rstd(fv)
    fhat = fv * r
    dy = scale * dh
    z = dy * g
    df = r * (z - fhat * jnp.mean(z * fhat, axis=-1, keepdims=True))
    return df, _colsum(dy * fhat)


def loss_post_bwd(h, f, g_post, target, scale):
    T, D = h.shape

    def body(i, n, tiled, prev, nxt, params, outs, accs):
        hv, fv, tv = tiled[0][...], tiled[1][...], tiled[2][...]
        g = params[0][...]
        e = hv + scale * (fv * _rstd(fv) * g) - tv
        accs[0][...] += jnp.full(accs[0].shape, 0.5 / D, F32) * jnp.sum(e * e)
        dh = e * (1.0 / D)
        outs[0][...] = dh
        df, dg = _post_bwd_math(dh, fv, g, scale)
        outs[1][...] = df.astype(BF16)
        accs[1][...] += dg

    return _row_call("loss_post_bwd", body, T, [h, f, target], [g_post],
                     [_sds((T, D)), _sds((T, D), BF16)], [_sds((1, LANES)), _sds((1, D))])


def post_bwd(dh, f, g_post, scale):
    T, D = dh.shape

    def body(i, n, tiled, prev, nxt, params, outs, accs):
        df, dg = _post_bwd_math(tiled[0][...], tiled[1][...], params[0][...], scale)
        outs[0][...] = df.astype(BF16)
        accs[0][...] += dg

    return _row_call("post_bwd", body, T, [dh, f], [g_post], [_sds((T, D), BF16)], [_sds((1, D))])


def pre_bwd(dn, h, g_pre, dres):
    T, D = h.shape

    def body(i, n, tiled, prev, nxt, params, outs, accs):
        dnv, hv = tiled[0][...], tiled[1][...]
        r = _rstd(hv)
        hhat = hv * r
        z = dnv * params[0][...]
        outs[0][...] = tiled[2][...] + r * (z - hhat * jnp.mean(z * hhat, axis=-1, keepdims=True))
        accs[0][...] += _colsum(dnv * hhat)

    return _row_call("pre_bwd", body, T, [dn, h, dres], [g_pre], [_sds((T, D))], [_sds((1, D))])


def _head_ones():
    i = lax.broadcasted_iota(jnp.int32, (LANES, LANES), 0)
    j = lax.broadcasted_iota(jnp.int32, (LANES, LANES), 1)
    return jnp.where((i < HEAD) == (j < HEAD), 1.0, 0.0).astype(BF16)


def _headsum(x):
    e = _head_ones()
    hi, rest = _hi_lo(x)
    mid, lo = _hi_lo(rest)
    pieces = [p.astype(BF16) for p in (hi, mid, lo)]
    parts = [sum(jnp.dot(p[:, s:s + LANES], e, preferred_element_type=F32) for p in pieces)
             for s in range(0, x.shape[1], LANES)]
    return parts[0] if len(parts) == 1 else jnp.concatenate(parts, axis=1)


def _shift_down(x, before):
    row = lax.broadcasted_iota(jnp.int32, x.shape, 0)
    return jnp.where(row == 0, before, pltpu.roll(x, 1, 0))


def _shift_up(x, after):
    row = lax.broadcasted_iota(jnp.int32, x.shape, 0)
    return jnp.where(row == x.shape[0] - 1, after, pltpu.roll(x, x.shape[0] - 1, 0))


def _last_row(ref, keep):
    r = ref[ref.shape[0] - 1:ref.shape[0], :]
    return jnp.where(keep, r, jnp.zeros_like(r))


def _first_row(ref, keep):
    r = ref[0:1, :]
    return jnp.where(keep, r, jnp.zeros_like(r))


def _softplus(u):
    return jnp.maximum(u, 0.0) + jnp.log(1.0 + jnp.exp(-jnp.abs(u)))


def _dotb(a, b, contract):
    return lax.dot_general(a.astype(BF16), b.astype(BF16), (contract, ((), ())), preferred_element_type=F32)


_NN, _NT, _TN = ((1,), (0,)), ((1,), (1,)), ((0,), (0,))


def _prep_forward(z, zprev_row, zl, zlprev_row, mu, mul, w0, a0, kk_w, ka_w, w2p, a2p, g2p):
    W = w0.shape[1]
    zs = z + (_shift_down(z, zprev_row) - z) * mu
    zls = zl + (_shift_down(zl, zlprev_row) - zl) * mul
    r, k, v = zs[:, :W], zs[:, W:2 * W], zs[:, 2 * W:]
    th, sg = jnp.tanh(zls), jax.nn.sigmoid(zls)
    xw = w0 + _dotb(th, w2p, _NN)
    wlog = -_softplus(-xw) - 0.5
    ew = jnp.exp(wlog)
    decay = jnp.exp(-ew)
    a = jax.nn.sigmoid(a0 + _dotb(zls, a2p, _NN))
    gate = _dotb(sg, g2p, _NN)
    q = k * kk_w
    nrm = jnp.sqrt(_headsum(q * q))
    den = jnp.maximum(nrm, L2_EPS)
    kk = q / den
    kmod = k * (1.0 + (a - 1.0) * ka_w)
    return dict(zs=zs, zls=zls, r=r, k=k, v=v, th=th, sg=sg, xw=xw, ew=ew, decay=decay, a=a, gate=gate,
                nrm=nrm, den=den, kk=kk, kmod=kmod)


def rwkv_prep(p, cols, mu, mul, w0, a0, kk_w, ka_w, w2p, a2p, g2p):
    T = p.shape[0]
    W = w0.shape[1]

    def body(i, n, tiled, prev, nxt, params, outs, accs):
        c = _prep_forward(tiled[0][...], _last_row(prev[0], i > 0), tiled[1][...], _last_row(prev[1], i > 0),
                          *[q[...] for q in params])
        for o, val in zip(outs, (c["r"], c["decay"], c["kmod"], c["v"], -c["kk"], c["kk"] * c["a"], c["gate"])):
            o[...] = val

    return _row_call("rwkv_prep", body, T, [cols["rkv"], cols["lat"]],
                     [mu, mul, w0, a0, kk_w, ka_w, w2p, a2p, g2p], [_sds((T, W))] * 7,
                     prev=[cols["rkv"], cols["lat"]], tile=RWKV_ROW_TILE)


def _post_forward(y, r, kmod, v, gn_w, gn_b, rk):
    mean = _headsum(y) * (1.0 / HEAD)
    yc = y - mean
    rstd = lax.rsqrt(_headsum(yc * yc) * (1.0 / HEAD) + GN_EPS)
    yn = yc * rstd
    s = _headsum(r * kmod * rk)
    return yn, rstd, s, yn * gn_w + gn_b + s * v


def rwkv_post(y, r, kmod, v, gate, gn_w, gn_b, rk):
    T, W = y.shape

    def body(i, n, tiled, prev, nxt, params, outs, accs):
        yv, rv, kv, vv, gv = [t[...] for t in tiled]
        _, _, _, o = _post_forward(yv, rv, kv, vv, *[q[...] for q in params])
        outs[0][...] = (o * gv).astype(BF16)

    return _row_call("rwkv_post", body, T, [y, r, kmod, v, gate], [gn_w, gn_b, rk], [_sds((T, W), BF16)],
                     tile=RWKV_ROW_TILE)[0]


def rwkv_post_bwd(dout, y, r, kmod, v, gate, gn_w, gn_b, rk):
    T, W = y.shape

    def body(i, n, tiled, prev, nxt, params, outs, accs):
        dv_, yv, rv, kv, vv, gv = [t[...] for t in tiled]
        gn_w_, gn_b_, rk_ = [q[...] for q in params]
        yn, rstd, s, o = _post_forward(yv, rv, kv, vv, gn_w_, gn_b_, rk_)
        do = dv_ * gv
        outs[4][...] = dv_ * o
        accs[0][...] += _colsum(do * yn)
        accs[1][...] += _colsum(do)
        dyn = do * gn_w_
        outs[0][...] = rstd * (dyn - _headsum(dyn) * (1.0 / HEAD) - yn * (_headsum(dyn * yn) * (1.0 / HEAD)))
        ds = _headsum(do * vv)
        outs[1][...] = ds * kv * rk_
        outs[2][...] = ds * rv * rk_
        outs[3][...] = do * s
        accs[2][...] += _colsum(ds * rv * kv)

    return _row_call("rwkv_post_bwd", body, T, [dout, y, r, kmod, v, gate], [gn_w, gn_b, rk],
                     [_sds((T, W))] * 5, [_sds((1, W))] * 3, tile=RWKV_ROW_TILE)


def rwkv_prep_bwd(p, cols, grads, mu, mul, w0, a0, kk_w, ka_w, w2p, a2p, g2p):
    T = p.shape[0]
    W = w0.shape[1]
    latp = w2p.shape[0]

    def body(i, n, tiled, prev, nxt, params, outs, accs):
        pv = [q[...] for q in params]
        mu_, mul_, w0_, a0_, kk_w_, ka_w_, w2p_, a2p_, g2p_ = pv
        c = _prep_forward(tiled[0][...], _last_row(prev[0], i > 0), tiled[1][...], _last_row(prev[1], i > 0), *pv)
        dr_s, dr_x, ddecay, dk_s, dk_x, dv_s, dv_x, dneg, db, dgate = [t[...] for t in tiled[2:]]
        k, a, kk = c["k"], c["a"], c["kk"]
        dkmod = dk_s + dk_x
        dk = dkmod * (1.0 + (a - 1.0) * ka_w_)
        da = dkmod * k * ka_w_ + db * kk
        accs[0][...] += _colsum(dkmod * k * (a - 1.0))
        dkk = db * a - dneg
        dq = jnp.where(c["nrm"] > L2_EPS, dkk - kk * _headsum(dkk * kk), dkk) / c["den"]
        dk = dk + dq * kk_w_
        accs[1][...] += _colsum(dq * k)
        dxa = da * a * (1.0 - a)
        accs[2][...] += _colsum(dxa)
        accs[4][...] += _dotb(c["zls"], dxa, _TN)
        dzls = _dotb(dxa, a2p_, _NT)
        dxw = (-ddecay * c["decay"] * c["ew"]) * jax.nn.sigmoid(-c["xw"])
        accs[3][...] += _colsum(dxw)
        accs[5][...] += _dotb(c["th"], dxw, _TN)
        dzls = dzls + _dotb(dxw, w2p_, _NT) * (1.0 - c["th"] * c["th"])
        accs[6][...] += _dotb(c["sg"], dgate, _TN)
        dzls = dzls + _dotb(dgate, g2p_, _NT) * c["sg"] * (1.0 - c["sg"])
        outs[0][...] = jnp.concatenate([dr_s + dr_x, dk, dv_s + dv_x], axis=1)
        outs[1][...] = dzls

    return _row_call("rwkv_prep_bwd", body, T, [cols["rkv"], cols["lat"]] + list(grads),
                     [mu, mul, w0, a0, kk_w, ka_w, w2p, a2p, g2p], [_sds((T, 3 * W)), _sds((T, latp))],
                     [_sds((1, W))] * 4 + [_sds((latp, W))] * 3, prev=[cols["rkv"], cols["lat"]], tile=RWKV_ROW_TILE)


def shift_bwd(cols, dzs, dzls, mu, mul):
    T = dzs.shape[0]

    def body(i, n, tiled, prev, nxt, params, outs, accs):
        for j in range(2):
            z, d, m = tiled[j][...], tiled[2 + j][...], params[j][...]
            zprev = _shift_down(z, _last_row(prev[j], i > 0))
            dnext = _shift_up(d, _first_row(nxt[j], i < n - 1))
            outs[j][...] = (d * (1.0 - m) + dnext * m).astype(BF16)
            accs[j][...] += _colsum(d * (zprev - z))

    return _row_call("shift_bwd", body, T, [cols["rkv"], cols["lat"], dzs, dzls], [mu, mul],
                     [_sds(dzs.shape, BF16), _sds(dzls.shape, BF16)], [_sds(mu.shape), _sds(mul.shape)],
                     prev=[cols["rkv"], cols["lat"]], nxt=[dzs, dzls])


def _window_pick(x, windows):
    gid = lax.broadcasted_iota(jnp.int32, x.shape, 1) // (x.shape[1] // len(windows))
    out = windows[-1]
    for g in range(len(windows) - 2, -1, -1):
        out = jnp.where(gid == g, windows[g], out)
    return out


def _pool_counts(t0, rows, width):
    t = (t0 + lax.broadcasted_iota(jnp.int32, (rows, width), 0) + 1).astype(F32)
    return _window_pick(t, [jnp.minimum(t, float(w)) for w in POOL_WINDOWS])


def _pool_mixed(x, before, t0):
    tm, width = x.shape
    xe = jnp.concatenate([before, x], axis=0)
    sums, s, span = [], xe, 1
    for w in POOL_WINDOWS:
        while span < w:
            s = s + pltpu.roll(s, span, 0)
            span *= 2
        sums.append(s[POOL_HALO:, :])
    return _window_pick(x, sums) / _pool_counts(t0, tm, width) - x


def _group_dot(x, w_ref, contract):
    gd = w_ref.shape[-1]
    parts = [_dotb(x[:, g * gd:(g + 1) * gd], w_ref[g], contract) for g in range(w_ref.shape[0])]
    return jnp.concatenate(parts, axis=1)


def pool_fwd(cols, pool_w, pool_scale):
    T, width = cols["pool"][0].shape[0], cols["pool"][1]
    tm = min(ROW_TILE, T)

    def body(i, n, tiled, prev, nxt, params, outs, accs):
        before = jnp.where(i > 0, prev[0][...], 0.0)
        mixed = _pool_mixed(tiled[0][...], before, i * tm)
        outs[0][...] = (_group_dot(mixed, params[0], _NN) * params[1][...]).astype(BF16)

    return _row_call("pool_fwd", body, T, [cols["pool"]], [pool_w, pool_scale], [_sds((T, width), BF16)],
                     prev=[cols["pool"]], halo=POOL_HALO)[0]


def pool_bwd(cols, dout, pool_w, pool_scale):
    T, width = dout.shape
    tm = min(ROW_TILE, T)

    def body(i, n, tiled, prev, nxt, params, outs, accs):
        w_ref, scale = params[0], params[1][...]
        before = jnp.where(i > 0, prev[0][...], 0.0)
        mixed = _pool_mixed(tiled[0][...], before, i * tm)
        dv = tiled[1][...]
        accs[1][...] += _colsum(dv * _group_dot(mixed, w_ref, _NN))
        after = jnp.where(i < n - 1, nxt[0][...], 0.0)
        dys = jnp.concatenate([dv, after], axis=0) * scale
        gd = w_ref.shape[-1]
        for g in range(w_ref.shape[0]):
            accs[0][g] += _dotb(mixed[:, g * gd:(g + 1) * gd], dys[:tm, g * gd:(g + 1) * gd], _TN)
        dmixed = _group_dot(dys, w_ref, _NT)
        u = dmixed / _pool_counts(i * tm, tm + POOL_HALO, width)
        rows = tm + POOL_HALO
        sums, s, span = [], u, 1
        for w in POOL_WINDOWS:
            while span < w:
                s = s + pltpu.roll(s, rows - span, 0)
                span *= 2
            sums.append(s[:tm, :])
        outs[0][...] = (_window_pick(dv, sums) - dmixed[:tm, :]).astype(BF16)

    return _row_call("pool_bwd", body, T, [cols["pool"], dout], [pool_w, pool_scale], [_sds((T, width), BF16)],
                     [_sds(pool_w.shape), _sds((1, width))], prev=[cols["pool"]], nxt=[dout], halo=POOL_HALO)


def _wkv_consts(pairs):
    lane = lax.broadcasted_iota(jnp.int32, (HEAD, LANES), 1)
    sub = lax.broadcasted_iota(jnp.int32, (pairs * HEAD, LANES), 0)
    lane_all = lax.broadcasted_iota(jnp.int32, (pairs * HEAD, LANES), 1)
    i = lax.broadcasted_iota(jnp.int32, (LANES, LANES), 0)
    j = lax.broadcasted_iota(jnp.int32, (LANES, LANES), 1)
    ones = jnp.where((i < HEAD) == (j < HEAD), 1.0, 0.0).astype(BF16)
    diag = jnp.where((lane_all & (HEAD - 1)) == (sub & (HEAD - 1)), 1.0, 0.0).astype(F32)
    return lane < HEAD, diag, ones


def _segsum(p, in_a):
    sa = jnp.sum(jnp.where(in_a, p, 0.0), axis=1, keepdims=True)
    sb = jnp.sum(jnp.where(in_a, 0.0, p), axis=1, keepdims=True)
    return jnp.where(in_a, sa, sb)


def _hi_lo(p):
    hi = lax.bitcast_convert_type(lax.bitcast_convert_type(p, jnp.uint32) & jnp.uint32(0xFFFF0000), F32)
    return hi, p - hi


def _segsum_mxu(p, ones):
    hi, lo = _hi_lo(p)
    return (jnp.dot(hi.astype(BF16), ones, preferred_element_type=F32)
            + jnp.dot(lo.astype(BF16), ones, preferred_element_type=F32))


def _cat(parts, axis):
    return parts[0] if len(parts) == 1 else jnp.concatenate(parts, axis=axis)


def _tile_rows(row, pairs):
    return _cat([jnp.broadcast_to(row[:, p * LANES:(p + 1) * LANES], (HEAD, LANES)) for p in range(pairs)], 0)


def _spread(row, pairs, diag16, ones):
    hi, lo = _hi_lo(row)
    return (jnp.dot(_tile_rows(hi.astype(BF16), pairs) * diag16, ones, preferred_element_type=F32)
            + jnp.dot(_tile_rows(lo.astype(BF16), pairs) * diag16, ones, preferred_element_type=F32))


def _pair_colsums(x, pairs):
    return _cat([_colsum(x[p * HEAD:(p + 1) * HEAD]) for p in range(pairs)], 1)


def _spread_split(row, pairs, in_a, diag, diag16, ones):
    n_mxu = min(WKV_MXU_PAIRS, pairs)
    parts = [_spread(row[:, :n_mxu * LANES], n_mxu, diag16[:n_mxu * HEAD], ones)]
    parts += [_segsum(row[:, p * LANES:(p + 1) * LANES] * diag[:HEAD], in_a) for p in range(n_mxu, pairs)]
    return _cat(parts, 0)


def _segsum_split(x, pairs, in_a, ones):
    n_mxu = min(WKV_MXU_PAIRS, pairs)
    parts = [_segsum_mxu(x[:n_mxu * HEAD], ones)]
    parts += [_segsum(x[p * HEAD:(p + 1) * HEAD], in_a) for p in range(n_mxu, pairs)]
    return _cat(parts, 0)


def wkv_fwd(r, w, k, v, a, b, carry=None):
    T, W = r.shape
    P = W // LANES
    PB = min(WKV_PAIRS, P)
    chunk = min(WKV_CHUNK_FWD, T)
    NC = T // chunk
    R = PB * HEAD
    ahead = min(WKV_UNROLL, chunk // 2)
    c_in, c_out = (len(carry.inputs), len(carry.out_shapes)) if carry else (0, 0)

    def body(*refs):
        r_ref, w_ref, k_ref, v_ref, a_ref, b_ref = refs[:6]
        y_ref, st_ref, sa_ref = refs[6 + c_in:9 + c_in]
        vt_ref, s_ref = refs[9 + c_in + c_out:11 + c_in + c_out]
        carried = refs[6:6 + c_in], refs[9 + c_in:9 + c_in + c_out], refs[11 + c_in + c_out:]
        g, c = pl.program_id(0), pl.program_id(1)

        if carry:
            @pl.when((g == 0) & (c == 0))
            def _():
                carry.start(*carried)

        @pl.when(c == 0)
        def _():
            s_ref[...] = jnp.zeros_like(s_ref)

        in_a, diag, ones = _wkv_consts(PB)
        diag16 = diag.astype(BF16)

        def step(t, _):
            rows = [ref[pl.ds(t, 1), :] for ref in (w_ref, k_ref, a_ref, b_ref)]
            for p in range(PB):
                wt, kt, at, bt = [x[:, p * LANES:(p + 1) * LANES] for x in rows]
                rs = pl.ds(p * HEAD, HEAD)
                S = s_ref[rs]
                sa = _segsum(S * at, in_a)
                sa_ref[t, rs] = sa
                S = S * wt + sa * bt + vt_ref[t, rs] * kt
                st_ref[t, rs] = S
                s_ref[rs] = S
            return 0

        def spread_step(t, _):
            vt_ref[t + ahead] = _spread(v_ref[pl.ds(t + ahead, 1), :], PB, diag16, ones)
            return step(t, 0)

        for t in range(ahead):
            vt_ref[t] = _spread_split(v_ref[t:t + 1, :], PB, in_a, diag, diag16, ones)
        lax.fori_loop(0, chunk - ahead, spread_step, 0, unroll=WKV_UNROLL)
        lax.fori_loop(chunk - ahead, chunk, step, 0, unroll=WKV_UNROLL)

        def readout(t, _):
            yt = _segsum_split(st_ref[t] * _tile_rows(r_ref[pl.ds(t, 1), :], PB), PB, in_a, ones) * diag
            y_ref[pl.ds(t, 1), :] = _pair_colsums(yt, PB)
            return 0

        lax.fori_loop(0, chunk, readout, 0, unroll=WKV_UNROLL)

        if carry:
            @pl.when((g == P // PB - 1) & (c == NC - 1))
            def _():
                carry.finish(*carried)

    spec = pl.BlockSpec((chunk, PB * LANES), lambda g, c: (c, g))
    tiles = pl.BlockSpec((chunk, R, LANES), lambda g, c: (c, g, 0))
    res = _pallas(
        body, name="wkv_fwd", grid=(P // PB, NC), in_specs=[spec] * 6 + [ANY] * c_in,
        out_specs=[spec, tiles, tiles] + [ANY] * c_out,
        out_shape=[_sds((T, W)), _sds((T, P * HEAD, LANES)), _sds((T, P * HEAD, LANES))]
        + (list(carry.out_shapes) if carry else []),
        scratch_shapes=[pltpu.VMEM((chunk, R, LANES), F32), pltpu.VMEM((R, LANES), F32)]
        + (list(carry.scratch) if carry else []),
        compiler_params=_params("arbitrary", "arbitrary") if carry else _params("parallel", "arbitrary"),
    )(r, w, k, v, a, b, *(carry.inputs if carry else []))
    return (res[:3], res[3:]) if carry else res


def wkv_bwd(r, w, k, v, a, b, dy, st, sa):
    T, W = r.shape
    P = W // LANES
    PB = min(WKV_PAIRS, P)
    chunk = min(WKV_CHUNK, T)
    NC = T // chunk
    R = PB * HEAD
    ahead = min(WKV_UNROLL, chunk // 2)

    def body(r_ref, w_ref, k_ref, v_ref, a_ref, b_ref, dy_ref, st_ref, before_ref, sa_ref,
             dr_ref, dw_ref, dk_ref, dv_ref, da_ref, db_ref, ds_ref, dyt_ref, dst_ref, dsa_ref):
        c = pl.program_id(1)

        @pl.when(c == 0)
        def _():
            ds_ref[...] = jnp.zeros_like(ds_ref)

        in_a, diag, ones = _wkv_consts(PB)
        diag16 = diag.astype(BF16)

        def bstep(n, _):
            t = chunk - 1 - n
            rows = [ref[pl.ds(t, 1), :] for ref in (r_ref, w_ref, a_ref, b_ref)]
            for p in range(PB):
                rt, wt, at, bt = [x[:, p * LANES:(p + 1) * LANES] for x in rows]
                rs = pl.ds(p * HEAD, HEAD)
                dS = ds_ref[rs] + dyt_ref[t, rs] * rt
                dst_ref[t, rs] = dS
                dsa = _segsum(dS * bt, in_a)
                dsa_ref[t, rs] = dsa
                ds_ref[rs] = dS * wt + dsa * at
            return 0

        def spread_bstep(n, _):
            t = chunk - 1 - ahead - n
            dyt_ref[t] = _spread(dy_ref[pl.ds(t, 1), :], PB, diag16, ones)
            return bstep(n, 0)

        for t in range(chunk - ahead, chunk):
            dyt_ref[t] = _spread_split(dy_ref[t:t + 1, :], PB, in_a, diag, diag16, ones)
        lax.fori_loop(0, chunk - ahead, spread_bstep, 0, unroll=WKV_UNROLL)
        lax.fori_loop(chunk - ahead, chunk, bstep, 0, unroll=WKV_UNROLL)

        def collect(t, _):
            sn, dS, dsa = st_ref[t], dst_ref[t], dsa_ref[t]
            sp = st_ref[jnp.maximum(t - 1, 0)]
            dvt = _segsum_split(dS * _tile_rows(k_ref[pl.ds(t, 1), :], PB), PB, in_a, ones) * diag
            vt = _spread_split(v_ref[pl.ds(t, 1), :], PB, in_a, diag, diag16, ones)
            for ref, val in ((dr_ref, sn * dyt_ref[t]), (dw_ref, dS * sp), (dk_ref, dS * vt), (dv_ref, dvt),
                             (da_ref, sp * dsa), (db_ref, dS * sa_ref[t])):
                ref[pl.ds(t, 1), :] = _pair_colsums(val, PB)
            return 0

        lax.fori_loop(0, chunk, collect, 0, unroll=WKV_UNROLL)
        first = jnp.where(c == NC - 1, 0.0, before_ref[0])
        dw_ref[0:1, :] = _pair_colsums(dst_ref[0] * first, PB)
        da_ref[0:1, :] = _pair_colsums(first * dsa_ref[0], PB)

    spec = pl.BlockSpec((chunk, PB * LANES), lambda g, c: (NC - 1 - c, g))
    tiles = pl.BlockSpec((chunk, R, LANES), lambda g, c: (NC - 1 - c, g, 0))
    before = pl.BlockSpec((1, R, LANES), lambda g, c: (jnp.maximum((NC - 1 - c) * chunk - 1, 0), g, 0))

    def scratch(n):
        return pltpu.VMEM((n, R, LANES), F32)

    return _pallas(
        body, name="wkv_bwd", grid=(P // PB, NC), in_specs=[spec] * 7 + [tiles, before, tiles],
        out_specs=[spec] * 6, out_shape=[_sds((T, W))] * 6,
        scratch_shapes=[pltpu.VMEM((R, LANES), F32), scratch(chunk), scratch(chunk), scratch(chunk)],
        compiler_params=_params("parallel", "arbitrary"),
    )(r, w, k, v, a, b, dy, st, st, sa)


def _position():
    return lax.axis_index("x"), lax.axis_index("y"), lax.axis_index("c")


def _other_chips(x, y):
    return [(1 - x, y), (x, 1 - y), (1 - x, 1 - y)]


class _Carry:
    def __init__(self, inputs, out_shapes, scratch, start, finish):
        self.inputs, self.out_shapes, self.scratch, self.start, self.finish = inputs, out_shapes, scratch, start, finish


def _run_carry(name, carry):
    n_in, n_out = len(carry.inputs), len(carry.out_shapes)

    def body(*refs):
        parts = refs[:n_in], refs[n_in:n_in + n_out], refs[n_in + n_out:]
        carry.start(*parts)
        carry.finish(*parts)

    return _pallas(body, name=name, in_specs=[ANY] * n_in, out_specs=[ANY] * n_out, out_shape=list(carry.out_shapes),
                   scratch_shapes=list(carry.scratch))(*carry.inputs)


def gather_carry(shards):
    n = len(shards)

    def plan(x_refs, out_refs, sems):
        send_sems, recv_sems, local_sems = sems
        x, y, c = _position()
        me, sibling = (x, y, c), (x, y, 1 - c)
        chips = _other_chips(x, y)

        def slot(ref, pos):
            return ref.at[4 * pos[0] + 2 * pos[1] + pos[2]]

        def copy(t, j, block, to, src=None):
            dst = slot(out_refs[t], block)
            return pltpu.make_async_remote_copy(
                src_ref=dst if src is None else src, dst_ref=dst, send_sem=send_sems.at[t, j],
                recv_sem=recv_sems.at[t, j], device_id=to, device_id_type=MESH)

        mine = [pltpu.make_async_copy(x_refs[t], slot(out_refs[t], me), local_sems.at[t]) for t in range(n)]
        first = []
        for t in range(n):
            first.append(copy(t, 0, me, sibling, src=x_refs[t]))
            first += [copy(t, 1 + j, me, (*chip, c), src=x_refs[t]) for j, chip in enumerate(chips)]
        return c, me, sibling, chips, copy, mine, first

    def start(x_refs, out_refs, sems):
        _, _, _, _, _, mine, first = plan(x_refs, out_refs, sems)
        for cp in mine + first:
            cp.start()

    def finish(x_refs, out_refs, sems):
        c, me, sibling, chips, copy, mine, first = plan(x_refs, out_refs, sems)
        passed = []
        for t in range(n):
            for j, chip in enumerate(chips):
                copy(t, 1 + j, (*chip, c), me).wait_recv()
                fwd = copy(t, 4 + j, (*chip, c), sibling)
                fwd.start()
                passed.append(fwd)
        for t in range(n):
            copy(t, 0, sibling, me).wait_recv()
            for j, chip in enumerate(chips):
                copy(t, 4 + j, (*chip, 1 - c), me).wait_recv()
        for cp in first + passed:
            cp.wait_send()
        for cp in mine:
            cp.wait()

    return _Carry(list(shards), [_sds((N_DEV,) + s.shape, s.dtype) for s in shards],
                  [pltpu.SemaphoreType.DMA((n, 7)), pltpu.SemaphoreType.DMA((n, 7)), pltpu.SemaphoreType.DMA((n,))],
                  start, finish)


def all_gather(shards):
    return _run_carry("all_gather", gather_carry(shards))


def exchange_sibling(parts):
    n = len(parts)

    def body(*refs):
        p_refs, out_refs = refs[:n], refs[n:2 * n]
        send_sems, recv_sems = refs[2 * n:]
        x, y, c = _position()
        copies = []
        for t in range(n):
            for q in range(N_CHIP):
                cp = pltpu.make_async_remote_copy(
                    src_ref=p_refs[t].at[q, 1 - c], dst_ref=out_refs[t].at[q], send_sem=send_sems.at[t, q],
                    recv_sem=recv_sems.at[t, q], device_id=(x, y, 1 - c), device_id_type=MESH)
                cp.start()
                copies.append(cp)
        for cp in copies:
            cp.wait()

    return _pallas(
        body, name="exchange_sibling", in_specs=[ANY] * n, out_specs=[ANY] * n,
        out_shape=[_sds((N_CHIP,) + p.shape[2:], p.dtype) for p in parts],
        scratch_shapes=[pltpu.SemaphoreType.DMA((n, N_CHIP)), pltpu.SemaphoreType.DMA((n, N_CHIP))],
    )(*parts)


def chips_carry(parts):
    n = len(parts)

    def plan(p_refs, out_refs, sems):
        send_sems, recv_sems, local_sems = sems
        x, y, c = _position()
        local = [pltpu.make_async_copy(p_refs[t].at[2 * x + y], out_refs[t].at[3], local_sems.at[t]) for t in range(n)]
        remote = [pltpu.make_async_remote_copy(
            src_ref=p_refs[t].at[2 * cx + cy], dst_ref=out_refs[t].at[j], send_sem=send_sems.at[t, j],
            recv_sem=recv_sems.at[t, j], device_id=(cx, cy, c), device_id_type=MESH)
            for t in range(n) for j, (cx, cy) in enumerate(_other_chips(x, y))]
        return local, remote

    def start(p_refs, out_refs, sems):
        local, remote = plan(p_refs, out_refs, sems)
        for cp in local + remote:
            cp.start()

    def finish(p_refs, out_refs, sems):
        local, remote = plan(p_refs, out_refs, sems)
        for cp in remote + local:
            cp.wait()

    return _Carry(list(parts), [_sds(p.shape, p.dtype) for p in parts],
                  [pltpu.SemaphoreType.DMA((n, 3)), pltpu.SemaphoreType.DMA((n, 3)), pltpu.SemaphoreType.DMA((n,))],
                  start, finish)


def _flat_tile(rows, cols):
    tr = rows
    for d in range(16, min(rows, 512) + 1, 16):
        if rows % d == 0 and d * cols * 4 <= 2 * 1024 * 1024:
            tr = d
    return tr


def pair_add(part, recv):
    _, _, R, C = part.shape
    tr = _flat_tile(R, C)
    core = jnp.reshape(lax.axis_index("c"), (1,)).astype(jnp.int32)

    def body(core_ref, p_ref, r_ref, o_ref):
        o_ref[...] = (p_ref[...] + r_ref[...]).astype(BF16)

    grid_spec = pltpu.PrefetchScalarGridSpec(
        num_scalar_prefetch=1, grid=(N_CHIP, R // tr),
        in_specs=[pl.BlockSpec((None, None, tr, C), lambda q, i, core_ref: (q, core_ref[0], i, 0)),
                  pl.BlockSpec((None, tr, C), lambda q, i, core_ref: (q, i, 0))],
        out_specs=pl.BlockSpec((None, tr, C), lambda q, i, core_ref: (q, i, 0)))
    return _pallas(body, name="pair_add", grid_spec=grid_spec, out_shape=_sds((N_CHIP, R, C), BF16),
                   compiler_params=_params("parallel", "parallel"))(core, part, recv)


def adamw(w, m, v, slabs):
    R, C = w.shape
    tr = _flat_tile(R, C)
    n = slabs.shape[0]

    def body(w_ref, m_ref, v_ref, s_ref, g_ref, d_ref, nm_ref, nv_ref):
        g = s_ref[0].astype(F32)
        for j in range(1, n):
            g = g + s_ref[j].astype(F32)
        m2 = ADAM_B1 * m_ref[...] + (1.0 - ADAM_B1) * g
        v2 = ADAM_B2 * v_ref[...] + (1.0 - ADAM_B2) * (g * g)
        m_hat = m2 / (1.0 - ADAM_B1 ** ADAM_STEP)
        v_hat = v2 / (1.0 - ADAM_B2 ** ADAM_STEP)
        g_ref[...] = g
        d_ref[...] = -ADAM_LR * (m_hat / (jnp.sqrt(v_hat) + ADAM_EPS) + ADAM_WD * w_ref[...])
        nm_ref[...] = m2
        nv_ref[...] = v2

    spec = pl.BlockSpec((tr, C), lambda i: (i, 0))
    return _pallas(body, name="adamw", grid=(R // tr,),
                   in_specs=[spec] * 3 + [pl.BlockSpec((n, tr, C), lambda i: (0, i, 0))], out_specs=[spec] * 4,
                   out_shape=[_sds((R, C))] * 4, compiler_params=_params("parallel"))(w, m, v, slabs)


def _unshard_cols(g):
    return jnp.transpose(g, (1, 0, 2)).reshape(g.shape[1], -1)


def _unshard_rows(g):
    return g.reshape(-1, g.shape[2])


def _shard_cols(full):
    R, C = full.shape
    return jnp.transpose(full.reshape(R, N_DEV, C // N_DEV), (1, 0, 2)).reshape(N_CHIP, 2, R, C // N_DEV)


def _shard_rows(full):
    R, C = full.shape
    return full.reshape(N_CHIP, 2, R // N_DEV, C)


WEIGHTS = ['ln_ffn1_pre', 'ln_ffn1_post', 'ffn1_gate', 'ffn1_up', 'ffn1_down', 'ln_mix_pre', 'ln_mix_post', 'w_in',
           'rwkv_mu', 'rwkv_w0', 'rwkv_w2', 'rwkv_a0', 'rwkv_a2', 'rwkv_g2', 'rwkv_k_k', 'rwkv_k_a', 'rwkv_r_k',
           'rwkv_gn_w', 'rwkv_gn_b', 'w_proj_a', 'pool_w', 'pool_scale', 'w_proj_b', 'w_out', 'ln_ffn2_pre',
           'ln_ffn2_post', 'ffn2_gate', 'ffn2_up', 'ffn2_down']
COL_SHARDED = ['ffn1_gate', 'ffn1_up', 'ffn2_gate', 'ffn2_up', 'w_in', 'rwkv_w2', 'rwkv_a2', 'rwkv_g2', 'w_proj_a',
               'w_proj_b']
ROW_SHARDED = ['ffn1_down', 'ffn2_down', 'w_out', 'pool_w']
TRANSPOSED = ['ffn1_gate', 'ffn1_up', 'ffn2_gate', 'ffn2_up', 'w_in', 'w_proj_a', 'w_proj_b']
SHARDED = COL_SHARDED + ROW_SHARDED
REPLICATED = [n for n in WEIGHTS if n not in SHARDED]


def _step(args):
    wts = {n: args[n] if args[n].ndim == 2 else args[n][0] for n in WEIGHTS}
    x, target = args["x"][0], args["loss_target"][0]
    T, D = x.shape
    W = wts["rwkv_w0"].shape[1]
    PW = wts["pool_scale"].shape[1]
    LW, LA, LG = wts["rwkv_w2"].shape[0], wts["rwkv_a2"].shape[0], wts["rwkv_g2"].shape[0]
    lat = LW + LA + LG
    latp = _round_up(lat, LAT_ALIGN)
    rc = 3 * W + lat
    base = 3 * W + PW + 2 * D
    n_groups, gshard, gd = wts["pool_w"].shape

    pool_w_shard = wts["pool_w"].reshape(n_groups * gshard, gd)
    shards = {n: (pool_w_shard if n == "pool_w" else wts[n]).astype(BF16) for n in SHARDED}
    shards.update({n: shards[n].T for n in TRANSPOSED})
    full = {}

    def fetch(names):
        return gather_carry([shards[n] for n in names])

    def arrived(names, got):
        for n, g in zip(names, got):
            if n == "pool_w":
                full[n] = jnp.transpose(g.reshape(N_DEV, n_groups, gshard, gd), (1, 0, 2, 3)).reshape(n_groups, gd, gd)
            elif n in COL_SHARDED and n not in TRANSPOSED:
                full[n] = _unshard_cols(g)
            else:
                full[n] = _unshard_rows(g)

    arrived(["ffn1_gate", "ffn1_up"], all_gather([shards["ffn1_gate"], shards["ffn1_up"]]))
    mu = wts["rwkv_mu"]
    mu_rkv = mu[:, :3 * W]
    mu_lat = jnp.concatenate([mu[:, 3 * W:], jnp.zeros((1, latp - lat), F32)], axis=1)
    rk = wts["rwkv_r_k"].reshape(1, W)

    n1 = rms_pre(x, wts["ln_ffn1_pre"])
    (g1, u1, act1), got = _mm("ffn1_up", [n1], [full["ffn1_gate"], full["ffn1_up"]], "nt", [BF16] * 3,
                              epilogue=_swiglu_fwd_epi, carry=fetch(["ffn1_down"]))
    arrived(["ffn1_down"], got)
    (f1,), got = _mm("ffn1_down", [act1], [full["ffn1_down"]], "nn", [F32], tm=512, carry=fetch(["w_in"]))
    arrived(["w_in"], got)
    w_in = full["w_in"]
    w_in_p = jnp.concatenate([w_in[:3 * W], w_in[rc:], w_in[3 * W:rc], jnp.zeros((latp - lat, D), BF16)], axis=0)
    h1, nm = post_pre(x, f1, wts["ln_ffn1_post"], wts["ln_mix_pre"], MACARON)
    mixer = ["rwkv_w2", "rwkv_a2", "rwkv_g2", "w_proj_a", "w_proj_b", "pool_w", "w_out"]
    (p,), got = _mm("in_proj", [nm], [w_in_p], "nt", [F32], carry=fetch(mixer))
    arrived(mixer, got)
    pool_w = full["pool_w"]

    def pad_rows(m, at):
        return jnp.zeros((latp, W), BF16).at[at:at + m.shape[0]].set(m)

    w2p, a2p, g2p = pad_rows(full["rwkv_w2"], 0), pad_rows(full["rwkv_a2"], LW), pad_rows(full["rwkv_g2"], LW + LA)
    small = [mu_rkv, mu_lat, wts["rwkv_w0"], wts["rwkv_a0"], wts["rwkv_k_k"], wts["rwkv_k_a"], w2p, a2p, g2p]
    cols = {"rkv": (p, 3 * W, 0), "pool": (p, PW, 3 * W // PW), "lat": (p, latp, base // latp)}
    r, decay, kmod, v, aneg, bpos, gate = rwkv_prep(p, cols, *small)
    (y, states, sdota), got = wkv_fwd(r, decay, kmod, v, aneg, bpos, carry=fetch(["ffn2_gate", "ffn2_up"]))
    arrived(["ffn2_gate", "ffn2_up"], got)
    ya_in = rwkv_post(y, r, kmod, v, gate, wts["rwkv_gn_w"], wts["rwkv_gn_b"], rk)
    yb_in = pool_fwd(cols, pool_w, wts["pool_scale"])
    gates = [(p, 3 * W + PW), (p, 3 * W + PW + D)]
    m, ya, yb = _mm("mix", [ya_in, yb_in], [full["w_proj_a"], full["w_proj_b"]], "nt", [BF16] * 3,
                    extras=gates, epilogue=_mix_fwd_epi)
    mx = _mm("out_proj", [m], [full["w_out"]], "nn", [F32])[0]
    h2, n2 = post_pre(h1, mx, wts["ln_mix_post"], wts["ln_ffn2_pre"], 1.0)
    (g2_, u2, act2), got = _mm("ffn2_up", [n2], [full["ffn2_gate"], full["ffn2_up"]], "nt", [BF16] * 3,
                               epilogue=_swiglu_fwd_epi, carry=fetch(["ffn2_down"]))
    arrived(["ffn2_down"], got)
    f2 = _mm("ffn2_down", [act2], [full["ffn2_down"]], "nn", [F32], tm=512)[0]

    grads, slabs = {}, {}
    to_sibling = {}

    def pair_sums(names):
        parts = []
        for n in names:
            if n == "pool_w":
                parts.append(jnp.transpose(grads[n].reshape(n_groups, N_DEV, gshard, gd), (1, 0, 2, 3)).reshape(
                    N_CHIP, 2, n_groups * gshard, gd))
            elif n in COL_SHARDED and n not in TRANSPOSED:
                parts.append(_shard_cols(grads[n]))
            else:
                parts.append(_shard_rows(grads[n]))
        sent = [_shard_rows(to_sibling[n]) if n in to_sibling else part for n, part in zip(names, parts)]
        return [pair_add(part, rcv) for part, rcv in zip(parts, exchange_sibling(sent))]

    def landed(names, got):
        slabs.update(zip(names, got))

    dh3, df2, loss_part, grads["ln_ffn2_post"] = loss_post_bwd(h2, f2, wts["ln_ffn2_post"], target, MACARON)
    dg2, du2 = _mm("ffn2_dact", [df2], [full["ffn2_down"]], "nt", [BF16] * 2, extras=[(g2_, 0), (u2, 0)],
                   epilogue=_swiglu_bwd_epi)
    grads["ffn2_down"], to_sibling["ffn2_down"] = _mm("ffn2_ddown", [act2], [df2], "tn", [F32, BF16], tm=512, tn=1024,
                                                      epilogue=_twice_epi)
    grads["ffn2_gate"], grads["ffn2_up"], to_sibling["ffn2_gate"], to_sibling["ffn2_up"] = _mm(
        "ffn2_dup", [dg2, du2], [n2], "tn", [F32, F32, BF16, BF16], tm=512, epilogue=_twice_epi)
    dn2 = _mm("ffn2_dn_gate", [dg2], [full["ffn2_gate"]], "nn", [F32], tm=512)[0]
    dn2 = _mm("ffn2_dn", [du2], [full["ffn2_up"]], "nn", [F32], tm=512, extras=[(dn2, 0)], epilogue=_add_epi)[0]
    sums2 = pair_sums(["ffn2_down", "ffn2_gate", "ffn2_up"])
    dh2, grads["ln_ffn2_pre"] = pre_bwd(dn2, h2, wts["ln_ffn2_pre"], dh3)
    dmx, grads["ln_mix_post"] = post_bwd(dh2, mx, wts["ln_mix_post"], 1.0)
    (dya, dyb, dga, dgb), got = _mm("dmix", [dmx], [full["w_out"]], "nt", [BF16] * 4, extras=gates + [(ya, 0), (yb, 0)],
                                    epilogue=_mix_bwd_epi, carry=chips_carry(sums2[:1]))
    landed(["ffn2_down"], got)
    grads["w_out"] = _mm("dw_out", [m], [dmx], "tn", [F32])[0]
    dya_in = _mm("dproj_a", [dya], [full["w_proj_a"]], "nn", [F32])[0]
    dyb_in = _mm("dproj_b", [dyb], [full["w_proj_b"]], "nn", [F32])[0]
    grads["w_proj_a"] = _mm("dw_proj_a", [dya], [ya_in], "tn", [F32])[0]
    grads["w_proj_b"] = _mm("dw_proj_b", [dyb], [yb_in], "tn", [F32])[0]
    dz_pool, grads["pool_w"], grads["pool_scale"] = pool_bwd(cols, dyb_in, pool_w, wts["pool_scale"])
    dy, dr_x, dk_x, dv_x, dgate, grads["rwkv_gn_w"], grads["rwkv_gn_b"], drk = rwkv_post_bwd(
        dya_in, y, r, kmod, v, gate, wts["rwkv_gn_w"], wts["rwkv_gn_b"], rk)
    grads["rwkv_r_k"] = drk.reshape(wts["rwkv_r_k"].shape)
    dr_s, ddecay, dk_s, dv_s, dneg, dbpos = wkv_bwd(r, decay, kmod, v, aneg, bpos, dy, states, sdota)
    (dzs, dzls, grads["rwkv_k_a"], grads["rwkv_k_k"], grads["rwkv_a0"], grads["rwkv_w0"], da2p, dw2p, dg2p) = rwkv_prep_bwd(
        p, cols, [dr_s, dr_x, ddecay, dk_s, dk_x, dv_s, dv_x, dneg, dbpos, dgate], *small)
    grads["rwkv_w2"], grads["rwkv_a2"], grads["rwkv_g2"] = dw2p[:LW], da2p[LW:LW + LA], dg2p[LW + LA:lat]
    dz_rkv, dz_lat, dmu_rkv, dmu_lat = shift_bwd(cols, dzs, dzls, mu_rkv, mu_lat)
    grads["rwkv_mu"] = jnp.concatenate([dmu_rkv, dmu_lat[:, :lat]], axis=1)
    dp = jnp.concatenate([dz_rkv, dz_pool, dga, dgb, dz_lat], axis=1)
    (dnm,), got = _mm("din_proj", [dp], [w_in_p], "nn", [F32], tm=512, carry=chips_carry(sums2[1:2]))
    landed(["ffn2_gate"], got)
    (dw_in_p,), got = _mm("dw_in", [dp], [nm], "tn", [F32], tm=512, tn=1024, carry=chips_carry(sums2[2:]))
    landed(["ffn2_up"], got)
    grads["w_in"] = jnp.concatenate([dw_in_p[:3 * W], dw_in_p[base:base + lat], dw_in_p[3 * W:base]], axis=0)
    sums_mix = pair_sums(["w_in"] + mixer)
    dh1, grads["ln_mix_pre"] = pre_bwd(dnm, h1, wts["ln_mix_pre"], dh2)
    df1, grads["ln_ffn1_post"] = post_bwd(dh1, f1, wts["ln_ffn1_post"], MACARON)
    (dg1, du1), got = _mm("ffn1_dact", [df1], [full["ffn1_down"]], "nt", [BF16] * 2, extras=[(g1, 0), (u1, 0)],
                          epilogue=_swiglu_bwd_epi, carry=chips_carry(sums_mix[:1]))
    landed(["w_in"], got)
    (grads["ffn1_down"], to_sibling["ffn1_down"]), got = _mm(
        "ffn1_ddown", [act1], [df1], "tn", [F32, BF16], tm=512, tn=1024, epilogue=_twice_epi,
        carry=chips_carry(sums_mix[1:]))
    landed(mixer, got)
    (grads["ffn1_gate"], grads["ffn1_up"], to_sibling["ffn1_gate"], to_sibling["ffn1_up"]), got = _mm(
        "ffn1_dup", [dg1, du1], [n1], "tn", [F32, F32, BF16, BF16], tm=512, epilogue=_twice_epi,
        carry=chips_carry(pair_sums(["ffn1_down"])))
    landed(["ffn1_down"], got)
    sums1 = pair_sums(["ffn1_gate", "ffn1_up"])
    (dn1,), got = _mm("ffn1_dn_gate", [dg1], [full["ffn1_gate"]], "nn", [F32], tm=512, carry=chips_carry(sums1[:1]))
    landed(["ffn1_gate"], got)
    (dn1,), got = _mm("ffn1_dn", [du1], [full["ffn1_up"]], "nn", [F32], tm=512, extras=[(dn1, 0)], epilogue=_add_epi,
                      carry=chips_carry(sums1[1:]))
    landed(["ffn1_up"], got)
    grad_x, grads["ln_ffn1_pre"] = pre_bwd(dn1, x, wts["ln_ffn1_pre"], dh1)

    flat = jnp.concatenate([grads[n].reshape(-1) for n in REPLICATED])
    n_small = flat.shape[0]
    rows = _round_up(n_small, 8 * LANES) // LANES
    flat = jnp.concatenate([flat, jnp.zeros((rows * LANES - n_small,), F32)]).reshape(rows, LANES)
    small_slabs = all_gather([flat])[0]

    def packed(prefix):
        vals = jnp.concatenate([args[prefix + n].reshape(-1) for n in REPLICATED])
        return jnp.concatenate([vals, jnp.ones((rows * LANES - n_small,), F32)]).reshape(rows, LANES)

    outs = {}
    small_out = adamw(packed(""), packed("m_"), packed("v_"), small_slabs)
    offset = 0
    for n in REPLICATED:
        size = args[n].size
        outs[n] = [o.reshape(-1)[offset:offset + size].reshape(args[n].shape) for o in small_out]
        offset += size
    for n in SHARDED:
        slab = jnp.swapaxes(slabs[n], 1, 2) if n in TRANSPOSED else slabs[n]
        shard2d = slab.shape[1:]
        res = adamw(*[args[pre + n].reshape(shard2d) for pre in ("", "m_", "v_")], slab)
        outs[n] = [o.reshape(args[n].shape) for o in res]

    loss = lax.psum(loss_part[0, 0], ("x", "y", "c"))
    return (loss, grad_x[None], *[outs[n][0] for n in WEIGHTS], *[outs[n][1] for n in WEIGHTS],
            *[outs[n][2] for n in WEIGHTS], *[outs[n][3] for n in WEIGHTS])


ARG_NAMES = ["x"] + WEIGHTS + ["loss_target"] + ["m_" + n for n in WEIGHTS] + ["v_" + n for n in WEIGHTS]


def kernel(x, ln_ffn1_pre, ln_ffn1_post, ffn1_gate, ffn1_up, ffn1_down, ln_mix_pre, ln_mix_post, w_in, rwkv_mu, rwkv_w0,
           rwkv_w2, rwkv_a0, rwkv_a2, rwkv_g2, rwkv_k_k, rwkv_k_a, rwkv_r_k, rwkv_gn_w, rwkv_gn_b, w_proj_a, pool_w,
           pool_scale, w_proj_b, w_out, ln_ffn2_pre, ln_ffn2_post, ffn2_gate, ffn2_up, ffn2_down, loss_target,
           m_ln_ffn1_pre, m_ln_ffn1_post, m_ffn1_gate, m_ffn1_up, m_ffn1_down, m_ln_mix_pre, m_ln_mix_post, m_w_in,
           m_rwkv_mu, m_rwkv_w0, m_rwkv_w2, m_rwkv_a0, m_rwkv_a2, m_rwkv_g2, m_rwkv_k_k, m_rwkv_k_a, m_rwkv_r_k,
           m_rwkv_gn_w, m_rwkv_gn_b, m_w_proj_a, m_pool_w, m_pool_scale, m_w_proj_b, m_w_out, m_ln_ffn2_pre,
           m_ln_ffn2_post, m_ffn2_gate, m_ffn2_up, m_ffn2_down, v_ln_ffn1_pre, v_ln_ffn1_post, v_ffn1_gate, v_ffn1_up,
           v_ffn1_down, v_ln_mix_pre, v_ln_mix_post, v_w_in, v_rwkv_mu, v_rwkv_w0, v_rwkv_w2, v_rwkv_a0, v_rwkv_a2,
           v_rwkv_g2, v_rwkv_k_k, v_rwkv_k_a, v_rwkv_r_k, v_rwkv_gn_w, v_rwkv_gn_b, v_w_proj_a, v_pool_w, v_pool_scale,
           v_w_proj_b, v_w_out, v_ln_ffn2_pre, v_ln_ffn2_post, v_ffn2_gate, v_ffn2_up, v_ffn2_down):
    given = locals()
    return _step({n: given[n] for n in ARG_NAMES})
```

```python
import jax
import jax.numpy as jnp
from jax import lax
from jax.experimental import pallas as pl
from jax.experimental.pallas import tpu as pltpu

F32, BF16 = jnp.float32, jnp.bfloat16
N_DEV = 8
N_CHIP = 4
HEAD = 64
LANES = 2 * HEAD
NORM_EPS, GN_EPS, L2_EPS = 1e-6, 64e-5, 1e-12
POOL_WINDOWS = (2, 4, 8, 16)
POOL_HALO = 16
MACARON = 0.5
ADAM_LR, ADAM_B1, ADAM_B2, ADAM_EPS, ADAM_WD, ADAM_STEP = 0.001, 0.9, 0.999, 1e-08, 0.01, 10
VMEM_LIMIT = 48 * 1024 * 1024
MM_VMEM_BUDGET = 40 * 1024 * 1024
ROW_TILE = 256
RWKV_ROW_TILE = 128
LAT_ALIGN = 512
WKV_CHUNK, WKV_PAIRS = 16, 8
WKV_CHUNK_FWD = 32
WKV_UNROLL = 4
WKV_MXU_PAIRS = 5
MESH = pl.DeviceIdType.MESH


def _pallas(body, **kw):
    return pl.pallas_call(body, **kw)


def _params(*sem):
    return pltpu.CompilerParams(dimension_semantics=sem, vmem_limit_bytes=VMEM_LIMIT)


def _tile(n, target, align=128):
    best = None
    for d in range(align, min(n, target) + 1, align):
        if n % d == 0:
            best = d
    return best if best is not None else n


def _round_up(n, m):
    return (n + m - 1) // m * m


ANY = pl.BlockSpec(memory_space=pl.ANY)


def _mm(name, a_list, b_list, mode, out_dtypes, *, sum_pairs=False, extras=(), epilogue=None, tm=1024, tn=512,
        carry=None):
    n_a, n_b = len(a_list), len(b_list)
    n_prod = max(n_a, n_b)
    assert n_a in (1, n_prod) and n_b in (1, n_prod)
    a0, b0 = a_list[0], b_list[0]
    if mode == "nn":
        (M, K), N = a0.shape, b0.shape[1]
    elif mode == "nt":
        (M, K), N = a0.shape, b0.shape[0]
    else:
        (K, M), N = a0.shape, b0.shape[1]
    tm, tn = _tile(M, tm), _tile(N, tn)
    n_acc = 1 if sum_pairs else n_prod
    n_ex = len(extras)

    def planned(tk):
        operands = 2 * 2 * tk * (n_a * tm + n_b * tn)
        tiles = 2 * tm * tn * (sum(e.dtype.itemsize for e, _ in extras) + sum(jnp.dtype(d).itemsize for d in out_dtypes))
        return operands + tiles + 4 * tm * tn * (n_acc + len(out_dtypes))

    tk = max([d for d in range(128, K + 1, 128) if K % d == 0 and planned(d) <= MM_VMEM_BUDGET] or [_tile(K, 512)])
    nk = K // tk
    if mode == "tn":
        a_spec = pl.BlockSpec((tk, tm), lambda i, j, k: (k, i))
    else:
        a_spec = pl.BlockSpec((tm, tk), lambda i, j, k: (i, k))
    if mode == "nt":
        b_spec = pl.BlockSpec((tn, tk), lambda i, j, k: (j, k))
    else:
        b_spec = pl.BlockSpec((tk, tn), lambda i, j, k: (k, j))
    contract = {"nn": ((1,), (0,)), "nt": ((1,), (1,)), "tn": ((0,), (0,))}[mode]
    e_specs = []
    for _, col in extras:
        assert col % tn == 0
        e_specs.append(pl.BlockSpec((tm, tn), lambda i, j, k, off=col // tn: (i, j + off)))
    o_spec = pl.BlockSpec((tm, tn), lambda i, j, k: (i, j))

    n_in, n_out, n_scr = n_a + n_b + n_ex, len(out_dtypes), (n_acc if nk > 1 else 0)
    c_in, c_out = (len(carry.inputs), len(carry.out_shapes)) if carry else (0, 0)
    grid = (M // tm, N // tn, nk)

    def body(*refs):
        a_refs, b_refs, e_refs = refs[:n_a], refs[n_a:n_a + n_b], refs[n_a + n_b:n_in]
        o_refs = refs[n_in + c_in:n_in + c_in + n_out]
        acc_refs = refs[n_in + c_in + n_out + c_out:n_in + c_in + n_out + c_out + n_scr]
        carried = (refs[n_in:n_in + c_in], refs[n_in + c_in + n_out:n_in + c_in + n_out + c_out],
                   refs[n_in + c_in + n_out + c_out + n_scr:])
        at = [pl.program_id(d) for d in range(3)]

        if carry:
            @pl.when((at[0] == 0) & (at[1] == 0) & (at[2] == 0))
            def _():
                carry.start(*carried)

        def products():
            a_vals, b_vals = [a[...] for a in a_refs], [b[...] for b in b_refs]
            prods = [lax.dot_general(a_vals[p if n_a > 1 else 0], b_vals[p if n_b > 1 else 0], (contract, ((), ())),
                                     preferred_element_type=F32) for p in range(n_prod)]
            return [sum(prods[1:], prods[0])] if sum_pairs else prods

        def finish(results):
            outs = epilogue(results, [e[...] for e in e_refs]) if epilogue else results
            for o_ref, o in zip(o_refs, outs):
                o_ref[...] = o.astype(o_ref.dtype)

        if nk == 1:
            finish(products())
        else:
            @pl.when(at[2] == 0)
            def _():
                for acc in acc_refs:
                    acc[...] = jnp.zeros_like(acc)

            for acc, prod in zip(acc_refs, products()):
                acc[...] += prod

            @pl.when(at[2] == nk - 1)
            def _():
                finish([acc[...] for acc in acc_refs])

        if carry:
            @pl.when((at[0] == grid[0] - 1) & (at[1] == grid[1] - 1) & (at[2] == grid[2] - 1))
            def _():
                carry.finish(*carried)

    res = _pallas(
        body, name=name, grid=grid,
        in_specs=[a_spec] * n_a + [b_spec] * n_b + e_specs + [ANY] * c_in,
        out_specs=[o_spec] * n_out + [ANY] * c_out,
        out_shape=[jax.ShapeDtypeStruct((M, N), dt) for dt in out_dtypes] + (list(carry.out_shapes) if carry else []),
        scratch_shapes=[pltpu.VMEM((tm, tn), F32)] * n_scr + (list(carry.scratch) if carry else []),
        compiler_params=_params("arbitrary", "arbitrary", "arbitrary") if carry else _params("parallel", "parallel", "arbitrary"),
    )(*a_list, *b_list, *[e for e, _ in extras], *(carry.inputs if carry else []))
    return (res[:n_out], res[n_out:]) if carry else res


def _swiglu_fwd_epi(accs, _):
    g, u = accs
    return [g, u, g * jax.nn.sigmoid(g) * u]


def _swiglu_bwd_epi(accs, ex):
    dact = accs[0]
    g, u = ex[0].astype(F32), ex[1].astype(F32)
    sg = jax.nn.sigmoid(g)
    return [dact * u * (sg * (1.0 + g * (1.0 - sg))), dact * (g * sg)]


def _add_epi(accs, ex):
    return [accs[0] + ex[0]]


def _twice_epi(accs, _):
    return list(accs) + list(accs)


def _mix_fwd_epi(accs, ex):
    ya, yb = accs
    return [jax.nn.sigmoid(ex[0]) * ya + jax.nn.sigmoid(ex[1]) * yb, ya, yb]


def _mix_bwd_epi(accs, ex):
    dm = accs[0]
    sa, sb = jax.nn.sigmoid(ex[0]), jax.nn.sigmoid(ex[1])
    ya, yb = ex[2].astype(F32), ex[3].astype(F32)
    return [dm * sa, dm * sb, dm * ya * sa * (1.0 - sa), dm * yb * sb * (1.0 - sb)]


def _row_call(name, body, T, tiled, params, outs, accs=(), prev=(), nxt=(), halo=8, tile=ROW_TILE):
    tm = min(tile, T)
    n_tiles = T // tm

    def norm(e):
        return e if isinstance(e, tuple) else (e, e.shape[1], 0)

    tiled, prev, nxt = [norm(e) for e in tiled], [norm(e) for e in prev], [norm(e) for e in nxt]
    per_halo, n_halo = tm // halo, T // halo
    in_specs = [pl.BlockSpec((tm, w), lambda i, cb=cb: (i, cb)) for _, w, cb in tiled]
    in_specs += [pl.BlockSpec((halo, w), lambda i, cb=cb: (jnp.maximum(i * per_halo - 1, 0), cb)) for _, w, cb in prev]
    in_specs += [pl.BlockSpec((halo, w), lambda i, cb=cb: (jnp.minimum((i + 1) * per_halo, n_halo - 1), cb))
                 for _, w, cb in nxt]
    in_specs += [pl.BlockSpec(p.shape, lambda i, nd=p.ndim: (0,) * nd) for p in params]
    out_specs = [pl.BlockSpec((tm, o.shape[1]), lambda i: (i, 0)) for o in outs]
    out_specs += [pl.BlockSpec(a.shape, lambda i, nd=len(a.shape): (0,) * nd) for a in accs]
    n1, n2, n3, n4, n5 = len(tiled), len(prev), len(nxt), len(params), len(outs)

    def kernel_body(*refs):
        i = pl.program_id(0)
        acc_refs = refs[n1 + n2 + n3 + n4 + n5:]

        @pl.when(i == 0)
        def _():
            for a in acc_refs:
                a[...] = jnp.zeros_like(a)

        body(i, n_tiles, refs[:n1], refs[n1:n1 + n2], refs[n1 + n2:n1 + n2 + n3],
             refs[n1 + n2 + n3:n1 + n2 + n3 + n4], refs[n1 + n2 + n3 + n4:n1 + n2 + n3 + n4 + n5], acc_refs)

    return _pallas(
        kernel_body, name=name, grid=(n_tiles,), in_specs=in_specs, out_specs=out_specs,
        out_shape=list(outs) + list(accs),
        compiler_params=_params("arbitrary"),
    )(*[e[0] for e in tiled + prev + nxt], *params)


def _sds(shape, dtype=F32):
    return jax.ShapeDtypeStruct(tuple(shape), dtype)


def _rstd(x):
    return lax.rsqrt(jnp.mean(x * x, axis=-1, keepdims=True) + NORM_EPS)


def _colsum(x):
    return jnp.sum(x, axis=0, keepdims=True)


def rms_pre(x, g):
    T, D = x.shape

    def body(i, n, tiled, prev, nxt, params, outs, accs):
        xv = tiled[0][...]
        outs[0][...] = (xv * _rstd(xv) * params[0][...]).astype(BF16)

    return _row_call("rms_pre", body, T, [x], [g], [_sds((T, D), BF16)])[0]


def post_pre(h, f, g_post, g_pre, scale):
    T, D = h.shape

    def body(i, n, tiled, prev, nxt, params, outs, accs):
        hv, fv = tiled[0][...], tiled[1][...]
        h2 = hv + scale * (fv * _rstd(fv) * params[0][...])
        outs[0][...] = h2
        outs[1][...] = (h2 * _rstd(h2) * params[1][...]).astype(BF16)

    return _row_call("post_pre", body, T, [h, f], [g_post, g_pre], [_sds((T, D)), _sds((T, D), BF16)])


def _post_bwd_math(dh, fv, g, scale):
    r = _rstd(fv)
    fhat = fv * r
    dy = scale * dh
    z = dy * g
    df = r * (z - fhat * jnp.mean(z * fhat, axis=-1, keepdims=True))
    return df, _colsum(dy * fhat)


def loss_post_bwd(h, f, g_post, target, scale):
    T, D = h.shape

    def body(i, n, tiled, prev, nxt, params, outs, accs):
        hv, fv, tv = tiled[0][...], tiled[1][...], tiled[2][...]
        g = params[0][...]
        e = hv + scale * (fv * _rstd(fv) * g) - tv
        accs[0][...] += jnp.full(accs[0].shape, 0.5 / D, F32) * jnp.sum(e * e)
        dh = e * (1.0 / D)
        outs[0][...] = dh
        df, dg = _post_bwd_math(dh, fv, g, scale)
        outs[1][...] = df.astype(BF16)
        accs[1][...] += dg

    return _row_call("loss_post_bwd", body, T, [h, f, target], [g_post],
                     [_sds((T, D)), _sds((T, D), BF16)], [_sds((1, LANES)), _sds((1, D))])


def post_bwd(dh, f, g_post, scale):
    T, D = dh.shape

    def body(i, n, tiled, prev, nxt, params, outs, accs):
        df, dg = _post_bwd_math(tiled[0][...], tiled[1][...], params[0][...], scale)
        outs[0][...] = df.astype(BF16)
        accs[0][...] += dg

    return _row_call("post_bwd", body, T, [dh, f], [g_post], [_sds((T, D), BF16)], [_sds((1, D))])


def pre_bwd(dn, h, g_pre, dres):
    T, D = h.shape

    def body(i, n, tiled, prev, nxt, params, outs, accs):
        dnv, hv = tiled[0][...], tiled[1][...]
        r = _rstd(hv)
        hhat = hv * r
        z = dnv * params[0][...]
        outs[0][...] = tiled[2][...] + r * (z - hhat * jnp.mean(z * hhat, axis=-1, keepdims=True))
        accs[0][...] += _colsum(dnv * hhat)

    return _row_call("pre_bwd", body, T, [dn, h, dres], [g_pre], [_sds((T, D))], [_sds((1, D))])


def _head_ones():
    i = lax.broadcasted_iota(jnp.int32, (LANES, LANES), 0)
    j = lax.broadcasted_iota(jnp.int32, (LANES, LANES), 1)
    return jnp.where((i < HEAD) == (j < HEAD), 1.0, 0.0).astype(F32)


def _headsum(x):
    e = _head_ones()
    parts = [jnp.dot(x[:, s:s + LANES], e, precision=lax.Precision.HIGHEST, preferred_element_type=F32)
             for s in range(0, x.shape[1], LANES)]
    return parts[0] if len(parts) == 1 else jnp.concatenate(parts, axis=1)


def _headsum_cut(x):
    e = _head_ones().astype(BF16)
    hi, rest = _hi_lo(x)
    mid, lo = _hi_lo(rest)
    pieces = [p.astype(BF16) for p in (hi, mid, lo)]
    parts = [sum(jnp.dot(p[:, s:s + LANES], e, preferred_element_type=F32) for p in pieces)
             for s in range(0, x.shape[1], LANES)]
    return parts[0] if len(parts) == 1 else jnp.concatenate(parts, axis=1)


def _shift_down(x, before):
    row = lax.broadcasted_iota(jnp.int32, x.shape, 0)
    return jnp.where(row == 0, before, pltpu.roll(x, 1, 0))


def _shift_up(x, after):
    row = lax.broadcasted_iota(jnp.int32, x.shape, 0)
    return jnp.where(row == x.shape[0] - 1, after, pltpu.roll(x, x.shape[0] - 1, 0))


def _last_row(ref, keep):
    r = ref[ref.shape[0] - 1:ref.shape[0], :]
    return jnp.where(keep, r, jnp.zeros_like(r))


def _first_row(ref, keep):
    r = ref[0:1, :]
    return jnp.where(keep, r, jnp.zeros_like(r))


def _softplus(u):
    return jnp.maximum(u, 0.0) + jnp.log(1.0 + jnp.exp(-jnp.abs(u)))


def _dotb(a, b, contract):
    return lax.dot_general(a.astype(BF16), b.astype(BF16), (contract, ((), ())), preferred_element_type=F32)


_NN, _NT, _TN = ((1,), (0,)), ((1,), (1,)), ((0,), (0,))


def _prep_forward(z, zprev_row, zl, zlprev_row, mu, mul, w0, a0, kk_w, ka_w, w2p, a2p, g2p):
    W = w0.shape[1]
    zs = z + (_shift_down(z, zprev_row) - z) * mu
    zls = zl + (_shift_down(zl, zlprev_row) - zl) * mul
    r, k, v = zs[:, :W], zs[:, W:2 * W], zs[:, 2 * W:]
    th, sg = jnp.tanh(zls), jax.nn.sigmoid(zls)
    xw = w0 + _dotb(th, w2p, _NN)
    wlog = -_softplus(-xw) - 0.5
    ew = jnp.exp(wlog)
    decay = jnp.exp(-ew)
    a = jax.nn.sigmoid(a0 + _dotb(zls, a2p, _NN))
    gate = _dotb(sg, g2p, _NN)
    q = k * kk_w
    nrm = jnp.sqrt(_headsum(q * q))
    den = jnp.maximum(nrm, L2_EPS)
    kk = q / den
    kmod = k * (1.0 + (a - 1.0) * ka_w)
    return dict(zs=zs, zls=zls, r=r, k=k, v=v, th=th, sg=sg, xw=xw, ew=ew, decay=decay, a=a, gate=gate,
                nrm=nrm, den=den, kk=kk, kmod=kmod)


def rwkv_prep(p, cols, mu, mul, w0, a0, kk_w, ka_w, w2p, a2p, g2p):
    T = p.shape[0]
    W = w0.shape[1]

    def body(i, n, tiled, prev, nxt, params, outs, accs):
        c = _prep_forward(tiled[0][...], _last_row(prev[0], i > 0), tiled[1][...], _last_row(prev[1], i > 0),
                          *[q[...] for q in params])
        for o, val in zip(outs, (c["r"], c["decay"], c["kmod"], c["v"], -c["kk"], c["kk"] * c["a"], c["gate"])):
            o[...] = val

    return _row_call("rwkv_prep", body, T, [cols["rkv"], cols["lat"]],
                     [mu, mul, w0, a0, kk_w, ka_w, w2p, a2p, g2p], [_sds((T, W))] * 7,
                     prev=[cols["rkv"], cols["lat"]], tile=RWKV_ROW_TILE)


def _post_forward(y, r, kmod, v, gn_w, gn_b, rk):
    mean = _headsum_cut(y) * (1.0 / HEAD)
    yc = y - mean
    rstd = lax.rsqrt(_headsum_cut(yc * yc) * (1.0 / HEAD) + GN_EPS)
    yn = yc * rstd
    s = _headsum_cut(r * kmod * rk)
    return yn, rstd, s, yn * gn_w + gn_b + s * v


def rwkv_post(y, r, kmod, v, gate, gn_w, gn_b, rk):
    T, W = y.shape

    def body(i, n, tiled, prev, nxt, params, outs, accs):
        yv, rv, kv, vv, gv = [t[...] for t in tiled]
        _, _, _, o = _post_forward(yv, rv, kv, vv, *[q[...] for q in params])
        outs[0][...] = (o * gv).astype(BF16)

    return _row_call("rwkv_post", body, T, [y, r, kmod, v, gate], [gn_w, gn_b, rk], [_sds((T, W), BF16)],
                     tile=RWKV_ROW_TILE)[0]


def rwkv_post_bwd(dout, y, r, kmod, v, gate, gn_w, gn_b, rk):
    T, W = y.shape

    def body(i, n, tiled, prev, nxt, params, outs, accs):
        dv_, yv, rv, kv, vv, gv = [t[...] for t in tiled]
        gn_w_, gn_b_, rk_ = [q[...] for q in params]
        yn, rstd, s, o = _post_forward(yv, rv, kv, vv, gn_w_, gn_b_, rk_)
        do = dv_ * gv
        outs[4][...] = dv_ * o
        accs[0][...] += _colsum(do * yn)
        accs[1][...] += _colsum(do)
        dyn = do * gn_w_
        outs[0][...] = rstd * (dyn - _headsum_cut(dyn) * (1.0 / HEAD) - yn * (_headsum_cut(dyn * yn) * (1.0 / HEAD)))
        ds = _headsum_cut(do * vv)
        outs[1][...] = ds * kv * rk_
        outs[2][...] = ds * rv * rk_
        outs[3][...] = do * s
        accs[2][...] += _colsum(ds * rv * kv)

    return _row_call("rwkv_post_bwd", body, T, [dout, y, r, kmod, v, gate], [gn_w, gn_b, rk],
                     [_sds((T, W))] * 5, [_sds((1, W))] * 3, tile=RWKV_ROW_TILE)


def rwkv_prep_bwd(p, cols, grads, mu, mul, w0, a0, kk_w, ka_w, w2p, a2p, g2p):
    T = p.shape[0]
    W = w0.shape[1]
    latp = w2p.shape[0]

    def body(i, n, tiled, prev, nxt, params, outs, accs):
        pv = [q[...] for q in params]
        mu_, mul_, w0_, a0_, kk_w_, ka_w_, w2p_, a2p_, g2p_ = pv
        c = _prep_forward(tiled[0][...], _last_row(prev[0], i > 0), tiled[1][...], _last_row(prev[1], i > 0), *pv)
        dr_s, dr_x, ddecay, dk_s, dk_x, dv_s, dv_x, dneg, db, dgate = [t[...] for t in tiled[2:]]
        k, a, kk = c["k"], c["a"], c["kk"]
        dkmod = dk_s + dk_x
        dk = dkmod * (1.0 + (a - 1.0) * ka_w_)
        da = dkmod * k * ka_w_ + db * kk
        accs[0][...] += _colsum(dkmod * k * (a - 1.0))
        dkk = db * a - dneg
        dq = jnp.where(c["nrm"] > L2_EPS, dkk - kk * _headsum(dkk * kk), dkk) / c["den"]
        dk = dk + dq * kk_w_
        accs[1][...] += _colsum(dq * k)
        dxa = da * a * (1.0 - a)
        accs[2][...] += _colsum(dxa)
        accs[4][...] += _dotb(c["zls"], dxa, _TN)
        dzls = _dotb(dxa, a2p_, _NT)
        dxw = (-ddecay * c["decay"] * c["ew"]) * jax.nn.sigmoid(-c["xw"])
        accs[3][...] += _colsum(dxw)
        accs[5][...] += _dotb(c["th"], dxw, _TN)
        dzls = dzls + _dotb(dxw, w2p_, _NT) * (1.0 - c["th"] * c["th"])
        accs[6][...] += _dotb(c["sg"], dgate, _TN)
        dzls = dzls + _dotb(dgate, g2p_, _NT) * c["sg"] * (1.0 - c["sg"])
        outs[0][...] = jnp.concatenate([dr_s + dr_x, dk, dv_s + dv_x], axis=1)
        outs[1][...] = dzls

    return _row_call("rwkv_prep_bwd", body, T, [cols["rkv"], cols["lat"]] + list(grads),
                     [mu, mul, w0, a0, kk_w, ka_w, w2p, a2p, g2p], [_sds((T, 3 * W)), _sds((T, latp))],
                     [_sds((1, W))] * 4 + [_sds((latp, W))] * 3, prev=[cols["rkv"], cols["lat"]], tile=RWKV_ROW_TILE)


def shift_bwd(cols, dzs, dzls, mu, mul):
    T = dzs.shape[0]

    def body(i, n, tiled, prev, nxt, params, outs, accs):
        for j in range(2):
            z, d, m = tiled[j][...], tiled[2 + j][...], params[j][...]
            zprev = _shift_down(z, _last_row(prev[j], i > 0))
            dnext = _shift_up(d, _first_row(nxt[j], i < n - 1))
            outs[j][...] = (d * (1.0 - m) + dnext * m).astype(BF16)
            accs[j][...] += _colsum(d * (zprev - z))

    return _row_call("shift_bwd", body, T, [cols["rkv"], cols["lat"], dzs, dzls], [mu, mul],
                     [_sds(dzs.shape, BF16), _sds(dzls.shape, BF16)], [_sds(mu.shape), _sds(mul.shape)],
                     prev=[cols["rkv"], cols["lat"]], nxt=[dzs, dzls])


def _window_pick(x, windows):
    gid = lax.broadcasted_iota(jnp.int32, x.shape, 1) // (x.shape[1] // len(windows))
    out = windows[-1]
    for g in range(len(windows) - 2, -1, -1):
        out = jnp.where(gid == g, windows[g], out)
    return out


def _pool_counts(t0, rows, width):
    t = (t0 + lax.broadcasted_iota(jnp.int32, (rows, width), 0) + 1).astype(F32)
    return _window_pick(t, [jnp.minimum(t, float(w)) for w in POOL_WINDOWS])


def _pool_mixed(x, before, t0):
    tm, width = x.shape
    xe = jnp.concatenate([before, x], axis=0)
    sums, s, span = [], xe, 1
    for w in POOL_WINDOWS:
        while span < w:
            s = s + pltpu.roll(s, span, 0)
            span *= 2
        sums.append(s[POOL_HALO:, :])
    return _window_pick(x, sums) / _pool_counts(t0, tm, width) - x


def _group_dot(x, w_ref, contract):
    gd = w_ref.shape[-1]
    parts = [_dotb(x[:, g * gd:(g + 1) * gd], w_ref[g], contract) for g in range(w_ref.shape[0])]
    return jnp.concatenate(parts, axis=1)


def pool_fwd(cols, pool_w, pool_scale):
    T, width = cols["pool"][0].shape[0], cols["pool"][1]
    tm = min(ROW_TILE, T)

    def body(i, n, tiled, prev, nxt, params, outs, accs):
        before = jnp.where(i > 0, prev[0][...], 0.0)
        mixed = _pool_mixed(tiled[0][...], before, i * tm)
        outs[0][...] = (_group_dot(mixed, params[0], _NN) * params[1][...]).astype(BF16)

    return _row_call("pool_fwd", body, T, [cols["pool"]], [pool_w, pool_scale], [_sds((T, width), BF16)],
                     prev=[cols["pool"]], halo=POOL_HALO)[0]


def pool_bwd(cols, dout, pool_w, pool_scale):
    T, width = dout.shape
    tm = min(ROW_TILE, T)

    def body(i, n, tiled, prev, nxt, params, outs, accs):
        w_ref, scale = params[0], params[1][...]
        before = jnp.where(i > 0, prev[0][...], 0.0)
        mixed = _pool_mixed(tiled[0][...], before, i * tm)
        dv = tiled[1][...]
        accs[1][...] += _colsum(dv * _group_dot(mixed, w_ref, _NN))
        after = jnp.where(i < n - 1, nxt[0][...], 0.0)
        dys = jnp.concatenate([dv, after], axis=0) * scale
        gd = w_ref.shape[-1]
        for g in range(w_ref.shape[0]):
            accs[0][g] += _dotb(mixed[:, g * gd:(g + 1) * gd], dys[:tm, g * gd:(g + 1) * gd], _TN)
        dmixed = _group_dot(dys, w_ref, _NT)
        u = dmixed / _pool_counts(i * tm, tm + POOL_HALO, width)
        rows = tm + POOL_HALO
        sums, s, span = [], u, 1
        for w in POOL_WINDOWS:
            while span < w:
                s = s + pltpu.roll(s, rows - span, 0)
                span *= 2
            sums.append(s[:tm, :])
        outs[0][...] = (_window_pick(dv, sums) - dmixed[:tm, :]).astype(BF16)

    return _row_call("pool_bwd", body, T, [cols["pool"], dout], [pool_w, pool_scale], [_sds((T, width), BF16)],
                     [_sds(pool_w.shape), _sds((1, width))], prev=[cols["pool"]], nxt=[dout], halo=POOL_HALO)


def _wkv_consts(pairs):
    lane = lax.broadcasted_iota(jnp.int32, (HEAD, LANES), 1)
    sub = lax.broadcasted_iota(jnp.int32, (pairs * HEAD, LANES), 0)
    lane_all = lax.broadcasted_iota(jnp.int32, (pairs * HEAD, LANES), 1)
    i = lax.broadcasted_iota(jnp.int32, (LANES, LANES), 0)
    j = lax.broadcasted_iota(jnp.int32, (LANES, LANES), 1)
    ones = jnp.where((i < HEAD) == (j < HEAD), 1.0, 0.0).astype(BF16)
    diag = jnp.where((lane_all & (HEAD - 1)) == (sub & (HEAD - 1)), 1.0, 0.0).astype(F32)
    return lane < HEAD, diag, ones


def _segsum(p, in_a):
    sa = jnp.sum(jnp.where(in_a, p, 0.0), axis=1, keepdims=True)
    sb = jnp.sum(jnp.where(in_a, 0.0, p), axis=1, keepdims=True)
    return jnp.where(in_a, sa, sb)


def _hi_lo(p):
    hi = lax.bitcast_convert_type(lax.bitcast_convert_type(p, jnp.uint32) & jnp.uint32(0xFFFF0000), F32)
    return hi, p - hi


def _segsum_mxu(p, ones):
    hi, lo = _hi_lo(p)
    return (jnp.dot(hi.astype(BF16), ones, preferred_element_type=F32)
            + jnp.dot(lo.astype(BF16), ones, preferred_element_type=F32))


def _cat(parts, axis):
    return parts[0] if len(parts) == 1 else jnp.concatenate(parts, axis=axis)


def _tile_rows(row, pairs):
    return _cat([jnp.broadcast_to(row[:, p * LANES:(p + 1) * LANES], (HEAD, LANES)) for p in range(pairs)], 0)


def _spread(row, pairs, diag16, ones):
    hi, lo = _hi_lo(row)
    return (jnp.dot(_tile_rows(hi.astype(BF16), pairs) * diag16, ones, preferred_element_type=F32)
            + jnp.dot(_tile_rows(lo.astype(BF16), pairs) * diag16, ones, preferred_element_type=F32))


def _pair_colsums(x, pairs):
    return _cat([_colsum(x[p * HEAD:(p + 1) * HEAD]) for p in range(pairs)], 1)


def _spread_split(row, pairs, in_a, diag, diag16, ones):
    n_mxu = min(WKV_MXU_PAIRS, pairs)
    parts = [_spread(row[:, :n_mxu * LANES], n_mxu, diag16[:n_mxu * HEAD], ones)]
    parts += [_segsum(row[:, p * LANES:(p + 1) * LANES] * diag[:HEAD], in_a) for p in range(n_mxu, pairs)]
    return _cat(parts, 0)


def _segsum_split(x, pairs, in_a, ones):
    n_mxu = min(WKV_MXU_PAIRS, pairs)
    parts = [_segsum_mxu(x[:n_mxu * HEAD], ones)]
    parts += [_segsum(x[p * HEAD:(p + 1) * HEAD], in_a) for p in range(n_mxu, pairs)]
    return _cat(parts, 0)


def wkv_fwd(r, w, k, v, a, b, carry=None):
    T, W = r.shape
    P = W // LANES
    PB = min(WKV_PAIRS, P)
    chunk = min(WKV_CHUNK_FWD, T)
    NC = T // chunk
    R = PB * HEAD
    ahead = min(WKV_UNROLL, chunk // 2)
    c_in, c_out = (len(carry.inputs), len(carry.out_shapes)) if carry else (0, 0)

    def body(*refs):
        r_ref, w_ref, k_ref, v_ref, a_ref, b_ref = refs[:6]
        y_ref, st_ref, sa_ref = refs[6 + c_in:9 + c_in]
        vt_ref, s_ref = refs[9 + c_in + c_out:11 + c_in + c_out]
        carried = refs[6:6 + c_in], refs[9 + c_in:9 + c_in + c_out], refs[11 + c_in + c_out:]
        g, c = pl.program_id(0), pl.program_id(1)

        if carry:
            @pl.when((g == 0) & (c == 0))
            def _():
                carry.start(*carried)

        @pl.when(c == 0)
        def _():
            s_ref[...] = jnp.zeros_like(s_ref)

        in_a, diag, ones = _wkv_consts(PB)
        diag16 = diag.astype(BF16)

        def step(t, _):
            rows = [ref[pl.ds(t, 1), :] for ref in (w_ref, k_ref, a_ref, b_ref)]
            for p in range(PB):
                wt, kt, at, bt = [x[:, p * LANES:(p + 1) * LANES] for x in rows]
                rs = pl.ds(p * HEAD, HEAD)
                S = s_ref[rs]
                sa = _segsum(S * at, in_a)
                sa_ref[t, rs] = sa
                S = S * wt + sa * bt + vt_ref[t, rs] * kt
                st_ref[t, rs] = S
                s_ref[rs] = S
            return 0

        def spread_step(t, _):
            vt_ref[t + ahead] = _spread(v_ref[pl.ds(t + ahead, 1), :], PB, diag16, ones)
            return step(t, 0)

        for t in range(ahead):
            vt_ref[t] = _spread_split(v_ref[t:t + 1, :], PB, in_a, diag, diag16, ones)
        lax.fori_loop(0, chunk - ahead, spread_step, 0, unroll=WKV_UNROLL)
        lax.fori_loop(chunk - ahead, chunk, step, 0, unroll=WKV_UNROLL)

        def readout(t, _):
            yt = _segsum_split(st_ref[t] * _tile_rows(r_ref[pl.ds(t, 1), :], PB), PB, in_a, ones) * diag
            y_ref[pl.ds(t, 1), :] = _pair_colsums(yt, PB)
            return 0

        lax.fori_loop(0, chunk, readout, 0, unroll=2 * WKV_UNROLL)

        if carry:
            @pl.when((g == P // PB - 1) & (c == NC - 1))
            def _():
                carry.finish(*carried)

    spec = pl.BlockSpec((chunk, PB * LANES), lambda g, c: (c, g))
    tiles = pl.BlockSpec((chunk, R, LANES), lambda g, c: (c, g, 0))
    res = _pallas(
        body, name="wkv_fwd", grid=(P // PB, NC), in_specs=[spec] * 6 + [ANY] * c_in,
        out_specs=[spec, tiles, tiles] + [ANY] * c_out,
        out_shape=[_sds((T, W)), _sds((T, P * HEAD, LANES)), _sds((T, P * HEAD, LANES))]
        + (list(carry.out_shapes) if carry else []),
        scratch_shapes=[pltpu.VMEM((chunk, R, LANES), F32), pltpu.VMEM((R, LANES), F32)]
        + (list(carry.scratch) if carry else []),
        compiler_params=_params("arbitrary", "arbitrary") if carry else _params("parallel", "arbitrary"),
    )(r, w, k, v, a, b, *(carry.inputs if carry else []))
    return (res[:3], res[3:]) if carry else res


def wkv_bwd(r, w, k, v, a, b, dy, st, sa):
    T, W = r.shape
    P = W // LANES
    PB = min(WKV_PAIRS, P)
    chunk = min(WKV_CHUNK, T)
    NC = T // chunk
    R = PB * HEAD
    ahead = min(WKV_UNROLL, chunk // 2)

    def body(r_ref, w_ref, k_ref, v_ref, a_ref, b_ref, dy_ref, st_ref, before_ref, sa_ref,
             dr_ref, dw_ref, dk_ref, dv_ref, da_ref, db_ref, ds_ref, dyt_ref, dst_ref, dsa_ref):
        c = pl.program_id(1)

        @pl.when(c == 0)
        def _():
            ds_ref[...] = jnp.zeros_like(ds_ref)

        in_a, diag, ones = _wkv_consts(PB)
        diag16 = diag.astype(BF16)

        def bstep(n, _):
            t = chunk - 1 - n
            rows = [ref[pl.ds(t, 1), :] for ref in (r_ref, w_ref, a_ref, b_ref)]
            for p in range(PB):
                rt, wt, at, bt = [x[:, p * LANES:(p + 1) * LANES] for x in rows]
                rs = pl.ds(p * HEAD, HEAD)
                dS = ds_ref[rs] + dyt_ref[t, rs] * rt
                dst_ref[t, rs] = dS
                dsa = _segsum(dS * bt, in_a)
                dsa_ref[t, rs] = dsa
                ds_ref[rs] = dS * wt + dsa * at
            return 0

        def spread_bstep(n, _):
            t = chunk - 1 - ahead - n
            dyt_ref[t] = _spread(dy_ref[pl.ds(t, 1), :], PB, diag16, ones)
            return bstep(n, 0)

        for t in range(chunk - ahead, chunk):
            dyt_ref[t] = _spread_split(dy_ref[t:t + 1, :], PB, in_a, diag, diag16, ones)
        lax.fori_loop(0, chunk - ahead, spread_bstep, 0, unroll=WKV_UNROLL)
        lax.fori_loop(chunk - ahead, chunk, bstep, 0, unroll=WKV_UNROLL)

        def collect(t, _):
            sn, dS, dsa = st_ref[t], dst_ref[t], dsa_ref[t]
            sp = st_ref[jnp.maximum(t - 1, 0)]
            dvt = _segsum_split(dS * _tile_rows(k_ref[pl.ds(t, 1), :], PB), PB, in_a, ones) * diag
            vt = _spread_split(v_ref[pl.ds(t, 1), :], PB, in_a, diag, diag16, ones)
            for ref, val in ((dr_ref, sn * dyt_ref[t]), (dw_ref, dS * sp), (dk_ref, dS * vt), (dv_ref, dvt),
                             (da_ref, sp * dsa), (db_ref, dS * sa_ref[t])):
                ref[pl.ds(t, 1), :] = _pair_colsums(val, PB)
            return 0

        lax.fori_loop(0, chunk, collect, 0, unroll=WKV_UNROLL)
        first = jnp.where(c == NC - 1, 0.0, before_ref[0])
        dw_ref[0:1, :] = _pair_colsums(dst_ref[0] * first, PB)
        da_ref[0:1, :] = _pair_colsums(first * dsa_ref[0], PB)

    spec = pl.BlockSpec((chunk, PB * LANES), lambda g, c: (NC - 1 - c, g))
    tiles = pl.BlockSpec((chunk, R, LANES), lambda g, c: (NC - 1 - c, g, 0))
    before = pl.BlockSpec((1, R, LANES), lambda g, c: (jnp.maximum((NC - 1 - c) * chunk - 1, 0), g, 0))

    def scratch(n):
        return pltpu.VMEM((n, R, LANES), F32)

    return _pallas(
        body, name="wkv_bwd", grid=(P // PB, NC), in_specs=[spec] * 7 + [tiles, before, tiles],
        out_specs=[spec] * 6, out_shape=[_sds((T, W))] * 6,
        scratch_shapes=[pltpu.VMEM((R, LANES), F32), scratch(chunk), scratch(chunk), scratch(chunk)],
        compiler_params=_params("parallel", "arbitrary"),
    )(r, w, k, v, a, b, dy, st, st, sa)


def _position():
    return lax.axis_index("x"), lax.axis_index("y"), lax.axis_index("c")


def _other_chips(x, y):
    return [(1 - x, y), (x, 1 - y), (1 - x, 1 - y)]


class _Carry:
    def __init__(self, inputs, out_shapes, scratch, start, finish):
        self.inputs, self.out_shapes, self.scratch, self.start, self.finish = inputs, out_shapes, scratch, start, finish


def _run_carry(name, carry):
    n_in, n_out = len(carry.inputs), len(carry.out_shapes)

    def body(*refs):
        parts = refs[:n_in], refs[n_in:n_in + n_out], refs[n_in + n_out:]
        carry.start(*parts)
        carry.finish(*parts)

    return _pallas(body, name=name, in_specs=[ANY] * n_in, out_specs=[ANY] * n_out, out_shape=list(carry.out_shapes),
                   scratch_shapes=list(carry.scratch))(*carry.inputs)


def gather_carry(shards):
    n = len(shards)

    def plan(x_refs, out_refs, sems):
        send_sems, recv_sems, local_sems = sems
        x, y, c = _position()
        me, sibling = (x, y, c), (x, y, 1 - c)
        chips = _other_chips(x, y)

        def slot(ref, pos):
            return ref.at[4 * pos[0] + 2 * pos[1] + pos[2]]

        def copy(t, j, block, to, src=None):
            dst = slot(out_refs[t], block)
            return pltpu.make_async_remote_copy(
                src_ref=dst if src is None else src, dst_ref=dst, send_sem=send_sems.at[t, j],
                recv_sem=recv_sems.at[t, j], device_id=to, device_id_type=MESH)

        mine = [pltpu.make_async_copy(x_refs[t], slot(out_refs[t], me), local_sems.at[t]) for t in range(n)]
        first = []
        for t in range(n):
            first.append(copy(t, 0, me, sibling, src=x_refs[t]))
            first += [copy(t, 1 + j, me, (*chip, c), src=x_refs[t]) for j, chip in enumerate(chips)]
        return c, me, sibling, chips, copy, mine, first

    def start(x_refs, out_refs, sems):
        _, _, _, _, _, mine, first = plan(x_refs, out_refs, sems)
        for cp in mine + first:
            cp.start()

    def finish(x_refs, out_refs, sems):
        c, me, sibling, chips, copy, mine, first = plan(x_refs, out_refs, sems)
        passed = []
        for t in range(n):
            for j, chip in enumerate(chips):
                copy(t, 1 + j, (*chip, c), me).wait_recv()
                fwd = copy(t, 4 + j, (*chip, c), sibling)
                fwd.start()
                passed.append(fwd)
        for t in range(n):
            copy(t, 0, sibling, me).wait_recv()
            for j, chip in enumerate(chips):
                copy(t, 4 + j, (*chip, 1 - c), me).wait_recv()
        for cp in first + passed:
            cp.wait_send()
        for cp in mine:
            cp.wait()

    return _Carry(list(shards), [_sds((N_DEV,) + s.shape, s.dtype) for s in shards],
                  [pltpu.SemaphoreType.DMA((n, 7)), pltpu.SemaphoreType.DMA((n, 7)), pltpu.SemaphoreType.DMA((n,))],
                  start, finish)


def all_gather(shards):
    return _run_carry("all_gather", gather_carry(shards))


def exchange_sibling(parts):
    n = len(parts)

    def body(*refs):
        p_refs, out_refs = refs[:n], refs[n:2 * n]
        send_sems, recv_sems = refs[2 * n:]
        x, y, c = _position()
        copies = []
        for t in range(n):
            for q in range(N_CHIP):
                cp = pltpu.make_async_remote_copy(
                    src_ref=p_refs[t].at[q, 1 - c], dst_ref=out_refs[t].at[q], send_sem=send_sems.at[t, q],
                    recv_sem=recv_sems.at[t, q], device_id=(x, y, 1 - c), device_id_type=MESH)
                cp.start()
                copies.append(cp)
        for cp in copies:
            cp.wait()

    return _pallas(
        body, name="exchange_sibling", in_specs=[ANY] * n, out_specs=[ANY] * n,
        out_shape=[_sds((N_CHIP,) + p.shape[2:], p.dtype) for p in parts],
        scratch_shapes=[pltpu.SemaphoreType.DMA((n, N_CHIP)), pltpu.SemaphoreType.DMA((n, N_CHIP))],
    )(*parts)


def chips_carry(parts):
    n = len(parts)

    def plan(p_refs, out_refs, sems):
        send_sems, recv_sems, local_sems = sems
        x, y, c = _position()
        local = [pltpu.make_async_copy(p_refs[t].at[2 * x + y], out_refs[t].at[3], local_sems.at[t]) for t in range(n)]
        remote = [pltpu.make_async_remote_copy(
            src_ref=p_refs[t].at[2 * cx + cy], dst_ref=out_refs[t].at[j], send_sem=send_sems.at[t, j],
            recv_sem=recv_sems.at[t, j], device_id=(cx, cy, c), device_id_type=MESH)
            for t in range(n) for j, (cx, cy) in enumerate(_other_chips(x, y))]
        return local, remote

    def start(p_refs, out_refs, sems):
        local, remote = plan(p_refs, out_refs, sems)
        for cp in local + remote:
            cp.start()

    def finish(p_refs, out_refs, sems):
        local, remote = plan(p_refs, out_refs, sems)
        for cp in remote + local:
            cp.wait()

    return _Carry(list(parts), [_sds(p.shape, p.dtype) for p in parts],
                  [pltpu.SemaphoreType.DMA((n, 3)), pltpu.SemaphoreType.DMA((n, 3)), pltpu.SemaphoreType.DMA((n,))],
                  start, finish)


def _flat_tile(rows, cols):
    tr = rows
    for d in range(16, min(rows, 512) + 1, 16):
        if rows % d == 0 and d * cols * 4 <= 2 * 1024 * 1024:
            tr = d
    return tr


def pair_add(part, recv):
    _, _, R, C = part.shape
    tr = _flat_tile(R, C)
    core = jnp.reshape(lax.axis_index("c"), (1,)).astype(jnp.int32)

    def body(core_ref, p_ref, r_ref, o_ref):
        o_ref[...] = (p_ref[...] + r_ref[...]).astype(BF16)

    grid_spec = pltpu.PrefetchScalarGridSpec(
        num_scalar_prefetch=1, grid=(N_CHIP, R // tr),
        in_specs=[pl.BlockSpec((None, None, tr, C), lambda q, i, core_ref: (q, core_ref[0], i, 0)),
                  pl.BlockSpec((None, tr, C), lambda q, i, core_ref: (q, i, 0))],
        out_specs=pl.BlockSpec((None, tr, C), lambda q, i, core_ref: (q, i, 0)))
    return _pallas(body, name="pair_add", grid_spec=grid_spec, out_shape=_sds((N_CHIP, R, C), BF16),
                   compiler_params=_params("parallel", "parallel"))(core, part, recv)


def adamw(w, m, v, slabs):
    R, C = w.shape
    tr = _flat_tile(R, C)
    n = slabs.shape[0]

    def body(w_ref, m_ref, v_ref, s_ref, g_ref, d_ref, nm_ref, nv_ref):
        g = s_ref[0].astype(F32)
        for j in range(1, n):
            g = g + s_ref[j].astype(F32)
        m2 = ADAM_B1 * m_ref[...] + (1.0 - ADAM_B1) * g
        v2 = ADAM_B2 * v_ref[...] + (1.0 - ADAM_B2) * (g * g)
        m_hat = m2 / (1.0 - ADAM_B1 ** ADAM_STEP)
        v_hat = v2 / (1.0 - ADAM_B2 ** ADAM_STEP)
        g_ref[...] = g
        d_ref[...] = -ADAM_LR * (m_hat / (jnp.sqrt(v_hat) + ADAM_EPS) + ADAM_WD * w_ref[...])
        nm_ref[...] = m2
        nv_ref[...] = v2

    spec = pl.BlockSpec((tr, C), lambda i: (i, 0))
    return _pallas(body, name="adamw", grid=(R // tr,),
                   in_specs=[spec] * 3 + [pl.BlockSpec((n, tr, C), lambda i: (0, i, 0))], out_specs=[spec] * 4,
                   out_shape=[_sds((R, C))] * 4, compiler_params=_params("parallel"))(w, m, v, slabs)


def _unshard_cols(g):
    return jnp.transpose(g, (1, 0, 2)).reshape(g.shape[1], -1)


def _unshard_rows(g):
    return g.reshape(-1, g.shape[2])


def _shard_cols(full):
    R, C = full.shape
    return jnp.transpose(full.reshape(R, N_DEV, C // N_DEV), (1, 0, 2)).reshape(N_CHIP, 2, R, C // N_DEV)


def _shard_rows(full):
    R, C = full.shape
    return full.reshape(N_CHIP, 2, R // N_DEV, C)


WEIGHTS = ['ln_ffn1_pre', 'ln_ffn1_post', 'ffn1_gate', 'ffn1_up', 'ffn1_down', 'ln_mix_pre', 'ln_mix_post', 'w_in',
           'rwkv_mu', 'rwkv_w0', 'rwkv_w2', 'rwkv_a0', 'rwkv_a2', 'rwkv_g2', 'rwkv_k_k', 'rwkv_k_a', 'rwkv_r_k',
           'rwkv_gn_w', 'rwkv_gn_b', 'w_proj_a', 'pool_w', 'pool_scale', 'w_proj_b', 'w_out', 'ln_ffn2_pre',
           'ln_ffn2_post', 'ffn2_gate', 'ffn2_up', 'ffn2_down']
COL_SHARDED = ['ffn1_gate', 'ffn1_up', 'ffn2_gate', 'ffn2_up', 'w_in', 'rwkv_w2', 'rwkv_a2', 'rwkv_g2', 'w_proj_a',
               'w_proj_b']
ROW_SHARDED = ['ffn1_down', 'ffn2_down', 'w_out', 'pool_w']
TRANSPOSED = ['ffn1_gate', 'ffn1_up', 'ffn2_gate', 'ffn2_up', 'w_in', 'w_proj_a', 'w_proj_b']
SHARDED = COL_SHARDED + ROW_SHARDED
REPLICATED = [n for n in WEIGHTS if n not in SHARDED]


def _step(args):
    wts = {n: args[n] if args[n].ndim == 2 else args[n][0] for n in WEIGHTS}
    x, target = args["x"][0], args["loss_target"][0]
    T, D = x.shape
    W = wts["rwkv_w0"].shape[1]
    PW = wts["pool_scale"].shape[1]
    LW, LA, LG = wts["rwkv_w2"].shape[0], wts["rwkv_a2"].shape[0], wts["rwkv_g2"].shape[0]
    lat = LW + LA + LG
    latp = _round_up(lat, LAT_ALIGN)
    rc = 3 * W + lat
    base = 3 * W + PW + 2 * D
    n_groups, gshard, gd = wts["pool_w"].shape

    pool_w_shard = wts["pool_w"].reshape(n_groups * gshard, gd)
    shards = {n: (pool_w_shard if n == "pool_w" else wts[n]).astype(BF16) for n in SHARDED}
    shards.update({n: shards[n].T for n in TRANSPOSED})
    full = {}

    def fetch(names):
        return gather_carry([shards[n] for n in names])

    def arrived(names, got):
        for n, g in zip(names, got):
            if n == "pool_w":
                full[n] = jnp.transpose(g.reshape(N_DEV, n_groups, gshard, gd), (1, 0, 2, 3)).reshape(n_groups, gd, gd)
            elif n in COL_SHARDED and n not in TRANSPOSED:
                full[n] = _unshard_cols(g)
            else:
                full[n] = _unshard_rows(g)

    arrived(["ffn1_gate", "ffn1_up"], all_gather([shards["ffn1_gate"], shards["ffn1_up"]]))
    mu = wts["rwkv_mu"]
    mu_rkv = mu[:, :3 * W]
    mu_lat = jnp.concatenate([mu[:, 3 * W:], jnp.zeros((1, latp - lat), F32)], axis=1)
    rk = wts["rwkv_r_k"].reshape(1, W)

    n1 = rms_pre(x, wts["ln_ffn1_pre"])
    (g1, u1, act1), got = _mm("ffn1_up", [n1], [full["ffn1_gate"], full["ffn1_up"]], "nt", [BF16] * 3,
                              epilogue=_swiglu_fwd_epi, carry=fetch(["ffn1_down"]))
    arrived(["ffn1_down"], got)
    (f1,), got = _mm("ffn1_down", [act1], [full["ffn1_down"]], "nn", [F32], tm=512, carry=fetch(["w_in"]))
    arrived(["w_in"], got)
    w_in = full["w_in"]
    w_in_p = jnp.concatenate([w_in[:3 * W], w_in[rc:], w_in[3 * W:rc], jnp.zeros((latp - lat, D), BF16)], axis=0)
    h1, nm = post_pre(x, f1, wts["ln_ffn1_post"], wts["ln_mix_pre"], MACARON)
    mixer = ["rwkv_w2", "rwkv_a2", "rwkv_g2", "w_proj_a", "w_proj_b", "pool_w", "w_out"]
    (p,), got = _mm("in_proj", [nm], [w_in_p], "nt", [F32], carry=fetch(mixer))
    arrived(mixer, got)
    pool_w = full["pool_w"]

    def pad_rows(m, at):
        return jnp.zeros((latp, W), BF16).at[at:at + m.shape[0]].set(m)

    w2p, a2p, g2p = pad_rows(full["rwkv_w2"], 0), pad_rows(full["rwkv_a2"], LW), pad_rows(full["rwkv_g2"], LW + LA)
    small = [mu_rkv, mu_lat, wts["rwkv_w0"], wts["rwkv_a0"], wts["rwkv_k_k"], wts["rwkv_k_a"], w2p, a2p, g2p]
    cols = {"rkv": (p, 3 * W, 0), "pool": (p, PW, 3 * W // PW), "lat": (p, latp, base // latp)}
    r, decay, kmod, v, aneg, bpos, gate = rwkv_prep(p, cols, *small)
    (y, states, sdota), got = wkv_fwd(r, decay, kmod, v, aneg, bpos, carry=fetch(["ffn2_gate", "ffn2_up"]))
    arrived(["ffn2_gate", "ffn2_up"], got)
    ya_in = rwkv_post(y, r, kmod, v, gate, wts["rwkv_gn_w"], wts["rwkv_gn_b"], rk)
    yb_in = pool_fwd(cols, pool_w, wts["pool_scale"])
    gates = [(p, 3 * W + PW), (p, 3 * W + PW + D)]
    m, ya, yb = _mm("mix", [ya_in, yb_in], [full["w_proj_a"], full["w_proj_b"]], "nt", [BF16] * 3,
                    extras=gates, epilogue=_mix_fwd_epi)
    mx = _mm("out_proj", [m], [full["w_out"]], "nn", [F32])[0]
    h2, n2 = post_pre(h1, mx, wts["ln_mix_post"], wts["ln_ffn2_pre"], 1.0)
    (g2_, u2, act2), got = _mm("ffn2_up", [n2], [full["ffn2_gate"], full["ffn2_up"]], "nt", [BF16] * 3,
                               epilogue=_swiglu_fwd_epi, carry=fetch(["ffn2_down"]))
    arrived(["ffn2_down"], got)
    f2 = _mm("ffn2_down", [act2], [full["ffn2_down"]], "nn", [F32], tm=512)[0]

    grads, slabs = {}, {}
    to_sibling = {}

    def pair_sums(names):
        parts = []
        for n in names:
            if n == "pool_w":
                parts.append(jnp.transpose(grads[n].reshape(n_groups, N_DEV, gshard, gd), (1, 0, 2, 3)).reshape(
                    N_CHIP, 2, n_groups * gshard, gd))
            elif n in COL_SHARDED and n not in TRANSPOSED:
                parts.append(_shard_cols(grads[n]))
            else:
                parts.append(_shard_rows(grads[n]))
        sent = [_shard_rows(to_sibling[n]) if n in to_sibling else part for n, part in zip(names, parts)]
        return [pair_add(part, rcv) for part, rcv in zip(parts, exchange_sibling(sent))]

    def landed(names, got):
        slabs.update(zip(names, got))

    dh3, df2, loss_part, grads["ln_ffn2_post"] = loss_post_bwd(h2, f2, wts["ln_ffn2_post"], target, MACARON)
    dg2, du2 = _mm("ffn2_dact", [df2], [full["ffn2_down"]], "nt", [BF16] * 2, extras=[(g2_, 0), (u2, 0)],
                   epilogue=_swiglu_bwd_epi)
    grads["ffn2_down"], to_sibling["ffn2_down"] = _mm("ffn2_ddown", [act2], [df2], "tn", [F32, BF16], tm=512, tn=1024,
                                                      epilogue=_twice_epi)
    grads["ffn2_gate"], grads["ffn2_up"], to_sibling["ffn2_gate"], to_sibling["ffn2_up"] = _mm(
        "ffn2_dup", [dg2, du2], [n2], "tn", [F32, F32, BF16, BF16], tm=512, epilogue=_twice_epi)
    dn2 = _mm("ffn2_dn_gate", [dg2], [full["ffn2_gate"]], "nn", [F32], tm=512)[0]
    dn2 = _mm("ffn2_dn", [du2], [full["ffn2_up"]], "nn", [F32], tm=512, extras=[(dn2, 0)], epilogue=_add_epi)[0]
    sums2 = pair_sums(["ffn2_down", "ffn2_gate", "ffn2_up"])
    dh2, grads["ln_ffn2_pre"] = pre_bwd(dn2, h2, wts["ln_ffn2_pre"], dh3)
    dmx, grads["ln_mix_post"] = post_bwd(dh2, mx, wts["ln_mix_post"], 1.0)
    (dya, dyb, dga, dgb), got = _mm("dmix", [dmx], [full["w_out"]], "nt", [BF16] * 4, extras=gates + [(ya, 0), (yb, 0)],
                                    epilogue=_mix_bwd_epi, carry=chips_carry(sums2[:1]))
    landed(["ffn2_down"], got)
    grads["w_out"] = _mm("dw_out", [m], [dmx], "tn", [F32])[0]
    dya_in = _mm("dproj_a", [dya], [full["w_proj_a"]], "nn", [F32])[0]
    dyb_in = _mm("dproj_b", [dyb], [full["w_proj_b"]], "nn", [F32])[0]
    grads["w_proj_a"] = _mm("dw_proj_a", [dya], [ya_in], "tn", [F32])[0]
    grads["w_proj_b"] = _mm("dw_proj_b", [dyb], [yb_in], "tn", [F32])[0]
    dz_pool, grads["pool_w"], grads["pool_scale"] = pool_bwd(cols, dyb_in, pool_w, wts["pool_scale"])
    dy, dr_x, dk_x, dv_x, dgate, grads["rwkv_gn_w"], grads["rwkv_gn_b"], drk = rwkv_post_bwd(
        dya_in, y, r, kmod, v, gate, wts["rwkv_gn_w"], wts["rwkv_gn_b"], rk)
    grads["rwkv_r_k"] = drk.reshape(wts["rwkv_r_k"].shape)
    dr_s, ddecay, dk_s, dv_s, dneg, dbpos = wkv_bwd(r, decay, kmod, v, aneg, bpos, dy, states, sdota)
    (dzs, dzls, grads["rwkv_k_a"], grads["rwkv_k_k"], grads["rwkv_a0"], grads["rwkv_w0"], da2p, dw2p, dg2p) = rwkv_prep_bwd(
        p, cols, [dr_s, dr_x, ddecay, dk_s, dk_x, dv_s, dv_x, dneg, dbpos, dgate], *small)
    grads["rwkv_w2"], grads["rwkv_a2"], grads["rwkv_g2"] = dw2p[:LW], da2p[LW:LW + LA], dg2p[LW + LA:lat]
    dz_rkv, dz_lat, dmu_rkv, dmu_lat = shift_bwd(cols, dzs, dzls, mu_rkv, mu_lat)
    grads["rwkv_mu"] = jnp.concatenate([dmu_rkv, dmu_lat[:, :lat]], axis=1)
    dp = jnp.concatenate([dz_rkv, dz_pool, dga, dgb, dz_lat], axis=1)
    (dnm,), got = _mm("din_proj", [dp], [w_in_p], "nn", [F32], tm=512, carry=chips_carry(sums2[1:2]))
    landed(["ffn2_gate"], got)
    (dw_in_p,), got = _mm("dw_in", [dp], [nm], "tn", [F32], tm=512, tn=1024, carry=chips_carry(sums2[2:]))
    landed(["ffn2_up"], got)
    grads["w_in"] = jnp.concatenate([dw_in_p[:3 * W], dw_in_p[base:base + lat], dw_in_p[3 * W:base]], axis=0)
    sums_mix = pair_sums(["w_in"] + mixer)
    dh1, grads["ln_mix_pre"] = pre_bwd(dnm, h1, wts["ln_mix_pre"], dh2)
    df1, grads["ln_ffn1_post"] = post_bwd(dh1, f1, wts["ln_ffn1_post"], MACARON)
    (dg1, du1), got = _mm("ffn1_dact", [df1], [full["ffn1_down"]], "nt", [BF16] * 2, extras=[(g1, 0), (u1, 0)],
                          epilogue=_swiglu_bwd_epi, carry=chips_carry(sums_mix[:1]))
    landed(["w_in"], got)
    (grads["ffn1_down"], to_sibling["ffn1_down"]), got = _mm(
        "ffn1_ddown", [act1], [df1], "tn", [F32, BF16], tm=512, tn=1024, epilogue=_twice_epi,
        carry=chips_carry(sums_mix[1:]))
    landed(mixer, got)
    (grads["ffn1_gate"], grads["ffn1_up"], to_sibling["ffn1_gate"], to_sibling["ffn1_up"]), got = _mm(
        "ffn1_dup", [dg1, du1], [n1], "tn", [F32, F32, BF16, BF16], tm=512, epilogue=_twice_epi,
        carry=chips_carry(pair_sums(["ffn1_down"])))
    landed(["ffn1_down"], got)
    sums1 = pair_sums(["ffn1_gate", "ffn1_up"])
    (dn1,), got = _mm("ffn1_dn_gate", [dg1], [full["ffn1_gate"]], "nn", [F32], tm=512, carry=chips_carry(sums1[:1]))
    landed(["ffn1_gate"], got)
    (dn1,), got = _mm("ffn1_dn", [du1], [full["ffn1_up"]], "nn", [F32], tm=512, extras=[(dn1, 0)], epilogue=_add_epi,
                      carry=chips_carry(sums1[1:]))
    landed(["ffn1_up"], got)
    grad_x, grads["ln_ffn1_pre"] = pre_bwd(dn1, x, wts["ln_ffn1_pre"], dh1)

    flat = jnp.concatenate([grads[n].reshape(-1) for n in REPLICATED])
    n_small = flat.shape[0]
    rows = _round_up(n_small, 8 * LANES) // LANES
    flat = jnp.concatenate([flat, jnp.zeros((rows * LANES - n_small,), F32)]).reshape(rows, LANES)
    small_slabs = all_gather([flat])[0]

    def packed(prefix):
        vals = jnp.concatenate([args[prefix + n].reshape(-1) for n in REPLICATED])
        return jnp.concatenate([vals, jnp.ones((rows * LANES - n_small,), F32)]).reshape(rows, LANES)

    outs = {}
    small_out = adamw(packed(""), packed("m_"), packed("v_"), small_slabs)
    offset = 0
    for n in REPLICATED:
        size = args[n].size
        outs[n] = [o.reshape(-1)[offset:offset + size].reshape(args[n].shape) for o in small_out]
        offset += size
    for n in SHARDED:
        slab = jnp.swapaxes(slabs[n], 1, 2) if n in TRANSPOSED else slabs[n]
        shard2d = slab.shape[1:]
        res = adamw(*[args[pre + n].reshape(shard2d) for pre in ("", "m_", "v_")], slab)
        outs[n] = [o.reshape(args[n].shape) for o in res]

    loss = lax.psum(loss_part[0, 0], ("x", "y", "c"))
    return (loss, grad_x[None], *[outs[n][0] for n in WEIGHTS], *[outs[n][1] for n in WEIGHTS],
            *[outs[n][2] for n in WEIGHTS], *[outs[n][3] for n in WEIGHTS])


ARG_NAMES = ["x"] + WEIGHTS + ["loss_target"] + ["m_" + n for n in WEIGHTS] + ["v_" + n for n in WEIGHTS]


def kernel(x, ln_ffn1_pre, ln_ffn1_post, ffn1_gate, ffn1_up, ffn1_down, ln_mix_pre, ln_mix_post, w_in, rwkv_mu, rwkv_w0,
           rwkv_w2, rwkv_a0, rwkv_a2, rwkv_g2, rwkv_k_k, rwkv_k_a, rwkv_r_k, rwkv_gn_w, rwkv_gn_b, w_proj_a, pool_w,
           pool_scale, w_proj_b, w_out, ln_ffn2_pre, ln_ffn2_post, ffn2_gate, ffn2_up, ffn2_down, loss_target,
           m_ln_ffn1_pre, m_ln_ffn1_post, m_ffn1_gate, m_ffn1_up, m_ffn1_down, m_ln_mix_pre, m_ln_mix_post, m_w_in,
           m_rwkv_mu, m_rwkv_w0, m_rwkv_w2, m_rwkv_a0, m_rwkv_a2, m_rwkv_g2, m_rwkv_k_k, m_rwkv_k_a, m_rwkv_r_k,
           m_rwkv_gn_w, m_rwkv_gn_b, m_w_proj_a, m_pool_w, m_pool_scale, m_w_proj_b, m_w_out, m_ln_ffn2_pre,
           m_ln_ffn2_post, m_ffn2_gate, m_ffn2_up, m_ffn2_down, v_ln_ffn1_pre, v_ln_ffn1_post, v_ffn1_gate, v_ffn1_up,
           v_ffn1_down, v_ln_mix_pre, v_ln_mix_post, v_w_in, v_rwkv_mu, v_rwkv_w0, v_rwkv_w2, v_rwkv_a0, v_rwkv_a2,
           v_rwkv_g2, v_rwkv_k_k, v_rwkv_k_a, v_rwkv_r_k, v_rwkv_gn_w, v_rwkv_gn_b, v_w_proj_a, v_pool_w, v_pool_scale,
           v_w_proj_b, v_w_out, v_ln_ffn2_pre, v_ln_ffn2_post, v_ffn2_gate, v_ffn2_up, v_ffn2_down):
    given = locals()
    return _step({n: given[n] for n in ARG_NAMES})
```
